```python
import jax
import jax.numpy as jnp
from jax import lax
import numpy as np


D_MODEL = 1024
BATCH = 2
SEQ = 8192
DEPTH = 1

HEAD_DIM = 64
NA_HEADS = 8
NA_WIDTH = NA_HEADS * HEAD_DIM
NA_ROWS = 8
NA_COLS = 16
GRID_W = 64
DIL_PAIRS = ((128, 1), (512, 4), (2048, 16))
NB_GROUPS = len(DIL_PAIRS)
NB_HEADS_PER_GROUP = 4
NB_HEADS = NB_GROUPS * NB_HEADS_PER_GROUP
NB_WIDTH = NB_HEADS * HEAD_DIM
ALIBI_MAX_EXP = 8.0
LOCAL_QBLOCK = 64
D_FF = 2816
N_IN = 3 * NA_WIDTH + 3 * NB_WIDTH + 2 * D_MODEL
NORM_EPS = 1e-6
ATTN_SCALE = HEAD_DIM ** -0.5

kernel_name = 'hybrid_natten_dilated_macaron_block'


def _rmsnorm(x, g):
    xf = x.astype(jnp.float32)
    y = xf * lax.rsqrt(jnp.mean(xf * xf, axis=-1, keepdims=True) + NORM_EPS)
    return (y * g.astype(jnp.float32)).astype(x.dtype)


def _swiglu(x, w_gate, w_up, w_down):
    return (jax.nn.silu(x @ w_gate) * (x @ w_up)) @ w_down


def _neighbourhood_attention(q, k, v, rpb):
    B, T, H, Dh = q.shape
    rows = T // GRID_W
    kh = min(NA_ROWS, rows)
    r = jnp.arange(rows)
    row0 = jnp.clip(r - kh // 2, 0, rows - kh)
    key_rows = row0[:, None] + jnp.arange(kh)[None, :]
    c = jnp.arange(GRID_W)
    col0 = jnp.clip(c - NA_COLS // 2, 0, GRID_W - NA_COLS)
    col_ok = (c[None, :] >= col0[:, None]) & (c[None, :] < col0[:, None] + NA_COLS)
    col_rel = jnp.clip(c[None, :] - c[:, None], -(NA_COLS - 1), NA_COLS - 1)
    row_rel = key_rows - r[:, None]
    bias = rpb[:, (row_rel + NA_ROWS - 1)[:, None, :, None],
               (col_rel + NA_COLS - 1)[None, :, None, :]]
    qg = q.reshape(B, rows, GRID_W, H, Dh)
    kg = k.reshape(B, rows, GRID_W, H, Dh)[:, key_rows]
    vg = v.reshape(B, rows, GRID_W, H, Dh)[:, key_rows]
    s = jnp.einsum('brqhd,brkwhd->bhrqkw', qg, kg).astype(jnp.float32) * ATTN_SCALE
    s = s + bias[None].astype(jnp.float32)
    s = jnp.where(col_ok[:, None, :], s, -jnp.inf)
    p = jax.nn.softmax(s.reshape(B, H, rows, GRID_W, kh * GRID_W), axis=-1)
    p = p.reshape(s.shape).astype(v.dtype)
    o = jnp.einsum('bhrqkw,brkwhd->brqhd', p, vg)
    return o.reshape(B, T, H * Dh)


def _banded_attention(q, k, v, half, dilation, slopes):
    N, H, L, Dh = q.shape
    nb = -(-L // LOCAL_QBLOCK)
    lp = nb * LOCAL_QBLOCK
    pad = lp - L
    qp = jnp.pad(q, ((0, 0), (0, 0), (0, pad), (0, 0))).reshape(N, H, nb, LOCAL_QBLOCK, Dh)
    kp = jnp.pad(k, ((0, 0), (0, 0), (half, half + pad), (0, 0)))
    vp = jnp.pad(v, ((0, 0), (0, 0), (half, half + pad), (0, 0)))
    kspan = LOCAL_QBLOCK + 2 * half
    kidx = jnp.arange(nb)[:, None] * LOCAL_QBLOCK + jnp.arange(kspan)[None, :]
    kblk = kp[:, :, kidx]
    vblk = vp[:, :, kidx]
    s = jnp.einsum('nhbqd,nhbkd->nhbqk', qp, kblk).astype(jnp.float32) * ATTN_SCALE
    qpos = jnp.arange(nb)[:, None] * LOCAL_QBLOCK + jnp.arange(LOCAL_QBLOCK)[None, :]
    kpos = kidx - half
    rel = kpos[:, None, :] - qpos[:, :, None]
    valid = ((jnp.abs(rel) <= half) & (kpos[:, None, :] >= 0) & (kpos[:, None, :] < L)) | (qpos[:, :, None] >= L)
    dist = (dilation * jnp.abs(rel)).astype(jnp.float32)
    s = s - slopes[:, None, None, None] * dist[None]
    s = jnp.where(valid, s, -jnp.inf)
    lse = jax.nn.logsumexp(s, axis=-1)
    p = jnp.exp(s - lse[..., None]).astype(v.dtype)
    o = jnp.einsum('nhbqk,nhbkd->nhbqd', p, vblk).reshape(N, H, lp, Dh)[:, :, :L]
    return o, lse.reshape(N, H, lp)[:, :, :L]


def _dilated_attention(q, k, v, dilation, half, slopes):
    B, T, H, Dh = q.shape
    L = T // dilation

    def to_res(a):
        return a.reshape(B, L, dilation, H, Dh).transpose(0, 2, 3, 1, 4).reshape(B * dilation, H, L, Dh)

    o, lse = _banded_attention(to_res(q), to_res(k), to_res(v), half, dilation, slopes)
    o = o.reshape(B, dilation, H, L, Dh).transpose(0, 3, 1, 2, 4).reshape(B, T, H, Dh)
    lse = lse.reshape(B, dilation, H, L).transpose(0, 3, 1, 2).reshape(B, T, H)
    return o, lse


def _dilated_mixture(q, k, v, slopes):
    B, T, _ = q.shape
    shp = (B, T, NB_GROUPS, NB_HEADS_PER_GROUP, HEAD_DIM)
    q, k, v = q.reshape(shp), k.reshape(shp), v.reshape(shp)
    outs, lses = [], []
    for g, (window, dilation) in enumerate(DIL_PAIRS):
        half = window // (2 * dilation)
        sl = slopes[g * NB_HEADS_PER_GROUP:(g + 1) * NB_HEADS_PER_GROUP]
        o, lse = _dilated_attention(q[:, :, g], k[:, :, g], v[:, :, g], dilation, half, sl)
        outs.append(o)
        lses.append(lse)
    alpha = jax.nn.softmax(jnp.stack(lses, axis=0), axis=0)
    o = jnp.stack(outs, axis=0) * alpha[..., None].astype(outs[0].dtype)
    return o.transpose(1, 2, 0, 3, 4).reshape(B, T, NB_WIDTH)


def setup_inputs(seed: int = 0) -> dict:
    key = jax.random.key(seed)
    ks = jax.random.split(key, 20)

    def nrm(k, shape, scale):
        return jax.random.normal(k, shape, jnp.float32) * scale

    def gain(k):
        return 1.0 + 0.05 * jax.random.normal(k, (DEPTH, D_MODEL), jnp.float32)

    return {
        'x': nrm(ks[0], (BATCH, SEQ, D_MODEL), 1.0),
        'ffn1_pre_g': gain(ks[1]),
        'ffn1_w_gate': nrm(ks[2], (DEPTH, D_MODEL, D_FF), D_MODEL ** -0.5),
        'ffn1_w_up': nrm(ks[3], (DEPTH, D_MODEL, D_FF), D_MODEL ** -0.5),
        'ffn1_w_down': nrm(ks[4], (DEPTH, D_FF, D_MODEL), D_FF ** -0.5),
        'ffn1_post_g': gain(ks[5]),
        'mix_pre_g': gain(ks[6]),
        'w_in': nrm(ks[7], (DEPTH, D_MODEL, N_IN), D_MODEL ** -0.5),
        'na_rpb': nrm(ks[8], (DEPTH, NA_HEADS, 2 * NA_ROWS - 1, 2 * NA_COLS - 1), 0.1),
        'w_branch_a': nrm(ks[9], (DEPTH, NA_WIDTH, D_MODEL), NA_WIDTH ** -0.5),
        'w_branch_b': nrm(ks[10], (DEPTH, NB_WIDTH, D_MODEL), NB_WIDTH ** -0.5),
        'w_out': nrm(ks[11], (DEPTH, D_MODEL, D_MODEL), D_MODEL ** -0.5),
        'mix_post_g': gain(ks[12]),
        'ffn2_pre_g': gain(ks[13]),
        'ffn2_w_gate': nrm(ks[14], (DEPTH, D_MODEL, D_FF), D_MODEL ** -0.5),
        'ffn2_w_up': nrm(ks[15], (DEPTH, D_MODEL, D_FF), D_MODEL ** -0.5),
        'ffn2_w_down': nrm(ks[16], (DEPTH, D_FF, D_MODEL), D_FF ** -0.5),
        'ffn2_post_g': gain(ks[17]),
    }


def reference(x, ffn1_pre_g, ffn1_w_gate, ffn1_w_up, ffn1_w_down, ffn1_post_g,
              mix_pre_g, w_in, na_rpb, w_branch_a, w_branch_b, w_out, mix_post_g,
              ffn2_pre_g, ffn2_w_gate, ffn2_w_up, ffn2_w_down, ffn2_post_g):
    B, T, _ = x.shape
    slopes = jnp.exp2(-ALIBI_MAX_EXP * (jnp.arange(NB_HEADS, dtype=jnp.float32) + 1.0) / NB_HEADS)
    widths = [NA_WIDTH] * 3 + [NB_WIDTH] * 3 + [D_MODEL]
    splits = [int(s) for s in np.cumsum(widths)]
    for l in range(DEPTH):
        f = _swiglu(_rmsnorm(x, ffn1_pre_g[l]), ffn1_w_gate[l], ffn1_w_up[l], ffn1_w_down[l])
        x = x + 0.5 * _rmsnorm(f, ffn1_post_g[l])
        h = _rmsnorm(x, mix_pre_g[l])
        proj = h @ w_in[l]
        qa, ka, va, qb, kb, vb, gate_a, gate_b = jnp.split(proj, splits, axis=-1)
        ha = (B, T, NA_HEADS, HEAD_DIM)
        ya = _neighbourhood_attention(qa.reshape(ha), ka.reshape(ha), va.reshape(ha), na_rpb[l]) @ w_branch_a[l]
        yb = _dilated_mixture(qb, kb, vb, slopes) @ w_branch_b[l]
        merged = jax.nn.sigmoid(gate_a) * ya + jax.nn.sigmoid(gate_b) * yb
        x = x + _rmsnorm(merged @ w_out[l], mix_post_g[l])
        f = _swiglu(_rmsnorm(x, ffn2_pre_g[l]), ffn2_w_gate[l], ffn2_w_up[l], ffn2_w_down[l])
        x = x + 0.5 * _rmsnorm(f, ffn2_post_g[l])
    return x
```

```python
import functools
import math

import jax
import jax.numpy as jnp
from jax import lax
from jax.experimental import pallas as pl
from jax.experimental.pallas import tpu as pltpu

D_MODEL = 1024
HEAD_DIM = 64
NA_HEADS = 8
NA_WIDTH = NA_HEADS * HEAD_DIM
NA_ROWS = 8
NA_COLS = 16
GRID_W = 64
DIL_PAIRS = ((128, 1), (512, 4), (2048, 16))
NB_GROUPS = len(DIL_PAIRS)
NB_HEADS_PER_GROUP = 4
NB_HEADS = NB_GROUPS * NB_HEADS_PER_GROUP
NB_WIDTH = NB_HEADS * HEAD_DIM
GROUP_WIDTH = NB_HEADS_PER_GROUP * HEAD_DIM
ALIBI_MAX_EXP = 8.0
LOCAL_QBLOCK = 64
D_FF = 2816
NORM_EPS = 1e-6
ATTN_SCALE = HEAD_DIM ** -0.5

V7X_LANES = 128
V7X_MXU_DIM = 256
V7X_VMEM_BYTES = 64 * 1024 * 1024

BF16 = jnp.bfloat16
F32 = jnp.float32


def _vmem_limit(block_bytes, scratch_bytes, temp_bytes):
    need = block_bytes + scratch_bytes + temp_bytes
    assert need <= V7X_VMEM_BYTES - (6 << 20), need
    return int(need)


def _nbytes(shape, dtype):
    return math.prod(shape) * jnp.dtype(dtype).itemsize


def _resident(shape):
    nd = len(shape)
    return pl.BlockSpec(shape, lambda *_: (0,) * nd, pipeline_mode=pl.Buffered(1))


def _rmsnorm_f32(x, g):
    return x * lax.rsqrt(jnp.mean(x * x, axis=-1, keepdims=True) + NORM_EPS) * g


def _dot(a, b):
    return jnp.dot(a, b, preferred_element_type=F32)


def _dot_nt(a, b):
    return lax.dot_general(a, b, (((1,), (1,)), ((), ())), preferred_element_type=F32)


FFN_TM = 512
FFN_CK = 256


def _ffn_kernel(x_ref, pre_g_ref, wg_ref, wu_ref, wd_ref, post_g_ref, o_ref, h_ref, a_ref):
    x = x_ref[...]
    h_ref[...] = _rmsnorm_f32(x, pre_g_ref[...]).astype(BF16)
    for c in range(D_FF // FFN_CK):
        cols = slice(c * FFN_CK, (c + 1) * FFN_CK)
        h = h_ref[...]
        g = _dot(h, wg_ref[:, cols])
        u = _dot(h, wu_ref[:, cols])
        a_ref[:, cols] = (g * jax.nn.sigmoid(g) * u).astype(BF16)
    f = _dot(a_ref[...], wd_ref[...])
    o_ref[...] = x_ref[...] + 0.5 * _rmsnorm_f32(f, post_g_ref[...])


def _ffn(x, pre_g, wg, wu, wd, post_g):
    n, d = x.shape
    tm = FFN_TM
    row = lambda i: (i, 0)
    blocks = 2 * 2 * _nbytes((tm, d), F32) + 3 * _nbytes((d, D_FF), BF16) + 2 * _nbytes((1, d), F32)
    scratch = _nbytes((tm, d), BF16) + _nbytes((tm, D_FF), BF16)
    temps = 4 * _nbytes((tm, d), F32)
    return pl.pallas_call(
        _ffn_kernel,
        grid=(n // tm,),
        in_specs=[
            pl.BlockSpec((tm, d), row),
            _resident((1, d)),
            _resident((d, D_FF)),
            _resident((d, D_FF)),
            _resident((D_FF, d)),
            _resident((1, d)),
        ],
        out_specs=pl.BlockSpec((tm, d), row),
        out_shape=jax.ShapeDtypeStruct((n, d), F32),
        scratch_shapes=[pltpu.VMEM((tm, d), BF16), pltpu.VMEM((tm, D_FF), BF16)],
        compiler_params=pltpu.CompilerParams(
            dimension_semantics=("arbitrary",),
            vmem_limit_bytes=_vmem_limit(blocks, scratch, temps),
        ),
        name="ffn",
    )(x, pre_g, wg, wu, wd, post_g)


QKV_TM = 512
QKV_WIDTHS = (NA_WIDTH,) * 3 + (NB_WIDTH,) * 3
QKV_TOTAL = sum(QKV_WIDTHS)


def _qkv_kernel(x_ref, g_ref, w_ref, qa_ref, ka_ref, va_ref, qb_ref, kb_ref, vb_ref):
    h = _rmsnorm_f32(x_ref[...], g_ref[...]).astype(BF16)
    outs = (qa_ref, ka_ref, va_ref, qb_ref, kb_ref, vb_ref)
    off = 0
    for idx, (o_ref, width) in enumerate(zip(outs, QKV_WIDTHS)):
        y = _dot(h, w_ref[:, off:off + width])
        if idx in (0, 3):
            y = y * ATTN_SCALE
        o_ref[...] = y.astype(BF16)
        off += width


def _qkv(x, g, w_qkv):
    n, d = x.shape
    tm = QKV_TM
    row = lambda i: (i, 0)
    blocks = 2 * _nbytes((tm, d), F32) + _nbytes((d, QKV_TOTAL), BF16) + 2 * _nbytes((tm, QKV_TOTAL), BF16)
    temps = 2 * _nbytes((tm, d), F32) + 2 * _nbytes((tm, NB_WIDTH), F32)
    return pl.pallas_call(
        _qkv_kernel,
        grid=(n // tm,),
        in_specs=[pl.BlockSpec((tm, d), row), _resident((1, d)), _resident((d, QKV_TOTAL))],
        out_specs=[pl.BlockSpec((tm, w), row) for w in QKV_WIDTHS],
        out_shape=[jax.ShapeDtypeStruct((n, w), BF16) for w in QKV_WIDTHS],
        compiler_params=pltpu.CompilerParams(
            dimension_semantics=("arbitrary",),
            vmem_limit_bytes=_vmem_limit(blocks, 0, temps),
        ),
        name="qkv",
    )(x, g, w_qkv)


NA_HEADS_PER_STEP = V7X_LANES // HEAD_DIM
NA_SPAN = NA_ROWS * GRID_W
NA_VARIANTS = NA_ROWS


def _nbr_bias_table(rpb):
    qc = jnp.arange(GRID_W)[:, None]
    kc = jnp.arange(GRID_W)[None, :]
    col0 = jnp.clip(qc - NA_COLS // 2, 0, GRID_W - NA_COLS)
    col_ok = (kc >= col0) & (kc < col0 + NA_COLS)
    col_rel = jnp.clip(kc - qc, -(NA_COLS - 1), NA_COLS - 1) + NA_COLS - 1
    row_rel = jnp.arange(NA_VARIANTS)[:, None] + jnp.arange(NA_ROWS)[None, :]
    tbl = rpb[:, row_rel[:, :, None, None], col_rel[None, None]]
    tbl = jnp.where(col_ok[None, None, None], tbl.astype(F32), -jnp.inf)
    return tbl.transpose(0, 1, 3, 2, 4).reshape(rpb.shape[0], NA_VARIANTS, GRID_W, NA_SPAN)


def _nbr_kernel(q_ref, k_ref, v_ref, bias_ref, o_ref, *, rows):
    hp = NA_HEADS_PER_STEP
    lane_head = lax.broadcasted_iota(jnp.int32, (GRID_W, V7X_LANES), 1) // HEAD_DIM

    def body(r, carry):
        row0 = jnp.clip(r - NA_ROWS // 2, 0, rows - NA_ROWS)
        variant = row0 - r + (NA_ROWS - 1)
        q = q_ref[pl.ds(pl.multiple_of(r * GRID_W, GRID_W), GRID_W), :]
        kstart = pl.multiple_of(row0 * GRID_W, GRID_W)
        kw = k_ref[pl.ds(kstart, NA_SPAN), :]
        vw = v_ref[pl.ds(kstart, NA_SPAN), :]
        qs = jnp.concatenate([jnp.where(lane_head == h, q, jnp.zeros_like(q)) for h in range(hp)], axis=0)
        s = _dot_nt(qs, kw)
        s = s + bias_ref[:, variant].reshape(hp * GRID_W, NA_SPAN)
        m = jnp.max(s, axis=-1, keepdims=True)
        p = jnp.exp(s - m)
        l = jnp.sum(p, axis=-1, keepdims=True)
        o = _dot(p.astype(BF16), vw) * (1.0 / l)
        out = o[:GRID_W]
        for h in range(1, hp):
            out = jnp.where(lane_head == h, o[h * GRID_W:(h + 1) * GRID_W], out)
        o_ref[pl.ds(pl.multiple_of(r * GRID_W, GRID_W), GRID_W), :] = out.astype(BF16)
        return carry

    lax.fori_loop(0, rows, body, 0)


def _nbr_attention(q, k, v, bias_tbl, batch, seq):
    n, width = q.shape
    rows = seq // GRID_W
    n_pairs = width // V7X_LANES
    tok = lambda b, p: (b, p)
    blocks = 2 * 4 * _nbytes((seq, V7X_LANES), BF16) + 2 * _nbytes((NA_HEADS_PER_STEP, NA_VARIANTS, GRID_W, NA_SPAN), F32)
    temps = 8 * _nbytes((NA_HEADS_PER_STEP * GRID_W, NA_SPAN), F32)
    return pl.pallas_call(
        functools.partial(_nbr_kernel, rows=rows),
        grid=(batch, n_pairs),
        in_specs=[
            pl.BlockSpec((seq, V7X_LANES), tok),
            pl.BlockSpec((seq, V7X_LANES), tok),
            pl.BlockSpec((seq, V7X_LANES), tok),
            pl.BlockSpec((NA_HEADS_PER_STEP, NA_VARIANTS, GRID_W, NA_SPAN), lambda b, p: (p, 0, 0, 0)),
        ],
        out_specs=pl.BlockSpec((seq, V7X_LANES), tok),
        out_shape=jax.ShapeDtypeStruct((n, width), BF16),
        compiler_params=pltpu.CompilerParams(
            dimension_semantics=("arbitrary", "arbitrary"),
            vmem_limit_bytes=_vmem_limit(blocks, 0, temps),
        ),
        name="nbr_attn",
    )(q, k, v, bias_tbl)


DIL_HALF = 64
DIL_TQ = 128
DIL_TK = DIL_TQ + 2 * DIL_HALF
DIL_VARIANTS = 3


def _dil_bias_table(group, dilation):
    heads = jnp.arange(NB_HEADS_PER_GROUP, dtype=F32) + group * NB_HEADS_PER_GROUP
    slopes = jnp.exp2(-ALIBI_MAX_EXP * (heads + 1.0) / NB_HEADS)
    offs = jnp.array([0, -DIL_HALF, -2 * DIL_HALF], jnp.int32)
    rel = offs[:, None, None] + jnp.arange(DIL_TK)[None, None, :] - jnp.arange(DIL_TQ)[None, :, None]
    dist = (dilation * jnp.abs(rel)).astype(F32)
    bias = -(slopes[None, :, None, None] * dist[:, None])
    bias = jnp.where((jnp.abs(rel) <= DIL_HALF)[:, None], bias, -jnp.inf)
    return bias.reshape(DIL_VARIANTS, NB_HEADS_PER_GROUP * DIL_TQ, DIL_TK)


def _dil_kernel(q_ref, k_ref, v_ref, bias_ref, o_ref, lse_ref, *, seq_len, stretch):
    nh = NB_HEADS_PER_GROUP
    lane_head = lax.broadcasted_iota(jnp.int32, (DIL_TQ, GROUP_WIDTH), 1) // HEAD_DIM
    base = pl.program_id(2) * stretch

    def body(j, carry):
        qloc = pl.multiple_of(j * DIL_TQ, DIL_TQ)
        qs = base + qloc
        ws = jnp.clip(qs - DIL_HALF, 0, seq_len - DIL_TK)
        variant = (qs - ws) // DIL_HALF
        ws = pl.multiple_of(ws, DIL_HALF)
        q = q_ref[pl.ds(qloc, DIL_TQ), :]
        kw = k_ref[pl.ds(ws, DIL_TK), :]
        vw = v_ref[pl.ds(ws, DIL_TK), :]
        qst = jnp.concatenate([jnp.where(lane_head == h, q, jnp.zeros_like(q)) for h in range(nh)], axis=0)
        s = _dot_nt(qst, kw) + bias_ref[variant]
        m = jnp.max(s, axis=-1, keepdims=True)
        p = jnp.exp(s - m)
        l = jnp.sum(p, axis=-1, keepdims=True)
        o = _dot(p.astype(BF16), vw) * (1.0 / l)
        lse = m + jnp.log(l)
        out = o[:DIL_TQ]
        lse_out = jnp.broadcast_to(lse[:DIL_TQ], (DIL_TQ, GROUP_WIDTH))
        for h in range(1, nh):
            rows_h = slice(h * DIL_TQ, (h + 1) * DIL_TQ)
            out = jnp.where(lane_head == h, o[rows_h], out)
            lse_out = jnp.where(lane_head == h, lse[rows_h], lse_out)
        o_ref[pl.ds(qloc, DIL_TQ), :] = out.astype(BF16)
        lse_ref[pl.ds(qloc, DIL_TQ), :] = lse_out
        return carry

    lax.fori_loop(0, stretch // DIL_TQ, body, 0)


DIL_MAX_STRETCH = 2048


def _dil_attention(q, k, v, group, dilation, batch, seq):
    n = q.shape[0]
    seq_len = seq // dilation
    stretch = min(seq_len, DIL_MAX_STRETCH)
    n_st = seq_len // stretch
    groups_per_row = NB_WIDTH // GROUP_WIDTH
    view = lambda a: a.reshape(batch * seq_len, dilation * NB_WIDTH)
    qmap = lambda b, r, s: (b * n_st + s, r * groups_per_row + group)
    kvmap = lambda b, r, s: (b, r * groups_per_row + group)
    omap = lambda b, r, s: (b * n_st + s, r)
    bias = _dil_bias_table(group, dilation)
    blocks = 2 * (2 * _nbytes((stretch, GROUP_WIDTH), BF16) + 2 * _nbytes((seq_len, GROUP_WIDTH), BF16)
                  + _nbytes((stretch, GROUP_WIDTH), F32)) + _nbytes(bias.shape, F32)
    temps = 8 * _nbytes((NB_HEADS_PER_GROUP * DIL_TQ, DIL_TK), F32)
    o, lse = pl.pallas_call(
        functools.partial(_dil_kernel, seq_len=seq_len, stretch=stretch),
        grid=(batch, dilation, n_st),
        in_specs=[
            pl.BlockSpec((stretch, GROUP_WIDTH), qmap),
            pl.BlockSpec((seq_len, GROUP_WIDTH), kvmap),
            pl.BlockSpec((seq_len, GROUP_WIDTH), kvmap),
            _resident(bias.shape),
        ],
        out_specs=[pl.BlockSpec((stretch, GROUP_WIDTH), omap), pl.BlockSpec((stretch, GROUP_WIDTH), omap)],
        out_shape=[
            jax.ShapeDtypeStruct((batch * seq_len, dilation * GROUP_WIDTH), BF16),
            jax.ShapeDtypeStruct((batch * seq_len, dilation * GROUP_WIDTH), F32),
        ],
        compiler_params=pltpu.CompilerParams(
            dimension_semantics=("arbitrary", "arbitrary", "arbitrary"),
            vmem_limit_bytes=_vmem_limit(blocks, 0, temps),
        ),
        name=f"dil_attn_g{group}",
    )(view(q), view(k), view(v), bias)
    return o.reshape(n, GROUP_WIDTH), lse.reshape(n, GROUP_WIDTH)


MERGE_TM = 512


def _merge_kernel(x_ref, pre_g_ref, wga_ref, wgb_ref, oa_ref, wa_ref,
                  o0_ref, o1_ref, o2_ref, l0_ref, l1_ref, l2_ref, wb_ref, wout_ref, post_g_ref, out_ref):
    x = x_ref[...]
    h = _rmsnorm_f32(x, pre_g_ref[...]).astype(BF16)
    lses = (l0_ref[...], l1_ref[...], l2_ref[...])
    mx = jnp.maximum(jnp.maximum(lses[0], lses[1]), lses[2])
    es = [jnp.exp(l - mx) for l in lses]
    inv = 1.0 / (es[0] + es[1] + es[2])
    yb = None
    for g, (o_ref, e) in enumerate(zip((o0_ref, o1_ref, o2_ref), es)):
        og = (o_ref[...].astype(F32) * (e * inv)).astype(BF16)
        part = _dot(og, wb_ref[g * GROUP_WIDTH:(g + 1) * GROUP_WIDTH, :])
        yb = part if yb is None else yb + part
    ya = _dot(oa_ref[...], wa_ref[...])
    merged = jax.nn.sigmoid(_dot(h, wga_ref[...])) * ya + jax.nn.sigmoid(_dot(h, wgb_ref[...])) * yb
    z = _dot(merged.astype(BF16), wout_ref[...])
    out_ref[...] = x + _rmsnorm_f32(z, post_g_ref[...])


def _merge(x, pre_g, wga, wgb, oa, wa, obs, lses, wb, wout, post_g):
    n, d = x.shape
    tm = MERGE_TM
    row = lambda i: (i, 0)
    rowspec = lambda w: pl.BlockSpec((tm, w), row)
    blocks = (2 * 2 * _nbytes((tm, d), F32) + 2 * _nbytes((tm, NA_WIDTH), BF16)
              + 2 * 3 * (_nbytes((tm, GROUP_WIDTH), BF16) + _nbytes((tm, GROUP_WIDTH), F32))
              + _nbytes((3 * d + NA_WIDTH + NB_WIDTH, d), BF16))
    temps = 8 * _nbytes((tm, d), F32)
    return pl.pallas_call(
        _merge_kernel,
        grid=(n // tm,),
        in_specs=[
            rowspec(d), _resident((1, d)), _resident((d, d)), _resident((d, d)),
            rowspec(NA_WIDTH), _resident((NA_WIDTH, d)),
            rowspec(GROUP_WIDTH), rowspec(GROUP_WIDTH), rowspec(GROUP_WIDTH),
            rowspec(GROUP_WIDTH), rowspec(GROUP_WIDTH), rowspec(GROUP_WIDTH),
            _resident((NB_WIDTH, d)), _resident((d, d)), _resident((1, d)),
        ],
        out_specs=rowspec(d),
        out_shape=jax.ShapeDtypeStruct((n, d), F32),
        compiler_params=pltpu.CompilerParams(
            dimension_semantics=("arbitrary",),
            vmem_limit_bytes=_vmem_limit(blocks, 0, temps),
        ),
        name="merge",
    )(x, pre_g, wga, wgb, oa, wa, *obs, *lses, wb, wout, post_g)


def kernel(x, ffn1_pre_g, ffn1_w_gate, ffn1_w_up, ffn1_w_down, ffn1_post_g, mix_pre_g, w_in, na_rpb, w_branch_a, w_branch_b, w_out, mix_post_g, ffn2_pre_g, ffn2_w_gate, ffn2_w_up, ffn2_w_down, ffn2_post_g):
    batch, seq, d = x.shape
    depth = ffn1_pre_g.shape[0]
    gate_off = QKV_TOTAL
    xf = x.reshape(batch * seq, d)
    for l in range(depth):
        xf = _ffn(xf, ffn1_pre_g[l][None], ffn1_w_gate[l].astype(BF16), ffn1_w_up[l].astype(BF16),
                  ffn1_w_down[l].astype(BF16), ffn1_post_g[l][None])
        w_in_l = w_in[l].astype(BF16)
        qa, ka, va, qb, kb, vb = _qkv(xf, mix_pre_g[l][None], w_in_l[:, :gate_off])
        oa = _nbr_attention(qa, ka, va, _nbr_bias_table(na_rpb[l]), batch, seq)
        obs, lses = [], []
        for g, (window, dilation) in enumerate(DIL_PAIRS):
            assert window // (2 * dilation) == DIL_HALF
            o_g, lse_g = _dil_attention(qb, kb, vb, g, dilation, batch, seq)
            obs.append(o_g)
            lses.append(lse_g)
        xf = _merge(xf, mix_pre_g[l][None], w_in_l[:, gate_off:gate_off + d], w_in_l[:, gate_off + d:],
                    oa, w_branch_a[l].astype(BF16), obs, lses, w_branch_b[l].astype(BF16),
                    w_out[l].astype(BF16), mix_post_g[l][None])
        xf = _ffn(xf, ffn2_pre_g[l][None], ffn2_w_gate[l].astype(BF16), ffn2_w_up[l].astype(BF16),
                  ffn2_w_down[l].astype(BF16), ffn2_post_g[l][None])
    return xf.reshape(batch, seq, d)
```

```python
import functools
import math

import jax
import jax.numpy as jnp
from jax import lax
from jax.experimental import pallas as pl
from jax.experimental.pallas import tpu as pltpu

D_MODEL = 1024
HEAD_DIM = 64
NA_HEADS = 8
NA_WIDTH = NA_HEADS * HEAD_DIM
NA_ROWS = 8
NA_COLS = 16
GRID_W = 64
DIL_PAIRS = ((128, 1), (512, 4), (2048, 16))
DILATIONS = tuple(d for _, d in DIL_PAIRS)
NB_GROUPS = len(DIL_PAIRS)
NB_HEADS_PER_GROUP = 4
NB_HEADS = NB_GROUPS * NB_HEADS_PER_GROUP
NB_WIDTH = NB_HEADS * HEAD_DIM
GROUP_WIDTH = NB_HEADS_PER_GROUP * HEAD_DIM
ALIBI_MAX_EXP = 8.0
D_FF = 2816
NORM_EPS = 1e-6
ATTN_SCALE = HEAD_DIM ** -0.5

V7X_LANES = 128
GROUP_LANE_BLOCKS = GROUP_WIDTH // V7X_LANES
V7X_VMEM_BYTES = 64 * 1024 * 1024
V7X_VMEM_RESERVE = 6 * 1024 * 1024

BF16 = jnp.bfloat16
F32 = jnp.float32


def _vmem_limit(block_bytes, scratch_bytes, temp_bytes):
    need = block_bytes + scratch_bytes + temp_bytes
    assert need <= V7X_VMEM_BYTES - V7X_VMEM_RESERVE, need
    return int(need)


def _nbytes(shape, dtype):
    return math.prod(shape) * jnp.dtype(dtype).itemsize


def _resident(shape):
    nd = len(shape)
    return pl.BlockSpec(shape, lambda *_: (0,) * nd, pipeline_mode=pl.Buffered(1))


def _rmsnorm_f32(x, g):
    return x * lax.rsqrt(jnp.mean(x * x, axis=-1, keepdims=True) + NORM_EPS) * g


def _dot(a, b):
    return jnp.dot(a, b, preferred_element_type=F32)


def _dot_nt(a, b):
    return lax.dot_general(a, b, (((1,), (1,)), ((), ())), preferred_element_type=F32)


FFN_TM = 512
FFN_CK = 256


def _ffn_kernel(x_ref, pre_g_ref, wg_ref, wu_ref, wd_ref, post_g_ref, o_ref, h_ref, a_ref):
    x = x_ref[...]
    h_ref[...] = _rmsnorm_f32(x, pre_g_ref[...]).astype(BF16)
    for c in range(D_FF // FFN_CK):
        cols = slice(c * FFN_CK, (c + 1) * FFN_CK)
        h = h_ref[...]
        g = _dot(h, wg_ref[:, cols])
        u = _dot(h, wu_ref[:, cols])
        a_ref[:, cols] = (g * jax.nn.sigmoid(g) * u).astype(BF16)
    f = _dot(a_ref[...], wd_ref[...])
    o_ref[...] = x_ref[...] + 0.5 * _rmsnorm_f32(f, post_g_ref[...])


def _ffn(x, pre_g, wg, wu, wd, post_g):
    n, d = x.shape
    tm = FFN_TM
    row = lambda i: (i, 0)
    blocks = 2 * 2 * _nbytes((tm, d), F32) + 3 * _nbytes((d, D_FF), BF16) + 2 * _nbytes((1, d), F32)
    scratch = _nbytes((tm, d), BF16) + _nbytes((tm, D_FF), BF16)
    temps = 4 * _nbytes((tm, d), F32)
    return pl.pallas_call(
        _ffn_kernel,
        grid=(n // tm,),
        in_specs=[
            pl.BlockSpec((tm, d), row),
            _resident((1, d)),
            _resident((d, D_FF)),
            _resident((d, D_FF)),
            _resident((D_FF, d)),
            _resident((1, d)),
        ],
        out_specs=pl.BlockSpec((tm, d), row),
        out_shape=jax.ShapeDtypeStruct((n, d), F32),
        scratch_shapes=[pltpu.VMEM((tm, d), BF16), pltpu.VMEM((tm, D_FF), BF16)],
        compiler_params=pltpu.CompilerParams(
            dimension_semantics=("arbitrary",),
            vmem_limit_bytes=_vmem_limit(blocks, scratch, temps),
        ),
        name="ffn",
    )(x, pre_g, wg, wu, wd, post_g)


QKV_TM = 512
QKV_TOTAL = 3 * NA_WIDTH + 3 * NB_WIDTH


def _qkv_kernel(x_ref, g_ref, w_ref, *refs):
    na_refs, dil_refs, y_ref = refs[:3], refs[3:-1], refs[-1]
    tm = x_ref.shape[0]
    h = _rmsnorm_f32(x_ref[...], g_ref[...]).astype(BF16)
    for idx, o_ref in enumerate(na_refs):
        y = _dot(h, w_ref[:, idx * NA_WIDTH:(idx + 1) * NA_WIDTH])
        if idx == 0:
            y = y * ATTN_SCALE
        o_ref[...] = y.astype(BF16)
    for idx, o_ref in enumerate(dil_refs):
        which, g = divmod(idx, NB_GROUPS)
        dil = DILATIONS[g]
        off = 3 * NA_WIDTH + which * NB_WIDTH + g * GROUP_WIDTH
        y = _dot(h, w_ref[:, off:off + GROUP_WIDTH])
        if which == 0:
            y = y * ATTN_SCALE
        if dil == 1:
            o_ref[0] = y.astype(BF16)
        else:
            for c in range(GROUP_LANE_BLOCKS):
                y_ref[c] = y[:, c * V7X_LANES:(c + 1) * V7X_LANES]
            for r in range(dil):
                for c in range(GROUP_LANE_BLOCKS):
                    o_ref[r, :, c * V7X_LANES:(c + 1) * V7X_LANES] = (
                        y_ref[c, pl.ds(r, tm // dil, stride=dil), :].astype(BF16))


def _qkv(x, g, w_qkv):
    batch, seq, d = x.shape
    tm = QKV_TM
    tok = lambda b, i: (b, i, 0)
    res = lambda b, i: (b, 0, i, 0)
    out_specs = [pl.BlockSpec((None, tm, NA_WIDTH), tok)] * 3
    out_shape = [jax.ShapeDtypeStruct((batch, seq, NA_WIDTH), BF16)] * 3
    for _ in range(3):
        for dil in DILATIONS:
            out_specs.append(pl.BlockSpec((None, dil, tm // dil, GROUP_WIDTH), res))
            out_shape.append(jax.ShapeDtypeStruct((batch, dil, seq // dil, GROUP_WIDTH), BF16))
    blocks = 2 * _nbytes((tm, d), F32) + _nbytes((d, QKV_TOTAL), BF16) + 2 * _nbytes((tm, QKV_TOTAL), BF16)
    scratch = _nbytes((tm, GROUP_WIDTH), F32)
    temps = 2 * _nbytes((tm, d), F32) + 2 * _nbytes((tm, NB_WIDTH), F32)
    outs = pl.pallas_call(
        _qkv_kernel,
        grid=(batch, seq // tm),
        in_specs=[pl.BlockSpec((None, tm, d), tok), _resident((1, d)), _resident((d, QKV_TOTAL))],
        out_specs=out_specs,
        out_shape=out_shape,
        scratch_shapes=[pltpu.VMEM((GROUP_LANE_BLOCKS, tm, V7X_LANES), F32)],
        compiler_params=pltpu.CompilerParams(
            dimension_semantics=("arbitrary", "arbitrary"),
            vmem_limit_bytes=_vmem_limit(blocks, scratch, temps),
        ),
        name="qkv",
    )(x, g, w_qkv)
    qa, ka, va = outs[:3]
    qb, kb, vb = (outs[3 + w * NB_GROUPS:3 + (w + 1) * NB_GROUPS] for w in range(3))
    return qa, ka, va, qb, kb, vb


NA_HEADS_PER_STEP = V7X_LANES // HEAD_DIM
NA_SPAN = NA_ROWS * GRID_W
NA_VARIANTS = NA_ROWS
NA_RPB_ROWS = 2 * NA_ROWS - 1
NA_RPB_COLS = 2 * NA_COLS - 1
NA_UNROLL = 4


def _nbr_build_bias(rep_ref, tbl_ref):
    width = NA_RPB_ROWS * GRID_W
    qc = lax.broadcasted_iota(jnp.int32, (GRID_W, width), 0)
    kc = lax.broadcasted_iota(jnp.int32, (GRID_W, width), 1) % GRID_W
    diff = kc - qc
    col0 = jnp.clip(qc - NA_COLS // 2, 0, GRID_W - NA_COLS)
    col_ok = (kc >= col0) & (kc < col0 + NA_COLS)
    for h in range(NA_HEADS_PER_STEP):
        w = jnp.full((GRID_W, width), -jnp.inf, F32)
        for k in range(NA_RPB_COLS):
            w = jnp.where(diff == k - (NA_COLS - 1), rep_ref[h, k:k + 1, :], w)
        w = jnp.where(col_ok, w, -jnp.inf)
        for variant in range(NA_VARIANTS):
            tbl_ref[h, variant] = w[:, variant * GRID_W:variant * GRID_W + NA_SPAN]


def _nbr_kernel(q_ref, k_ref, v_ref, rep_ref, o_ref, tbl_ref, *, rows):
    hp = NA_HEADS_PER_STEP
    _nbr_build_bias(rep_ref, tbl_ref)
    lane_head = lax.broadcasted_iota(jnp.int32, (GRID_W, V7X_LANES), 1) // HEAD_DIM

    def scores(r):
        row0 = jnp.clip(r - NA_ROWS // 2, 0, rows - NA_ROWS)
        variant = row0 - r + (NA_ROWS - 1)
        q = q_ref[pl.ds(pl.multiple_of(r * GRID_W, GRID_W), GRID_W), :]
        kw = k_ref[pl.ds(pl.multiple_of(row0 * GRID_W, GRID_W), NA_SPAN), :]
        qs = jnp.concatenate([jnp.where(lane_head == h, q, jnp.zeros_like(q)) for h in range(hp)], axis=0)
        return _dot_nt(qs, kw) + tbl_ref[:, variant].reshape(hp * GRID_W, NA_SPAN)

    def probs(s):
        p = jnp.exp(s - jnp.max(s, axis=-1, keepdims=True))
        return p.astype(BF16), jnp.sum(p, axis=-1, keepdims=True)

    def finish(r, p, l):
        row0 = jnp.clip(r - NA_ROWS // 2, 0, rows - NA_ROWS)
        vw = v_ref[pl.ds(pl.multiple_of(row0 * GRID_W, GRID_W), NA_SPAN), :]
        o = _dot(p, vw) * (1.0 / l)
        out = o[:GRID_W]
        for h in range(1, hp):
            out = jnp.where(lane_head == h, o[h * GRID_W:(h + 1) * GRID_W], out)
        o_ref[pl.ds(pl.multiple_of(r * GRID_W, GRID_W), GRID_W), :] = out.astype(BF16)

    def body(i, carry):
        rs = [i * NA_UNROLL + u for u in range(NA_UNROLL)]
        ss = [scores(r) for r in rs]
        pls = [probs(s) for s in ss]
        for r, (p, l) in zip(rs, pls):
            finish(r, p, l)
        return carry

    lax.fori_loop(0, rows // NA_UNROLL, body, 0)


def _nbr_attention(q, k, v, rpb):
    batch, seq, width = q.shape
    rows = seq // GRID_W
    n_pairs = width // V7X_LANES
    rep = jnp.repeat(rpb.astype(F32).transpose(0, 2, 1), GRID_W, axis=-1)
    tok = lambda b, p: (b, 0, p)
    tbl_shape = (NA_HEADS_PER_STEP, NA_VARIANTS, GRID_W, NA_SPAN)
    rep_block = (NA_HEADS_PER_STEP, NA_RPB_COLS, NA_RPB_ROWS * GRID_W)
    blocks = 2 * 4 * _nbytes((seq, V7X_LANES), BF16) + 2 * _nbytes((NA_HEADS_PER_STEP, 32, 1024), F32)
    scratch = _nbytes(tbl_shape, F32)
    temps = 8 * NA_UNROLL * _nbytes((NA_HEADS_PER_STEP * GRID_W, NA_SPAN), F32)
    return pl.pallas_call(
        functools.partial(_nbr_kernel, rows=rows),
        grid=(batch, n_pairs),
        in_specs=[
            pl.BlockSpec((None, seq, V7X_LANES), tok),
            pl.BlockSpec((None, seq, V7X_LANES), tok),
            pl.BlockSpec((None, seq, V7X_LANES), tok),
            pl.BlockSpec(rep_block, lambda b, p: (p, 0, 0)),
        ],
        out_specs=pl.BlockSpec((None, seq, V7X_LANES), tok),
        out_shape=jax.ShapeDtypeStruct((batch, seq, width), BF16),
        scratch_shapes=[pltpu.VMEM(tbl_shape, F32)],
        compiler_params=pltpu.CompilerParams(
            dimension_semantics=("arbitrary", "arbitrary"),
            vmem_limit_bytes=_vmem_limit(blocks, scratch, temps),
        ),
        name="nbr_attn",
    )(q, k, v, rep)


DIL_HALF = 64
DIL_TQ = 128
DIL_TK = DIL_TQ + 2 * DIL_HALF
DIL_VARIANTS = 3
DIL_MAX_STRETCH = 2048
DIL_UNROLL = 2


def _dil_bias_table(group, dilation):
    heads = jnp.arange(NB_HEADS_PER_GROUP, dtype=F32) + group * NB_HEADS_PER_GROUP
    slopes = jnp.exp2(-ALIBI_MAX_EXP * (heads + 1.0) / NB_HEADS)
    offs = jnp.array([0, -DIL_HALF, -2 * DIL_HALF], jnp.int32)
    rel = offs[:, None, None] + jnp.arange(DIL_TK)[None, None, :] - jnp.arange(DIL_TQ)[None, :, None]
    dist = (dilation * jnp.abs(rel)).astype(F32)
    bias = -(slopes[None, :, None, None] * dist[:, None])
    bias = jnp.where((jnp.abs(rel) <= DIL_HALF)[:, None], bias, -jnp.inf)
    return bias.reshape(DIL_VARIANTS, NB_HEADS_PER_GROUP * DIL_TQ, DIL_TK)


def _dil_kernel(q_ref, k_ref, v_ref, bias_ref, o_ref, lse_ref, *, seq_len, stretch):
    nh = NB_HEADS_PER_GROUP
    lane_head = lax.broadcasted_iota(jnp.int32, (DIL_TQ, GROUP_WIDTH), 1) // HEAD_DIM
    base = pl.program_id(2) * stretch

    def window(j):
        qloc = pl.multiple_of(j * DIL_TQ, DIL_TQ)
        qs = base + qloc
        ws = jnp.clip(qs - DIL_HALF, 0, seq_len - DIL_TK)
        variant = (qs - ws) // DIL_HALF
        return qloc, pl.multiple_of(ws, DIL_HALF), variant

    def scores(j):
        qloc, ws, variant = window(j)
        q = q_ref[pl.ds(qloc, DIL_TQ), :]
        kw = k_ref[pl.ds(ws, DIL_TK), :]
        qst = jnp.concatenate([jnp.where(lane_head == h, q, jnp.zeros_like(q)) for h in range(nh)], axis=0)
        return _dot_nt(qst, kw) + bias_ref[variant]

    def probs(s):
        m = jnp.max(s, axis=-1, keepdims=True)
        p = jnp.exp(s - m)
        return p.astype(BF16), m, jnp.sum(p, axis=-1, keepdims=True)

    def finish(j, p, m, l):
        qloc, ws, _ = window(j)
        vw = v_ref[pl.ds(ws, DIL_TK), :]
        o = _dot(p, vw) * (1.0 / l)
        lse = m + jnp.log(l)
        out = o[:DIL_TQ]
        lse_out = jnp.broadcast_to(lse[:DIL_TQ], (DIL_TQ, GROUP_WIDTH))
        for h in range(1, nh):
            rows_h = slice(h * DIL_TQ, (h + 1) * DIL_TQ)
            out = jnp.where(lane_head == h, o[rows_h], out)
            lse_out = jnp.where(lane_head == h, lse[rows_h], lse_out)
        o_ref[pl.ds(qloc, DIL_TQ), :] = out.astype(BF16)
        lse_ref[pl.ds(qloc, DIL_TQ), :] = lse_out

    def body(i, carry):
        js = [i * DIL_UNROLL + u for u in range(DIL_UNROLL)]
        ss = [scores(j) for j in js]
        pmls = [probs(s) for s in ss]
        for j, pml in zip(js, pmls):
            finish(j, *pml)
        return carry

    lax.fori_loop(0, stretch // (DIL_TQ * DIL_UNROLL), body, 0)


def _dil_attention(q, k, v, group):
    batch, dilation, seq_len, width = q.shape
    stretch = min(seq_len, DIL_MAX_STRETCH)
    n_st = seq_len // stretch
    qmap = lambda b, r, s: (b, r, s, 0)
    kvmap = lambda b, r, s: (b, r, 0, 0)
    bias = _dil_bias_table(group, dilation)
    blocks = 2 * (2 * _nbytes((stretch, width), BF16) + 2 * _nbytes((seq_len, width), BF16)
                  + _nbytes((stretch, width), F32)) + _nbytes(bias.shape, F32)
    temps = 8 * DIL_UNROLL * _nbytes((NB_HEADS_PER_GROUP * DIL_TQ, DIL_TK), F32)
    return pl.pallas_call(
        functools.partial(_dil_kernel, seq_len=seq_len, stretch=stretch),
        grid=(batch, dilation, n_st),
        in_specs=[
            pl.BlockSpec((None, None, stretch, width), qmap),
            pl.BlockSpec((None, None, seq_len, width), kvmap),
            pl.BlockSpec((None, None, seq_len, width), kvmap),
            _resident(bias.shape),
        ],
        out_specs=[pl.BlockSpec((None, None, stretch, width), qmap)] * 2,
        out_shape=[jax.ShapeDtypeStruct(q.shape, BF16), jax.ShapeDtypeStruct(q.shape, F32)],
        compiler_params=pltpu.CompilerParams(
            dimension_semantics=("arbitrary", "arbitrary", "arbitrary"),
            vmem_limit_bytes=_vmem_limit(blocks, 0, temps),
        ),
        name=f"dil_attn_g{group}",
    )(q, k, v, bias)


MERGE_TM = 512


def _to_token_order(src_ref, scr_ref, dil):
    if dil == 1:
        return src_ref[0].astype(F32)
    rows = src_ref.shape[1]
    for r in range(dil):
        for c in range(GROUP_LANE_BLOCKS):
            scr_ref[c, pl.ds(r, rows, stride=dil), :] = (
                src_ref[r, :, c * V7X_LANES:(c + 1) * V7X_LANES].astype(F32))
    return jnp.concatenate([scr_ref[c] for c in range(GROUP_LANE_BLOCKS)], axis=-1)


def _merge_kernel(x_ref, pre_g_ref, wga_ref, wgb_ref, oa_ref, wa_ref,
                  o0_ref, o1_ref, o2_ref, l0_ref, l1_ref, l2_ref, wb_ref, wout_ref, post_g_ref, out_ref,
                  *scr_refs):
    x = x_ref[...]
    h = _rmsnorm_f32(x, pre_g_ref[...]).astype(BF16)
    o_scr, l_scr = scr_refs[:NB_GROUPS], scr_refs[NB_GROUPS:]
    os_ = [_to_token_order(r, s, d) for r, s, d in zip((o0_ref, o1_ref, o2_ref), o_scr, DILATIONS)]
    lses = [_to_token_order(r, s, d) for r, s, d in zip((l0_ref, l1_ref, l2_ref), l_scr, DILATIONS)]
    mx = jnp.maximum(jnp.maximum(lses[0], lses[1]), lses[2])
    es = [jnp.exp(l - mx) for l in lses]
    inv = 1.0 / (es[0] + es[1] + es[2])
    yb = None
    for g in range(NB_GROUPS):
        og = (os_[g] * (es[g] * inv)).astype(BF16)
        part = _dot(og, wb_ref[g * GROUP_WIDTH:(g + 1) * GROUP_WIDTH, :])
        yb = part if yb is None else yb + part
    ya = _dot(oa_ref[...], wa_ref[...])
    merged = jax.nn.sigmoid(_dot(h, wga_ref[...])) * ya + jax.nn.sigmoid(_dot(h, wgb_ref[...])) * yb
    z = _dot(merged.astype(BF16), wout_ref[...])
    out_ref[...] = x + _rmsnorm_f32(z, post_g_ref[...])


def _merge(x, pre_g, wga, wgb, oa, wa, obs, lses, wb, wout, post_g):
    batch, seq, d = x.shape
    tm = MERGE_TM
    tok = lambda b, i: (b, i, 0)
    res = lambda b, i: (b, 0, i, 0)
    tokspec = lambda w: pl.BlockSpec((None, tm, w), tok)
    resspecs = [pl.BlockSpec((None, dil, tm // dil, GROUP_WIDTH), res) for dil in DILATIONS]
    blocks = (2 * 2 * _nbytes((tm, d), F32) + 2 * _nbytes((tm, NA_WIDTH), BF16)
              + 2 * 3 * (_nbytes((tm, GROUP_WIDTH), BF16) + _nbytes((tm, GROUP_WIDTH), F32))
              + _nbytes((3 * d + NA_WIDTH + NB_WIDTH, d), BF16))
    scratch = 2 * NB_GROUPS * _nbytes((tm, GROUP_WIDTH), F32)
    temps = 8 * _nbytes((tm, d), F32)
    return pl.pallas_call(
        _merge_kernel,
        grid=(batch, seq // tm),
        in_specs=[
            tokspec(d), _resident((1, d)), _resident((d, d)), _resident((d, d)),
            tokspec(NA_WIDTH), _resident((NA_WIDTH, d)),
            *resspecs, *resspecs,
            _resident((NB_WIDTH, d)), _resident((d, d)), _resident((1, d)),
        ],
        out_specs=tokspec(d),
        out_shape=jax.ShapeDtypeStruct((batch, seq, d), F32),
        scratch_shapes=[pltpu.VMEM((GROUP_LANE_BLOCKS, tm, V7X_LANES), F32)] * (2 * NB_GROUPS),
        compiler_params=pltpu.CompilerParams(
            dimension_semantics=("arbitrary", "arbitrary"),
            vmem_limit_bytes=_vmem_limit(blocks, scratch, temps),
        ),
        name="merge",
    )(x, pre_g, wga, wgb, oa, wa, *obs, *lses, wb, wout, post_g)


def kernel(x, ffn1_pre_g, ffn1_w_gate, ffn1_w_up, ffn1_w_down, ffn1_post_g, mix_pre_g, w_in, na_rpb, w_branch_a, w_branch_b, w_out, mix_post_g, ffn2_pre_g, ffn2_w_gate, ffn2_w_up, ffn2_w_down, ffn2_post_g):
    batch, seq, d = x.shape
    depth = ffn1_pre_g.shape[0]
    gate_off = QKV_TOTAL
    for window, dilation in DIL_PAIRS:
        assert window // (2 * dilation) == DIL_HALF
    for l in range(depth):
        x = _ffn(x.reshape(batch * seq, d), ffn1_pre_g[l][None], ffn1_w_gate[l].astype(BF16),
                 ffn1_w_up[l].astype(BF16), ffn1_w_down[l].astype(BF16), ffn1_post_g[l][None])
        x = x.reshape(batch, seq, d)
        w_in_l = w_in[l].astype(BF16)
        qa, ka, va, qb, kb, vb = _qkv(x, mix_pre_g[l][None], w_in_l[:, :gate_off])
        oa = _nbr_attention(qa, ka, va, na_rpb[l])
        obs, lses = [], []
        for g in range(NB_GROUPS):
            o_g, lse_g = _dil_attention(qb[g], kb[g], vb[g], g)
            obs.append(o_g)
            lses.append(lse_g)
        x = _merge(x, mix_pre_g[l][None], w_in_l[:, gate_off:gate_off + d], w_in_l[:, gate_off + d:],
                   oa, w_branch_a[l].astype(BF16), obs, lses, w_branch_b[l].astype(BF16),
                   w_out[l].astype(BF16), mix_post_g[l][None])
        x = _ffn(x.reshape(batch * seq, d), ffn2_pre_g[l][None], ffn2_w_gate[l].astype(BF16),
                 ffn2_w_up[l].astype(BF16), ffn2_w_down[l].astype(BF16), ffn2_post_g[l][None])
        x = x.reshape(batch, seq, d)
    return x
```

```python
import functools
import math

import jax
import jax.numpy as jnp
from jax import lax
from jax.experimental import pallas as pl
from jax.experimental.pallas import tpu as pltpu

D_MODEL = 1024
HEAD_DIM = 64
NA_HEADS = 8
NA_WIDTH = NA_HEADS * HEAD_DIM
NA_ROWS = 8
NA_COLS = 16
GRID_W = 64
DIL_PAIRS = ((128, 1), (512, 4), (2048, 16))
DILATIONS = tuple(d for _, d in DIL_PAIRS)
NB_GROUPS = len(DIL_PAIRS)
NB_HEADS_PER_GROUP = 4
NB_HEADS = NB_GROUPS * NB_HEADS_PER_GROUP
NB_WIDTH = NB_HEADS * HEAD_DIM
GROUP_WIDTH = NB_HEADS_PER_GROUP * HEAD_DIM
ALIBI_MAX_EXP = 8.0
D_FF = 2816
NORM_EPS = 1e-6
ATTN_SCALE = HEAD_DIM ** -0.5

V7X_LANES = 128
GROUP_LANE_BLOCKS = GROUP_WIDTH // V7X_LANES
V7X_VMEM_BYTES = 64 * 1024 * 1024
V7X_VMEM_RESERVE = 6 * 1024 * 1024

BF16 = jnp.bfloat16
F32 = jnp.float32


def _vmem_limit(block_bytes, scratch_bytes, temp_bytes):
    need = block_bytes + scratch_bytes + temp_bytes
    assert need <= V7X_VMEM_BYTES - V7X_VMEM_RESERVE, need
    return int(need)


def _nbytes(shape, dtype):
    return math.prod(shape) * jnp.dtype(dtype).itemsize


def _resident(shape):
    nd = len(shape)
    return pl.BlockSpec(shape, lambda *_: (0,) * nd, pipeline_mode=pl.Buffered(1))


def _rmsnorm_f32(x, g):
    return x * lax.rsqrt(jnp.mean(x * x, axis=-1, keepdims=True) + NORM_EPS) * g


def _dot(a, b):
    return jnp.dot(a, b, preferred_element_type=F32)


def _dot_nt(a, b):
    return lax.dot_general(a, b, (((1,), (1,)), ((), ())), preferred_element_type=F32)


def _pipelined_attention(n_groups, unroll, scores, probs, finish, s_scr, p_scr, stat_scrs):
    assert n_groups % 2 == 0 and n_groups >= 2
    tiles = lambda i: [i * unroll + u for u in range(unroll)]

    def stage_scores(i, slot):
        for u, t in enumerate(tiles(i)):
            s_scr[slot, u] = scores(t)

    def stage_probs(slot):
        for u in range(unroll):
            p, *stats = probs(s_scr[slot, u])
            p_scr[slot, u] = p
            for ref, stat in zip(stat_scrs, stats):
                ref[slot, u] = stat

    def stage_finish(i, slot):
        for u, t in enumerate(tiles(i)):
            finish(t, p_scr[slot, u], *[ref[slot, u] for ref in stat_scrs])

    def step(i, parity):
        stage_scores(i + 1, 1 - parity)
        stage_finish(i - 1, 1 - parity)
        stage_probs(parity)

    stage_scores(0, 0)
    stage_scores(1, 1)
    stage_probs(0)

    def body(i, carry):
        @pl.when(i % 2 == 1)
        def _():
            step(i, 1)

        @pl.when(i % 2 == 0)
        def _():
            step(i, 0)

        return carry

    lax.fori_loop(1, n_groups - 1, body, 0)
    stage_finish(n_groups - 2, 0)
    stage_probs(1)
    stage_finish(n_groups - 1, 1)


FFN_TM = 512
FFN_CK = 256


def _ffn_kernel(x_ref, pre_g_ref, wg_ref, wu_ref, wd_ref, post_g_ref, o_ref, h_ref, a_ref):
    x = x_ref[...]
    h_ref[...] = _rmsnorm_f32(x, pre_g_ref[...]).astype(BF16)
    for c in range(D_FF // FFN_CK):
        cols = slice(c * FFN_CK, (c + 1) * FFN_CK)
        h = h_ref[...]
        g = _dot(h, wg_ref[:, cols])
        u = _dot(h, wu_ref[:, cols])
        a_ref[:, cols] = (g * jax.nn.sigmoid(g) * u).astype(BF16)
    f = _dot(a_ref[...], wd_ref[...])
    o_ref[...] = x_ref[...] + 0.5 * _rmsnorm_f32(f, post_g_ref[...])


def _ffn(x, pre_g, wg, wu, wd, post_g):
    n, d = x.shape
    tm = FFN_TM
    row = lambda i: (i, 0)
    blocks = 2 * 2 * _nbytes((tm, d), F32) + 3 * _nbytes((d, D_FF), BF16) + 2 * _nbytes((1, d), F32)
    scratch = _nbytes((tm, d), BF16) + _nbytes((tm, D_FF), BF16)
    temps = 4 * _nbytes((tm, d), F32)
    return pl.pallas_call(
        _ffn_kernel,
        grid=(n // tm,),
        in_specs=[
            pl.BlockSpec((tm, d), row),
            _resident((1, d)),
            _resident((d, D_FF)),
            _resident((d, D_FF)),
            _resident((D_FF, d)),
            _resident((1, d)),
        ],
        out_specs=pl.BlockSpec((tm, d), row),
        out_shape=jax.ShapeDtypeStruct((n, d), F32),
        scratch_shapes=[pltpu.VMEM((tm, d), BF16), pltpu.VMEM((tm, D_FF), BF16)],
        compiler_params=pltpu.CompilerParams(
            dimension_semantics=("arbitrary",),
            vmem_limit_bytes=_vmem_limit(blocks, scratch, temps),
        ),
        name="ffn",
    )(x, pre_g, wg, wu, wd, post_g)


QKV_TM = 512
QKV_TOTAL = 3 * NA_WIDTH + 3 * NB_WIDTH


def _qkv_kernel(x_ref, g_ref, w_ref, *refs):
    na_refs, dil_refs, y_ref = refs[:3], refs[3:-1], refs[-1]
    tm = x_ref.shape[0]
    h = _rmsnorm_f32(x_ref[...], g_ref[...]).astype(BF16)
    for idx, o_ref in enumerate(na_refs):
        y = _dot(h, w_ref[:, idx * NA_WIDTH:(idx + 1) * NA_WIDTH])
        if idx == 0:
            y = y * ATTN_SCALE
        o_ref[...] = y.astype(BF16)
    for idx, o_ref in enumerate(dil_refs):
        which, g = divmod(idx, NB_GROUPS)
        dil = DILATIONS[g]
        off = 3 * NA_WIDTH + which * NB_WIDTH + g * GROUP_WIDTH
        y = _dot(h, w_ref[:, off:off + GROUP_WIDTH])
        if which == 0:
            y = y * ATTN_SCALE
        if dil == 1:
            o_ref[0] = y.astype(BF16)
        else:
            for c in range(GROUP_LANE_BLOCKS):
                y_ref[c] = y[:, c * V7X_LANES:(c + 1) * V7X_LANES]
            for r in range(dil):
                for c in range(GROUP_LANE_BLOCKS):
                    o_ref[r, :, c * V7X_LANES:(c + 1) * V7X_LANES] = (
                        y_ref[c, pl.ds(r, tm // dil, stride=dil), :].astype(BF16))


def _qkv(x, g, w_qkv):
    batch, seq, d = x.shape
    tm = QKV_TM
    tok = lambda b, i: (b, i, 0)
    res = lambda b, i: (b, 0, i, 0)
    out_specs = [pl.BlockSpec((None, tm, NA_WIDTH), tok)] * 3
    out_shape = [jax.ShapeDtypeStruct((batch, seq, NA_WIDTH), BF16)] * 3
    for _ in range(3):
        for dil in DILATIONS:
            out_specs.append(pl.BlockSpec((None, dil, tm // dil, GROUP_WIDTH), res))
            out_shape.append(jax.ShapeDtypeStruct((batch, dil, seq // dil, GROUP_WIDTH), BF16))
    blocks = 2 * _nbytes((tm, d), F32) + _nbytes((d, QKV_TOTAL), BF16) + 2 * _nbytes((tm, QKV_TOTAL), BF16)
    scratch = _nbytes((tm, GROUP_WIDTH), F32)
    temps = 2 * _nbytes((tm, d), F32) + 2 * _nbytes((tm, NB_WIDTH), F32)
    outs = pl.pallas_call(
        _qkv_kernel,
        grid=(batch, seq // tm),
        in_specs=[pl.BlockSpec((None, tm, d), tok), _resident((1, d)), _resident((d, QKV_TOTAL))],
        out_specs=out_specs,
        out_shape=out_shape,
        scratch_shapes=[pltpu.VMEM((GROUP_LANE_BLOCKS, tm, V7X_LANES), F32)],
        compiler_params=pltpu.CompilerParams(
            dimension_semantics=("arbitrary", "arbitrary"),
            vmem_limit_bytes=_vmem_limit(blocks, scratch, temps),
        ),
        name="qkv",
    )(x, g, w_qkv)
    qa, ka, va = outs[:3]
    qb, kb, vb = (outs[3 + w * NB_GROUPS:3 + (w + 1) * NB_GROUPS] for w in range(3))
    return qa, ka, va, qb, kb, vb


NA_HEADS_PER_STEP = V7X_LANES // HEAD_DIM
NA_SPAN = NA_ROWS * GRID_W
NA_VARIANTS = NA_ROWS
NA_RPB_ROWS = 2 * NA_ROWS - 1
NA_RPB_COLS = 2 * NA_COLS - 1
NA_UNROLL = 4


def _nbr_build_bias(rep_ref, tbl_ref):
    width = NA_RPB_ROWS * GRID_W
    qc = lax.broadcasted_iota(jnp.int32, (GRID_W, width), 0)
    kc = lax.broadcasted_iota(jnp.int32, (GRID_W, width), 1) % GRID_W
    diff = kc - qc
    col0 = jnp.clip(qc - NA_COLS // 2, 0, GRID_W - NA_COLS)
    col_ok = (kc >= col0) & (kc < col0 + NA_COLS)
    for h in range(NA_HEADS_PER_STEP):
        w = jnp.full((GRID_W, width), -jnp.inf, F32)
        for k in range(NA_RPB_COLS):
            w = jnp.where(diff == k - (NA_COLS - 1), rep_ref[h, k:k + 1, :], w)
        w = jnp.where(col_ok, w, -jnp.inf)
        for variant in range(NA_VARIANTS):
            tbl_ref[h, variant] = w[:, variant * GRID_W:variant * GRID_W + NA_SPAN]


def _nbr_kernel(q_ref, k_ref, v_ref, rep_ref, o_ref, tbl_ref, s_scr, p_scr, *, rows):
    hp = NA_HEADS_PER_STEP
    ones = jnp.ones((NA_SPAN, V7X_LANES), BF16)

    @pl.when(pl.program_id(1) == 0)
    def _():
        _nbr_build_bias(rep_ref, tbl_ref)

    lane_head = lax.broadcasted_iota(jnp.int32, (GRID_W, V7X_LANES), 1) // HEAD_DIM

    def scores(r):
        row0 = jnp.clip(r - NA_ROWS // 2, 0, rows - NA_ROWS)
        variant = row0 - r + (NA_ROWS - 1)
        q = q_ref[pl.ds(pl.multiple_of(r * GRID_W, GRID_W), GRID_W), :]
        kw = k_ref[pl.ds(pl.multiple_of(row0 * GRID_W, GRID_W), NA_SPAN), :]
        qs = jnp.concatenate([jnp.where(lane_head == h, q, jnp.zeros_like(q)) for h in range(hp)], axis=0)
        return _dot_nt(qs, kw) + tbl_ref[:, variant].reshape(hp * GRID_W, NA_SPAN)

    def probs(s):
        return (jnp.exp(s - jnp.max(s, axis=-1, keepdims=True)).astype(BF16),)

    def finish(r, p):
        row0 = jnp.clip(r - NA_ROWS // 2, 0, rows - NA_ROWS)
        vw = v_ref[pl.ds(pl.multiple_of(row0 * GRID_W, GRID_W), NA_SPAN), :]
        o = _dot(p, jnp.concatenate([vw, ones], axis=1))
        o = o[:, :V7X_LANES] * (1.0 / o[:, V7X_LANES:])
        out = o[:GRID_W]
        for h in range(1, hp):
            out = jnp.where(lane_head == h, o[h * GRID_W:(h + 1) * GRID_W], out)
        o_ref[pl.ds(pl.multiple_of(r * GRID_W, GRID_W), GRID_W), :] = out.astype(BF16)

    _pipelined_attention(rows // NA_UNROLL, NA_UNROLL, scores, probs, finish, s_scr, p_scr, ())


def _nbr_attention(q, k, v, rpb):
    batch, seq, width = q.shape
    rows = seq // GRID_W
    n_pairs = width // V7X_LANES
    rep = jnp.repeat(rpb.astype(F32).transpose(0, 2, 1), GRID_W, axis=-1)
    tok = lambda p, b: (b, 0, p)
    tbl_shape = (NA_HEADS_PER_STEP, NA_VARIANTS, GRID_W, NA_SPAN)
    rep_block = (NA_HEADS_PER_STEP, NA_RPB_COLS, NA_RPB_ROWS * GRID_W)
    blocks = 2 * 4 * _nbytes((seq, V7X_LANES), BF16) + 2 * _nbytes((NA_HEADS_PER_STEP, 32, 1024), F32)
    tile = (2, NA_UNROLL, NA_HEADS_PER_STEP * GRID_W, NA_SPAN)
    scratch = _nbytes(tbl_shape, F32) + _nbytes(tile, F32) + _nbytes(tile, BF16)
    temps = 8 * NA_UNROLL * _nbytes(tile[2:], F32)
    return pl.pallas_call(
        functools.partial(_nbr_kernel, rows=rows),
        grid=(n_pairs, batch),
        in_specs=[
            pl.BlockSpec((None, seq, V7X_LANES), tok),
            pl.BlockSpec((None, seq, V7X_LANES), tok),
            pl.BlockSpec((None, seq, V7X_LANES), tok),
            pl.BlockSpec(rep_block, lambda p, b: (p, 0, 0)),
        ],
        out_specs=pl.BlockSpec((None, seq, V7X_LANES), tok),
        out_shape=jax.ShapeDtypeStruct((batch, seq, width), BF16),
        scratch_shapes=[pltpu.VMEM(tbl_shape, F32), pltpu.VMEM(tile, F32), pltpu.VMEM(tile, BF16)],
        compiler_params=pltpu.CompilerParams(
            dimension_semantics=("arbitrary", "arbitrary"),
            vmem_limit_bytes=_vmem_limit(blocks, scratch, temps),
        ),
        name="nbr_attn",
    )(q, k, v, rep)


DIL_HALF = 64
DIL_TQ = 128
DIL_TK = DIL_TQ + 2 * DIL_HALF
DIL_VARIANTS = 3
DIL_STEP_ROWS = 4096
DIL_UNROLL = 4


def _dil_bias_table(group, dilation):
    heads = jnp.arange(NB_HEADS_PER_GROUP, dtype=F32) + group * NB_HEADS_PER_GROUP
    slopes = jnp.exp2(-ALIBI_MAX_EXP * (heads + 1.0) / NB_HEADS)
    offs = jnp.array([0, -DIL_HALF, -2 * DIL_HALF], jnp.int32)
    rel = offs[:, None, None] + jnp.arange(DIL_TK)[None, None, :] - jnp.arange(DIL_TQ)[None, :, None]
    dist = (dilation * jnp.abs(rel)).astype(F32)
    bias = -(slopes[None, :, None, None] * dist[:, None])
    bias = jnp.where((jnp.abs(rel) <= DIL_HALF)[:, None], bias, -jnp.inf)
    return bias.reshape(DIL_VARIANTS, NB_HEADS_PER_GROUP * DIL_TQ, DIL_TK)


def _dil_kernel(q_ref, k_ref, v_ref, bias_ref, o_ref, lse_ref, s_scr, p_scr, m_scr, *, seq_len, stretch):
    nh = NB_HEADS_PER_GROUP
    hpb = V7X_LANES // HEAD_DIM
    ones = jnp.ones((DIL_TK, V7X_LANES), BF16)
    group_lane_head = lax.broadcasted_iota(jnp.int32, (DIL_TQ, GROUP_WIDTH), 1) // HEAD_DIM
    lane_head = lax.broadcasted_iota(jnp.int32, (DIL_TQ, V7X_LANES), 1) // HEAD_DIM
    base = pl.program_id(2) * stretch
    tiles_per_residue = stretch // DIL_TQ

    def window(t):
        r, j = t // tiles_per_residue, t % tiles_per_residue
        qloc = pl.multiple_of(j * DIL_TQ, DIL_TQ)
        qs = base + qloc
        ws = jnp.clip(qs - DIL_HALF, 0, seq_len - DIL_TK)
        variant = (qs - ws) // DIL_HALF
        return r, qloc, pl.multiple_of(ws, DIL_HALF), variant

    def scores(t):
        r, qloc, ws, variant = window(t)
        q = q_ref[r, pl.ds(qloc, DIL_TQ), :]
        kw = k_ref[r, pl.ds(ws, DIL_TK), :]
        qst = jnp.concatenate([jnp.where(group_lane_head == h, q, jnp.zeros_like(q)) for h in range(nh)],
                              axis=0)
        return _dot_nt(qst, kw) + bias_ref[variant]

    def probs(s):
        m = jnp.max(s, axis=-1, keepdims=True)
        return jnp.exp(s - m).astype(BF16), jnp.broadcast_to(m, (s.shape[0], V7X_LANES))

    def finish(t, p, m):
        r, qloc, ws, _ = window(t)
        vw = v_ref[r, pl.ds(ws, DIL_TK), :]
        outs, lses = [], []
        for c in range(GROUP_LANE_BLOCKS):
            rows_c = slice(c * hpb * DIL_TQ, (c + 1) * hpb * DIL_TQ)
            v_aug = jnp.concatenate([vw[:, c * V7X_LANES:(c + 1) * V7X_LANES], ones], axis=1)
            o = _dot(p[rows_c], v_aug)
            l = o[:, V7X_LANES:]
            o = o[:, :V7X_LANES] * (1.0 / l)
            lse = m[rows_c] + jnp.log(l)
            out_c, lse_c = o[:DIL_TQ], lse[:DIL_TQ]
            for h in range(1, hpb):
                rows_h = slice(h * DIL_TQ, (h + 1) * DIL_TQ)
                out_c = jnp.where(lane_head == h, o[rows_h], out_c)
                lse_c = jnp.where(lane_head == h, lse[rows_h], lse_c)
            outs.append(out_c)
            lses.append(lse_c)
        o_ref[r, pl.ds(qloc, DIL_TQ), :] = jnp.concatenate(outs, axis=1).astype(BF16)
        lse_ref[r, pl.ds(qloc, DIL_TQ), :] = jnp.concatenate(lses, axis=1)

    n_tiles = q_ref.shape[0] * tiles_per_residue
    _pipelined_attention(n_tiles // DIL_UNROLL, DIL_UNROLL, scores, probs, finish, s_scr, p_scr, (m_scr,))


def _dil_attention(q, k, v, group):
    batch, dilation, seq_len, width = q.shape
    stretch = min(seq_len, DIL_STEP_ROWS)
    res_blk = DIL_STEP_ROWS // stretch
    assert dilation % res_blk == 0 and seq_len % stretch == 0
    qmap = lambda b, r, s: (b, r, s, 0)
    kvmap = lambda b, r, s: (b, r, 0, 0)
    bias = _dil_bias_table(group, dilation)
    tile = (2, DIL_UNROLL, NB_HEADS_PER_GROUP * DIL_TQ, DIL_TK)
    stat = tile[:3] + (V7X_LANES,)
    blocks = 2 * (2 * _nbytes((DIL_STEP_ROWS, width), BF16) + 2 * _nbytes((res_blk * seq_len, width), BF16)
                  + _nbytes((DIL_STEP_ROWS, width), F32)) + _nbytes(bias.shape, F32)
    scratch = _nbytes(tile, F32) + _nbytes(tile, BF16) + _nbytes(stat, F32)
    temps = 4 * DIL_UNROLL * _nbytes(tile[2:], F32)
    return pl.pallas_call(
        functools.partial(_dil_kernel, seq_len=seq_len, stretch=stretch),
        grid=(batch, dilation // res_blk, seq_len // stretch),
        in_specs=[
            pl.BlockSpec((None, res_blk, stretch, width), qmap),
            pl.BlockSpec((None, res_blk, seq_len, width), kvmap),
            pl.BlockSpec((None, res_blk, seq_len, width), kvmap),
            _resident(bias.shape),
        ],
        out_specs=[pl.BlockSpec((None, res_blk, stretch, width), qmap)] * 2,
        out_shape=[jax.ShapeDtypeStruct(q.shape, BF16), jax.ShapeDtypeStruct(q.shape, F32)],
        scratch_shapes=[pltpu.VMEM(tile, F32), pltpu.VMEM(tile, BF16), pltpu.VMEM(stat, F32)],
        compiler_params=pltpu.CompilerParams(
            dimension_semantics=("arbitrary", "arbitrary", "arbitrary"),
            vmem_limit_bytes=_vmem_limit(blocks, scratch, temps),
        ),
        name=f"dil_attn_g{group}",
    )(q, k, v, bias)


MERGE_TM = 512


def _to_token_order(src_ref, scr_ref, dil):
    if dil == 1:
        return src_ref[0].astype(F32)
    rows = src_ref.shape[1]
    for r in range(dil):
        for c in range(GROUP_LANE_BLOCKS):
            scr_ref[c, pl.ds(r, rows, stride=dil), :] = (
                src_ref[r, :, c * V7X_LANES:(c + 1) * V7X_LANES].astype(F32))
    return jnp.concatenate([scr_ref[c] for c in range(GROUP_LANE_BLOCKS)], axis=-1)


def _merge_kernel(x_ref, pre_g_ref, wga_ref, wgb_ref, oa_ref, wa_ref,
                  o0_ref, o1_ref, o2_ref, l0_ref, l1_ref, l2_ref, wb_ref, wout_ref, post_g_ref, out_ref,
                  *scr_refs):
    x = x_ref[...]
    h = _rmsnorm_f32(x, pre_g_ref[...]).astype(BF16)
    o_scr, l_scr = scr_refs[:NB_GROUPS], scr_refs[NB_GROUPS:]
    os_ = [_to_token_order(r, s, d) for r, s, d in zip((o0_ref, o1_ref, o2_ref), o_scr, DILATIONS)]
    lses = [_to_token_order(r, s, d) for r, s, d in zip((l0_ref, l1_ref, l2_ref), l_scr, DILATIONS)]
    mx = jnp.maximum(jnp.maximum(lses[0], lses[1]), lses[2])
    es = [jnp.exp(l - mx) for l in lses]
    inv = 1.0 / (es[0] + es[1] + es[2])
    yb = None
    for g in range(NB_GROUPS):
        og = (os_[g] * (es[g] * inv)).astype(BF16)
        part = _dot(og, wb_ref[g * GROUP_WIDTH:(g + 1) * GROUP_WIDTH, :])
        yb = part if yb is None else yb + part
    ya = _dot(oa_ref[...], wa_ref[...])
    merged = jax.nn.sigmoid(_dot(h, wga_ref[...])) * ya + jax.nn.sigmoid(_dot(h, wgb_ref[...])) * yb
    z = _dot(merged.astype(BF16), wout_ref[...])
    out_ref[...] = x + _rmsnorm_f32(z, post_g_ref[...])


def _merge(x, pre_g, wga, wgb, oa, wa, obs, lses, wb, wout, post_g):
    batch, seq, d = x.shape
    tm = MERGE_TM
    tok = lambda b, i: (b, i, 0)
    res = lambda b, i: (b, 0, i, 0)
    tokspec = lambda w: pl.BlockSpec((None, tm, w), tok)
    resspecs = [pl.BlockSpec((None, dil, tm // dil, GROUP_WIDTH), res) for dil in DILATIONS]
    blocks = (2 * 2 * _nbytes((tm, d), F32) + 2 * _nbytes((tm, NA_WIDTH), BF16)
              + 2 * 3 * (_nbytes((tm, GROUP_WIDTH), BF16) + _nbytes((tm, GROUP_WIDTH), F32))
              + _nbytes((3 * d + NA_WIDTH + NB_WIDTH, d), BF16))
    scratch = 2 * NB_GROUPS * _nbytes((tm, GROUP_WIDTH), F32)
    temps = 8 * _nbytes((tm, d), F32)
    return pl.pallas_call(
        _merge_kernel,
        grid=(batch, seq // tm),
        in_specs=[
            tokspec(d), _resident((1, d)), _resident((d, d)), _resident((d, d)),
            tokspec(NA_WIDTH), _resident((NA_WIDTH, d)),
            *resspecs, *resspecs,
            _resident((NB_WIDTH, d)), _resident((d, d)), _resident((1, d)),
        ],
        out_specs=tokspec(d),
        out_shape=jax.ShapeDtypeStruct((batch, seq, d), F32),
        scratch_shapes=[pltpu.VMEM((GROUP_LANE_BLOCKS, tm, V7X_LANES), F32)] * (2 * NB_GROUPS),
        compiler_params=pltpu.CompilerParams(
            dimension_semantics=("arbitrary", "arbitrary"),
            vmem_limit_bytes=_vmem_limit(blocks, scratch, temps),
        ),
        name="merge",
    )(x, pre_g, wga, wgb, oa, wa, *obs, *lses, wb, wout, post_g)


def kernel(x, ffn1_pre_g, ffn1_w_gate, ffn1_w_up, ffn1_w_down, ffn1_post_g, mix_pre_g, w_in, na_rpb, w_branch_a, w_branch_b, w_out, mix_post_g, ffn2_pre_g, ffn2_w_gate, ffn2_w_up, ffn2_w_down, ffn2_post_g):
    batch, seq, d = x.shape
    depth = ffn1_pre_g.shape[0]
    gate_off = QKV_TOTAL
    for window, dilation in DIL_PAIRS:
        assert window // (2 * dilation) == DIL_HALF
    for l in range(depth):
        x = _ffn(x.reshape(batch * seq, d), ffn1_pre_g[l][None], ffn1_w_gate[l].astype(BF16),
                 ffn1_w_up[l].astype(BF16), ffn1_w_down[l].astype(BF16), ffn1_post_g[l][None])
        x = x.reshape(batch, seq, d)
        w_in_l = w_in[l].astype(BF16)
        qa, ka, va, qb, kb, vb = _qkv(x, mix_pre_g[l][None], w_in_l[:, :gate_off])
        oa = _nbr_attention(qa, ka, va, na_rpb[l])
        obs, lses = [], []
        for g in range(NB_GROUPS):
            o_g, lse_g = _dil_attention(qb[g], kb[g], vb[g], g)
            obs.append(o_g)
            lses.append(lse_g)
        x = _merge(x, mix_pre_g[l][None], w_in_l[:, gate_off:gate_off + d], w_in_l[:, gate_off + d:],
                   oa, w_branch_a[l].astype(BF16), obs, lses, w_branch_b[l].astype(BF16),
                   w_out[l].astype(BF16), mix_post_g[l][None])
        x = _ffn(x.reshape(batch * seq, d), ffn2_pre_g[l][None], ffn2_w_gate[l].astype(BF16),
                 ffn2_w_up[l].astype(BF16), ffn2_w_down[l].astype(BF16), ffn2_post_g[l][None])
        x = x.reshape(batch, seq, d)
    return x
```

```python
import functools
import math

import jax
import jax.numpy as jnp
from jax import lax
from jax.experimental import pallas as pl
from jax.experimental.pallas import tpu as pltpu

D_MODEL = 1024
HEAD_DIM = 64
NA_HEADS = 8
NA_WIDTH = NA_HEADS * HEAD_DIM
NA_ROWS = 8
NA_COLS = 16
GRID_W = 64
DIL_PAIRS = ((128, 1), (512, 4), (2048, 16))
DILATIONS = tuple(d for _, d in DIL_PAIRS)
NB_GROUPS = len(DIL_PAIRS)
NB_HEADS_PER_GROUP = 4
NB_HEADS = NB_GROUPS * NB_HEADS_PER_GROUP
NB_WIDTH = NB_HEADS * HEAD_DIM
GROUP_WIDTH = NB_HEADS_PER_GROUP * HEAD_DIM
ALIBI_MAX_EXP = 8.0
D_FF = 2816
NORM_EPS = 1e-6
ATTN_SCALE = HEAD_DIM ** -0.5

V7X_LANES = 128
GROUP_LANE_BLOCKS = GROUP_WIDTH // V7X_LANES
V7X_VMEM_BYTES = 64 * 1024 * 1024
V7X_VMEM_RESERVE = 6 * 1024 * 1024

BF16 = jnp.bfloat16
F32 = jnp.float32


def _vmem_limit(block_bytes, scratch_bytes, temp_bytes):
    need = block_bytes + scratch_bytes + temp_bytes
    assert need <= V7X_VMEM_BYTES - V7X_VMEM_RESERVE, need
    return int(need)


def _nbytes(shape, dtype):
    return math.prod(shape) * jnp.dtype(dtype).itemsize


def _resident(shape):
    nd = len(shape)
    return pl.BlockSpec(shape, lambda *_: (0,) * nd, pipeline_mode=pl.Buffered(1))


def _rmsnorm_f32(x, g):
    return x * lax.rsqrt(jnp.mean(x * x, axis=-1, keepdims=True) + NORM_EPS) * g


def _dot(a, b):
    return jnp.dot(a, b, preferred_element_type=F32)


def _dot_nt(a, b):
    return lax.dot_general(a, b, (((1,), (1,)), ((), ())), preferred_element_type=F32)


def _pipelined_attention(n_groups, unroll, scores, probs, finish, s_scr, p_scr, stat_scrs):
    assert n_groups % 2 == 0 and n_groups >= 2
    tiles = lambda i: [i * unroll + u for u in range(unroll)]

    def stage_scores(i, slot):
        for u, t in enumerate(tiles(i)):
            s_scr[slot, u] = scores(t)

    def stage_probs(slot):
        for u in range(unroll):
            p, *stats = probs(s_scr[slot, u])
            p_scr[slot, u] = p
            for ref, stat in zip(stat_scrs, stats):
                ref[slot, u] = stat

    def stage_finish(i, slot):
        for u, t in enumerate(tiles(i)):
            finish(t, p_scr[slot, u], *[ref[slot, u] for ref in stat_scrs])

    def step(i, parity):
        stage_scores(i + 1, 1 - parity)
        stage_finish(i - 1, 1 - parity)
        stage_probs(parity)

    stage_scores(0, 0)
    stage_scores(1, 1)
    stage_probs(0)

    def body(i, carry):
        @pl.when(i % 2 == 1)
        def _():
            step(i, 1)

        @pl.when(i % 2 == 0)
        def _():
            step(i, 0)

        return carry

    lax.fori_loop(1, n_groups - 1, body, 0)
    stage_finish(n_groups - 2, 0)
    stage_probs(1)
    stage_finish(n_groups - 1, 1)


FFN_TM = 1024
SUB_TM = 512
FFN_CK = 256


def _sub_tiles(tm):
    return [slice(k * SUB_TM, (k + 1) * SUB_TM) for k in range(tm // SUB_TM)]


def _ffn_kernel(x_ref, pre_g_ref, wg_ref, wu_ref, wd_ref, post_g_ref, o_ref, h_ref, a_ref):
    subs = _sub_tiles(x_ref.shape[0])
    for rows in subs:
        h_ref[rows] = _rmsnorm_f32(x_ref[rows], pre_g_ref[...]).astype(BF16)
        for c in range(D_FF // FFN_CK):
            cols = slice(c * FFN_CK, (c + 1) * FFN_CK)
            h = h_ref[rows]
            g = _dot(h, wg_ref[:, cols])
            u = _dot(h, wu_ref[:, cols])
            a_ref[rows, cols] = (g * jax.nn.sigmoid(g) * u).astype(BF16)
    for rows in subs:
        o_ref[rows] = _dot(a_ref[rows], wd_ref[...])
    for rows in subs:
        o_ref[rows] = x_ref[rows] + 0.5 * _rmsnorm_f32(o_ref[rows], post_g_ref[...])


def _ffn(x, pre_g, wg, wu, wd, post_g):
    n, d = x.shape
    tm = FFN_TM
    row = lambda i: (i, 0)
    blocks = 2 * 2 * _nbytes((tm, d), F32) + 3 * _nbytes((d, D_FF), BF16) + 2 * _nbytes((1, d), F32)
    scratch = _nbytes((tm, d), BF16) + _nbytes((tm, D_FF), BF16)
    temps = 4 * _nbytes((SUB_TM, d), F32)
    return pl.pallas_call(
        _ffn_kernel,
        grid=(n // tm,),
        in_specs=[
            pl.BlockSpec((tm, d), row),
            _resident((1, d)),
            _resident((d, D_FF)),
            _resident((d, D_FF)),
            _resident((D_FF, d)),
            _resident((1, d)),
        ],
        out_specs=pl.BlockSpec((tm, d), row),
        out_shape=jax.ShapeDtypeStruct((n, d), F32),
        scratch_shapes=[pltpu.VMEM((tm, d), BF16), pltpu.VMEM((tm, D_FF), BF16)],
        compiler_params=pltpu.CompilerParams(
            dimension_semantics=("arbitrary",),
            vmem_limit_bytes=_vmem_limit(blocks, scratch, temps),
        ),
        name="ffn",
    )(x, pre_g, wg, wu, wd, post_g)


QKV_TM = 1024
QKV_TOTAL = 3 * NA_WIDTH + 3 * NB_WIDTH
N_DIL_OUTS = 3 * NB_GROUPS
STAGED_OUTS = [i for i in range(N_DIL_OUTS) if DILATIONS[i % NB_GROUPS] > 1]
N_STAGED_OUTS = len(STAGED_OUTS)


def _qkv_kernel(x_ref, g_ref, w_ref, *refs):
    na_refs, dil_refs, y_refs = refs[:3], refs[3:3 + N_DIL_OUTS], refs[3 + N_DIL_OUTS:]
    for k, rows in enumerate(_sub_tiles(x_ref.shape[0])):
        h = _rmsnorm_f32(x_ref[rows], g_ref[...]).astype(BF16)
        for idx, o_ref in enumerate(na_refs):
            y = _dot(h, w_ref[:, idx * NA_WIDTH:(idx + 1) * NA_WIDTH])
            if idx == 0:
                y = y * ATTN_SCALE
            o_ref[rows] = y.astype(BF16)
        for idx, o_ref in enumerate(dil_refs):
            which, g = divmod(idx, NB_GROUPS)
            dil = DILATIONS[g]
            off = 3 * NA_WIDTH + which * NB_WIDTH + g * GROUP_WIDTH
            y = _dot(h, w_ref[:, off:off + GROUP_WIDTH])
            if which == 0:
                y = y * ATTN_SCALE
            if dil == 1:
                o_ref[0, rows] = y.astype(BF16)
                continue
            y_ref = y_refs[k * N_STAGED_OUTS + STAGED_OUTS.index(idx)]
            per_res = SUB_TM // dil
            for c in range(GROUP_LANE_BLOCKS):
                y_ref[c] = y[:, c * V7X_LANES:(c + 1) * V7X_LANES]
            for r in range(dil):
                for c in range(GROUP_LANE_BLOCKS):
                    o_ref[r, k * per_res:(k + 1) * per_res, c * V7X_LANES:(c + 1) * V7X_LANES] = (
                        y_ref[c, pl.ds(r, per_res, stride=dil), :].astype(BF16))


def _qkv(x, g, w_qkv):
    batch, seq, d = x.shape
    tm = QKV_TM
    tok = lambda b, i: (b, i, 0)
    res = lambda b, i: (b, 0, i, 0)
    out_specs = [pl.BlockSpec((None, tm, NA_WIDTH), tok)] * 3
    out_shape = [jax.ShapeDtypeStruct((batch, seq, NA_WIDTH), BF16)] * 3
    for _ in range(3):
        for dil in DILATIONS:
            out_specs.append(pl.BlockSpec((None, dil, tm // dil, GROUP_WIDTH), res))
            out_shape.append(jax.ShapeDtypeStruct((batch, dil, seq // dil, GROUP_WIDTH), BF16))
    blocks = 2 * _nbytes((tm, d), F32) + _nbytes((d, QKV_TOTAL), BF16) + 2 * _nbytes((tm, QKV_TOTAL), BF16)
    stage = (GROUP_LANE_BLOCKS, SUB_TM, V7X_LANES)
    n_stage = N_STAGED_OUTS * (tm // SUB_TM)
    scratch = n_stage * _nbytes(stage, F32)
    temps = 2 * _nbytes((SUB_TM, d), F32) + 2 * _nbytes((SUB_TM, NB_WIDTH), F32)
    outs = pl.pallas_call(
        _qkv_kernel,
        grid=(batch, seq // tm),
        in_specs=[pl.BlockSpec((None, tm, d), tok), _resident((1, d)), _resident((d, QKV_TOTAL))],
        out_specs=out_specs,
        out_shape=out_shape,
        scratch_shapes=[pltpu.VMEM(stage, F32)] * n_stage,
        compiler_params=pltpu.CompilerParams(
            dimension_semantics=("arbitrary", "arbitrary"),
            vmem_limit_bytes=_vmem_limit(blocks, scratch, temps),
        ),
        name="qkv",
    )(x, g, w_qkv)
    qa, ka, va = outs[:3]
    qb, kb, vb = (outs[3 + w * NB_GROUPS:3 + (w + 1) * NB_GROUPS] for w in range(3))
    return qa, ka, va, qb, kb, vb


NA_HEADS_PER_STEP = V7X_LANES // HEAD_DIM
NA_SPAN = NA_ROWS * GRID_W
NA_VARIANTS = NA_ROWS
NA_RPB_ROWS = 2 * NA_ROWS - 1
NA_RPB_COLS = 2 * NA_COLS - 1
NA_UNROLL = 8


def _nbr_build_bias(rep_ref, tbl_ref):
    width = NA_RPB_ROWS * GRID_W
    qc = lax.broadcasted_iota(jnp.int32, (GRID_W, width), 0)
    kc = lax.broadcasted_iota(jnp.int32, (GRID_W, width), 1) % GRID_W
    diff = kc - qc
    col0 = jnp.clip(qc - NA_COLS // 2, 0, GRID_W - NA_COLS)
    col_ok = (kc >= col0) & (kc < col0 + NA_COLS)
    for h in range(NA_HEADS_PER_STEP):
        w = jnp.full((GRID_W, width), -jnp.inf, F32)
        for k in range(NA_RPB_COLS):
            w = jnp.where(diff == k - (NA_COLS - 1), rep_ref[h, k:k + 1, :], w)
        w = jnp.where(col_ok, w, -jnp.inf)
        for variant in range(NA_VARIANTS):
            tbl_ref[h, variant] = w[:, variant * GRID_W:variant * GRID_W + NA_SPAN]


def _nbr_kernel(q_ref, k_ref, v_ref, rep_ref, o_ref, tbl_ref, s_scr, p_scr, *, rows):
    hp = NA_HEADS_PER_STEP
    ones = jnp.ones((NA_SPAN, V7X_LANES), BF16)

    @pl.when(pl.program_id(1) == 0)
    def _():
        _nbr_build_bias(rep_ref, tbl_ref)

    lane_head = lax.broadcasted_iota(jnp.int32, (GRID_W, V7X_LANES), 1) // HEAD_DIM

    def scores(r):
        row0 = jnp.clip(r - NA_ROWS // 2, 0, rows - NA_ROWS)
        variant = row0 - r + (NA_ROWS - 1)
        q = q_ref[pl.ds(pl.multiple_of(r * GRID_W, GRID_W), GRID_W), :]
        kw = k_ref[pl.ds(pl.multiple_of(row0 * GRID_W, GRID_W), NA_SPAN), :]
        qs = jnp.concatenate([jnp.where(lane_head == h, q, jnp.zeros_like(q)) for h in range(hp)], axis=0)
        return _dot_nt(qs, kw) + tbl_ref[:, variant].reshape(hp * GRID_W, NA_SPAN)

    def probs(s):
        return (jnp.exp(s - jnp.max(s, axis=-1, keepdims=True)).astype(BF16),)

    def finish(r, p):
        row0 = jnp.clip(r - NA_ROWS // 2, 0, rows - NA_ROWS)
        vw = v_ref[pl.ds(pl.multiple_of(row0 * GRID_W, GRID_W), NA_SPAN), :]
        o = _dot(p, jnp.concatenate([vw, ones], axis=1))
        o = o[:, :V7X_LANES] * (1.0 / o[:, V7X_LANES:])
        out = o[:GRID_W]
        for h in range(1, hp):
            out = jnp.where(lane_head == h, o[h * GRID_W:(h + 1) * GRID_W], out)
        o_ref[pl.ds(pl.multiple_of(r * GRID_W, GRID_W), GRID_W), :] = out.astype(BF16)

    _pipelined_attention(rows // NA_UNROLL, NA_UNROLL, scores, probs, finish, s_scr, p_scr, ())


def _nbr_attention(q, k, v, rpb):
    batch, seq, width = q.shape
    rows = seq // GRID_W
    n_pairs = width // V7X_LANES
    rep = jnp.repeat(rpb.astype(F32).transpose(0, 2, 1), GRID_W, axis=-1)
    tok = lambda p, b: (b, 0, p)
    tbl_shape = (NA_HEADS_PER_STEP, NA_VARIANTS, GRID_W, NA_SPAN)
    rep_block = (NA_HEADS_PER_STEP, NA_RPB_COLS, NA_RPB_ROWS * GRID_W)
    blocks = 2 * 4 * _nbytes((seq, V7X_LANES), BF16) + 2 * _nbytes((NA_HEADS_PER_STEP, 32, 1024), F32)
    tile = (2, NA_UNROLL, NA_HEADS_PER_STEP * GRID_W, NA_SPAN)
    scratch = _nbytes(tbl_shape, F32) + _nbytes(tile, F32) + _nbytes(tile, BF16)
    temps = 8 * NA_UNROLL * _nbytes(tile[2:], F32)
    return pl.pallas_call(
        functools.partial(_nbr_kernel, rows=rows),
        grid=(n_pairs, batch),
        in_specs=[
            pl.BlockSpec((None, seq, V7X_LANES), tok),
            pl.BlockSpec((None, seq, V7X_LANES), tok),
            pl.BlockSpec((None, seq, V7X_LANES), tok),
            pl.BlockSpec(rep_block, lambda p, b: (p, 0, 0)),
        ],
        out_specs=pl.BlockSpec((None, seq, V7X_LANES), tok),
        out_shape=jax.ShapeDtypeStruct((batch, seq, width), BF16),
        scratch_shapes=[pltpu.VMEM(tbl_shape, F32), pltpu.VMEM(tile, F32), pltpu.VMEM(tile, BF16)],
        compiler_params=pltpu.CompilerParams(
            dimension_semantics=("arbitrary", "arbitrary"),
            vmem_limit_bytes=_vmem_limit(blocks, scratch, temps),
        ),
        name="nbr_attn",
    )(q, k, v, rep)


DIL_HALF = 64
DIL_TQ = 128
DIL_TK = DIL_TQ + 2 * DIL_HALF
DIL_VARIANTS = 3
DIL_STEP_ROWS = 4096
DIL_UNROLL = 4


def _dil_bias_table(group, dilation):
    heads = jnp.arange(NB_HEADS_PER_GROUP, dtype=F32) + group * NB_HEADS_PER_GROUP
    slopes = jnp.exp2(-ALIBI_MAX_EXP * (heads + 1.0) / NB_HEADS)
    offs = jnp.array([0, -DIL_HALF, -2 * DIL_HALF], jnp.int32)
    rel = offs[:, None, None] + jnp.arange(DIL_TK)[None, None, :] - jnp.arange(DIL_TQ)[None, :, None]
    dist = (dilation * jnp.abs(rel)).astype(F32)
    bias = -(slopes[None, :, None, None] * dist[:, None])
    bias = jnp.where((jnp.abs(rel) <= DIL_HALF)[:, None], bias, -jnp.inf)
    return bias.reshape(DIL_VARIANTS, NB_HEADS_PER_GROUP * DIL_TQ, DIL_TK)


def _dil_kernel(q_ref, k_ref, v_ref, bias_ref, o_ref, lse_ref, s_scr, p_scr, m_scr, *, seq_len, stretch):
    nh = NB_HEADS_PER_GROUP
    hpb = V7X_LANES // HEAD_DIM
    ones = jnp.ones((DIL_TK, V7X_LANES), BF16)
    group_lane_head = lax.broadcasted_iota(jnp.int32, (DIL_TQ, GROUP_WIDTH), 1) // HEAD_DIM
    lane_head = lax.broadcasted_iota(jnp.int32, (DIL_TQ, V7X_LANES), 1) // HEAD_DIM
    base = pl.program_id(2) * stretch
    tiles_per_residue = stretch // DIL_TQ

    def window(t):
        r, j = t // tiles_per_residue, t % tiles_per_residue
        qloc = pl.multiple_of(j * DIL_TQ, DIL_TQ)
        qs = base + qloc
        ws = jnp.clip(qs - DIL_HALF, 0, seq_len - DIL_TK)
        variant = (qs - ws) // DIL_HALF
        return r, qloc, pl.multiple_of(ws, DIL_HALF), variant

    def scores(t):
        r, qloc, ws, variant = window(t)
        q = q_ref[r, pl.ds(qloc, DIL_TQ), :]
        kw = k_ref[r, pl.ds(ws, DIL_TK), :]
        qst = jnp.concatenate([jnp.where(group_lane_head == h, q, jnp.zeros_like(q)) for h in range(nh)],
                              axis=0)
        return _dot_nt(qst, kw) + bias_ref[variant]

    def probs(s):
        m = jnp.max(s, axis=-1, keepdims=True)
        return jnp.exp(s - m).astype(BF16), jnp.broadcast_to(m, (s.shape[0], V7X_LANES))

    def finish(t, p, m):
        r, qloc, ws, _ = window(t)
        vw = v_ref[r, pl.ds(ws, DIL_TK), :]
        outs, lses = [], []
        for c in range(GROUP_LANE_BLOCKS):
            rows_c = slice(c * hpb * DIL_TQ, (c + 1) * hpb * DIL_TQ)
            v_aug = jnp.concatenate([vw[:, c * V7X_LANES:(c + 1) * V7X_LANES], ones], axis=1)
            o = _dot(p[rows_c], v_aug)
            l = o[:, V7X_LANES:]
            o = o[:, :V7X_LANES] * (1.0 / l)
            lse = m[rows_c] + jnp.log(l)
            out_c, lse_c = o[:DIL_TQ], lse[:DIL_TQ]
            for h in range(1, hpb):
                rows_h = slice(h * DIL_TQ, (h + 1) * DIL_TQ)
                out_c = jnp.where(lane_head == h, o[rows_h], out_c)
                lse_c = jnp.where(lane_head == h, lse[rows_h], lse_c)
            outs.append(out_c)
            lses.append(lse_c)
        o_ref[r, pl.ds(qloc, DIL_TQ), :] = jnp.concatenate(outs, axis=1).astype(BF16)
        lse_ref[r, pl.ds(qloc, DIL_TQ), :] = jnp.concatenate(lses, axis=1)

    n_tiles = q_ref.shape[0] * tiles_per_residue
    _pipelined_attention(n_tiles // DIL_UNROLL, DIL_UNROLL, scores, probs, finish, s_scr, p_scr, (m_scr,))


def _dil_attention(q, k, v, group):
    batch, dilation, seq_len, width = q.shape
    stretch = min(seq_len, DIL_STEP_ROWS)
    res_blk = DIL_STEP_ROWS // stretch
    assert dilation % res_blk == 0 and seq_len % stretch == 0
    qmap = lambda b, r, s: (b, r, s, 0)
    kvmap = lambda b, r, s: (b, r, 0, 0)
    bias = _dil_bias_table(group, dilation)
    tile = (2, DIL_UNROLL, NB_HEADS_PER_GROUP * DIL_TQ, DIL_TK)
    stat = tile[:3] + (V7X_LANES,)
    blocks = 2 * (2 * _nbytes((DIL_STEP_ROWS, width), BF16) + 2 * _nbytes((res_blk * seq_len, width), BF16)
                  + _nbytes((DIL_STEP_ROWS, width), F32)) + _nbytes(bias.shape, F32)
    scratch = _nbytes(tile, F32) + _nbytes(tile, BF16) + _nbytes(stat, F32)
    temps = 4 * DIL_UNROLL * _nbytes(tile[2:], F32)
    return pl.pallas_call(
        functools.partial(_dil_kernel, seq_len=seq_len, stretch=stretch),
        grid=(batch, dilation // res_blk, seq_len // stretch),
        in_specs=[
            pl.BlockSpec((None, res_blk, stretch, width), qmap),
            pl.BlockSpec((None, res_blk, seq_len, width), kvmap),
            pl.BlockSpec((None, res_blk, seq_len, width), kvmap),
            _resident(bias.shape),
        ],
        out_specs=[pl.BlockSpec((None, res_blk, stretch, width), qmap)] * 2,
        out_shape=[jax.ShapeDtypeStruct(q.shape, BF16), jax.ShapeDtypeStruct(q.shape, F32)],
        scratch_shapes=[pltpu.VMEM(tile, F32), pltpu.VMEM(tile, BF16), pltpu.VMEM(stat, F32)],
        compiler_params=pltpu.CompilerParams(
            dimension_semantics=("arbitrary", "arbitrary", "arbitrary"),
            vmem_limit_bytes=_vmem_limit(blocks, scratch, temps),
        ),
        name=f"dil_attn_g{group}",
    )(q, k, v, bias)


MERGE_TM = 1024
MERGE_CN = 256


def _to_token_order(src_ref, scr_ref, dil, k):
    per_res = SUB_TM // dil
    res_rows = slice(k * per_res, (k + 1) * per_res)
    if dil == 1:
        return src_ref[0, res_rows].astype(F32)
    for r in range(dil):
        for c in range(GROUP_LANE_BLOCKS):
            scr_ref[c, pl.ds(k * SUB_TM + r, per_res, stride=dil), :] = (
                src_ref[r, res_rows, c * V7X_LANES:(c + 1) * V7X_LANES].astype(F32))
    rows = slice(k * SUB_TM, (k + 1) * SUB_TM)
    return jnp.concatenate([scr_ref[c, rows] for c in range(GROUP_LANE_BLOCKS)], axis=-1)


def _merge_kernel(x_ref, pre_g_ref, wga_ref, wgb_ref, oa_ref, wa_ref,
                  o0_ref, o1_ref, o2_ref, l0_ref, l1_ref, l2_ref, wb_ref, wout_ref, post_g_ref, out_ref,
                  h_ref, ob_ref, mg_ref, *scr_refs):
    o_scr, l_scr = scr_refs[:NB_GROUPS], scr_refs[NB_GROUPS:]
    subs = _sub_tiles(x_ref.shape[0])
    for k, rows in enumerate(subs):
        h_ref[rows] = _rmsnorm_f32(x_ref[rows], pre_g_ref[...]).astype(BF16)
        os_ = [_to_token_order(r, s, d, k) for r, s, d in zip((o0_ref, o1_ref, o2_ref), o_scr, DILATIONS)]
        lses = [_to_token_order(r, s, d, k) for r, s, d in zip((l0_ref, l1_ref, l2_ref), l_scr, DILATIONS)]
        mx = jnp.maximum(jnp.maximum(lses[0], lses[1]), lses[2])
        es = [jnp.exp(l - mx) for l in lses]
        inv = 1.0 / (es[0] + es[1] + es[2])
        for g in range(NB_GROUPS):
            ob_ref[rows, g * GROUP_WIDTH:(g + 1) * GROUP_WIDTH] = (os_[g] * (es[g] * inv)).astype(BF16)
        for n in range(D_MODEL // MERGE_CN):
            cols = slice(n * MERGE_CN, (n + 1) * MERGE_CN)
            h = h_ref[rows]
            ya = _dot(oa_ref[rows], wa_ref[:, cols])
            yb = _dot(ob_ref[rows], wb_ref[:, cols])
            merged = jax.nn.sigmoid(_dot(h, wga_ref[:, cols])) * ya + jax.nn.sigmoid(_dot(h, wgb_ref[:, cols])) * yb
            mg_ref[rows, cols] = merged.astype(BF16)
    for rows in subs:
        out_ref[rows] = _dot(mg_ref[rows], wout_ref[...])
    for rows in subs:
        out_ref[rows] = x_ref[rows] + _rmsnorm_f32(out_ref[rows], post_g_ref[...])


def _merge(x, pre_g, wga, wgb, oa, wa, obs, lses, wb, wout, post_g):
    batch, seq, d = x.shape
    tm = MERGE_TM
    tok = lambda b, i: (b, i, 0)
    res = lambda b, i: (b, 0, i, 0)
    tokspec = lambda w: pl.BlockSpec((None, tm, w), tok)
    resspecs = [pl.BlockSpec((None, dil, tm // dil, GROUP_WIDTH), res) for dil in DILATIONS]
    blocks = (2 * 2 * _nbytes((tm, d), F32) + 2 * _nbytes((tm, NA_WIDTH), BF16)
              + 2 * 3 * (_nbytes((tm, GROUP_WIDTH), BF16) + _nbytes((tm, GROUP_WIDTH), F32))
              + _nbytes((3 * d + NA_WIDTH + NB_WIDTH, d), BF16))
    work = [pltpu.VMEM((tm, d), BF16), pltpu.VMEM((tm, NB_WIDTH), BF16), pltpu.VMEM((tm, d), BF16)]
    scratch = (2 * NB_GROUPS * _nbytes((tm, GROUP_WIDTH), F32) + 2 * _nbytes((tm, d), BF16)
               + _nbytes((tm, NB_WIDTH), BF16))
    temps = 3 * _nbytes((SUB_TM, d), F32)
    return pl.pallas_call(
        _merge_kernel,
        grid=(batch, seq // tm),
        in_specs=[
            tokspec(d), _resident((1, d)), _resident((d, d)), _resident((d, d)),
            tokspec(NA_WIDTH), _resident((NA_WIDTH, d)),
            *resspecs, *resspecs,
            _resident((NB_WIDTH, d)), _resident((d, d)), _resident((1, d)),
        ],
        out_specs=tokspec(d),
        out_shape=jax.ShapeDtypeStruct((batch, seq, d), F32),
        scratch_shapes=work + [pltpu.VMEM((GROUP_LANE_BLOCKS, tm, V7X_LANES), F32)] * (2 * NB_GROUPS),
        compiler_params=pltpu.CompilerParams(
            dimension_semantics=("arbitrary", "arbitrary"),
            vmem_limit_bytes=_vmem_limit(blocks, scratch, temps),
        ),
        name="merge",
    )(x, pre_g, wga, wgb, oa, wa, *obs, *lses, wb, wout, post_g)


def kernel(x, ffn1_pre_g, ffn1_w_gate, ffn1_w_up, ffn1_w_down, ffn1_post_g, mix_pre_g, w_in, na_rpb, w_branch_a, w_branch_b, w_out, mix_post_g, ffn2_pre_g, ffn2_w_gate, ffn2_w_up, ffn2_w_down, ffn2_post_g):
    batch, seq, d = x.shape
    depth = ffn1_pre_g.shape[0]
    gate_off = QKV_TOTAL
    for window, dilation in DIL_PAIRS:
        assert window // (2 * dilation) == DIL_HALF
    for l in range(depth):
        x = _ffn(x.reshape(batch * seq, d), ffn1_pre_g[l][None], ffn1_w_gate[l].astype(BF16),
                 ffn1_w_up[l].astype(BF16), ffn1_w_down[l].astype(BF16), ffn1_post_g[l][None])
        x = x.reshape(batch, seq, d)
        w_in_l = w_in[l].astype(BF16)
        qa, ka, va, qb, kb, vb = _qkv(x, mix_pre_g[l][None], w_in_l[:, :gate_off])
        oa = _nbr_attention(qa, ka, va, na_rpb[l])
        obs, lses = [], []
        for g in range(NB_GROUPS):
            o_g, lse_g = _dil_attention(qb[g], kb[g], vb[g], g)
            obs.append(o_g)
            lses.append(lse_g)
        x = _merge(x, mix_pre_g[l][None], w_in_l[:, gate_off:gate_off + d], w_in_l[:, gate_off + d:],
                   oa, w_branch_a[l].astype(BF16), obs, lses, w_branch_b[l].astype(BF16),
                   w_out[l].astype(BF16), mix_post_g[l][None])
        x = _ffn(x.reshape(batch * seq, d), ffn2_pre_g[l][None], ffn2_w_gate[l].astype(BF16),
                 ffn2_w_up[l].astype(BF16), ffn2_w_down[l].astype(BF16), ffn2_post_g[l][None])
        x = x.reshape(batch, seq, d)
    return x
```

```python
import functools
import math

import jax
import jax.numpy as jnp
from jax import lax
from jax.experimental import pallas as pl
from jax.experimental.pallas import tpu as pltpu

D_MODEL = 1024
HEAD_DIM = 64
NA_HEADS = 8
NA_WIDTH = NA_HEADS * HEAD_DIM
NA_ROWS = 8
NA_COLS = 16
GRID_W = 64
DIL_PAIRS = ((128, 1), (512, 4), (2048, 16))
DILATIONS = tuple(d for _, d in DIL_PAIRS)
NB_GROUPS = len(DIL_PAIRS)
NB_HEADS_PER_GROUP = 4
NB_HEADS = NB_GROUPS * NB_HEADS_PER_GROUP
NB_WIDTH = NB_HEADS * HEAD_DIM
GROUP_WIDTH = NB_HEADS_PER_GROUP * HEAD_DIM
ALIBI_MAX_EXP = 8.0
D_FF = 2816
NORM_EPS = 1e-6
ATTN_SCALE = HEAD_DIM ** -0.5

V7X_LANES = 128
GROUP_LANE_BLOCKS = GROUP_WIDTH // V7X_LANES
V7X_VMEM_BYTES = 64 * 1024 * 1024
V7X_VMEM_RESERVE = 6 * 1024 * 1024

BF16 = jnp.bfloat16
F32 = jnp.float32


def _vmem_limit(block_bytes, scratch_bytes, temp_bytes):
    need = block_bytes + scratch_bytes + temp_bytes
    assert need <= V7X_VMEM_BYTES - V7X_VMEM_RESERVE, need
    return int(need)


def _nbytes(shape, dtype):
    return math.prod(shape) * jnp.dtype(dtype).itemsize


def _resident(shape):
    nd = len(shape)
    return pl.BlockSpec(shape, lambda *_: (0,) * nd, pipeline_mode=pl.Buffered(1))


def _rmsnorm_f32(x, g):
    return x * lax.rsqrt(jnp.mean(x * x, axis=-1, keepdims=True) + NORM_EPS) * g


def _dot(a, b):
    return jnp.dot(a, b, preferred_element_type=F32)


def _dot_nt(a, b):
    return lax.dot_general(a, b, (((1,), (1,)), ((), ())), preferred_element_type=F32)


def _pipelined_attention(n_groups, unroll, scores, probs, finish, s_scr, p_scr, stat_scrs):
    assert n_groups % 2 == 0 and n_groups >= 2
    tiles = lambda i: [i * unroll + u for u in range(unroll)]

    def stage_scores(i, slot):
        for u, t in enumerate(tiles(i)):
            s_scr[slot, u] = scores(t)

    def stage_probs(slot):
        for u in range(unroll):
            p, *stats = probs(s_scr[slot, u])
            p_scr[slot, u] = p
            for ref, stat in zip(stat_scrs, stats):
                ref[slot, u] = stat

    def stage_finish(i, slot):
        for u, t in enumerate(tiles(i)):
            finish(t, p_scr[slot, u], *[ref[slot, u] for ref in stat_scrs])

    def step(i, parity):
        stage_scores(i + 1, 1 - parity)
        stage_finish(i - 1, 1 - parity)
        stage_probs(parity)

    stage_scores(0, 0)
    stage_scores(1, 1)
    stage_probs(0)

    def body(i, carry):
        @pl.when(i % 2 == 1)
        def _():
            step(i, 1)

        @pl.when(i % 2 == 0)
        def _():
            step(i, 0)

        return carry

    lax.fori_loop(1, n_groups - 1, body, 0)
    stage_finish(n_groups - 2, 0)
    stage_probs(1)
    stage_finish(n_groups - 1, 1)


FFN_TM = 512
SUB_TM = 512
FFN_CK = 256


def _sub_tiles(tm):
    return [slice(k * SUB_TM, (k + 1) * SUB_TM) for k in range(tm // SUB_TM)]


def _ffn_kernel(x_ref, pre_g_ref, wg_ref, wu_ref, wd_ref, post_g_ref, o_ref, h_ref, a_ref):
    h_ref[...] = _rmsnorm_f32(x_ref[...], pre_g_ref[...]).astype(BF16)
    chunks = [slice(c * FFN_CK, (c + 1) * FFN_CK) for c in range(D_FF // FFN_CK)]
    for cols in chunks:
        h = h_ref[...]
        g = _dot(h, wg_ref[:, cols].astype(BF16))
        u = _dot(h, wu_ref[:, cols].astype(BF16))
        a_ref[:, cols] = (g * jax.nn.sigmoid(g) * u).astype(BF16)
    f = None
    for cols in chunks:
        part = _dot(a_ref[:, cols], wd_ref[cols, :].astype(BF16))
        f = part if f is None else f + part
    o_ref[...] = x_ref[...] + 0.5 * _rmsnorm_f32(f, post_g_ref[...])


def _ffn(x, pre_g, wg, wu, wd, post_g):
    n, d = x.shape
    tm = FFN_TM
    row = lambda i: (i, 0)
    blocks = 2 * 2 * _nbytes((tm, d), F32) + 3 * _nbytes((d, D_FF), wg.dtype) + 2 * _nbytes((1, d), F32)
    scratch = _nbytes((tm, d), BF16) + _nbytes((tm, D_FF), BF16)
    temps = 4 * _nbytes((SUB_TM, d), F32)
    return pl.pallas_call(
        _ffn_kernel,
        grid=(n // tm,),
        in_specs=[
            pl.BlockSpec((tm, d), row),
            _resident((1, d)),
            _resident((d, D_FF)),
            _resident((d, D_FF)),
            _resident((D_FF, d)),
            _resident((1, d)),
        ],
        out_specs=pl.BlockSpec((tm, d), row),
        out_shape=jax.ShapeDtypeStruct((n, d), F32),
        scratch_shapes=[pltpu.VMEM((tm, d), BF16), pltpu.VMEM((tm, D_FF), BF16)],
        compiler_params=pltpu.CompilerParams(
            dimension_semantics=("arbitrary",),
            vmem_limit_bytes=_vmem_limit(blocks, scratch, temps),
        ),
        name="ffn",
    )(x, pre_g, wg, wu, wd, post_g)


QKV_TM = 1024
QKV_TOTAL = 3 * NA_WIDTH + 3 * NB_WIDTH
N_DIL_OUTS = 3 * NB_GROUPS
STAGED_OUTS = [i for i in range(N_DIL_OUTS) if DILATIONS[i % NB_GROUPS] > 1]
N_STAGED_OUTS = len(STAGED_OUTS)


def _qkv_kernel(x_ref, g_ref, w_ref, *refs):
    na_refs, dil_refs, y_refs = refs[:3], refs[3:3 + N_DIL_OUTS], refs[3 + N_DIL_OUTS:]
    for k, rows in enumerate(_sub_tiles(x_ref.shape[0])):
        h = _rmsnorm_f32(x_ref[rows], g_ref[...]).astype(BF16)
        for idx, o_ref in enumerate(na_refs):
            y = _dot(h, w_ref[:, idx * NA_WIDTH:(idx + 1) * NA_WIDTH])
            if idx == 0:
                y = y * ATTN_SCALE
            o_ref[rows] = y.astype(BF16)
        for idx, o_ref in enumerate(dil_refs):
            which, g = divmod(idx, NB_GROUPS)
            dil = DILATIONS[g]
            off = 3 * NA_WIDTH + which * NB_WIDTH + g * GROUP_WIDTH
            y = _dot(h, w_ref[:, off:off + GROUP_WIDTH])
            if which == 0:
                y = y * ATTN_SCALE
            if dil == 1:
                o_ref[0, rows] = y.astype(BF16)
                continue
            y_ref = y_refs[k * N_STAGED_OUTS + STAGED_OUTS.index(idx)]
            per_res = SUB_TM // dil
            for c in range(GROUP_LANE_BLOCKS):
                y_ref[c] = y[:, c * V7X_LANES:(c + 1) * V7X_LANES]
            for r in range(dil):
                for c in range(GROUP_LANE_BLOCKS):
                    o_ref[r, k * per_res:(k + 1) * per_res, c * V7X_LANES:(c + 1) * V7X_LANES] = (
                        y_ref[c, pl.ds(r, per_res, stride=dil), :].astype(BF16))


def _qkv(x, g, w_qkv):
    batch, seq, d = x.shape
    tm = QKV_TM
    tok = lambda b, i: (b, i, 0)
    res = lambda b, i: (b, 0, i, 0)
    out_specs = [pl.BlockSpec((None, tm, NA_WIDTH), tok)] * 3
    out_shape = [jax.ShapeDtypeStruct((batch, seq, NA_WIDTH), BF16)] * 3
    for _ in range(3):
        for dil in DILATIONS:
            out_specs.append(pl.BlockSpec((None, dil, tm // dil, GROUP_WIDTH), res))
            out_shape.append(jax.ShapeDtypeStruct((batch, dil, seq // dil, GROUP_WIDTH), BF16))
    blocks = 2 * _nbytes((tm, d), F32) + _nbytes((d, QKV_TOTAL), BF16) + 2 * _nbytes((tm, QKV_TOTAL), BF16)
    stage = (GROUP_LANE_BLOCKS, SUB_TM, V7X_LANES)
    n_stage = N_STAGED_OUTS * (tm // SUB_TM)
    scratch = n_stage * _nbytes(stage, F32)
    temps = 2 * _nbytes((SUB_TM, d), F32) + 2 * _nbytes((SUB_TM, NB_WIDTH), F32)
    outs = pl.pallas_call(
        _qkv_kernel,
        grid=(batch, seq // tm),
        in_specs=[pl.BlockSpec((None, tm, d), tok), _resident((1, d)), _resident((d, QKV_TOTAL))],
        out_specs=out_specs,
        out_shape=out_shape,
        scratch_shapes=[pltpu.VMEM(stage, F32)] * n_stage,
        compiler_params=pltpu.CompilerParams(
            dimension_semantics=("arbitrary", "arbitrary"),
            vmem_limit_bytes=_vmem_limit(blocks, scratch, temps),
        ),
        name="qkv",
    )(x, g, w_qkv)
    qa, ka, va = outs[:3]
    qb, kb, vb = (outs[3 + w * NB_GROUPS:3 + (w + 1) * NB_GROUPS] for w in range(3))
    return qa, ka, va, qb, kb, vb


NA_HEADS_PER_STEP = V7X_LANES // HEAD_DIM
NA_SPAN = NA_ROWS * GRID_W
NA_VARIANTS = NA_ROWS
NA_RPB_ROWS = 2 * NA_ROWS - 1
NA_RPB_COLS = 2 * NA_COLS - 1
NA_UNROLL = 8


def _nbr_build_bias(rep_ref, tbl_ref):
    width = NA_RPB_ROWS * GRID_W
    qc = lax.broadcasted_iota(jnp.int32, (GRID_W, width), 0)
    kc = lax.broadcasted_iota(jnp.int32, (GRID_W, width), 1) % GRID_W
    diff = kc - qc
    col0 = jnp.clip(qc - NA_COLS // 2, 0, GRID_W - NA_COLS)
    col_ok = (kc >= col0) & (kc < col0 + NA_COLS)
    for h in range(NA_HEADS_PER_STEP):
        w = jnp.full((GRID_W, width), -jnp.inf, F32)
        for k in range(NA_RPB_COLS):
            w = jnp.where(diff == k - (NA_COLS - 1), rep_ref[h, k:k + 1, :], w)
        w = jnp.where(col_ok, w, -jnp.inf)
        for variant in range(NA_VARIANTS):
            tbl_ref[h, variant] = w[:, variant * GRID_W:variant * GRID_W + NA_SPAN]


def _nbr_kernel(q_ref, k_ref, v_ref, rep_ref, o_ref, tbl_ref, s_scr, p_scr, *, rows):
    hp = NA_HEADS_PER_STEP
    ones = jnp.ones((NA_SPAN, V7X_LANES), BF16)

    @pl.when(pl.program_id(1) == 0)
    def _():
        _nbr_build_bias(rep_ref, tbl_ref)

    lane_head = lax.broadcasted_iota(jnp.int32, (GRID_W, V7X_LANES), 1) // HEAD_DIM

    def scores(r):
        row0 = jnp.clip(r - NA_ROWS // 2, 0, rows - NA_ROWS)
        variant = row0 - r + (NA_ROWS - 1)
        q = q_ref[pl.ds(pl.multiple_of(r * GRID_W, GRID_W), GRID_W), :]
        kw = k_ref[pl.ds(pl.multiple_of(row0 * GRID_W, GRID_W), NA_SPAN), :]
        qs = jnp.concatenate([jnp.where(lane_head == h, q, jnp.zeros_like(q)) for h in range(hp)], axis=0)
        return _dot_nt(qs, kw) + tbl_ref[:, variant].reshape(hp * GRID_W, NA_SPAN)

    def probs(s):
        return (jnp.exp(s - jnp.max(s, axis=-1, keepdims=True)).astype(BF16),)

    def finish(r, p):
        row0 = jnp.clip(r - NA_ROWS // 2, 0, rows - NA_ROWS)
        vw = v_ref[pl.ds(pl.multiple_of(row0 * GRID_W, GRID_W), NA_SPAN), :]
        o = _dot(p, jnp.concatenate([vw, ones], axis=1))
        o = o[:, :V7X_LANES] * (1.0 / o[:, V7X_LANES:])
        out = o[:GRID_W]
        for h in range(1, hp):
            out = jnp.where(lane_head == h, o[h * GRID_W:(h + 1) * GRID_W], out)
        o_ref[pl.ds(pl.multiple_of(r * GRID_W, GRID_W), GRID_W), :] = out.astype(BF16)

    _pipelined_attention(rows // NA_UNROLL, NA_UNROLL, scores, probs, finish, s_scr, p_scr, ())


def _nbr_attention(q, k, v, rpb):
    batch, seq, width = q.shape
    rows = seq // GRID_W
    n_pairs = width // V7X_LANES
    rep = jnp.repeat(rpb.astype(F32).transpose(0, 2, 1), GRID_W, axis=-1)
    tok = lambda p, b: (b, 0, p)
    tbl_shape = (NA_HEADS_PER_STEP, NA_VARIANTS, GRID_W, NA_SPAN)
    rep_block = (NA_HEADS_PER_STEP, NA_RPB_COLS, NA_RPB_ROWS * GRID_W)
    blocks = 2 * 4 * _nbytes((seq, V7X_LANES), BF16) + 2 * _nbytes((NA_HEADS_PER_STEP, 32, 1024), F32)
    tile = (2, NA_UNROLL, NA_HEADS_PER_STEP * GRID_W, NA_SPAN)
    scratch = _nbytes(tbl_shape, F32) + _nbytes(tile, F32) + _nbytes(tile, BF16)
    temps = 8 * NA_UNROLL * _nbytes(tile[2:], F32)
    return pl.pallas_call(
        functools.partial(_nbr_kernel, rows=rows),
        grid=(n_pairs, batch),
        in_specs=[
            pl.BlockSpec((None, seq, V7X_LANES), tok),
            pl.BlockSpec((None, seq, V7X_LANES), tok),
            pl.BlockSpec((None, seq, V7X_LANES), tok),
            pl.BlockSpec(rep_block, lambda p, b: (p, 0, 0)),
        ],
        out_specs=pl.BlockSpec((None, seq, V7X_LANES), tok),
        out_shape=jax.ShapeDtypeStruct((batch, seq, width), BF16),
        scratch_shapes=[pltpu.VMEM(tbl_shape, F32), pltpu.VMEM(tile, F32), pltpu.VMEM(tile, BF16)],
        compiler_params=pltpu.CompilerParams(
            dimension_semantics=("arbitrary", "arbitrary"),
            vmem_limit_bytes=_vmem_limit(blocks, scratch, temps),
        ),
        name="nbr_attn",
    )(q, k, v, rep)


DIL_HALF = 64
DIL_TQ = 128
DIL_TK = DIL_TQ + 2 * DIL_HALF
DIL_VARIANTS = 3
DIL_STEP_ROWS = 4096
DIL_UNROLL = 4


def _dil_bias_table(group, dilation):
    heads = jnp.arange(NB_HEADS_PER_GROUP, dtype=F32) + group * NB_HEADS_PER_GROUP
    slopes = jnp.exp2(-ALIBI_MAX_EXP * (heads + 1.0) / NB_HEADS)
    offs = jnp.array([0, -DIL_HALF, -2 * DIL_HALF], jnp.int32)
    rel = offs[:, None, None] + jnp.arange(DIL_TK)[None, None, :] - jnp.arange(DIL_TQ)[None, :, None]
    dist = (dilation * jnp.abs(rel)).astype(F32)
    bias = -(slopes[None, :, None, None] * dist[:, None])
    bias = jnp.where((jnp.abs(rel) <= DIL_HALF)[:, None], bias, -jnp.inf)
    return bias.reshape(DIL_VARIANTS, NB_HEADS_PER_GROUP * DIL_TQ, DIL_TK)


def _dil_kernel(q_ref, k_ref, v_ref, bias_ref, o_ref, lse_ref, s_scr, p_scr, m_scr, *, seq_len, stretch):
    nh = NB_HEADS_PER_GROUP
    hpb = V7X_LANES // HEAD_DIM
    ones = jnp.ones((DIL_TK, V7X_LANES), BF16)
    group_lane_head = lax.broadcasted_iota(jnp.int32, (DIL_TQ, GROUP_WIDTH), 1) // HEAD_DIM
    lane_head = lax.broadcasted_iota(jnp.int32, (DIL_TQ, V7X_LANES), 1) // HEAD_DIM
    base = pl.program_id(2) * stretch
    tiles_per_residue = stretch // DIL_TQ

    def window(t):
        r, j = t // tiles_per_residue, t % tiles_per_residue
        qloc = pl.multiple_of(j * DIL_TQ, DIL_TQ)
        qs = base + qloc
        ws = jnp.clip(qs - DIL_HALF, 0, seq_len - DIL_TK)
        variant = (qs - ws) // DIL_HALF
        return r, qloc, pl.multiple_of(ws, DIL_HALF), variant

    def scores(t):
        r, qloc, ws, variant = window(t)
        q = q_ref[r, pl.ds(qloc, DIL_TQ), :]
        kw = k_ref[r, pl.ds(ws, DIL_TK), :]
        qst = jnp.concatenate([jnp.where(group_lane_head == h, q, jnp.zeros_like(q)) for h in range(nh)],
                              axis=0)
        return _dot_nt(qst, kw) + bias_ref[variant]

    def probs(s):
        m = jnp.max(s, axis=-1, keepdims=True)
        return jnp.exp(s - m).astype(BF16), jnp.broadcast_to(m, (s.shape[0], V7X_LANES))

    def finish(t, p, m):
        r, qloc, ws, _ = window(t)
        vw = v_ref[r, pl.ds(ws, DIL_TK), :]
        outs, lses = [], []
        for c in range(GROUP_LANE_BLOCKS):
            rows_c = slice(c * hpb * DIL_TQ, (c + 1) * hpb * DIL_TQ)
            v_aug = jnp.concatenate([vw[:, c * V7X_LANES:(c + 1) * V7X_LANES], ones], axis=1)
            o = _dot(p[rows_c], v_aug)
            l = o[:, V7X_LANES:]
            o = o[:, :V7X_LANES] * (1.0 / l)
            lse = m[rows_c] + jnp.log(l)
            out_c, lse_c = o[:DIL_TQ], lse[:DIL_TQ]
            for h in range(1, hpb):
                rows_h = slice(h * DIL_TQ, (h + 1) * DIL_TQ)
                out_c = jnp.where(lane_head == h, o[rows_h], out_c)
                lse_c = jnp.where(lane_head == h, lse[rows_h], lse_c)
            outs.append(out_c)
            lses.append(lse_c)
        o_ref[r, pl.ds(qloc, DIL_TQ), :] = jnp.concatenate(outs, axis=1).astype(BF16)
        lse_ref[r, pl.ds(qloc, DIL_TQ), :] = jnp.concatenate(lses, axis=1)

    n_tiles = q_ref.shape[0] * tiles_per_residue
    _pipelined_attention(n_tiles // DIL_UNROLL, DIL_UNROLL, scores, probs, finish, s_scr, p_scr, (m_scr,))


def _dil_attention(q, k, v, group):
    batch, dilation, seq_len, width = q.shape
    stretch = min(seq_len, DIL_STEP_ROWS)
    res_blk = DIL_STEP_ROWS // stretch
    assert dilation % res_blk == 0 and seq_len % stretch == 0
    qmap = lambda b, r, s: (b, r, s, 0)
    kvmap = lambda b, r, s: (b, r, 0, 0)
    bias = _dil_bias_table(group, dilation)
    tile = (2, DIL_UNROLL, NB_HEADS_PER_GROUP * DIL_TQ, DIL_TK)
    stat = tile[:3] + (V7X_LANES,)
    blocks = 2 * (2 * _nbytes((DIL_STEP_ROWS, width), BF16) + 2 * _nbytes((res_blk * seq_len, width), BF16)
                  + _nbytes((DIL_STEP_ROWS, width), F32)) + _nbytes(bias.shape, F32)
    scratch = _nbytes(tile, F32) + _nbytes(tile, BF16) + _nbytes(stat, F32)
    temps = 4 * DIL_UNROLL * _nbytes(tile[2:], F32)
    return pl.pallas_call(
        functools.partial(_dil_kernel, seq_len=seq_len, stretch=stretch),
        grid=(batch, dilation // res_blk, seq_len // stretch),
        in_specs=[
            pl.BlockSpec((None, res_blk, stretch, width), qmap),
            pl.BlockSpec((None, res_blk, seq_len, width), kvmap),
            pl.BlockSpec((None, res_blk, seq_len, width), kvmap),
            _resident(bias.shape),
        ],
        out_specs=[pl.BlockSpec((None, res_blk, stretch, width), qmap)] * 2,
        out_shape=[jax.ShapeDtypeStruct(q.shape, BF16), jax.ShapeDtypeStruct(q.shape, F32)],
        scratch_shapes=[pltpu.VMEM(tile, F32), pltpu.VMEM(tile, BF16), pltpu.VMEM(stat, F32)],
        compiler_params=pltpu.CompilerParams(
            dimension_semantics=("arbitrary", "arbitrary", "arbitrary"),
            vmem_limit_bytes=_vmem_limit(blocks, scratch, temps),
        ),
        name=f"dil_attn_g{group}",
    )(q, k, v, bias)


MERGE_TM = 1024
MERGE_CN = 256


def _to_token_order(src_ref, scr_ref, dil, k):
    per_res = SUB_TM // dil
    res_rows = slice(k * per_res, (k + 1) * per_res)
    if dil == 1:
        return src_ref[0, res_rows].astype(F32)
    for r in range(dil):
        for c in range(GROUP_LANE_BLOCKS):
            scr_ref[c, pl.ds(k * SUB_TM + r, per_res, stride=dil), :] = (
                src_ref[r, res_rows, c * V7X_LANES:(c + 1) * V7X_LANES].astype(F32))
    rows = slice(k * SUB_TM, (k + 1) * SUB_TM)
    return jnp.concatenate([scr_ref[c, rows] for c in range(GROUP_LANE_BLOCKS)], axis=-1)


def _merge_kernel(x_ref, pre_g_ref, wga_ref, wgb_ref, oa_ref, wa_ref,
                  o0_ref, o1_ref, o2_ref, l0_ref, l1_ref, l2_ref, wb_ref, wout_ref, post_g_ref, out_ref,
                  h_ref, ob_ref, mg_ref, *scr_refs):
    o_scr, l_scr = scr_refs[:NB_GROUPS], scr_refs[NB_GROUPS:]
    subs = _sub_tiles(x_ref.shape[0])
    for k, rows in enumerate(subs):
        h_ref[rows] = _rmsnorm_f32(x_ref[rows], pre_g_ref[...]).astype(BF16)
        os_ = [_to_token_order(r, s, d, k) for r, s, d in zip((o0_ref, o1_ref, o2_ref), o_scr, DILATIONS)]
        lses = [_to_token_order(r, s, d, k) for r, s, d in zip((l0_ref, l1_ref, l2_ref), l_scr, DILATIONS)]
        mx = jnp.maximum(jnp.maximum(lses[0], lses[1]), lses[2])
        es = [jnp.exp(l - mx) for l in lses]
        inv = 1.0 / (es[0] + es[1] + es[2])
        for g in range(NB_GROUPS):
            ob_ref[rows, g * GROUP_WIDTH:(g + 1) * GROUP_WIDTH] = (os_[g] * (es[g] * inv)).astype(BF16)
        for n in range(D_MODEL // MERGE_CN):
            cols = slice(n * MERGE_CN, (n + 1) * MERGE_CN)
            h = h_ref[rows]
            ya = _dot(oa_ref[rows], wa_ref[:, cols])
            yb = _dot(ob_ref[rows], wb_ref[:, cols])
            merged = jax.nn.sigmoid(_dot(h, wga_ref[:, cols])) * ya + jax.nn.sigmoid(_dot(h, wgb_ref[:, cols])) * yb
            mg_ref[rows, cols] = merged.astype(BF16)
    for rows in subs:
        out_ref[rows] = _dot(mg_ref[rows], wout_ref[...])
    for rows in subs:
        out_ref[rows] = x_ref[rows] + _rmsnorm_f32(out_ref[rows], post_g_ref[...])


def _merge(x, pre_g, wga, wgb, oa, wa, obs, lses, wb, wout, post_g):
    batch, seq, d = x.shape
    tm = MERGE_TM
    tok = lambda b, i: (b, i, 0)
    res = lambda b, i: (b, 0, i, 0)
    tokspec = lambda w: pl.BlockSpec((None, tm, w), tok)
    resspecs = [pl.BlockSpec((None, dil, tm // dil, GROUP_WIDTH), res) for dil in DILATIONS]
    blocks = (2 * 2 * _nbytes((tm, d), F32) + 2 * _nbytes((tm, NA_WIDTH), BF16)
              + 2 * 3 * (_nbytes((tm, GROUP_WIDTH), BF16) + _nbytes((tm, GROUP_WIDTH), F32))
              + _nbytes((3 * d + NA_WIDTH + NB_WIDTH, d), BF16))
    work = [pltpu.VMEM((tm, d), BF16), pltpu.VMEM((tm, NB_WIDTH), BF16), pltpu.VMEM((tm, d), BF16)]
    scratch = (2 * NB_GROUPS * _nbytes((tm, GROUP_WIDTH), F32) + 2 * _nbytes((tm, d), BF16)
               + _nbytes((tm, NB_WIDTH), BF16))
    temps = 3 * _nbytes((SUB_TM, d), F32)
    return pl.pallas_call(
        _merge_kernel,
        grid=(batch, seq // tm),
        in_specs=[
            tokspec(d), _resident((1, d)), _resident((d, d)), _resident((d, d)),
            tokspec(NA_WIDTH), _resident((NA_WIDTH, d)),
            *resspecs, *resspecs,
            _resident((NB_WIDTH, d)), _resident((d, d)), _resident((1, d)),
        ],
        out_specs=tokspec(d),
        out_shape=jax.ShapeDtypeStruct((batch, seq, d), F32),
        scratch_shapes=work + [pltpu.VMEM((GROUP_LANE_BLOCKS, tm, V7X_LANES), F32)] * (2 * NB_GROUPS),
        compiler_params=pltpu.CompilerParams(
            dimension_semantics=("arbitrary", "arbitrary"),
            vmem_limit_bytes=_vmem_limit(blocks, scratch, temps),
        ),
        name="merge",
    )(x, pre_g, wga, wgb, oa, wa, *obs, *lses, wb, wout, post_g)


def kernel(x, ffn1_pre_g, ffn1_w_gate, ffn1_w_up, ffn1_w_down, ffn1_post_g, mix_pre_g, w_in, na_rpb, w_branch_a, w_branch_b, w_out, mix_post_g, ffn2_pre_g, ffn2_w_gate, ffn2_w_up, ffn2_w_down, ffn2_post_g):
    batch, seq, d = x.shape
    depth = ffn1_pre_g.shape[0]
    gate_off = QKV_TOTAL
    for window, dilation in DIL_PAIRS:
        assert window // (2 * dilation) == DIL_HALF
    for l in range(depth):
        x = _ffn(x.reshape(batch * seq, d), ffn1_pre_g[l][None], ffn1_w_gate[l], ffn1_w_up[l],
                 ffn1_w_down[l], ffn1_post_g[l][None])
        x = x.reshape(batch, seq, d)
        w_in_l = w_in[l].astype(BF16)
        qa, ka, va, qb, kb, vb = _qkv(x, mix_pre_g[l][None], w_in_l[:, :gate_off])
        oa = _nbr_attention(qa, ka, va, na_rpb[l])
        obs, lses = [], []
        for g in range(NB_GROUPS):
            o_g, lse_g = _dil_attention(qb[g], kb[g], vb[g], g)
            obs.append(o_g)
            lses.append(lse_g)
        x = _merge(x, mix_pre_g[l][None], w_in_l[:, gate_off:gate_off + d], w_in_l[:, gate_off + d:],
                   oa, w_branch_a[l].astype(BF16), obs, lses, w_branch_b[l].astype(BF16),
                   w_out[l].astype(BF16), mix_post_g[l][None])
        x = _ffn(x.reshape(batch * seq, d), ffn2_pre_g[l][None], ffn2_w_gate[l], ffn2_w_up[l],
                 ffn2_w_down[l], ffn2_post_g[l][None])
        x = x.reshape(batch, seq, d)
    return x
```

```python
import functools
import math

import jax
import jax.numpy as jnp
from jax import lax
from jax.experimental import pallas as pl
from jax.experimental.pallas import tpu as pltpu

D_MODEL = 1024
HEAD_DIM = 64
NA_HEADS = 8
NA_WIDTH = NA_HEADS * HEAD_DIM
NA_ROWS = 8
NA_COLS = 16
GRID_W = 64
DIL_PAIRS = ((128, 1), (512, 4), (2048, 16))
DILATIONS = tuple(d for _, d in DIL_PAIRS)
NB_GROUPS = len(DIL_PAIRS)
NB_HEADS_PER_GROUP = 4
NB_HEADS = NB_GROUPS * NB_HEADS_PER_GROUP
NB_WIDTH = NB_HEADS * HEAD_DIM
GROUP_WIDTH = NB_HEADS_PER_GROUP * HEAD_DIM
ALIBI_MAX_EXP = 8.0
D_FF = 2816
NORM_EPS = 1e-6
ATTN_SCALE = HEAD_DIM ** -0.5

V7X_LANES = 128
GROUP_LANE_BLOCKS = GROUP_WIDTH // V7X_LANES
V7X_VMEM_BYTES = 64 * 1024 * 1024
V7X_VMEM_RESERVE = 6 * 1024 * 1024

BF16 = jnp.bfloat16
F32 = jnp.float32


def _vmem_limit(block_bytes, scratch_bytes, temp_bytes):
    need = block_bytes + scratch_bytes + temp_bytes
    assert need <= V7X_VMEM_BYTES - V7X_VMEM_RESERVE, need
    return int(need)


def _nbytes(shape, dtype):
    return math.prod(shape) * jnp.dtype(dtype).itemsize


def _resident(shape):
    nd = len(shape)
    return pl.BlockSpec(shape, lambda *_: (0,) * nd, pipeline_mode=pl.Buffered(1))


def _rmsnorm_f32(x, g):
    return x * lax.rsqrt(jnp.mean(x * x, axis=-1, keepdims=True) + NORM_EPS) * g


def _dot(a, b):
    return jnp.dot(a, b, preferred_element_type=F32)


def _dot_nt(a, b):
    return lax.dot_general(a, b, (((1,), (1,)), ((), ())), preferred_element_type=F32)


def _pipelined_attention(n_groups, unroll, scores, probs, finish, s_scr, p_scr, stat_scrs):
    assert n_groups % 2 == 0 and n_groups >= 2
    tiles = lambda i: [i * unroll + u for u in range(unroll)]

    def stage_scores(i, slot):
        for u, t in enumerate(tiles(i)):
            s_scr[slot, u] = scores(t)

    def stage_probs(slot):
        for u in range(unroll):
            p, *stats = probs(s_scr[slot, u])
            p_scr[slot, u] = p
            for ref, stat in zip(stat_scrs, stats):
                ref[slot, u] = stat

    def stage_finish(i, slot):
        for u, t in enumerate(tiles(i)):
            finish(t, p_scr[slot, u], *[ref[slot, u] for ref in stat_scrs])

    def step(i, parity):
        stage_scores(i + 1, 1 - parity)
        stage_finish(i - 1, 1 - parity)
        stage_probs(parity)

    stage_scores(0, 0)
    stage_scores(1, 1)
    stage_probs(0)

    def body(i, carry):
        @pl.when(i % 2 == 1)
        def _():
            step(i, 1)

        @pl.when(i % 2 == 0)
        def _():
            step(i, 0)

        return carry

    lax.fori_loop(1, n_groups - 1, body, 0)
    stage_finish(n_groups - 2, 0)
    stage_probs(1)
    stage_finish(n_groups - 1, 1)


FFN_TM = 512
SUB_TM = 512
FFN_CK = 256


def _sub_tiles(tm):
    return [slice(k * SUB_TM, (k + 1) * SUB_TM) for k in range(tm // SUB_TM)]


def _ffn_kernel(x_ref, pre_g_ref, wg_ref, wu_ref, wd_ref, post_g_ref, o_ref, h_ref, a_ref):
    h_ref[...] = _rmsnorm_f32(x_ref[...], pre_g_ref[...]).astype(BF16)
    chunks = [slice(c * FFN_CK, (c + 1) * FFN_CK) for c in range(D_FF // FFN_CK)]
    for cols in chunks:
        h = h_ref[...]
        g = _dot(h, wg_ref[:, cols].astype(BF16))
        u = _dot(h, wu_ref[:, cols].astype(BF16))
        a_ref[:, cols] = (g * jax.nn.sigmoid(g) * u).astype(BF16)
    f = None
    for cols in chunks:
        part = _dot(a_ref[:, cols], wd_ref[cols, :].astype(BF16))
        f = part if f is None else f + part
    o_ref[...] = x_ref[...] + 0.5 * _rmsnorm_f32(f, post_g_ref[...])


def _ffn(x, pre_g, wg, wu, wd, post_g):
    n, d = x.shape
    tm = FFN_TM
    row = lambda i: (i, 0)
    blocks = 2 * 2 * _nbytes((tm, d), F32) + 3 * _nbytes((d, D_FF), wg.dtype) + 2 * _nbytes((1, d), F32)
    scratch = _nbytes((tm, d), BF16) + _nbytes((tm, D_FF), BF16)
    temps = 4 * _nbytes((SUB_TM, d), F32)
    return pl.pallas_call(
        _ffn_kernel,
        grid=(n // tm,),
        in_specs=[
            pl.BlockSpec((tm, d), row),
            _resident((1, d)),
            _resident((d, D_FF)),
            _resident((d, D_FF)),
            _resident((D_FF, d)),
            _resident((1, d)),
        ],
        out_specs=pl.BlockSpec((tm, d), row),
        out_shape=jax.ShapeDtypeStruct((n, d), F32),
        scratch_shapes=[pltpu.VMEM((tm, d), BF16), pltpu.VMEM((tm, D_FF), BF16)],
        compiler_params=pltpu.CompilerParams(
            dimension_semantics=("arbitrary",),
            vmem_limit_bytes=_vmem_limit(blocks, scratch, temps),
        ),
        name="ffn",
    )(x, pre_g, wg, wu, wd, post_g)


QKV_TM = 1024
QKV_TOTAL = 3 * NA_WIDTH + 3 * NB_WIDTH
N_DIL_OUTS = 3 * NB_GROUPS
STAGED_OUTS = [i for i in range(N_DIL_OUTS) if DILATIONS[i % NB_GROUPS] > 1]
N_STAGED_OUTS = len(STAGED_OUTS)


def _qkv_kernel(x_ref, g_ref, w_ref, *refs):
    na_refs, dil_refs, y_refs = refs[:3], refs[3:3 + N_DIL_OUTS], refs[3 + N_DIL_OUTS:]
    for k, rows in enumerate(_sub_tiles(x_ref.shape[0])):
        h = _rmsnorm_f32(x_ref[rows], g_ref[...]).astype(BF16)
        for idx, o_ref in enumerate(na_refs):
            y = _dot(h, w_ref[:, idx * NA_WIDTH:(idx + 1) * NA_WIDTH].astype(BF16))
            if idx == 0:
                y = y * ATTN_SCALE
            o_ref[rows] = y.astype(BF16)
        for idx, o_ref in enumerate(dil_refs):
            which, g = divmod(idx, NB_GROUPS)
            dil = DILATIONS[g]
            off = 3 * NA_WIDTH + which * NB_WIDTH + g * GROUP_WIDTH
            y = _dot(h, w_ref[:, off:off + GROUP_WIDTH].astype(BF16))
            if which == 0:
                y = y * ATTN_SCALE
            if dil == 1:
                o_ref[0, rows] = y.astype(BF16)
                continue
            y_ref = y_refs[k * N_STAGED_OUTS + STAGED_OUTS.index(idx)]
            per_res = SUB_TM // dil
            for c in range(GROUP_LANE_BLOCKS):
                y_ref[c] = y[:, c * V7X_LANES:(c + 1) * V7X_LANES]
            for r in range(dil):
                for c in range(GROUP_LANE_BLOCKS):
                    o_ref[r, k * per_res:(k + 1) * per_res, c * V7X_LANES:(c + 1) * V7X_LANES] = (
                        y_ref[c, pl.ds(r, per_res, stride=dil), :].astype(BF16))


def _qkv(x, g, w_in):
    batch, seq, d = x.shape
    tm = QKV_TM
    tok = lambda b, i: (b, i, 0)
    res = lambda b, i: (b, 0, i, 0)
    out_specs = [pl.BlockSpec((None, tm, NA_WIDTH), tok)] * 3
    out_shape = [jax.ShapeDtypeStruct((batch, seq, NA_WIDTH), BF16)] * 3
    for _ in range(3):
        for dil in DILATIONS:
            out_specs.append(pl.BlockSpec((None, dil, tm // dil, GROUP_WIDTH), res))
            out_shape.append(jax.ShapeDtypeStruct((batch, dil, seq // dil, GROUP_WIDTH), BF16))
    blocks = 2 * _nbytes((tm, d), F32) + _nbytes((d, QKV_TOTAL), w_in.dtype) + 2 * _nbytes((tm, QKV_TOTAL), BF16)
    stage =(GROUP_LANE_BLOCKS, SUB_TM, V7X_LANES)
    n_stage = N_STAGED_OUTS * (tm // SUB_TM)
    scratch = n_stage * _nbytes(stage, F32)
    temps = 2 * _nbytes((SUB_TM, d), F32) + 2 * _nbytes((SUB_TM, NB_WIDTH), F32)
    outs = pl.pallas_call(
        _qkv_kernel,
        grid=(batch, seq // tm),
        in_specs=[pl.BlockSpec((None, tm, d), tok), _resident((1, d)), _resident((d, QKV_TOTAL))],
        out_specs=out_specs,
        out_shape=out_shape,
        scratch_shapes=[pltpu.VMEM(stage, F32)] * n_stage,
        compiler_params=pltpu.CompilerParams(
            dimension_semantics=("arbitrary", "arbitrary"),
            vmem_limit_bytes=_vmem_limit(blocks, scratch, temps),
        ),
        name="qkv",
    )(x, g, w_in)
    qa, ka, va = outs[:3]
    qb, kb, vb = (outs[3 + w * NB_GROUPS:3 + (w + 1) * NB_GROUPS] for w in range(3))
    return qa, ka, va, qb, kb, vb


NA_HEADS_PER_STEP = V7X_LANES // HEAD_DIM
NA_SPAN = NA_ROWS * GRID_W
NA_VARIANTS = NA_ROWS
NA_RPB_ROWS = 2 * NA_ROWS - 1
NA_RPB_COLS = 2 * NA_COLS - 1
NA_UNROLL = 8


def _nbr_build_bias(rep_ref, tbl_ref):
    width = NA_RPB_ROWS * GRID_W
    qc = lax.broadcasted_iota(jnp.int32, (GRID_W, width), 0)
    kc = lax.broadcasted_iota(jnp.int32, (GRID_W, width), 1) % GRID_W
    diff = kc - qc
    col0 = jnp.clip(qc - NA_COLS // 2, 0, GRID_W - NA_COLS)
    col_ok = (kc >= col0) & (kc < col0 + NA_COLS)
    for h in range(NA_HEADS_PER_STEP):
        w = jnp.full((GRID_W, width), -jnp.inf, F32)
        for k in range(NA_RPB_COLS):
            w = jnp.where(diff == k - (NA_COLS - 1), rep_ref[h, k:k + 1, :], w)
        w = jnp.where(col_ok, w, -jnp.inf)
        for variant in range(NA_VARIANTS):
            tbl_ref[h, variant] = w[:, variant * GRID_W:variant * GRID_W + NA_SPAN]


def _nbr_kernel(q_ref, k_ref, v_ref, rep_ref, o_ref, tbl_ref, s_scr, p_scr, *, rows):
    hp = NA_HEADS_PER_STEP
    ones = jnp.ones((NA_SPAN, V7X_LANES), BF16)

    @pl.when(pl.program_id(1) == 0)
    def _():
        _nbr_build_bias(rep_ref, tbl_ref)

    lane_head = lax.broadcasted_iota(jnp.int32, (GRID_W, V7X_LANES), 1) // HEAD_DIM

    def scores(r):
        row0 = jnp.clip(r - NA_ROWS // 2, 0, rows - NA_ROWS)
        variant = row0 - r + (NA_ROWS - 1)
        q = q_ref[pl.ds(pl.multiple_of(r * GRID_W, GRID_W), GRID_W), :]
        kw = k_ref[pl.ds(pl.multiple_of(row0 * GRID_W, GRID_W), NA_SPAN), :]
        qs = jnp.concatenate([jnp.where(lane_head == h, q, jnp.zeros_like(q)) for h in range(hp)], axis=0)
        return _dot_nt(qs, kw) + tbl_ref[:, variant].reshape(hp * GRID_W, NA_SPAN)

    def probs(s):
        return (jnp.exp(s - jnp.max(s, axis=-1, keepdims=True)).astype(BF16),)

    def finish(r, p):
        row0 = jnp.clip(r - NA_ROWS // 2, 0, rows - NA_ROWS)
        vw = v_ref[pl.ds(pl.multiple_of(row0 * GRID_W, GRID_W), NA_SPAN), :]
        o = _dot(p, jnp.concatenate([vw, ones], axis=1))
        o = o[:, :V7X_LANES] * (1.0 / o[:, V7X_LANES:])
        out = o[:GRID_W]
        for h in range(1, hp):
            out = jnp.where(lane_head == h, o[h * GRID_W:(h + 1) * GRID_W], out)
        o_ref[pl.ds(pl.multiple_of(r * GRID_W, GRID_W), GRID_W), :] = out.astype(BF16)

    _pipelined_attention(rows // NA_UNROLL, NA_UNROLL, scores, probs, finish, s_scr, p_scr, ())


def _nbr_attention(q, k, v, rpb):
    batch, seq, width = q.shape
    rows = seq // GRID_W
    n_pairs = width // V7X_LANES
    rep = jnp.repeat(rpb.astype(F32).transpose(0, 2, 1), GRID_W, axis=-1)
    tok = lambda p, b: (b, 0, p)
    tbl_shape = (NA_HEADS_PER_STEP, NA_VARIANTS, GRID_W, NA_SPAN)
    rep_block = (NA_HEADS_PER_STEP, NA_RPB_COLS, NA_RPB_ROWS * GRID_W)
    blocks = 2 * 4 * _nbytes((seq, V7X_LANES), BF16) + 2 * _nbytes((NA_HEADS_PER_STEP, 32, 1024), F32)
    tile = (2, NA_UNROLL, NA_HEADS_PER_STEP * GRID_W, NA_SPAN)
    scratch = _nbytes(tbl_shape, F32) + _nbytes(tile, F32) + _nbytes(tile, BF16)
    temps = 8 * NA_UNROLL * _nbytes(tile[2:], F32)
    return pl.pallas_call(
        functools.partial(_nbr_kernel, rows=rows),
        grid=(n_pairs, batch),
        in_specs=[
            pl.BlockSpec((None, seq, V7X_LANES), tok),
            pl.BlockSpec((None, seq, V7X_LANES), tok),
            pl.BlockSpec((None, seq, V7X_LANES), tok),
            pl.BlockSpec(rep_block, lambda p, b: (p, 0, 0)),
        ],
        out_specs=pl.BlockSpec((None, seq, V7X_LANES), tok),
        out_shape=jax.ShapeDtypeStruct((batch, seq, width), BF16),
        scratch_shapes=[pltpu.VMEM(tbl_shape, F32), pltpu.VMEM(tile, F32), pltpu.VMEM(tile, BF16)],
        compiler_params=pltpu.CompilerParams(
            dimension_semantics=("arbitrary", "arbitrary"),
            vmem_limit_bytes=_vmem_limit(blocks, scratch, temps),
        ),
        name="nbr_attn",
    )(q, k, v, rep)


DIL_HALF = 64
DIL_TQ = 128
DIL_TK = DIL_TQ + 2 * DIL_HALF
DIL_VARIANTS = 3
DIL_STEP_ROWS = 4096
DIL_UNROLL = 8


def _dil_bias_table(group, dilation):
    heads = jnp.arange(NB_HEADS_PER_GROUP, dtype=F32) + group * NB_HEADS_PER_GROUP
    slopes = jnp.exp2(-ALIBI_MAX_EXP * (heads + 1.0) / NB_HEADS)
    offs = jnp.array([0, -DIL_HALF, -2 * DIL_HALF], jnp.int32)
    rel = offs[:, None, None] + jnp.arange(DIL_TK)[None, None, :] - jnp.arange(DIL_TQ)[None, :, None]
    dist = (dilation * jnp.abs(rel)).astype(F32)
    bias = -(slopes[None, :, None, None] * dist[:, None])
    bias = jnp.where((jnp.abs(rel) <= DIL_HALF)[:, None], bias, -jnp.inf)
    return bias.reshape(DIL_VARIANTS, NB_HEADS_PER_GROUP * DIL_TQ, DIL_TK)


def _dil_kernel(q_ref, k_ref, v_ref, bias_ref, o_ref, lse_ref, s_scr, p_scr, m_scr, *, seq_len, stretch):
    nh = NB_HEADS_PER_GROUP
    hpb = V7X_LANES // HEAD_DIM
    ones = jnp.ones((DIL_TK, V7X_LANES), BF16)
    group_lane_head = lax.broadcasted_iota(jnp.int32, (DIL_TQ, GROUP_WIDTH), 1) // HEAD_DIM
    lane_head = lax.broadcasted_iota(jnp.int32, (DIL_TQ, V7X_LANES), 1) // HEAD_DIM
    base = pl.program_id(2) * stretch
    tiles_per_residue = stretch // DIL_TQ

    def window(t):
        r, j = t // tiles_per_residue, t % tiles_per_residue
        qloc = pl.multiple_of(j * DIL_TQ, DIL_TQ)
        qs = base + qloc
        ws = jnp.clip(qs - DIL_HALF, 0, seq_len - DIL_TK)
        variant = (qs - ws) // DIL_HALF
        return r, qloc, pl.multiple_of(ws, DIL_HALF), variant

    def scores(t):
        r, qloc, ws, variant = window(t)
        q = q_ref[r, pl.ds(qloc, DIL_TQ), :]
        kw = k_ref[r, pl.ds(ws, DIL_TK), :]
        qst = jnp.concatenate([jnp.where(group_lane_head == h, q, jnp.zeros_like(q)) for h in range(nh)],
                              axis=0)
        return _dot_nt(qst, kw) + bias_ref[variant]

    def probs(s):
        m = jnp.max(s, axis=-1, keepdims=True)
        return jnp.exp(s - m).astype(BF16), jnp.broadcast_to(m, (s.shape[0], V7X_LANES))

    def finish(t, p, m):
        r, qloc, ws, _ = window(t)
        vw = v_ref[r, pl.ds(ws, DIL_TK), :]
        outs, lses = [], []
        for c in range(GROUP_LANE_BLOCKS):
            rows_c = slice(c * hpb * DIL_TQ, (c + 1) * hpb * DIL_TQ)
            v_aug = jnp.concatenate([vw[:, c * V7X_LANES:(c + 1) * V7X_LANES], ones], axis=1)
            o = _dot(p[rows_c], v_aug)
            l = o[:, V7X_LANES:]
            o = o[:, :V7X_LANES] * (1.0 / l)
            lse = m[rows_c] + jnp.log(l)
            out_c, lse_c = o[:DIL_TQ], lse[:DIL_TQ]
            for h in range(1, hpb):
                rows_h = slice(h * DIL_TQ, (h + 1) * DIL_TQ)
                out_c = jnp.where(lane_head == h, o[rows_h], out_c)
                lse_c = jnp.where(lane_head == h, lse[rows_h], lse_c)
            outs.append(out_c)
            lses.append(lse_c)
        o_ref[r, pl.ds(qloc, DIL_TQ), :] = jnp.concatenate(outs, axis=1).astype(BF16)
        lse_ref[r, pl.ds(qloc, DIL_TQ), :] = jnp.concatenate(lses, axis=1)

    n_tiles = q_ref.shape[0] * tiles_per_residue
    _pipelined_attention(n_tiles // DIL_UNROLL, DIL_UNROLL, scores, probs, finish, s_scr, p_scr, (m_scr,))


def _dil_attention(q, k, v, group):
    batch, dilation, seq_len, width = q.shape
    stretch = min(seq_len, DIL_STEP_ROWS)
    res_blk = DIL_STEP_ROWS // stretch
    assert dilation % res_blk == 0 and seq_len % stretch == 0
    qmap = lambda b, r, s: (b, r, s, 0)
    kvmap = lambda b, r, s: (b, r, 0, 0)
    bias = _dil_bias_table(group, dilation)
    tile = (2, DIL_UNROLL, NB_HEADS_PER_GROUP * DIL_TQ, DIL_TK)
    stat = tile[:3] + (V7X_LANES,)
    kv_buffers = 1 if seq_len // stretch > 1 else 2
    kv_spec = pl.BlockSpec((None, res_blk, seq_len, width), kvmap, pipeline_mode=pl.Buffered(kv_buffers))
    blocks = (2 * (2 * _nbytes((DIL_STEP_ROWS, width), BF16) + _nbytes((DIL_STEP_ROWS, width), F32))
              + kv_buffers * 2 * _nbytes((res_blk * seq_len, width), BF16) + _nbytes(bias.shape, F32))
    scratch = _nbytes(tile, F32) + _nbytes(tile, BF16) + _nbytes(stat, F32)
    temps = 2 * DIL_UNROLL * _nbytes(tile[2:], F32)
    return pl.pallas_call(
        functools.partial(_dil_kernel, seq_len=seq_len, stretch=stretch),
        grid=(batch, dilation // res_blk, seq_len // stretch),
        in_specs=[
            pl.BlockSpec((None, res_blk, stretch, width), qmap),
            kv_spec,
            kv_spec,
            _resident(bias.shape),
        ],
        out_specs=[pl.BlockSpec((None, res_blk, stretch, width), qmap)] * 2,
        out_shape=[jax.ShapeDtypeStruct(q.shape, BF16), jax.ShapeDtypeStruct(q.shape, F32)],
        scratch_shapes=[pltpu.VMEM(tile, F32), pltpu.VMEM(tile, BF16), pltpu.VMEM(stat, F32)],
        compiler_params=pltpu.CompilerParams(
            dimension_semantics=("arbitrary", "arbitrary", "arbitrary"),
            vmem_limit_bytes=_vmem_limit(blocks, scratch, temps),
        ),
        name=f"dil_attn_g{group}",
    )(q, k, v, bias)


MERGE_TM = 1024
MERGE_CN = 256


def _to_token_order(src_ref, scr_ref, dil, k):
    per_res = SUB_TM // dil
    res_rows = slice(k * per_res, (k + 1) * per_res)
    if dil == 1:
        return src_ref[0, res_rows].astype(F32)
    for r in range(dil):
        for c in range(GROUP_LANE_BLOCKS):
            scr_ref[c, pl.ds(k * SUB_TM + r, per_res, stride=dil), :] = (
                src_ref[r, res_rows, c * V7X_LANES:(c + 1) * V7X_LANES].astype(F32))
    rows = slice(k * SUB_TM, (k + 1) * SUB_TM)
    return jnp.concatenate([scr_ref[c, rows] for c in range(GROUP_LANE_BLOCKS)], axis=-1)


def _merge_kernel(x_ref, pre_g_ref, wgate_ref, oa_ref, wa_ref,
                  o0_ref, o1_ref, o2_ref, l0_ref, l1_ref, l2_ref, wb_ref, wout_ref, post_g_ref, out_ref,
                  h_ref, ob_ref, mg_ref, *scr_refs):
    o_scr, l_scr = scr_refs[:NB_GROUPS], scr_refs[NB_GROUPS:]
    subs = _sub_tiles(x_ref.shape[0])
    for k, rows in enumerate(subs):
        h_ref[rows] = _rmsnorm_f32(x_ref[rows], pre_g_ref[...]).astype(BF16)
        os_ = [_to_token_order(r, s, d, k) for r, s, d in zip((o0_ref, o1_ref, o2_ref), o_scr, DILATIONS)]
        lses = [_to_token_order(r, s, d, k) for r, s, d in zip((l0_ref, l1_ref, l2_ref), l_scr, DILATIONS)]
        mx = jnp.maximum(jnp.maximum(lses[0], lses[1]), lses[2])
        es = [jnp.exp(l - mx) for l in lses]
        inv = 1.0 / (es[0] + es[1] + es[2])
        for g in range(NB_GROUPS):
            ob_ref[rows, g * GROUP_WIDTH:(g + 1) * GROUP_WIDTH] = (os_[g] * (es[g] * inv)).astype(BF16)
        for n in range(D_MODEL // MERGE_CN):
            cols = slice(n * MERGE_CN, (n + 1) * MERGE_CN)
            h = h_ref[rows]
            ya = _dot(oa_ref[rows], wa_ref[:, cols])
            yb = _dot(ob_ref[rows], wb_ref[:, cols])
            cols_b = slice(D_MODEL + n * MERGE_CN, D_MODEL + (n + 1) * MERGE_CN)
            merged = (jax.nn.sigmoid(_dot(h, wgate_ref[:, cols])) * ya
                      + jax.nn.sigmoid(_dot(h, wgate_ref[:, cols_b])) * yb)
            mg_ref[rows, cols] = merged.astype(BF16)
    for rows in subs:
        out_ref[rows] = _dot(mg_ref[rows], wout_ref[...])
    for rows in subs:
        out_ref[rows] = x_ref[rows] + _rmsnorm_f32(out_ref[rows], post_g_ref[...])


def _merge(x, pre_g, wgate, oa, wa, obs, lses, wb, wout, post_g):
    batch, seq, d = x.shape
    tm = MERGE_TM
    tok = lambda b, i: (b, i, 0)
    res = lambda b, i: (b, 0, i, 0)
    tokspec = lambda w: pl.BlockSpec((None, tm, w), tok)
    resspecs = [pl.BlockSpec((None, dil, tm // dil, GROUP_WIDTH), res) for dil in DILATIONS]
    blocks = (2 * 2 * _nbytes((tm, d), F32) + 2 * _nbytes((tm, NA_WIDTH), BF16)
              + 2 * 3 * (_nbytes((tm, GROUP_WIDTH), BF16) + _nbytes((tm, GROUP_WIDTH), F32))
              + _nbytes((3 * d + NA_WIDTH + NB_WIDTH, d), BF16))
    work = [pltpu.VMEM((tm, d), BF16), pltpu.VMEM((tm, NB_WIDTH), BF16), pltpu.VMEM((tm, d), BF16)]
    scratch = (2 * NB_GROUPS * _nbytes((tm, GROUP_WIDTH), F32) + 2 * _nbytes((tm, d), BF16)
               + _nbytes((tm, NB_WIDTH), BF16))
    temps = 3 * _nbytes((SUB_TM, d), F32)
    return pl.pallas_call(
        _merge_kernel,
        grid=(batch, seq // tm),
        in_specs=[
            tokspec(d), _resident((1, d)), _resident((d, 2 * d)),
            tokspec(NA_WIDTH), _resident((NA_WIDTH, d)),
            *resspecs, *resspecs,
            _resident((NB_WIDTH, d)), _resident((d, d)), _resident((1, d)),
        ],
        out_specs=tokspec(d),
        out_shape=jax.ShapeDtypeStruct((batch, seq, d), F32),
        scratch_shapes=work + [pltpu.VMEM((GROUP_LANE_BLOCKS, tm, V7X_LANES), F32)] * (2 * NB_GROUPS),
        compiler_params=pltpu.CompilerParams(
            dimension_semantics=("arbitrary", "arbitrary"),
            vmem_limit_bytes=_vmem_limit(blocks, scratch, temps),
        ),
        name="merge",
    )(x, pre_g, wgate, oa, wa, *obs, *lses, wb, wout, post_g)


def kernel(x, ffn1_pre_g, ffn1_w_gate, ffn1_w_up, ffn1_w_down, ffn1_post_g, mix_pre_g, w_in, na_rpb, w_branch_a, w_branch_b, w_out, mix_post_g, ffn2_pre_g, ffn2_w_gate, ffn2_w_up, ffn2_w_down, ffn2_post_g):
    batch, seq, d = x.shape
    depth = ffn1_pre_g.shape[0]
    gate_off = QKV_TOTAL
    for window, dilation in DIL_PAIRS:
        assert window // (2 * dilation) == DIL_HALF
    for l in range(depth):
        x = _ffn(x.reshape(batch * seq, d), ffn1_pre_g[l][None], ffn1_w_gate[l], ffn1_w_up[l],
                 ffn1_w_down[l], ffn1_post_g[l][None])
        x = x.reshape(batch, seq, d)
        w_gates = w_in[l][:, gate_off:].astype(BF16)
        qa, ka, va, qb, kb, vb = _qkv(x, mix_pre_g[l][None], w_in[l])
        oa = _nbr_attention(qa, ka, va, na_rpb[l])
        obs, lses = [], []
        for g in range(NB_GROUPS):
            o_g, lse_g = _dil_attention(qb[g], kb[g], vb[g], g)
            obs.append(o_g)
            lses.append(lse_g)
        x = _merge(x, mix_pre_g[l][None], w_gates,
                   oa, w_branch_a[l].astype(BF16), obs, lses, w_branch_b[l].astype(BF16),
                   w_out[l].astype(BF16), mix_post_g[l][None])
        x = _ffn(x.reshape(batch * seq, d), ffn2_pre_g[l][None], ffn2_w_gate[l], ffn2_w_up[l],
                 ffn2_w_down[l], ffn2_post_g[l][None])
        x = x.reshape(batch, seq, d)
    return x
```

```python
import functools
import math

import jax
import jax.numpy as jnp
from jax import lax
from jax.experimental import pallas as pl
from jax.experimental.pallas import tpu as pltpu

D_MODEL = 1024
HEAD_DIM = 64
NA_HEADS = 8
NA_WIDTH = NA_HEADS * HEAD_DIM
NA_ROWS = 8
NA_COLS = 16
GRID_W = 64
DIL_PAIRS = ((128, 1), (512, 4), (2048, 16))
DILATIONS = tuple(d for _, d in DIL_PAIRS)
NB_GROUPS = len(DIL_PAIRS)
NB_HEADS_PER_GROUP = 4
NB_HEADS = NB_GROUPS * NB_HEADS_PER_GROUP
NB_WIDTH = NB_HEADS * HEAD_DIM
GROUP_WIDTH = NB_HEADS_PER_GROUP * HEAD_DIM
ALIBI_MAX_EXP = 8.0
D_FF = 2816
NORM_EPS = 1e-6
ATTN_SCALE = HEAD_DIM ** -0.5

V7X_LANES = 128
GROUP_LANE_BLOCKS = GROUP_WIDTH // V7X_LANES
V7X_VMEM_BYTES = 64 * 1024 * 1024
V7X_VMEM_RESERVE = 6 * 1024 * 1024

BF16 = jnp.bfloat16
F32 = jnp.float32


def _vmem_limit(block_bytes, scratch_bytes, temp_bytes):
    need = block_bytes + scratch_bytes + temp_bytes
    assert need <= V7X_VMEM_BYTES - V7X_VMEM_RESERVE, need
    return int(need)


def _nbytes(shape, dtype):
    return math.prod(shape) * jnp.dtype(dtype).itemsize


def _resident(shape):
    nd = len(shape)
    return pl.BlockSpec(shape, lambda *_: (0,) * nd, pipeline_mode=pl.Buffered(1))


def _rmsnorm_f32(x, g):
    return x * lax.rsqrt(jnp.mean(x * x, axis=-1, keepdims=True) + NORM_EPS) * g


def _dot(a, b):
    return jnp.dot(a, b, preferred_element_type=F32)


def _dot_nt(a, b):
    return lax.dot_general(a, b, (((1,), (1,)), ((), ())), preferred_element_type=F32)


def _pipelined_attention(n_groups, unroll, scores, probs, finish, s_scr, p_scr, stat_scrs):
    assert n_groups % 2 == 0 and n_groups >= 2
    tiles = lambda i: [i * unroll + u for u in range(unroll)]

    def stage_scores(i, slot):
        for u, t in enumerate(tiles(i)):
            s_scr[slot, u] = scores(t)

    def stage_probs(slot):
        for u in range(unroll):
            p, *stats = probs(s_scr[slot, u])
            p_scr[slot, u] = p
            for ref, stat in zip(stat_scrs, stats):
                ref[slot, u] = stat

    def stage_finish(i, slot):
        for u, t in enumerate(tiles(i)):
            finish(t, p_scr[slot, u], *[ref[slot, u] for ref in stat_scrs])

    def step(i, parity):
        stage_scores(i + 1, 1 - parity)
        stage_finish(i - 1, 1 - parity)
        stage_probs(parity)

    stage_scores(0, 0)
    stage_scores(1, 1)
    stage_probs(0)

    def body(i, carry):
        @pl.when(i % 2 == 1)
        def _():
            step(i, 1)

        @pl.when(i % 2 == 0)
        def _():
            step(i, 0)

        return carry

    lax.fori_loop(1, n_groups - 1, body, 0)
    stage_finish(n_groups - 2, 0)
    stage_probs(1)
    stage_finish(n_groups - 1, 1)


FFN_TM = 512
SUB_TM = 512
FFN_CK = 256


def _sub_tiles(tm):
    return [slice(k * SUB_TM, (k + 1) * SUB_TM) for k in range(tm // SUB_TM)]


def _ffn_kernel(x_ref, pre_g_ref, wg_ref, wu_ref, wd_ref, post_g_ref, o_ref, h_ref, a_ref):
    h_ref[...] = _rmsnorm_f32(x_ref[...], pre_g_ref[...]).astype(BF16)
    chunks = [slice(c * FFN_CK, (c + 1) * FFN_CK) for c in range(D_FF // FFN_CK)]
    for cols in chunks:
        h = h_ref[...]
        g = _dot(h, wg_ref[:, cols].astype(BF16))
        u = _dot(h, wu_ref[:, cols].astype(BF16))
        a_ref[:, cols] = (g * jax.nn.sigmoid(g) * u).astype(BF16)
    f = None
    for cols in chunks:
        part = _dot(a_ref[:, cols], wd_ref[cols, :].astype(BF16))
        f = part if f is None else f + part
    o_ref[...] = x_ref[...] + 0.5 * _rmsnorm_f32(f, post_g_ref[...])


def _ffn(x, pre_g, wg, wu, wd, post_g):
    n, d = x.shape
    tm = FFN_TM
    row = lambda i: (i, 0)
    blocks = 2 * 2 * _nbytes((tm, d), F32) + 3 * _nbytes((d, D_FF), wg.dtype) + 2 * _nbytes((1, d), F32)
    scratch = _nbytes((tm, d), BF16) + _nbytes((tm, D_FF), BF16)
    temps = 4 * _nbytes((SUB_TM, d), F32)
    return pl.pallas_call(
        _ffn_kernel,
        grid=(n // tm,),
        in_specs=[
            pl.BlockSpec((tm, d), row),
            _resident((1, d)),
            _resident((d, D_FF)),
            _resident((d, D_FF)),
            _resident((D_FF, d)),
            _resident((1, d)),
        ],
        out_specs=pl.BlockSpec((tm, d), row),
        out_shape=jax.ShapeDtypeStruct((n, d), F32),
        scratch_shapes=[pltpu.VMEM((tm, d), BF16), pltpu.VMEM((tm, D_FF), BF16)],
        compiler_params=pltpu.CompilerParams(
            dimension_semantics=("arbitrary",),
            vmem_limit_bytes=_vmem_limit(blocks, scratch, temps),
        ),
        name="ffn",
    )(x, pre_g, wg, wu, wd, post_g)


QKV_TM = 1024
QKV_TOTAL = 3 * NA_WIDTH + 3 * NB_WIDTH
N_DIL_OUTS = 3 * NB_GROUPS
STAGED_OUTS = [i for i in range(N_DIL_OUTS) if DILATIONS[i % NB_GROUPS] > 1]
N_STAGED_OUTS = len(STAGED_OUTS)


def _qkv_kernel(x_ref, g_ref, w_ref, *refs):
    na_refs, dil_refs, y_refs = refs[:3], refs[3:3 + N_DIL_OUTS], refs[3 + N_DIL_OUTS:]
    for k, rows in enumerate(_sub_tiles(x_ref.shape[0])):
        h = _rmsnorm_f32(x_ref[rows], g_ref[...]).astype(BF16)
        for idx, o_ref in enumerate(na_refs):
            y = _dot(h, w_ref[:, idx * NA_WIDTH:(idx + 1) * NA_WIDTH].astype(BF16))
            if idx == 0:
                y = y * ATTN_SCALE
            o_ref[rows] = y.astype(BF16)
        for idx, o_ref in enumerate(dil_refs):
            which, g = divmod(idx, NB_GROUPS)
            dil = DILATIONS[g]
            off = 3 * NA_WIDTH + which * NB_WIDTH + g * GROUP_WIDTH
            y = _dot(h, w_ref[:, off:off + GROUP_WIDTH].astype(BF16))
            if which == 0:
                y = y * ATTN_SCALE
            if dil == 1:
                o_ref[0, rows] = y.astype(BF16)
                continue
            y_ref = y_refs[k * N_STAGED_OUTS + STAGED_OUTS.index(idx)]
            per_res = SUB_TM // dil
            for c in range(GROUP_LANE_BLOCKS):
                y_ref[c] = y[:, c * V7X_LANES:(c + 1) * V7X_LANES]
            for r in range(dil):
                for c in range(GROUP_LANE_BLOCKS):
                    o_ref[r, k * per_res:(k + 1) * per_res, c * V7X_LANES:(c + 1) * V7X_LANES] = (
                        y_ref[c, pl.ds(r, per_res, stride=dil), :].astype(BF16))


def _qkv(x, g, w_in):
    batch, seq, d = x.shape
    tm = QKV_TM
    tok = lambda b, i: (b, i, 0)
    res = lambda b, i: (b, 0, i, 0)
    out_specs = [pl.BlockSpec((None, tm, NA_WIDTH), tok)] * 3
    out_shape = [jax.ShapeDtypeStruct((batch, seq, NA_WIDTH), BF16)] * 3
    for _ in range(3):
        for dil in DILATIONS:
            out_specs.append(pl.BlockSpec((None, dil, tm // dil, GROUP_WIDTH), res))
            out_shape.append(jax.ShapeDtypeStruct((batch, dil, seq // dil, GROUP_WIDTH), BF16))
    blocks = 2 * _nbytes((tm, d), F32) + _nbytes((d, QKV_TOTAL), w_in.dtype) + 2 * _nbytes((tm, QKV_TOTAL), BF16)
    stage =(GROUP_LANE_BLOCKS, SUB_TM, V7X_LANES)
    n_stage = N_STAGED_OUTS * (tm // SUB_TM)
    scratch = n_stage * _nbytes(stage, F32)
    temps = 2 * _nbytes((SUB_TM, d), F32) + 2 * _nbytes((SUB_TM, NB_WIDTH), F32)
    outs = pl.pallas_call(
        _qkv_kernel,
        grid=(batch, seq // tm),
        in_specs=[pl.BlockSpec((None, tm, d), tok), _resident((1, d)), _resident((d, QKV_TOTAL))],
        out_specs=out_specs,
        out_shape=out_shape,
        scratch_shapes=[pltpu.VMEM(stage, F32)] * n_stage,
        compiler_params=pltpu.CompilerParams(
            dimension_semantics=("arbitrary", "arbitrary"),
            vmem_limit_bytes=_vmem_limit(blocks, scratch, temps),
        ),
        name="qkv",
    )(x, g, w_in)
    qa, ka, va = outs[:3]
    qb, kb, vb = (outs[3 + w * NB_GROUPS:3 + (w + 1) * NB_GROUPS] for w in range(3))
    return qa, ka, va, qb, kb, vb


NA_HEADS_PER_STEP = V7X_LANES // HEAD_DIM
NA_SPAN = NA_ROWS * GRID_W
NA_VARIANTS = NA_ROWS
NA_RPB_ROWS = 2 * NA_ROWS - 1
NA_RPB_COLS = 2 * NA_COLS - 1
NA_UNROLL = 8


def _nbr_build_bias(rep_ref, tbl_ref):
    width = NA_RPB_ROWS * GRID_W
    qc = lax.broadcasted_iota(jnp.int32, (GRID_W, width), 0)
    kc = lax.broadcasted_iota(jnp.int32, (GRID_W, width), 1) % GRID_W
    diff = kc - qc
    col0 = jnp.clip(qc - NA_COLS // 2, 0, GRID_W - NA_COLS)
    col_ok = (kc >= col0) & (kc < col0 + NA_COLS)
    for h in range(NA_HEADS_PER_STEP):
        w = jnp.full((GRID_W, width), -jnp.inf, F32)
        for k in range(NA_RPB_COLS):
            w = jnp.where(diff == k - (NA_COLS - 1), rep_ref[h, k:k + 1, :], w)
        w = jnp.where(col_ok, w, -jnp.inf)
        for variant in range(NA_VARIANTS):
            tbl_ref[h, variant] = w[:, variant * GRID_W:variant * GRID_W + NA_SPAN]


def _nbr_kernel(q_ref, k_ref, v_ref, rep_ref, o_ref, tbl_ref, s_scr, p_scr, *, rows):
    hp = NA_HEADS_PER_STEP
    ones = jnp.ones((NA_SPAN, V7X_LANES), BF16)

    @pl.when(pl.program_id(1) == 0)
    def _():
        _nbr_build_bias(rep_ref, tbl_ref)

    lane_head = lax.broadcasted_iota(jnp.int32, (GRID_W, V7X_LANES), 1) // HEAD_DIM

    def scores(r):
        row0 = jnp.clip(r - NA_ROWS // 2, 0, rows - NA_ROWS)
        variant = row0 - r + (NA_ROWS - 1)
        q = q_ref[pl.ds(pl.multiple_of(r * GRID_W, GRID_W), GRID_W), :]
        kw = k_ref[pl.ds(pl.multiple_of(row0 * GRID_W, GRID_W), NA_SPAN), :]
        qs = jnp.concatenate([jnp.where(lane_head == h, q, jnp.zeros_like(q)) for h in range(hp)], axis=0)
        return _dot_nt(qs, kw) + tbl_ref[:, variant].reshape(hp * GRID_W, NA_SPAN)

    def probs(s):
        return (jnp.exp(s - jnp.max(s, axis=-1, keepdims=True)).astype(BF16),)

    def finish(r, p):
        row0 = jnp.clip(r - NA_ROWS // 2, 0, rows - NA_ROWS)
        vw = v_ref[pl.ds(pl.multiple_of(row0 * GRID_W, GRID_W), NA_SPAN), :]
        o = _dot(p, jnp.concatenate([vw, ones], axis=1))
        out, den = o[:GRID_W, :V7X_LANES], o[:GRID_W, V7X_LANES:]
        for h in range(1, hp):
            rows_h = slice(h * GRID_W, (h + 1) * GRID_W)
            out = jnp.where(lane_head == h, o[rows_h, :V7X_LANES], out)
            den = jnp.where(lane_head == h, o[rows_h, V7X_LANES:], den)
        o_ref[pl.ds(pl.multiple_of(r * GRID_W, GRID_W), GRID_W), :] = (out * (1.0 / den)).astype(BF16)

    _pipelined_attention(rows // NA_UNROLL, NA_UNROLL, scores, probs, finish, s_scr, p_scr, ())


def _nbr_attention(q, k, v, rpb):
    batch, seq, width = q.shape
    rows = seq // GRID_W
    n_pairs = width // V7X_LANES
    rep = jnp.repeat(rpb.astype(F32).transpose(0, 2, 1), GRID_W, axis=-1)
    tok = lambda p, b: (b, 0, p)
    tbl_shape = (NA_HEADS_PER_STEP, NA_VARIANTS, GRID_W, NA_SPAN)
    rep_block = (NA_HEADS_PER_STEP, NA_RPB_COLS, NA_RPB_ROWS * GRID_W)
    blocks = 2 * 4 * _nbytes((seq, V7X_LANES), BF16) + 2 * _nbytes((NA_HEADS_PER_STEP, 32, 1024), F32)
    tile = (2, NA_UNROLL, NA_HEADS_PER_STEP * GRID_W, NA_SPAN)
    scratch = _nbytes(tbl_shape, F32) + _nbytes(tile, F32) + _nbytes(tile, BF16)
    temps = 8 * NA_UNROLL * _nbytes(tile[2:], F32)
    return pl.pallas_call(
        functools.partial(_nbr_kernel, rows=rows),
        grid=(n_pairs, batch),
        in_specs=[
            pl.BlockSpec((None, seq, V7X_LANES), tok),
            pl.BlockSpec((None, seq, V7X_LANES), tok),
            pl.BlockSpec((None, seq, V7X_LANES), tok),
            pl.BlockSpec(rep_block, lambda p, b: (p, 0, 0)),
        ],
        out_specs=pl.BlockSpec((None, seq, V7X_LANES), tok),
        out_shape=jax.ShapeDtypeStruct((batch, seq, width), BF16),
        scratch_shapes=[pltpu.VMEM(tbl_shape, F32), pltpu.VMEM(tile, F32), pltpu.VMEM(tile, BF16)],
        compiler_params=pltpu.CompilerParams(
            dimension_semantics=("arbitrary", "arbitrary"),
            vmem_limit_bytes=_vmem_limit(blocks, scratch, temps),
        ),
        name="nbr_attn",
    )(q, k, v, rep)


DIL_HALF = 64
DIL_TQ = 128
DIL_TK = DIL_TQ + 2 * DIL_HALF
DIL_VARIANTS = 3
DIL_STEP_ROWS = 4096
DIL_UNROLL = 8


def _dil_bias_table(group, dilation):
    heads = jnp.arange(NB_HEADS_PER_GROUP, dtype=F32) + group * NB_HEADS_PER_GROUP
    slopes = jnp.exp2(-ALIBI_MAX_EXP * (heads + 1.0) / NB_HEADS)
    offs = jnp.array([0, -DIL_HALF, -2 * DIL_HALF], jnp.int32)
    rel = offs[:, None, None] + jnp.arange(DIL_TK)[None, None, :] - jnp.arange(DIL_TQ)[None, :, None]
    dist = (dilation * jnp.abs(rel)).astype(F32)
    bias = -(slopes[None, :, None, None] * dist[:, None])
    bias = jnp.where((jnp.abs(rel) <= DIL_HALF)[:, None], bias, -jnp.inf)
    return bias.reshape(DIL_VARIANTS, NB_HEADS_PER_GROUP * DIL_TQ, DIL_TK)


def _dil_kernel(q_ref, k_ref, v_ref, bias_ref, o_ref, lse_ref, s_scr, p_scr, m_scr, *, seq_len, stretch):
    nh = NB_HEADS_PER_GROUP
    hpb = V7X_LANES // HEAD_DIM
    ones = jnp.ones((DIL_TK, V7X_LANES), BF16)
    group_lane_head = lax.broadcasted_iota(jnp.int32, (DIL_TQ, GROUP_WIDTH), 1) // HEAD_DIM
    lane_head = lax.broadcasted_iota(jnp.int32, (DIL_TQ, V7X_LANES), 1) // HEAD_DIM
    base = pl.program_id(2) * stretch
    tiles_per_residue = stretch // DIL_TQ

    def window(t):
        r, j = t // tiles_per_residue, t % tiles_per_residue
        qloc = pl.multiple_of(j * DIL_TQ, DIL_TQ)
        qs = base + qloc
        ws = jnp.clip(qs - DIL_HALF, 0, seq_len - DIL_TK)
        variant = (qs - ws) // DIL_HALF
        return r, qloc, pl.multiple_of(ws, DIL_HALF), variant

    def scores(t):
        r, qloc, ws, variant = window(t)
        q = q_ref[r, pl.ds(qloc, DIL_TQ), :]
        kw = k_ref[r, pl.ds(ws, DIL_TK), :]
        qst = jnp.concatenate([jnp.where(group_lane_head == h, q, jnp.zeros_like(q)) for h in range(nh)],
                              axis=0)
        return _dot_nt(qst, kw) + bias_ref[variant]

    def probs(s):
        m = jnp.max(s, axis=-1, keepdims=True)
        return jnp.exp(s - m).astype(BF16), jnp.broadcast_to(m, (s.shape[0], V7X_LANES))

    def finish(t, p, m):
        r, qloc, ws, _ = window(t)
        vw = v_ref[r, pl.ds(ws, DIL_TK), :]
        outs, lses = [], []
        for c in range(GROUP_LANE_BLOCKS):
            rows_c = slice(c * hpb * DIL_TQ, (c + 1) * hpb * DIL_TQ)
            v_aug = jnp.concatenate([vw[:, c * V7X_LANES:(c + 1) * V7X_LANES], ones], axis=1)
            o = _dot(p[rows_c], v_aug)
            m_c = m[rows_c]
            out_c, den_c, max_c = o[:DIL_TQ, :V7X_LANES], o[:DIL_TQ, V7X_LANES:], m_c[:DIL_TQ]
            for h in range(1, hpb):
                rows_h = slice(h * DIL_TQ, (h + 1) * DIL_TQ)
                out_c = jnp.where(lane_head == h, o[rows_h, :V7X_LANES], out_c)
                den_c = jnp.where(lane_head == h, o[rows_h, V7X_LANES:], den_c)
                max_c = jnp.where(lane_head == h, m_c[rows_h], max_c)
            outs.append(out_c * (1.0 / den_c))
            lses.append(max_c + jnp.log(den_c))
        o_ref[r, pl.ds(qloc, DIL_TQ), :] = jnp.concatenate(outs, axis=1).astype(BF16)
        lse_ref[r, pl.ds(qloc, DIL_TQ), :] = jnp.concatenate(lses, axis=1)

    n_tiles = q_ref.shape[0] * tiles_per_residue
    _pipelined_attention(n_tiles // DIL_UNROLL, DIL_UNROLL, scores, probs, finish, s_scr, p_scr, (m_scr,))


def _dil_attention(q, k, v, group):
    batch, dilation, seq_len, width = q.shape
    stretch = min(seq_len, DIL_STEP_ROWS)
    res_blk = DIL_STEP_ROWS // stretch
    assert dilation % res_blk == 0 and seq_len % stretch == 0
    qmap = lambda b, r, s: (b, r, s, 0)
    kvmap = lambda b, r, s: (b, r, 0, 0)
    bias = _dil_bias_table(group, dilation)
    tile = (2, DIL_UNROLL, NB_HEADS_PER_GROUP * DIL_TQ, DIL_TK)
    stat = tile[:3] + (V7X_LANES,)
    kv_buffers = 1 if seq_len // stretch > 1 else 2
    kv_spec = pl.BlockSpec((None, res_blk, seq_len, width), kvmap, pipeline_mode=pl.Buffered(kv_buffers))
    blocks = (2 * (2 * _nbytes((DIL_STEP_ROWS, width), BF16) + _nbytes((DIL_STEP_ROWS, width), F32))
              + kv_buffers * 2 * _nbytes((res_blk * seq_len, width), BF16) + _nbytes(bias.shape, F32))
    scratch = _nbytes(tile, F32) + _nbytes(tile, BF16) + _nbytes(stat, F32)
    temps = DIL_UNROLL * _nbytes(tile[2:], F32)
    return pl.pallas_call(
        functools.partial(_dil_kernel, seq_len=seq_len, stretch=stretch),
        grid=(batch, dilation // res_blk, seq_len // stretch),
        in_specs=[
            pl.BlockSpec((None, res_blk, stretch, width), qmap),
            kv_spec,
            kv_spec,
            _resident(bias.shape),
        ],
        out_specs=[pl.BlockSpec((None, res_blk, stretch, width), qmap)] * 2,
        out_shape=[jax.ShapeDtypeStruct(q.shape, BF16), jax.ShapeDtypeStruct(q.shape, F32)],
        scratch_shapes=[pltpu.VMEM(tile, F32), pltpu.VMEM(tile, BF16), pltpu.VMEM(stat, F32)],
        compiler_params=pltpu.CompilerParams(
            dimension_semantics=("arbitrary", "arbitrary", "arbitrary"),
            vmem_limit_bytes=_vmem_limit(blocks, scratch, temps),
        ),
        name=f"dil_attn_g{group}",
    )(q, k, v, bias)


MERGE_TM = 1024
MERGE_CN = 256


def _to_token_order(src_ref, scr_ref, dil, k):
    per_res = SUB_TM // dil
    res_rows = slice(k * per_res, (k + 1) * per_res)
    if dil == 1:
        return src_ref[0, res_rows].astype(F32)
    for r in range(dil):
        for c in range(GROUP_LANE_BLOCKS):
            scr_ref[c, pl.ds(k * SUB_TM + r, per_res, stride=dil), :] = (
                src_ref[r, res_rows, c * V7X_LANES:(c + 1) * V7X_LANES].astype(F32))
    rows = slice(k * SUB_TM, (k + 1) * SUB_TM)
    return jnp.concatenate([scr_ref[c, rows] for c in range(GROUP_LANE_BLOCKS)], axis=-1)


def _merge_kernel(x_ref, pre_g_ref, wgate_ref, oa_ref, wa_ref, *refs):
    ng = NB_GROUPS
    o_refs, lse_refs = refs[:ng], refs[ng:2 * ng]
    wb_ref, wout_ref, post_g_ref, out_ref, h_ref, ob_ref, mg_ref = refs[2 * ng:2 * ng + 7]
    scr_refs = refs[2 * ng + 7:]
    o_scr, lse_scr = scr_refs[:ng], scr_refs[ng:]
    subs = _sub_tiles(x_ref.shape[0])
    for k, rows in enumerate(subs):
        h_ref[rows] = _rmsnorm_f32(x_ref[rows], pre_g_ref[...]).astype(BF16)
        os_ = [_to_token_order(r, s, d, k) for r, s, d in zip(o_refs, o_scr, DILATIONS)]
        lses = [_to_token_order(r, s, d, k) for r, s, d in zip(lse_refs, lse_scr, DILATIONS)]
        mx = jnp.maximum(jnp.maximum(lses[0], lses[1]), lses[2])
        es = [jnp.exp(l - mx) for l in lses]
        inv = 1.0 / (es[0] + es[1] + es[2])
        for g in range(ng):
            ob_ref[rows, g * GROUP_WIDTH:(g + 1) * GROUP_WIDTH] = (os_[g] * (es[g] * inv)).astype(BF16)
    for rows in subs:
        for n in range(D_MODEL // MERGE_CN):
            cols = slice(n * MERGE_CN, (n + 1) * MERGE_CN)
            cols_b = slice(D_MODEL + n * MERGE_CN, D_MODEL + (n + 1) * MERGE_CN)
            h = h_ref[rows]
            ga = _dot(h, wgate_ref[:, cols].astype(BF16))
            gb = _dot(h, wgate_ref[:, cols_b].astype(BF16))
            ya = _dot(oa_ref[rows], wa_ref[:, cols])
            yb = _dot(ob_ref[rows], wb_ref[:, cols])
            mg_ref[rows, cols] = (jax.nn.sigmoid(ga) * ya + jax.nn.sigmoid(gb) * yb).astype(BF16)
    for rows in subs:
        out_ref[rows] = _dot(mg_ref[rows], wout_ref[...])
    for rows in subs:
        out_ref[rows] = x_ref[rows] + _rmsnorm_f32(out_ref[rows], post_g_ref[...])


def _merge(x, pre_g, wgate, oa, wa, obs, lses, wb, wout, post_g):
    batch, seq, d = x.shape
    tm = MERGE_TM
    tok = lambda b, i: (b, i, 0)
    res = lambda b, i: (b, 0, i, 0)
    tokspec = lambda w: pl.BlockSpec((None, tm, w), tok)
    resspecs = [pl.BlockSpec((None, dil, tm // dil, GROUP_WIDTH), res) for dil in DILATIONS]
    blocks = (2 * 2 * _nbytes((tm, d), F32) + 2 * _nbytes((tm, NA_WIDTH), BF16)
              + 2 * 3 * (_nbytes((tm, GROUP_WIDTH), BF16) + _nbytes((tm, GROUP_WIDTH), F32))
              + _nbytes((d + NA_WIDTH + NB_WIDTH, d), BF16) + _nbytes((d, 2 * d), F32))
    work =[pltpu.VMEM((tm, d), BF16), pltpu.VMEM((tm, NB_WIDTH), BF16), pltpu.VMEM((tm, d), BF16)]
    scratch = (2 * NB_GROUPS * _nbytes((tm, GROUP_WIDTH), F32) + 2 * _nbytes((tm, d), BF16)
               + _nbytes((tm, NB_WIDTH), BF16))
    temps = 3 * _nbytes((SUB_TM, d), F32)
    return pl.pallas_call(
        _merge_kernel,
        grid=(batch, seq // tm),
        in_specs=[
            tokspec(d), _resident((1, d)),
            pl.BlockSpec((pl.Element(d), pl.Element(2 * d)), lambda *_: (0, QKV_TOTAL),
                         pipeline_mode=pl.Buffered(1)),
            tokspec(NA_WIDTH), _resident((NA_WIDTH, d)),
            *resspecs, *resspecs,
            _resident((NB_WIDTH, d)), _resident((d, d)), _resident((1, d)),
        ],
        out_specs=tokspec(d),
        out_shape=jax.ShapeDtypeStruct((batch, seq, d), F32),
        scratch_shapes=work + [pltpu.VMEM((GROUP_LANE_BLOCKS, tm, V7X_LANES), F32)] * (2 * NB_GROUPS),
        compiler_params=pltpu.CompilerParams(
            dimension_semantics=("arbitrary", "arbitrary"),
            vmem_limit_bytes=_vmem_limit(blocks, scratch, temps),
        ),
        name="merge",
    )(x, pre_g, wgate, oa, wa, *obs, *lses, wb, wout, post_g)


def kernel(x, ffn1_pre_g, ffn1_w_gate, ffn1_w_up, ffn1_w_down, ffn1_post_g, mix_pre_g, w_in, na_rpb, w_branch_a, w_branch_b, w_out, mix_post_g, ffn2_pre_g, ffn2_w_gate, ffn2_w_up, ffn2_w_down, ffn2_post_g):
    batch, seq, d = x.shape
    depth = ffn1_pre_g.shape[0]
    gate_off = QKV_TOTAL
    for window, dilation in DIL_PAIRS:
        assert window // (2 * dilation) == DIL_HALF
    for l in range(depth):
        x = _ffn(x.reshape(batch * seq, d), ffn1_pre_g[l][None], ffn1_w_gate[l], ffn1_w_up[l],
                 ffn1_w_down[l], ffn1_post_g[l][None])
        x = x.reshape(batch, seq, d)
        qa, ka, va, qb, kb, vb = _qkv(x, mix_pre_g[l][None], w_in[l])
        oa = _nbr_attention(qa, ka, va, na_rpb[l])
        obs, lses = [], []
        for g in range(NB_GROUPS):
            o_g, lse_g = _dil_attention(qb[g], kb[g], vb[g], g)
            obs.append(o_g)
            lses.append(lse_g)
        x = _merge(x, mix_pre_g[l][None], w_in[l],
                   oa, w_branch_a[l].astype(BF16), obs, lses, w_branch_b[l].astype(BF16),
                   w_out[l].astype(BF16), mix_post_g[l][None])
        x = _ffn(x.reshape(batch * seq, d), ffn2_pre_g[l][None], ffn2_w_gate[l], ffn2_w_up[l],
                 ffn2_w_down[l], ffn2_post_g[l][None])
        x = x.reshape(batch, seq, d)
    return x
```

```python
import functools
import math

import jax
import jax.numpy as jnp
from jax import lax
from jax.experimental import pallas as pl
from jax.experimental.pallas import tpu as pltpu

D_MODEL = 1024
HEAD_DIM = 64
NA_HEADS = 8
NA_WIDTH = NA_HEADS * HEAD_DIM
NA_ROWS = 8
NA_COLS = 16
GRID_W = 64
DIL_PAIRS = ((128, 1), (512, 4), (2048, 16))
DILATIONS = tuple(d for _, d in DIL_PAIRS)
NB_GROUPS = len(DIL_PAIRS)
NB_HEADS_PER_GROUP = 4
NB_HEADS = NB_GROUPS * NB_HEADS_PER_GROUP
NB_WIDTH = NB_HEADS * HEAD_DIM
GROUP_WIDTH = NB_HEADS_PER_GROUP * HEAD_DIM
ALIBI_MAX_EXP = 8.0
D_FF = 2816
NORM_EPS = 1e-6
ATTN_SCALE = HEAD_DIM ** -0.5
LOG2E = math.log2(math.e)
LN2 = math.log(2.0)
Q_SCALE = ATTN_SCALE * LOG2E

V7X_LANES = 128
GROUP_LANE_BLOCKS = GROUP_WIDTH // V7X_LANES
V7X_VMEM_BYTES = 64 * 1024 * 1024
V7X_VMEM_RESERVE = 6 * 1024 * 1024

BF16 = jnp.bfloat16
F32 = jnp.float32


def _vmem_limit(block_bytes, scratch_bytes, temp_bytes):
    need = block_bytes + scratch_bytes + temp_bytes
    assert need <= V7X_VMEM_BYTES - V7X_VMEM_RESERVE, need
    return int(need)


def _nbytes(shape, dtype):
    return math.prod(shape) * jnp.dtype(dtype).itemsize


def _resident(shape):
    nd = len(shape)
    return pl.BlockSpec(shape, lambda *_: (0,) * nd, pipeline_mode=pl.Buffered(1))


def _rmsnorm_f32(x, g):
    return x * lax.rsqrt(jnp.mean(x * x, axis=-1, keepdims=True) + NORM_EPS) * g


def _dot(a, b):
    return jnp.dot(a, b, preferred_element_type=F32)


def _dot_nt(a, b):
    return lax.dot_general(a, b, (((1,), (1,)), ((), ())), preferred_element_type=F32)


def _pipelined_attention(n_groups, unroll, scores, probs, finish, s_scr, p_scr, stat_scrs):
    assert n_groups % 2 == 0 and n_groups >= 2
    tiles = lambda i: [i * unroll + u for u in range(unroll)]

    def stage_scores(i, slot):
        for u, t in enumerate(tiles(i)):
            s_scr[slot, u] = scores(t)

    def stage_probs(slot):
        for u in range(unroll):
            p, *stats = probs(s_scr[slot, u])
            p_scr[slot, u] = p
            for ref, stat in zip(stat_scrs, stats):
                ref[slot, u] = stat

    def stage_finish(i, slot):
        for u, t in enumerate(tiles(i)):
            finish(t, p_scr[slot, u], *[ref[slot, u] for ref in stat_scrs])

    def step(i, parity):
        stage_scores(i + 1, 1 - parity)
        stage_finish(i - 1, 1 - parity)
        stage_probs(parity)

    stage_scores(0, 0)
    stage_scores(1, 1)
    stage_probs(0)

    def body(i, carry):
        @pl.when(i % 2 == 1)
        def _():
            step(i, 1)

        @pl.when(i % 2 == 0)
        def _():
            step(i, 0)

        return carry

    lax.fori_loop(1, n_groups - 1, body, 0)
    stage_finish(n_groups - 2, 0)
    stage_probs(1)
    stage_finish(n_groups - 1, 1)


FFN_TM = 512
SUB_TM = 512
FFN_CK = 256


def _sub_tiles(tm):
    return [slice(k * SUB_TM, (k + 1) * SUB_TM) for k in range(tm // SUB_TM)]


def _ffn_kernel(x_ref, pre_g_ref, wg_ref, wu_ref, wd_ref, post_g_ref, o_ref, h_ref, a_ref):
    h_ref[...] = _rmsnorm_f32(x_ref[...], pre_g_ref[...]).astype(BF16)
    chunks = [slice(c * FFN_CK, (c + 1) * FFN_CK) for c in range(D_FF // FFN_CK)]
    for cols in chunks:
        h = h_ref[...]
        g = _dot(h, wg_ref[:, cols].astype(BF16))
        u = _dot(h, wu_ref[:, cols].astype(BF16))
        a_ref[:, cols] = (g * jax.nn.sigmoid(g) * u).astype(BF16)
    f = None
    for cols in chunks:
        part = _dot(a_ref[:, cols], wd_ref[cols, :].astype(BF16))
        f = part if f is None else f + part
    o_ref[...] = x_ref[...] + 0.5 * _rmsnorm_f32(f, post_g_ref[...])


def _ffn(x, pre_g, wg, wu, wd, post_g):
    n, d = x.shape
    tm = FFN_TM
    row = lambda i: (i, 0)
    blocks = 2 * 2 * _nbytes((tm, d), F32) + 3 * _nbytes((d, D_FF), wg.dtype) + 2 * _nbytes((1, d), F32)
    scratch = _nbytes((tm, d), BF16) + _nbytes((tm, D_FF), BF16)
    temps = 4 * _nbytes((SUB_TM, d), F32)
    return pl.pallas_call(
        _ffn_kernel,
        grid=(n // tm,),
        in_specs=[
            pl.BlockSpec((tm, d), row),
            _resident((1, d)),
            _resident((d, D_FF)),
            _resident((d, D_FF)),
            _resident((D_FF, d)),
            _resident((1, d)),
        ],
        out_specs=pl.BlockSpec((tm, d), row),
        out_shape=jax.ShapeDtypeStruct((n, d), F32),
        scratch_shapes=[pltpu.VMEM((tm, d), BF16), pltpu.VMEM((tm, D_FF), BF16)],
        compiler_params=pltpu.CompilerParams(
            dimension_semantics=("arbitrary",),
            vmem_limit_bytes=_vmem_limit(blocks, scratch, temps),
        ),
        name="ffn",
    )(x, pre_g, wg, wu, wd, post_g)


QKV_TM = 1024
QKV_TOTAL = 3 * NA_WIDTH + 3 * NB_WIDTH
N_DIL_OUTS = 3 * NB_GROUPS
STAGED_OUTS = [i for i in range(N_DIL_OUTS) if DILATIONS[i % NB_GROUPS] > 1]
N_STAGED_OUTS = len(STAGED_OUTS)


def _qkv_kernel(x_ref, g_ref, w_ref, *refs):
    na_refs, dil_refs, y_refs = refs[:3], refs[3:3 + N_DIL_OUTS], refs[3 + N_DIL_OUTS:]
    for k, rows in enumerate(_sub_tiles(x_ref.shape[0])):
        h = _rmsnorm_f32(x_ref[rows], g_ref[...]).astype(BF16)
        for idx, o_ref in enumerate(dil_refs):
            which, g = divmod(idx, NB_GROUPS)
            dil = DILATIONS[g]
            off = 3 * NA_WIDTH + which * NB_WIDTH + g * GROUP_WIDTH
            y = _dot(h, w_ref[:, off:off + GROUP_WIDTH].astype(BF16))
            if which == 0:
                y = y * Q_SCALE
            if dil == 1:
                o_ref[0, rows] = y.astype(BF16)
                continue
            y_ref = y_refs[k * N_STAGED_OUTS + STAGED_OUTS.index(idx)]
            per_res = SUB_TM // dil
            for c in range(GROUP_LANE_BLOCKS):
                y_ref[c] = y[:, c * V7X_LANES:(c + 1) * V7X_LANES]
            for r in range(dil):
                for c in range(GROUP_LANE_BLOCKS):
                    o_ref[r, k * per_res:(k + 1) * per_res, c * V7X_LANES:(c + 1) * V7X_LANES] = (
                        y_ref[c, pl.ds(r, per_res, stride=dil), :].astype(BF16))
        for idx, o_ref in enumerate(na_refs):
            y = _dot(h, w_ref[:, idx * NA_WIDTH:(idx + 1) * NA_WIDTH].astype(BF16))
            if idx == 0:
                y = y * Q_SCALE
            o_ref[rows] = y.astype(BF16)


def _qkv(x, g, w_in):
    batch, seq, d = x.shape
    tm = QKV_TM
    tok = lambda b, i: (b, i, 0)
    res = lambda b, i: (b, 0, i, 0)
    out_specs = [pl.BlockSpec((None, tm, NA_WIDTH), tok)] * 3
    out_shape = [jax.ShapeDtypeStruct((batch, seq, NA_WIDTH), BF16)] * 3
    for _ in range(3):
        for dil in DILATIONS:
            out_specs.append(pl.BlockSpec((None, dil, tm // dil, GROUP_WIDTH), res))
            out_shape.append(jax.ShapeDtypeStruct((batch, dil, seq // dil, GROUP_WIDTH), BF16))
    blocks = 2 * _nbytes((tm, d), F32) + _nbytes((d, QKV_TOTAL), w_in.dtype) + 2 * _nbytes((tm, QKV_TOTAL), BF16)
    stage =(GROUP_LANE_BLOCKS, SUB_TM, V7X_LANES)
    n_stage = N_STAGED_OUTS * (tm // SUB_TM)
    scratch = n_stage * _nbytes(stage, F32)
    temps = 2 * _nbytes((SUB_TM, d), F32) + 2 * _nbytes((SUB_TM, NB_WIDTH), F32)
    outs = pl.pallas_call(
        _qkv_kernel,
        grid=(batch, seq // tm),
        in_specs=[pl.BlockSpec((None, tm, d), tok), _resident((1, d)), _resident((d, QKV_TOTAL))],
        out_specs=out_specs,
        out_shape=out_shape,
        scratch_shapes=[pltpu.VMEM(stage, F32)] * n_stage,
        compiler_params=pltpu.CompilerParams(
            dimension_semantics=("arbitrary", "arbitrary"),
            vmem_limit_bytes=_vmem_limit(blocks, scratch, temps),
        ),
        name="qkv",
    )(x, g, w_in)
    qa, ka, va = outs[:3]
    qb, kb, vb = (outs[3 + w * NB_GROUPS:3 + (w + 1) * NB_GROUPS] for w in range(3))
    return qa, ka, va, qb, kb, vb


NA_HEADS_PER_STEP = V7X_LANES // HEAD_DIM
NA_SPAN = NA_ROWS * GRID_W
NA_VARIANTS = NA_ROWS
NA_RPB_ROWS = 2 * NA_ROWS - 1
NA_RPB_COLS = 2 * NA_COLS - 1
NA_UNROLL = 16


def _nbr_build_bias(rep_ref, tbl_ref):
    width = NA_RPB_ROWS * GRID_W
    qc = lax.broadcasted_iota(jnp.int32, (GRID_W, width), 0)
    kc = lax.broadcasted_iota(jnp.int32, (GRID_W, width), 1) % GRID_W
    diff = kc - qc
    col0 = jnp.clip(qc - NA_COLS // 2, 0, GRID_W - NA_COLS)
    col_ok = (kc >= col0) & (kc < col0 + NA_COLS)
    for h in range(NA_HEADS_PER_STEP):
        w = jnp.full((GRID_W, width), -jnp.inf, F32)
        for k in range(NA_RPB_COLS):
            w = jnp.where(diff == k - (NA_COLS - 1), rep_ref[h, k:k + 1, :], w)
        w = jnp.where(col_ok, w * LOG2E, -jnp.inf)
        for variant in range(NA_VARIANTS):
            tbl_ref[h, variant] = w[:, variant * GRID_W:variant * GRID_W + NA_SPAN]


def _nbr_kernel(q_ref, k_ref, v_ref, rep_ref, o_ref, tbl_ref, s_scr, p_scr, *, rows):
    hp = NA_HEADS_PER_STEP
    ones = jnp.ones((NA_SPAN, V7X_LANES), BF16)

    @pl.when(pl.program_id(1) == 0)
    def _():
        _nbr_build_bias(rep_ref, tbl_ref)

    lane_head = lax.broadcasted_iota(jnp.int32, (GRID_W, V7X_LANES), 1) // HEAD_DIM

    def scores(r):
        row0 = jnp.clip(r - NA_ROWS // 2, 0, rows - NA_ROWS)
        variant = row0 - r + (NA_ROWS - 1)
        q = q_ref[pl.ds(pl.multiple_of(r * GRID_W, GRID_W), GRID_W), :]
        kw = k_ref[pl.ds(pl.multiple_of(row0 * GRID_W, GRID_W), NA_SPAN), :]
        qs = jnp.concatenate([jnp.where(lane_head == h, q, jnp.zeros_like(q)) for h in range(hp)], axis=0)
        return _dot_nt(qs, kw) + tbl_ref[:, variant].reshape(hp * GRID_W, NA_SPAN)

    def probs(s):
        return (jnp.exp2(s - jnp.max(s, axis=-1, keepdims=True)).astype(BF16),)

    def finish(r, p):
        row0 = jnp.clip(r - NA_ROWS // 2, 0, rows - NA_ROWS)
        vw = v_ref[pl.ds(pl.multiple_of(row0 * GRID_W, GRID_W), NA_SPAN), :]
        o = _dot(p, jnp.concatenate([vw, ones], axis=1))
        out, den = o[:GRID_W, :V7X_LANES], o[:GRID_W, V7X_LANES:]
        for h in range(1, hp):
            rows_h = slice(h * GRID_W, (h + 1) * GRID_W)
            out = jnp.where(lane_head == h, o[rows_h, :V7X_LANES], out)
            den = jnp.where(lane_head == h, o[rows_h, V7X_LANES:], den)
        o_ref[pl.ds(pl.multiple_of(r * GRID_W, GRID_W), GRID_W), :] = (out * (1.0 / den)).astype(BF16)

    _pipelined_attention(rows // NA_UNROLL, NA_UNROLL, scores, probs, finish, s_scr, p_scr, ())


def _nbr_attention(q, k, v, rpb):
    batch, seq, width = q.shape
    rows = seq // GRID_W
    n_pairs = width // V7X_LANES
    rep = jnp.repeat(rpb.astype(F32).transpose(0, 2, 1), GRID_W, axis=-1)
    tok = lambda p, b: (b, 0, p)
    tbl_shape = (NA_HEADS_PER_STEP, NA_VARIANTS, GRID_W, NA_SPAN)
    rep_block = (NA_HEADS_PER_STEP, NA_RPB_COLS, NA_RPB_ROWS * GRID_W)
    blocks = 2 * 4 * _nbytes((seq, V7X_LANES), BF16) + 2 * _nbytes((NA_HEADS_PER_STEP, 32, 1024), F32)
    tile = (2, NA_UNROLL, NA_HEADS_PER_STEP * GRID_W, NA_SPAN)
    scratch = _nbytes(tbl_shape, F32) + _nbytes(tile, F32) + _nbytes(tile, BF16)
    temps = NA_UNROLL * _nbytes(tile[2:], F32)
    return pl.pallas_call(
        functools.partial(_nbr_kernel, rows=rows),
        grid=(n_pairs, batch),
        in_specs=[
            pl.BlockSpec((None, seq, V7X_LANES), tok),
            pl.BlockSpec((None, seq, V7X_LANES), tok),
            pl.BlockSpec((None, seq, V7X_LANES), tok),
            pl.BlockSpec(rep_block, lambda p, b: (p, 0, 0)),
        ],
        out_specs=pl.BlockSpec((None, seq, V7X_LANES), tok),
        out_shape=jax.ShapeDtypeStruct((batch, seq, width), BF16),
        scratch_shapes=[pltpu.VMEM(tbl_shape, F32), pltpu.VMEM(tile, F32), pltpu.VMEM(tile, BF16)],
        compiler_params=pltpu.CompilerParams(
            dimension_semantics=("arbitrary", "arbitrary"),
            vmem_limit_bytes=_vmem_limit(blocks, scratch, temps),
        ),
        name="nbr_attn",
    )(q, k, v, rep)


DIL_HALF = 64
DIL_TQ = 128
DIL_TK = DIL_TQ + 2 * DIL_HALF
DIL_VARIANTS = 3
DIL_STEP_ROWS = 4096
DIL_UNROLL = 8


def _dil_bias_table(group, dilation):
    heads = jnp.arange(NB_HEADS_PER_GROUP, dtype=F32) + group * NB_HEADS_PER_GROUP
    slopes = jnp.exp2(-ALIBI_MAX_EXP * (heads + 1.0) / NB_HEADS)
    offs = jnp.array([0, -DIL_HALF, -2 * DIL_HALF], jnp.int32)
    rel = offs[:, None, None] + jnp.arange(DIL_TK)[None, None, :] - jnp.arange(DIL_TQ)[None, :, None]
    dist = (dilation * jnp.abs(rel)).astype(F32)
    bias = -(slopes[None, :, None, None] * dist[:, None]) * LOG2E
    bias = jnp.where((jnp.abs(rel) <= DIL_HALF)[:, None], bias, -jnp.inf)
    return bias.reshape(DIL_VARIANTS, NB_HEADS_PER_GROUP * DIL_TQ, DIL_TK)


def _dil_kernel(q_ref, k_ref, v_ref, bias_ref, o_ref, lse_ref, s_scr, p_scr, m_scr, *, seq_len, stretch):
    nh = NB_HEADS_PER_GROUP
    hpb = V7X_LANES // HEAD_DIM
    ones = jnp.ones((DIL_TK, V7X_LANES), BF16)
    group_lane_head = lax.broadcasted_iota(jnp.int32, (DIL_TQ, GROUP_WIDTH), 1) // HEAD_DIM
    lane_head = lax.broadcasted_iota(jnp.int32, (DIL_TQ, V7X_LANES), 1) // HEAD_DIM
    base = pl.program_id(2) * stretch
    tiles_per_residue = stretch // DIL_TQ

    def window(t):
        r, j = t // tiles_per_residue, t % tiles_per_residue
        qloc = pl.multiple_of(j * DIL_TQ, DIL_TQ)
        qs = base + qloc
        ws = jnp.clip(qs - DIL_HALF, 0, seq_len - DIL_TK)
        variant = (qs - ws) // DIL_HALF
        return r, qloc, pl.multiple_of(ws, DIL_HALF), variant

    def scores(t):
        r, qloc, ws, variant = window(t)
        q = q_ref[r, pl.ds(qloc, DIL_TQ), :]
        kw = k_ref[r, pl.ds(ws, DIL_TK), :]
        qst = jnp.concatenate([jnp.where(group_lane_head == h, q, jnp.zeros_like(q)) for h in range(nh)],
                              axis=0)
        return _dot_nt(qst, kw) + bias_ref[variant]

    def probs(s):
        m = jnp.max(s, axis=-1, keepdims=True)
        return jnp.exp2(s - m).astype(BF16), jnp.broadcast_to(m, (s.shape[0], V7X_LANES))

    def finish(t, p, m):
        r, qloc, ws, _ = window(t)
        vw = v_ref[r, pl.ds(ws, DIL_TK), :]
        outs, lses = [], []
        for c in range(GROUP_LANE_BLOCKS):
            rows_c = slice(c * hpb * DIL_TQ, (c + 1) * hpb * DIL_TQ)
            v_aug = jnp.concatenate([vw[:, c * V7X_LANES:(c + 1) * V7X_LANES], ones], axis=1)
            o = _dot(p[rows_c], v_aug)
            m_c = m[rows_c]
            out_c, den_c, max_c = o[:DIL_TQ, :V7X_LANES], o[:DIL_TQ, V7X_LANES:], m_c[:DIL_TQ]
            for h in range(1, hpb):
                rows_h = slice(h * DIL_TQ, (h + 1) * DIL_TQ)
                out_c = jnp.where(lane_head == h, o[rows_h, :V7X_LANES], out_c)
                den_c = jnp.where(lane_head == h, o[rows_h, V7X_LANES:], den_c)
                max_c = jnp.where(lane_head == h, m_c[rows_h], max_c)
            outs.append(out_c * (1.0 / den_c))
            lses.append(max_c * LN2 + jnp.log(den_c))
        o_ref[r, pl.ds(qloc, DIL_TQ), :] = jnp.concatenate(outs, axis=1).astype(BF16)
        lse_ref[r, pl.ds(qloc, DIL_TQ), :] = jnp.concatenate(lses, axis=1)

    n_tiles = q_ref.shape[0] * tiles_per_residue
    _pipelined_attention(n_tiles // DIL_UNROLL, DIL_UNROLL, scores, probs, finish, s_scr, p_scr, (m_scr,))


def _dil_attention(q, k, v, group):
    batch, dilation, seq_len, width = q.shape
    stretch = min(seq_len, DIL_STEP_ROWS)
    res_blk = DIL_STEP_ROWS // stretch
    assert dilation % res_blk == 0 and seq_len % stretch == 0
    qmap = lambda b, r, s: (b, r, s, 0)
    kvmap = lambda b, r, s: (b, r, 0, 0)
    bias = _dil_bias_table(group, dilation)
    tile = (2, DIL_UNROLL, NB_HEADS_PER_GROUP * DIL_TQ, DIL_TK)
    stat = tile[:3] + (V7X_LANES,)
    kv_buffers = 2
    kv_spec = pl.BlockSpec((None, res_blk, seq_len, width), kvmap)
    blocks = (2 * (2 * _nbytes((DIL_STEP_ROWS, width), BF16) + _nbytes((DIL_STEP_ROWS, width), F32))
              + kv_buffers * 2 * _nbytes((res_blk * seq_len, width), BF16) + _nbytes(bias.shape, F32))
    scratch = _nbytes(tile, F32) + _nbytes(tile, BF16) + _nbytes(stat, F32)
    temps = DIL_UNROLL * _nbytes(tile[2:], F32)
    return pl.pallas_call(
        functools.partial(_dil_kernel, seq_len=seq_len, stretch=stretch),
        grid=(batch, dilation // res_blk, seq_len // stretch),
        in_specs=[
            pl.BlockSpec((None, res_blk, stretch, width), qmap),
            kv_spec,
            kv_spec,
            _resident(bias.shape),
        ],
        out_specs=[pl.BlockSpec((None, res_blk, stretch, width), qmap)] * 2,
        out_shape=[jax.ShapeDtypeStruct(q.shape, BF16), jax.ShapeDtypeStruct(q.shape, F32)],
        scratch_shapes=[pltpu.VMEM(tile, F32), pltpu.VMEM(tile, BF16), pltpu.VMEM(stat, F32)],
        compiler_params=pltpu.CompilerParams(
            dimension_semantics=("arbitrary", "arbitrary", "arbitrary"),
            vmem_limit_bytes=_vmem_limit(blocks, scratch, temps),
        ),
        name=f"dil_attn_g{group}",
    )(q, k, v, bias)


MERGE_TM = 1024
MERGE_CN = 256


def _to_token_order(src_ref, scr_ref, dil, k):
    per_res = SUB_TM // dil
    res_rows = slice(k * per_res, (k + 1) * per_res)
    if dil == 1:
        return src_ref[0, res_rows].astype(F32)
    for r in range(dil):
        for c in range(GROUP_LANE_BLOCKS):
            scr_ref[c, pl.ds(k * SUB_TM + r, per_res, stride=dil), :] = (
                src_ref[r, res_rows, c * V7X_LANES:(c + 1) * V7X_LANES].astype(F32))
    rows = slice(k * SUB_TM, (k + 1) * SUB_TM)
    return jnp.concatenate([scr_ref[c, rows] for c in range(GROUP_LANE_BLOCKS)], axis=-1)


def _merge_kernel(x_ref, pre_g_ref, wgate_ref, oa_ref, wa_ref, *refs):
    ng = NB_GROUPS
    o_refs, lse_refs = refs[:ng], refs[ng:2 * ng]
    wb_ref, wout_ref, post_g_ref, out_ref, h_ref, ob_ref, mg_ref = refs[2 * ng:2 * ng + 7]
    scr_refs = refs[2 * ng + 7:]
    o_scr, lse_scr = scr_refs[:ng], scr_refs[ng:]
    subs = _sub_tiles(x_ref.shape[0])
    for k, rows in enumerate(subs):
        h_ref[rows] = _rmsnorm_f32(x_ref[rows], pre_g_ref[...]).astype(BF16)
        os_ = [_to_token_order(r, s, d, k) for r, s, d in zip(o_refs, o_scr, DILATIONS)]
        lses = [_to_token_order(r, s, d, k) for r, s, d in zip(lse_refs, lse_scr, DILATIONS)]
        mx = jnp.maximum(jnp.maximum(lses[0], lses[1]), lses[2])
        es = [jnp.exp(l - mx) for l in lses]
        inv = 1.0 / (es[0] + es[1] + es[2])
        for g in range(ng):
            ob_ref[rows, g * GROUP_WIDTH:(g + 1) * GROUP_WIDTH] = (os_[g] * (es[g] * inv)).astype(BF16)
    for rows in subs:
        for n in range(D_MODEL // MERGE_CN):
            cols = slice(n * MERGE_CN, (n + 1) * MERGE_CN)
            cols_b = slice(D_MODEL + n * MERGE_CN, D_MODEL + (n + 1) * MERGE_CN)
            h = h_ref[rows]
            ga = _dot(h, wgate_ref[:, cols].astype(BF16))
            gb = _dot(h, wgate_ref[:, cols_b].astype(BF16))
            ya = _dot(oa_ref[rows], wa_ref[:, cols])
            yb = _dot(ob_ref[rows], wb_ref[:, cols])
            mg_ref[rows, cols] = (jax.nn.sigmoid(ga) * ya + jax.nn.sigmoid(gb) * yb).astype(BF16)
    for rows in subs:
        out_ref[rows] = _dot(mg_ref[rows], wout_ref[...])
    for rows in subs:
        out_ref[rows] = x_ref[rows] + _rmsnorm_f32(out_ref[rows], post_g_ref[...])


def _merge(x, pre_g, wgate, oa, wa, obs, lses, wb, wout, post_g):
    batch, seq, d = x.shape
    tm = MERGE_TM
    tok = lambda b, i: (b, i, 0)
    res = lambda b, i: (b, 0, i, 0)
    tokspec = lambda w: pl.BlockSpec((None, tm, w), tok)
    resspecs = [pl.BlockSpec((None, dil, tm // dil, GROUP_WIDTH), res) for dil in DILATIONS]
    blocks = (2 * 2 * _nbytes((tm, d), F32) + 2 * _nbytes((tm, NA_WIDTH), BF16)
              + 2 * 3 * (_nbytes((tm, GROUP_WIDTH), BF16) + _nbytes((tm, GROUP_WIDTH), F32))
              + _nbytes((d + NA_WIDTH + NB_WIDTH, d), BF16) + _nbytes((d, 2 * d), F32))
    work =[pltpu.VMEM((tm, d), BF16), pltpu.VMEM((tm, NB_WIDTH), BF16), pltpu.VMEM((tm, d), BF16)]
    scratch = (2 * NB_GROUPS * _nbytes((tm, GROUP_WIDTH), F32) + 2 * _nbytes((tm, d), BF16)
               + _nbytes((tm, NB_WIDTH), BF16))
    temps = 3 * _nbytes((SUB_TM, d), F32)
    return pl.pallas_call(
        _merge_kernel,
        grid=(batch, seq // tm),
        in_specs=[
            tokspec(d), _resident((1, d)),
            pl.BlockSpec((pl.Element(d), pl.Element(2 * d)), lambda *_: (0, QKV_TOTAL),
                         pipeline_mode=pl.Buffered(1)),
            tokspec(NA_WIDTH), _resident((NA_WIDTH, d)),
            *resspecs, *resspecs,
            _resident((NB_WIDTH, d)), _resident((d, d)), _resident((1, d)),
        ],
        out_specs=tokspec(d),
        out_shape=jax.ShapeDtypeStruct((batch, seq, d), F32),
        scratch_shapes=work + [pltpu.VMEM((GROUP_LANE_BLOCKS, tm, V7X_LANES), F32)] * (2 * NB_GROUPS),
        compiler_params=pltpu.CompilerParams(
            dimension_semantics=("arbitrary", "arbitrary"),
            vmem_limit_bytes=_vmem_limit(blocks, scratch, temps),
        ),
        name="merge",
    )(x, pre_g, wgate, oa, wa, *obs, *lses, wb, wout, post_g)


def kernel(x, ffn1_pre_g, ffn1_w_gate, ffn1_w_up, ffn1_w_down, ffn1_post_g, mix_pre_g, w_in, na_rpb, w_branch_a, w_branch_b, w_out, mix_post_g, ffn2_pre_g, ffn2_w_gate, ffn2_w_up, ffn2_w_down, ffn2_post_g):
    batch, seq, d = x.shape
    depth = ffn1_pre_g.shape[0]
    gate_off = QKV_TOTAL
    for window, dilation in DIL_PAIRS:
        assert window // (2 * dilation) == DIL_HALF
    for l in range(depth):
        x = _ffn(x.reshape(batch * seq, d), ffn1_pre_g[l][None], ffn1_w_gate[l], ffn1_w_up[l],
                 ffn1_w_down[l], ffn1_post_g[l][None])
        x = x.reshape(batch, seq, d)
        qa, ka, va, qb, kb, vb = _qkv(x, mix_pre_g[l][None], w_in[l])
        oa = _nbr_attention(qa, ka, va, na_rpb[l])
        obs, lses = [], []
        for g in range(NB_GROUPS):
            o_g, lse_g = _dil_attention(qb[g], kb[g], vb[g], g)
            obs.append(o_g)
            lses.append(lse_g)
        x = _merge(x, mix_pre_g[l][None], w_in[l],
                   oa, w_branch_a[l].astype(BF16), obs, lses, w_branch_b[l].astype(BF16),
                   w_out[l].astype(BF16), mix_post_g[l][None])
        x = _ffn(x.reshape(batch * seq, d), ffn2_pre_g[l][None], ffn2_w_gate[l], ffn2_w_up[l],
                 ffn2_w_down[l], ffn2_post_g[l][None])
        x = x.reshape(batch, seq, d)
    return x
```

```python
import functools
import math

import jax
import jax.numpy as jnp
from jax import lax
from jax.experimental import pallas as pl
from jax.experimental.pallas import tpu as pltpu

D_MODEL = 1024
HEAD_DIM = 64
NA_HEADS = 8
NA_WIDTH = NA_HEADS * HEAD_DIM
NA_ROWS = 8
NA_COLS = 16
GRID_W = 64
DIL_PAIRS = ((128, 1), (512, 4), (2048, 16))
DILATIONS = tuple(d for _, d in DIL_PAIRS)
NB_GROUPS = len(DIL_PAIRS)
NB_HEADS_PER_GROUP = 4
NB_HEADS = NB_GROUPS * NB_HEADS_PER_GROUP
NB_WIDTH = NB_HEADS * HEAD_DIM
GROUP_WIDTH = NB_HEADS_PER_GROUP * HEAD_DIM
ALIBI_MAX_EXP = 8.0
D_FF = 2816
NORM_EPS = 1e-6
ATTN_SCALE = HEAD_DIM ** -0.5
LOG2E = math.log2(math.e)
LN2 = math.log(2.0)
Q_SCALE = ATTN_SCALE * LOG2E

V7X_LANES = 128
GROUP_LANE_BLOCKS = GROUP_WIDTH // V7X_LANES
V7X_VMEM_BYTES = 64 * 1024 * 1024
V7X_VMEM_RESERVE = 6 * 1024 * 1024

BF16 = jnp.bfloat16
F32 = jnp.float32


def _vmem_limit(block_bytes, scratch_bytes, temp_bytes):
    need = block_bytes + scratch_bytes + temp_bytes
    assert need <= V7X_VMEM_BYTES - V7X_VMEM_RESERVE, need
    return int(need)


def _nbytes(shape, dtype):
    return math.prod(shape) * jnp.dtype(dtype).itemsize


def _resident(shape):
    nd = len(shape)
    return pl.BlockSpec(shape, lambda *_: (0,) * nd, pipeline_mode=pl.Buffered(1))


def _rmsnorm_f32(x, g):
    return x * lax.rsqrt(jnp.mean(x * x, axis=-1, keepdims=True) + NORM_EPS) * g


def _dot(a, b):
    return jnp.dot(a, b, preferred_element_type=F32)


def _dot_nt(a, b):
    return lax.dot_general(a, b, (((1,), (1,)), ((), ())), preferred_element_type=F32)


def _pipelined_attention(n_groups, unroll, scores, probs, finish, s_scr, p_scr, stat_scrs):
    assert n_groups % 2 == 0 and n_groups >= 2
    tiles = lambda i: [i * unroll + u for u in range(unroll)]

    def stage_scores(i, slot):
        for u, t in enumerate(tiles(i)):
            s_scr[slot, u] = scores(t)

    def stage_probs(slot):
        for u in range(unroll):
            p, *stats = probs(s_scr[slot, u])
            p_scr[slot, u] = p
            for ref, stat in zip(stat_scrs, stats):
                ref[slot, u] = stat

    def stage_finish(i, slot):
        for u, t in enumerate(tiles(i)):
            finish(t, p_scr[slot, u], *[ref[slot, u] for ref in stat_scrs])

    def step(i, parity):
        stage_scores(i + 1, 1 - parity)
        stage_finish(i - 1, 1 - parity)
        stage_probs(parity)

    stage_scores(0, 0)
    stage_scores(1, 1)
    stage_probs(0)

    def body(i, carry):
        @pl.when(i % 2 == 1)
        def _():
            step(i, 1)

        @pl.when(i % 2 == 0)
        def _():
            step(i, 0)

        return carry

    lax.fori_loop(1, n_groups - 1, body, 0)
    stage_finish(n_groups - 2, 0)
    stage_probs(1)
    stage_finish(n_groups - 1, 1)


FFN_TM = 512
SUB_TM = 512
FFN_CK = 256


def _sub_tiles(tm):
    return [slice(k * SUB_TM, (k + 1) * SUB_TM) for k in range(tm // SUB_TM)]


FFN_CHUNKS = [slice(c * FFN_CK, (c + 1) * FFN_CK) for c in range(D_FF // FFN_CK)]


def _ffn_weight_copies(wg_hbm, wu_hbm, wd_hbm, wg_ref, wu_ref, wd_ref, sem):
    gate_up = [(pltpu.make_async_copy(wg_hbm.at[:, cols], wg_ref.at[:, cols], sem.at[0, c]),
                pltpu.make_async_copy(wu_hbm.at[:, cols], wu_ref.at[:, cols], sem.at[1, c]))
               for c, cols in enumerate(FFN_CHUNKS)]
    down = [pltpu.make_async_copy(wd_hbm.at[cols, :], wd_ref.at[cols, :], sem.at[2, c])
            for c, cols in enumerate(FFN_CHUNKS)]
    return gate_up, down


def _ffn_body(x_ref, pre_g_ref, wg_ref, wu_ref, wd_ref, post_g_ref, o_ref, h_ref, a_ref, wait_gate_up, wait_down):
    h_ref[...] = _rmsnorm_f32(x_ref[...], pre_g_ref[...]).astype(BF16)
    for c, cols in enumerate(FFN_CHUNKS):
        wait_gate_up(c)
        h = h_ref[...]
        g = _dot(h, wg_ref[:, cols].astype(BF16))
        u = _dot(h, wu_ref[:, cols].astype(BF16))
        a_ref[:, cols] = (g * jax.nn.sigmoid(g) * u).astype(BF16)
    f = None
    for c, cols in enumerate(FFN_CHUNKS):
        wait_down(c)
        part = _dot(a_ref[:, cols], wd_ref[cols, :].astype(BF16))
        f = part if f is None else f + part
    o_ref[...] = x_ref[...] + 0.5 * _rmsnorm_f32(f, post_g_ref[...])


def _ffn_kernel(x_ref, pre_g_ref, wg_hbm, wu_hbm, wd_hbm, post_g_ref, o_ref, h_ref, a_ref,
                wg_ref, wu_ref, wd_ref, sem):
    refs = (x_ref, pre_g_ref, wg_ref, wu_ref, wd_ref, post_g_ref, o_ref, h_ref, a_ref)
    first = pl.program_id(0) == 0

    @pl.when(first)
    def _():
        gate_up, down = _ffn_weight_copies(wg_hbm, wu_hbm, wd_hbm, wg_ref, wu_ref, wd_ref, sem)
        for cg, cu in gate_up:
            cg.start()
            cu.start()
        for cd in down:
            cd.start()

        def wait_gate_up(c):
            gate_up[c][0].wait()
            gate_up[c][1].wait()

        _ffn_body(*refs, wait_gate_up, lambda c: down[c].wait())

    @pl.when(jnp.logical_not(first))
    def _():
        _ffn_body(*refs, lambda c: None, lambda c: None)


def _ffn(x, pre_g, wg, wu, wd, post_g):
    n, d = x.shape
    tm = FFN_TM
    row = lambda i: (i, 0)
    n_chunks = len(FFN_CHUNKS)
    blocks = 2 * 2 * _nbytes((tm, d), F32) + 2 * _nbytes((1, d), F32)
    scratch = _nbytes((tm, d), BF16) + _nbytes((tm, D_FF), BF16) + 3 * _nbytes((d, D_FF), wg.dtype)
    temps = 4 * _nbytes((tm, d), F32)
    hbm = pl.BlockSpec(memory_space=pl.ANY)
    return pl.pallas_call(
        _ffn_kernel,
        grid=(n // tm,),
        in_specs=[pl.BlockSpec((tm, d), row), _resident((1, d)), hbm, hbm, hbm, _resident((1, d))],
        out_specs=pl.BlockSpec((tm, d), row),
        out_shape=jax.ShapeDtypeStruct((n, d), F32),
        scratch_shapes=[
            pltpu.VMEM((tm, d), BF16), pltpu.VMEM((tm, D_FF), BF16),
            pltpu.VMEM((d, D_FF), wg.dtype), pltpu.VMEM((d, D_FF), wu.dtype), pltpu.VMEM((D_FF, d), wd.dtype),
            pltpu.SemaphoreType.DMA((3, n_chunks)),
        ],
        compiler_params=pltpu.CompilerParams(
            dimension_semantics=("arbitrary",),
            vmem_limit_bytes=_vmem_limit(blocks, scratch, temps),
        ),
        name="ffn",
    )(x, pre_g, wg, wu, wd, post_g)


QKV_TM = 1024
QKV_TOTAL = 3 * NA_WIDTH + 3 * NB_WIDTH
N_DIL_OUTS = 3 * NB_GROUPS
STAGED_OUTS = [i for i in range(N_DIL_OUTS) if DILATIONS[i % NB_GROUPS] > 1]
N_STAGED_OUTS = len(STAGED_OUTS)


def _dil_cols(idx):
    which, g = divmod(idx, NB_GROUPS)
    off = 3 * NA_WIDTH + which * NB_WIDTH + g * GROUP_WIDTH
    return slice(off, off + GROUP_WIDTH)


def _na_cols(idx):
    return slice(idx * NA_WIDTH, (idx + 1) * NA_WIDTH)


QKV_COLS = [_dil_cols(i) for i in range(N_DIL_OUTS)] + [_na_cols(i) for i in range(3)]


def _qkv_kernel(x_ref, g_ref, w_hbm, *refs):
    w_ref, sem = refs[-2:]
    first = (pl.program_id(0) == 0) & (pl.program_id(1) == 0)

    @pl.when(first)
    def _():
        copies = [pltpu.make_async_copy(w_hbm.at[:, cols], w_ref.at[:, cols], sem.at[c])
                  for c, cols in enumerate(QKV_COLS)]
        for cp in copies:
            cp.start()
        _qkv_body(x_ref, g_ref, w_ref, *refs[:-2], wait=lambda c: copies[c].wait())

    @pl.when(jnp.logical_not(first))
    def _():
        _qkv_body(x_ref, g_ref, w_ref, *refs[:-2], wait=lambda c: None)


def _qkv_body(x_ref, g_ref, w_ref, *refs, wait):
    na_refs, dil_refs, y_refs = refs[:3], refs[3:3 + N_DIL_OUTS], refs[3 + N_DIL_OUTS:]
    for k, rows in enumerate(_sub_tiles(x_ref.shape[0])):
        h = _rmsnorm_f32(x_ref[rows], g_ref[...]).astype(BF16)
        for idx, o_ref in enumerate(dil_refs):
            which, g = divmod(idx, NB_GROUPS)
            dil = DILATIONS[g]
            if k == 0:
                wait(idx)
            y = _dot(h, w_ref[:, _dil_cols(idx)].astype(BF16))
            if which == 0:
                y = y * Q_SCALE
            if dil == 1:
                o_ref[0, rows] = y.astype(BF16)
                continue
            y_ref = y_refs[k * N_STAGED_OUTS + STAGED_OUTS.index(idx)]
            per_res = SUB_TM // dil
            for c in range(GROUP_LANE_BLOCKS):
                y_ref[c] = y[:, c * V7X_LANES:(c + 1) * V7X_LANES]
            for r in range(dil):
                for c in range(GROUP_LANE_BLOCKS):
                    o_ref[r, k * per_res:(k + 1) * per_res, c * V7X_LANES:(c + 1) * V7X_LANES] = (
                        y_ref[c, pl.ds(r, per_res, stride=dil), :].astype(BF16))
        for idx, o_ref in enumerate(na_refs):
            if k == 0:
                wait(N_DIL_OUTS + idx)
            y = _dot(h, w_ref[:, _na_cols(idx)].astype(BF16))
            if idx == 0:
                y = y * Q_SCALE
            o_ref[rows] = y.astype(BF16)


def _qkv(x, g, w_in):
    batch, seq, d = x.shape
    tm = QKV_TM
    tok = lambda b, i: (b, i, 0)
    res = lambda b, i: (b, 0, i, 0)
    out_specs = [pl.BlockSpec((None, tm, NA_WIDTH), tok)] * 3
    out_shape = [jax.ShapeDtypeStruct((batch, seq, NA_WIDTH), BF16)] * 3
    for _ in range(3):
        for dil in DILATIONS:
            out_specs.append(pl.BlockSpec((None, dil, tm // dil, GROUP_WIDTH), res))
            out_shape.append(jax.ShapeDtypeStruct((batch, dil, seq // dil, GROUP_WIDTH), BF16))
    blocks = 2 * _nbytes((tm, d), F32) + 2 * _nbytes((tm, QKV_TOTAL), BF16)
    stage = (GROUP_LANE_BLOCKS, SUB_TM, V7X_LANES)
    n_stage = N_STAGED_OUTS * (tm // SUB_TM)
    scratch = n_stage * _nbytes(stage, F32) + _nbytes((d, QKV_TOTAL), w_in.dtype)
    temps = 2 * _nbytes((SUB_TM, d), F32) + 2 * _nbytes((SUB_TM, NB_WIDTH), F32)
    outs = pl.pallas_call(
        _qkv_kernel,
        grid=(batch, seq // tm),
        in_specs=[pl.BlockSpec((None, tm, d), tok), _resident((1, d)), pl.BlockSpec(memory_space=pl.ANY)],
        out_specs=out_specs,
        out_shape=out_shape,
        scratch_shapes=([pltpu.VMEM(stage, F32)] * n_stage
                        + [pltpu.VMEM((d, QKV_TOTAL), w_in.dtype), pltpu.SemaphoreType.DMA((len(QKV_COLS),))]),
        compiler_params=pltpu.CompilerParams(
            dimension_semantics=("arbitrary", "arbitrary"),
            vmem_limit_bytes=_vmem_limit(blocks, scratch, temps),
        ),
        name="qkv",
    )(x, g, w_in)
    qa, ka, va = outs[:3]
    qb, kb, vb = (outs[3 + w * NB_GROUPS:3 + (w + 1) * NB_GROUPS] for w in range(3))
    return qa, ka, va, qb, kb, vb


NA_HEADS_PER_STEP = V7X_LANES // HEAD_DIM
NA_SPAN = NA_ROWS * GRID_W
NA_VARIANTS = NA_ROWS
NA_RPB_ROWS = 2 * NA_ROWS - 1
NA_RPB_COLS = 2 * NA_COLS - 1
NA_UNROLL = 16


def _nbr_build_bias(rep_ref, tbl_ref):
    width = NA_RPB_ROWS * GRID_W
    qc = lax.broadcasted_iota(jnp.int32, (GRID_W, width), 0)
    kc = lax.broadcasted_iota(jnp.int32, (GRID_W, width), 1) % GRID_W
    diff = kc - qc
    col0 = jnp.clip(qc - NA_COLS // 2, 0, GRID_W - NA_COLS)
    col_ok = (kc >= col0) & (kc < col0 + NA_COLS)
    for h in range(NA_HEADS_PER_STEP):
        w = jnp.full((GRID_W, width), -jnp.inf, F32)
        for k in range(NA_RPB_COLS):
            w = jnp.where(diff == k - (NA_COLS - 1), rep_ref[h, k:k + 1, :], w)
        w = jnp.where(col_ok, w * LOG2E, -jnp.inf)
        for variant in range(NA_VARIANTS):
            tbl_ref[h, variant] = w[:, variant * GRID_W:variant * GRID_W + NA_SPAN]


def _nbr_kernel(q_ref, k_ref, v_ref, rep_ref, o_ref, tbl_ref, s_scr, p_scr, *, rows):
    hp = NA_HEADS_PER_STEP
    ones = jnp.ones((NA_SPAN, V7X_LANES), BF16)

    @pl.when(pl.program_id(1) == 0)
    def _():
        _nbr_build_bias(rep_ref, tbl_ref)

    lane_head = lax.broadcasted_iota(jnp.int32, (GRID_W, V7X_LANES), 1) // HEAD_DIM

    def scores(r):
        row0 = jnp.clip(r - NA_ROWS // 2, 0, rows - NA_ROWS)
        variant = row0 - r + (NA_ROWS - 1)
        q = q_ref[pl.ds(pl.multiple_of(r * GRID_W, GRID_W), GRID_W), :]
        kw = k_ref[pl.ds(pl.multiple_of(row0 * GRID_W, GRID_W), NA_SPAN), :]
        qs = jnp.concatenate([jnp.where(lane_head == h, q, jnp.zeros_like(q)) for h in range(hp)], axis=0)
        return _dot_nt(qs, kw) + tbl_ref[:, variant].reshape(hp * GRID_W, NA_SPAN)

    def probs(s):
        return (jnp.exp2(s - jnp.max(s, axis=-1, keepdims=True)).astype(BF16),)

    def finish(r, p):
        row0 = jnp.clip(r - NA_ROWS // 2, 0, rows - NA_ROWS)
        vw = v_ref[pl.ds(pl.multiple_of(row0 * GRID_W, GRID_W), NA_SPAN), :]
        o = _dot(p, jnp.concatenate([vw, ones], axis=1))
        out, den = o[:GRID_W, :V7X_LANES], o[:GRID_W, V7X_LANES:]
        for h in range(1, hp):
            rows_h = slice(h * GRID_W, (h + 1) * GRID_W)
            out = jnp.where(lane_head == h, o[rows_h, :V7X_LANES], out)
            den = jnp.where(lane_head == h, o[rows_h, V7X_LANES:], den)
        o_ref[pl.ds(pl.multiple_of(r * GRID_W, GRID_W), GRID_W), :] = (out * (1.0 / den)).astype(BF16)

    _pipelined_attention(rows // NA_UNROLL, NA_UNROLL, scores, probs, finish, s_scr, p_scr, ())


def _nbr_attention(q, k, v, rpb):
    batch, seq, width = q.shape
    rows = seq // GRID_W
    n_pairs = width // V7X_LANES
    rep = jnp.repeat(rpb.astype(F32).transpose(0, 2, 1), GRID_W, axis=-1)
    tok = lambda p, b: (b, 0, p)
    tbl_shape = (NA_HEADS_PER_STEP, NA_VARIANTS, GRID_W, NA_SPAN)
    rep_block = (NA_HEADS_PER_STEP, NA_RPB_COLS, NA_RPB_ROWS * GRID_W)
    blocks = 2 * 4 * _nbytes((seq, V7X_LANES), BF16) + 2 * _nbytes((NA_HEADS_PER_STEP, 32, 1024), F32)
    tile = (2, NA_UNROLL, NA_HEADS_PER_STEP * GRID_W, NA_SPAN)
    scratch = _nbytes(tbl_shape, F32) + _nbytes(tile, F32) + _nbytes(tile, BF16)
    temps = NA_UNROLL * _nbytes(tile[2:], F32)
    return pl.pallas_call(
        functools.partial(_nbr_kernel, rows=rows),
        grid=(n_pairs, batch),
        in_specs=[
            pl.BlockSpec((None, seq, V7X_LANES), tok),
            pl.BlockSpec((None, seq, V7X_LANES), tok),
            pl.BlockSpec((None, seq, V7X_LANES), tok),
            pl.BlockSpec(rep_block, lambda p, b: (p, 0, 0)),
        ],
        out_specs=pl.BlockSpec((None, seq, V7X_LANES), tok),
        out_shape=jax.ShapeDtypeStruct((batch, seq, width), BF16),
        scratch_shapes=[pltpu.VMEM(tbl_shape, F32), pltpu.VMEM(tile, F32), pltpu.VMEM(tile, BF16)],
        compiler_params=pltpu.CompilerParams(
            dimension_semantics=("arbitrary", "arbitrary"),
            vmem_limit_bytes=_vmem_limit(blocks, scratch, temps),
        ),
        name="nbr_attn",
    )(q, k, v, rep)


DIL_HALF = 64
DIL_TQ = 128
DIL_TK = DIL_TQ + 2 * DIL_HALF
DIL_VARIANTS = 3
DIL_STEP_ROWS = 4096
DIL_UNROLL = 8


def _dil_bias_table(group, dilation):
    heads = jnp.arange(NB_HEADS_PER_GROUP, dtype=F32) + group * NB_HEADS_PER_GROUP
    slopes = jnp.exp2(-ALIBI_MAX_EXP * (heads + 1.0) / NB_HEADS)
    offs = jnp.array([0, -DIL_HALF, -2 * DIL_HALF], jnp.int32)
    rel = offs[:, None, None] + jnp.arange(DIL_TK)[None, None, :] - jnp.arange(DIL_TQ)[None, :, None]
    dist = (dilation * jnp.abs(rel)).astype(F32)
    bias = -(slopes[None, :, None, None] * dist[:, None]) * LOG2E
    bias = jnp.where((jnp.abs(rel) <= DIL_HALF)[:, None], bias, -jnp.inf)
    return bias.reshape(DIL_VARIANTS, NB_HEADS_PER_GROUP * DIL_TQ, DIL_TK)


def _dil_kernel(q_ref, k_ref, v_ref, bias_ref, o_ref, lse_ref, s_scr, p_scr, m_scr, *, seq_len, stretch):
    nh = NB_HEADS_PER_GROUP
    hpb = V7X_LANES // HEAD_DIM
    ones = jnp.ones((DIL_TK, V7X_LANES), BF16)
    group_lane_head = lax.broadcasted_iota(jnp.int32, (DIL_TQ, GROUP_WIDTH), 1) // HEAD_DIM
    lane_head = lax.broadcasted_iota(jnp.int32, (DIL_TQ, V7X_LANES), 1) // HEAD_DIM
    base = pl.program_id(2) * stretch
    tiles_per_residue = stretch // DIL_TQ

    def window(t):
        r, j = t // tiles_per_residue, t % tiles_per_residue
        qloc = pl.multiple_of(j * DIL_TQ, DIL_TQ)
        qs = base + qloc
        ws = jnp.clip(qs - DIL_HALF, 0, seq_len - DIL_TK)
        variant = (qs - ws) // DIL_HALF
        return r, qloc, pl.multiple_of(ws, DIL_HALF), variant

    def scores(t):
        r, qloc, ws, variant = window(t)
        q = q_ref[r, pl.ds(qloc, DIL_TQ), :]
        kw = k_ref[r, pl.ds(ws, DIL_TK), :]
        qst = jnp.concatenate([jnp.where(group_lane_head == h, q, jnp.zeros_like(q)) for h in range(nh)],
                              axis=0)
        return _dot_nt(qst, kw) + bias_ref[variant]

    def probs(s):
        m = jnp.max(s, axis=-1, keepdims=True)
        return jnp.exp2(s - m).astype(BF16), jnp.broadcast_to(m, (s.shape[0], V7X_LANES))

    def finish(t, p, m):
        r, qloc, ws, _ = window(t)
        vw = v_ref[r, pl.ds(ws, DIL_TK), :]
        outs, lses = [], []
        for c in range(GROUP_LANE_BLOCKS):
            rows_c = slice(c * hpb * DIL_TQ, (c + 1) * hpb * DIL_TQ)
            v_aug = jnp.concatenate([vw[:, c * V7X_LANES:(c + 1) * V7X_LANES], ones], axis=1)
            o = _dot(p[rows_c], v_aug)
            m_c = m[rows_c]
            out_c, den_c, max_c = o[:DIL_TQ, :V7X_LANES], o[:DIL_TQ, V7X_LANES:], m_c[:DIL_TQ]
            for h in range(1, hpb):
                rows_h = slice(h * DIL_TQ, (h + 1) * DIL_TQ)
                out_c = jnp.where(lane_head == h, o[rows_h, :V7X_LANES], out_c)
                den_c = jnp.where(lane_head == h, o[rows_h, V7X_LANES:], den_c)
                max_c = jnp.where(lane_head == h, m_c[rows_h], max_c)
            outs.append(out_c * (1.0 / den_c))
            lses.append(max_c * LN2 + jnp.log(den_c))
        o_ref[r, pl.ds(qloc, DIL_TQ), :] = jnp.concatenate(outs, axis=1).astype(BF16)
        lse_ref[r, pl.ds(qloc, DIL_TQ), :] = jnp.concatenate(lses, axis=1)

    n_tiles = q_ref.shape[0] * tiles_per_residue
    _pipelined_attention(n_tiles // DIL_UNROLL, DIL_UNROLL, scores, probs, finish, s_scr, p_scr, (m_scr,))


def _dil_attention(q, k, v, group):
    batch, dilation, seq_len, width = q.shape
    stretch = min(seq_len, DIL_STEP_ROWS)
    res_blk = DIL_STEP_ROWS // stretch
    assert dilation % res_blk == 0 and seq_len % stretch == 0
    qmap = lambda b, r, s: (b, r, s, 0)
    kvmap = lambda b, r, s: (b, r, 0, 0)
    bias = _dil_bias_table(group, dilation)
    tile = (2, DIL_UNROLL, NB_HEADS_PER_GROUP * DIL_TQ, DIL_TK)
    stat = tile[:3] + (V7X_LANES,)
    kv_buffers = 2
    kv_spec = pl.BlockSpec((None, res_blk, seq_len, width), kvmap)
    blocks = (2 * (2 * _nbytes((DIL_STEP_ROWS, width), BF16) + _nbytes((DIL_STEP_ROWS, width), F32))
              + kv_buffers * 2 * _nbytes((res_blk * seq_len, width), BF16) + _nbytes(bias.shape, F32))
    scratch = _nbytes(tile, F32) + _nbytes(tile, BF16) + _nbytes(stat, F32)
    temps = DIL_UNROLL * _nbytes(tile[2:], F32)
    return pl.pallas_call(
        functools.partial(_dil_kernel, seq_len=seq_len, stretch=stretch),
        grid=(batch, dilation // res_blk, seq_len // stretch),
        in_specs=[
            pl.BlockSpec((None, res_blk, stretch, width), qmap),
            kv_spec,
            kv_spec,
            _resident(bias.shape),
        ],
        out_specs=[pl.BlockSpec((None, res_blk, stretch, width), qmap)] * 2,
        out_shape=[jax.ShapeDtypeStruct(q.shape, BF16), jax.ShapeDtypeStruct(q.shape, F32)],
        scratch_shapes=[pltpu.VMEM(tile, F32), pltpu.VMEM(tile, BF16), pltpu.VMEM(stat, F32)],
        compiler_params=pltpu.CompilerParams(
            dimension_semantics=("arbitrary", "arbitrary", "arbitrary"),
            vmem_limit_bytes=_vmem_limit(blocks, scratch, temps),
        ),
        name=f"dil_attn_g{group}",
    )(q, k, v, bias)


MERGE_TM = 1024
MERGE_CN = 256


def _to_token_order(src_ref, scr_ref, dil, k):
    per_res = SUB_TM // dil
    res_rows = slice(k * per_res, (k + 1) * per_res)
    if dil == 1:
        return src_ref[0, res_rows].astype(F32)
    for r in range(dil):
        for c in range(GROUP_LANE_BLOCKS):
            scr_ref[c, pl.ds(k * SUB_TM + r, per_res, stride=dil), :] = (
                src_ref[r, res_rows, c * V7X_LANES:(c + 1) * V7X_LANES].astype(F32))
    rows = slice(k * SUB_TM, (k + 1) * SUB_TM)
    return jnp.concatenate([scr_ref[c, rows] for c in range(GROUP_LANE_BLOCKS)], axis=-1)


MERGE_CHUNKS = [slice(n * MERGE_CN, (n + 1) * MERGE_CN) for n in range(D_MODEL // MERGE_CN)]


def _shift(cols, by):
    return slice(cols.start + by, cols.stop + by)


def _merge_kernel(x_ref, pre_g_ref, win_hbm, oa_ref, wa_hbm, *refs):
    ng = NB_GROUPS
    wb_hbm, wout_hbm = refs[2 * ng:2 * ng + 2]
    wgate_ref, wa_ref, wb_ref, wout_ref, sem = refs[-5:]
    body_refs = (x_ref, pre_g_ref, wgate_ref, oa_ref, wa_ref, *refs[:2 * ng], wb_ref, wout_ref, *refs[2 * ng + 2:-5])
    first = (pl.program_id(0) == 0) & (pl.program_id(1) == 0)

    @pl.when(first)
    def _():
        chunk_copies = []
        for n, cols in enumerate(MERGE_CHUNKS):
            chunk_copies.append([
                pltpu.make_async_copy(win_hbm.at[:, _shift(cols, QKV_TOTAL)], wgate_ref.at[:, cols], sem.at[0, n]),
                pltpu.make_async_copy(win_hbm.at[:, _shift(cols, QKV_TOTAL + D_MODEL)],
                                      wgate_ref.at[:, _shift(cols, D_MODEL)], sem.at[1, n]),
                pltpu.make_async_copy(wa_hbm.at[:, cols], wa_ref.at[:, cols], sem.at[2, n]),
                pltpu.make_async_copy(wb_hbm.at[:, cols], wb_ref.at[:, cols], sem.at[3, n]),
            ])
        out_copy = pltpu.make_async_copy(wout_hbm, wout_ref, sem.at[4, 0])
        for group in chunk_copies:
            for cp in group:
                cp.start()
        out_copy.start()

        def wait_chunk(n):
            for cp in chunk_copies[n]:
                cp.wait()

        _merge_body(*body_refs, wait_chunk=wait_chunk, wait_out=out_copy.wait)

    @pl.when(jnp.logical_not(first))
    def _():
        _merge_body(*body_refs, wait_chunk=lambda n: None, wait_out=lambda: None)


def _merge_body(x_ref, pre_g_ref, wgate_ref, oa_ref, wa_ref, *refs, wait_chunk, wait_out):
    ng = NB_GROUPS
    o_refs, lse_refs = refs[:ng], refs[ng:2 * ng]
    wb_ref, wout_ref, post_g_ref, out_ref, h_ref, ob_ref, mg_ref = refs[2 * ng:2 * ng + 7]
    scr_refs = iter(refs[2 * ng + 7:])
    o_scr = [next(scr_refs) if dil > 1 else None for dil in DILATIONS]
    lse_scr = [next(scr_refs) if dil > 1 else None for dil in DILATIONS]
    subs = _sub_tiles(x_ref.shape[0])
    for k, rows in enumerate(subs):
        h_ref[rows] = _rmsnorm_f32(x_ref[rows], pre_g_ref[...]).astype(BF16)
        os_ = [_to_token_order(r, s, d, k) for r, s, d in zip(o_refs, o_scr, DILATIONS)]
        lses = [_to_token_order(r, s, d, k) for r, s, d in zip(lse_refs, lse_scr, DILATIONS)]
        mx = jnp.maximum(jnp.maximum(lses[0], lses[1]), lses[2])
        es = [jnp.exp(l - mx) for l in lses]
        inv = 1.0 / (es[0] + es[1] + es[2])
        for g in range(ng):
            ob_ref[rows, g * GROUP_WIDTH:(g + 1) * GROUP_WIDTH] = (os_[g] * (es[g] * inv)).astype(BF16)
    for k, rows in enumerate(subs):
        for n, cols in enumerate(MERGE_CHUNKS):
            cols_b = _shift(cols, D_MODEL)
            if k == 0:
                wait_chunk(n)
            h = h_ref[rows]
            ga = _dot(h, wgate_ref[:, cols].astype(BF16))
            gb = _dot(h, wgate_ref[:, cols_b].astype(BF16))
            ya = _dot(oa_ref[rows], wa_ref[:, cols])
            yb = _dot(ob_ref[rows], wb_ref[:, cols])
            mg_ref[rows, cols] = (jax.nn.sigmoid(ga) * ya + jax.nn.sigmoid(gb) * yb).astype(BF16)
    wait_out()
    for rows in subs:
        out_ref[rows] = _dot(mg_ref[rows], wout_ref[...])
    for rows in subs:
        out_ref[rows] = x_ref[rows] + _rmsnorm_f32(out_ref[rows], post_g_ref[...])


def _merge(x, pre_g, wgate, oa, wa, obs, lses, wb, wout, post_g):
    batch, seq, d = x.shape
    tm = MERGE_TM
    tok = lambda b, i: (b, i, 0)
    res = lambda b, i: (b, 0, i, 0)
    tokspec = lambda w: pl.BlockSpec((None, tm, w), tok)
    resspecs = [pl.BlockSpec((None, dil, tm // dil, GROUP_WIDTH), res) for dil in DILATIONS]
    blocks = (2 * 2 * _nbytes((tm, d), F32) + 2 * _nbytes((tm, NA_WIDTH), BF16)
              + 2 * 3 * (_nbytes((tm, GROUP_WIDTH), BF16) + _nbytes((tm, GROUP_WIDTH), F32)))
    work = [pltpu.VMEM((tm, d), BF16), pltpu.VMEM((tm, NB_WIDTH), BF16), pltpu.VMEM((tm, d), BF16)]
    n_interleave = 2 * sum(dil > 1 for dil in DILATIONS)
    weights = [pltpu.VMEM((d, 2 * d), wgate.dtype), pltpu.VMEM((NA_WIDTH, d), wa.dtype),
               pltpu.VMEM((NB_WIDTH, d), wb.dtype), pltpu.VMEM((d, d), wout.dtype)]
    scratch = (n_interleave * _nbytes((tm, GROUP_WIDTH), F32) + 2 * _nbytes((tm, d), BF16)
               + _nbytes((tm, NB_WIDTH), BF16) + _nbytes((d, 2 * d), wgate.dtype)
               + _nbytes((NA_WIDTH, d), wa.dtype) + _nbytes((d + NB_WIDTH, d), wb.dtype))
    temps = 2 * _nbytes((SUB_TM, d), F32)
    hbm = pl.BlockSpec(memory_space=pl.ANY)
    return pl.pallas_call(
        _merge_kernel,
        grid=(batch, seq // tm),
        in_specs=[
            tokspec(d), _resident((1, d)), hbm,
            tokspec(NA_WIDTH), hbm,
            *resspecs, *resspecs,
            hbm, hbm, _resident((1, d)),
        ],
        out_specs=tokspec(d),
        out_shape=jax.ShapeDtypeStruct((batch, seq, d), F32),
        scratch_shapes=(work + [pltpu.VMEM((GROUP_LANE_BLOCKS, tm, V7X_LANES), F32)] * n_interleave
                        + weights + [pltpu.SemaphoreType.DMA((5, len(MERGE_CHUNKS)))]),
        compiler_params=pltpu.CompilerParams(
            dimension_semantics=("arbitrary", "arbitrary"),
            vmem_limit_bytes=_vmem_limit(blocks, scratch, temps),
        ),
        name="merge",
    )(x, pre_g, wgate, oa, wa, *obs, *lses, wb, wout, post_g)


def kernel(x, ffn1_pre_g, ffn1_w_gate, ffn1_w_up, ffn1_w_down, ffn1_post_g, mix_pre_g, w_in, na_rpb, w_branch_a, w_branch_b, w_out, mix_post_g, ffn2_pre_g, ffn2_w_gate, ffn2_w_up, ffn2_w_down, ffn2_post_g):
    batch, seq, d = x.shape
    depth = ffn1_pre_g.shape[0]
    gate_off = QKV_TOTAL
    for window, dilation in DIL_PAIRS:
        assert window // (2 * dilation) == DIL_HALF
    for l in range(depth):
        x = _ffn(x.reshape(batch * seq, d), ffn1_pre_g[l][None], ffn1_w_gate[l], ffn1_w_up[l],
                 ffn1_w_down[l], ffn1_post_g[l][None])
        x = x.reshape(batch, seq, d)
        qa, ka, va, qb, kb, vb = _qkv(x, mix_pre_g[l][None], w_in[l])
        oa = _nbr_attention(qa, ka, va, na_rpb[l])
        obs, lses = [], []
        for g in range(NB_GROUPS):
            o_g, lse_g = _dil_attention(qb[g], kb[g], vb[g], g)
            obs.append(o_g)
            lses.append(lse_g)
        x = _merge(x, mix_pre_g[l][None], w_in[l],
                   oa, w_branch_a[l].astype(BF16), obs, lses, w_branch_b[l].astype(BF16),
                   w_out[l].astype(BF16), mix_post_g[l][None])
        x = _ffn(x.reshape(batch * seq, d), ffn2_pre_g[l][None], ffn2_w_gate[l], ffn2_w_up[l],
                 ffn2_w_down[l], ffn2_post_g[l][None])
        x = x.reshape(batch, seq, d)
    return x
```

```python
import functools
import math

import jax
import jax.numpy as jnp
from jax import lax
from jax.experimental import pallas as pl
from jax.experimental.pallas import tpu as pltpu

D_MODEL = 1024
HEAD_DIM = 64
NA_HEADS = 8
NA_WIDTH = NA_HEADS * HEAD_DIM
NA_ROWS = 8
NA_COLS = 16
GRID_W = 64
DIL_PAIRS = ((128, 1), (512, 4), (2048, 16))
DILATIONS = tuple(d for _, d in DIL_PAIRS)
NB_GROUPS = len(DIL_PAIRS)
NB_HEADS_PER_GROUP = 4
NB_HEADS = NB_GROUPS * NB_HEADS_PER_GROUP
NB_WIDTH = NB_HEADS * HEAD_DIM
GROUP_WIDTH = NB_HEADS_PER_GROUP * HEAD_DIM
ALIBI_MAX_EXP = 8.0
D_FF = 2816
NORM_EPS = 1e-6
ATTN_SCALE = HEAD_DIM ** -0.5
LOG2E = math.log2(math.e)
LN2 = math.log(2.0)
Q_SCALE = ATTN_SCALE * LOG2E

V7X_LANES = 128
GROUP_LANE_BLOCKS = GROUP_WIDTH // V7X_LANES
V7X_VMEM_BYTES = 64 * 1024 * 1024
V7X_VMEM_RESERVE = 6 * 1024 * 1024

BF16 = jnp.bfloat16
F32 = jnp.float32


def _vmem_limit(block_bytes, scratch_bytes, temp_bytes):
    need = block_bytes + scratch_bytes + temp_bytes
    assert need <= V7X_VMEM_BYTES - V7X_VMEM_RESERVE, need
    return int(need)


def _nbytes(shape, dtype):
    return math.prod(shape) * jnp.dtype(dtype).itemsize


def _resident(shape):
    nd = len(shape)
    return pl.BlockSpec(shape, lambda *_: (0,) * nd, pipeline_mode=pl.Buffered(1))


def _rmsnorm_f32(x, g):
    return x * lax.rsqrt(jnp.mean(x * x, axis=-1, keepdims=True) + NORM_EPS) * g


def _dot(a, b):
    return jnp.dot(a, b, preferred_element_type=F32)


def _dot_nt(a, b):
    return lax.dot_general(a, b, (((1,), (1,)), ((), ())), preferred_element_type=F32)


def _pipelined_attention(n_groups, unroll, scores, probs, finish, s_scr, p_scr, stat_scrs):
    assert n_groups % 2 == 0 and n_groups >= 2
    tiles = lambda i: [i * unroll + u for u in range(unroll)]

    def stage_scores(i, slot):
        for u, t in enumerate(tiles(i)):
            s_scr[slot, u] = scores(t)

    def stage_probs(slot):
        for u in range(unroll):
            p, *stats = probs(s_scr[slot, u])
            p_scr[slot, u] = p
            for ref, stat in zip(stat_scrs, stats):
                ref[slot, u] = stat

    def stage_finish(i, slot):
        for u, t in enumerate(tiles(i)):
            finish(t, p_scr[slot, u], *[ref[slot, u] for ref in stat_scrs])

    def step(i, parity):
        stage_scores(i + 1, 1 - parity)
        stage_finish(i - 1, 1 - parity)
        stage_probs(parity)

    stage_scores(0, 0)
    stage_scores(1, 1)
    stage_probs(0)

    def body(i, carry):
        @pl.when(i % 2 == 1)
        def _():
            step(i, 1)

        @pl.when(i % 2 == 0)
        def _():
            step(i, 0)

        return carry

    lax.fori_loop(1, n_groups - 1, body, 0)
    stage_finish(n_groups - 2, 0)
    stage_probs(1)
    stage_finish(n_groups - 1, 1)


FFN_TM = 512
SUB_TM = 512
FFN_CK = 256


def _sub_tiles(tm):
    return [slice(k * SUB_TM, (k + 1) * SUB_TM) for k in range(tm // SUB_TM)]


FFN_CHUNKS = [slice(c * FFN_CK, (c + 1) * FFN_CK) for c in range(D_FF // FFN_CK)]


def _ffn_weight_copies(wg_hbm, wu_hbm, wd_hbm, wg_ref, wu_ref, wd_ref, sem):
    gate_up = [(pltpu.make_async_copy(wg_hbm.at[:, cols], wg_ref.at[:, cols], sem.at[0, c]),
                pltpu.make_async_copy(wu_hbm.at[:, cols], wu_ref.at[:, cols], sem.at[1, c]))
               for c, cols in enumerate(FFN_CHUNKS)]
    down = [pltpu.make_async_copy(wd_hbm.at[cols, :], wd_ref.at[cols, :], sem.at[2, c])
            for c, cols in enumerate(FFN_CHUNKS)]
    return gate_up, down


def _ffn_body(x_ref, pre_g_ref, wg_ref, wu_ref, wd_ref, post_g_ref, o_ref, h_ref, a_ref, wait_gate_up, wait_down):
    h_ref[...] = _rmsnorm_f32(x_ref[...], pre_g_ref[...]).astype(BF16)
    for c, cols in enumerate(FFN_CHUNKS):
        wait_gate_up(c)
        h = h_ref[...]
        g = _dot(h, wg_ref[:, cols].astype(BF16))
        u = _dot(h, wu_ref[:, cols].astype(BF16))
        a_ref[:, cols] = (g * jax.nn.sigmoid(g) * u).astype(BF16)
    f = None
    for c, cols in enumerate(FFN_CHUNKS):
        wait_down(c)
        part = _dot(a_ref[:, cols], wd_ref[cols, :].astype(BF16))
        f = part if f is None else f + part
    o_ref[...] = x_ref[...] + 0.5 * _rmsnorm_f32(f, post_g_ref[...])


def _ffn_kernel(x_ref, pre_g_ref, wg_hbm, wu_hbm, wd_hbm, post_g_ref, o_ref, h_ref, a_ref,
                wg_ref, wu_ref, wd_ref, sem):
    refs = (x_ref, pre_g_ref, wg_ref, wu_ref, wd_ref, post_g_ref, o_ref, h_ref, a_ref)
    first = pl.program_id(0) == 0

    @pl.when(first)
    def _():
        gate_up, down = _ffn_weight_copies(wg_hbm, wu_hbm, wd_hbm, wg_ref, wu_ref, wd_ref, sem)
        for cg, cu in gate_up:
            cg.start()
            cu.start()
        for cd in down:
            cd.start()

        def wait_gate_up(c):
            gate_up[c][0].wait()
            gate_up[c][1].wait()

        _ffn_body(*refs, wait_gate_up, lambda c: down[c].wait())

    @pl.when(jnp.logical_not(first))
    def _():
        _ffn_body(*refs, lambda c: None, lambda c: None)


def _ffn(x, pre_g, wg, wu, wd, post_g):
    n, d = x.shape
    tm = FFN_TM
    row = lambda i: (i, 0)
    n_chunks = len(FFN_CHUNKS)
    blocks = 2 * 2 * _nbytes((tm, d), F32) + 2 * _nbytes((1, d), F32)
    scratch = _nbytes((tm, d), BF16) + _nbytes((tm, D_FF), BF16) + 3 * _nbytes((d, D_FF), wg.dtype)
    temps = 4 * _nbytes((tm, d), F32)
    hbm = pl.BlockSpec(memory_space=pl.ANY)
    return pl.pallas_call(
        _ffn_kernel,
        grid=(n // tm,),
        in_specs=[pl.BlockSpec((tm, d), row), _resident((1, d)), hbm, hbm, hbm, _resident((1, d))],
        out_specs=pl.BlockSpec((tm, d), row),
        out_shape=jax.ShapeDtypeStruct((n, d), F32),
        scratch_shapes=[
            pltpu.VMEM((tm, d), BF16), pltpu.VMEM((tm, D_FF), BF16),
            pltpu.VMEM((d, D_FF), wg.dtype), pltpu.VMEM((d, D_FF), wu.dtype), pltpu.VMEM((D_FF, d), wd.dtype),
            pltpu.SemaphoreType.DMA((3, n_chunks)),
        ],
        compiler_params=pltpu.CompilerParams(
            dimension_semantics=("arbitrary",),
            vmem_limit_bytes=_vmem_limit(blocks, scratch, temps),
        ),
        name="ffn",
    )(x, pre_g, wg, wu, wd, post_g)


QKV_TM = 1024
QKV_TOTAL = 3 * NA_WIDTH + 3 * NB_WIDTH
N_DIL_OUTS = 3 * NB_GROUPS
STAGED_OUTS = [i for i in range(N_DIL_OUTS) if DILATIONS[i % NB_GROUPS] > 1]
N_STAGED_OUTS = len(STAGED_OUTS)


def _qkv_kernel(x_ref, g_ref, w32_ref, *refs):
    na_refs, dil_refs = refs[:3], refs[3:3 + N_DIL_OUTS]
    w_ref, y_refs = refs[3 + N_DIL_OUTS], refs[4 + N_DIL_OUTS:]

    @pl.when((pl.program_id(0) == 0) & (pl.program_id(1) == 0))
    def _():
        for c in range(QKV_TOTAL // GROUP_WIDTH):
            cols = slice(c * GROUP_WIDTH, (c + 1) * GROUP_WIDTH)
            w_ref[:, cols] = w32_ref[:, cols].astype(BF16)

    for k, rows in enumerate(_sub_tiles(x_ref.shape[0])):
        h = _rmsnorm_f32(x_ref[rows], g_ref[...]).astype(BF16)
        for idx, o_ref in enumerate(dil_refs):
            which, g = divmod(idx, NB_GROUPS)
            dil = DILATIONS[g]
            off = 3 * NA_WIDTH + which * NB_WIDTH + g * GROUP_WIDTH
            y = _dot(h, w_ref[:, off:off + GROUP_WIDTH])
            if which == 0:
                y = y * Q_SCALE
            if dil == 1:
                o_ref[0, rows] = y.astype(BF16)
                continue
            y_ref = y_refs[k * N_STAGED_OUTS + STAGED_OUTS.index(idx)]
            per_res = SUB_TM // dil
            for c in range(GROUP_LANE_BLOCKS):
                y_ref[c] = y[:, c * V7X_LANES:(c + 1) * V7X_LANES]
            for r in range(dil):
                for c in range(GROUP_LANE_BLOCKS):
                    o_ref[r, k * per_res:(k + 1) * per_res, c * V7X_LANES:(c + 1) * V7X_LANES] = (
                        y_ref[c, pl.ds(r, per_res, stride=dil), :].astype(BF16))
        for idx, o_ref in enumerate(na_refs):
            y = _dot(h, w_ref[:, idx * NA_WIDTH:(idx + 1) * NA_WIDTH])
            if idx == 0:
                y = y * Q_SCALE
            o_ref[rows] = y.astype(BF16)


def _qkv(x, g, w_in):
    batch, seq, d = x.shape
    tm = QKV_TM
    tok = lambda b, i: (b, i, 0)
    res = lambda b, i: (b, 0, i, 0)
    out_specs = [pl.BlockSpec((None, tm, NA_WIDTH), tok)] * 3
    out_shape = [jax.ShapeDtypeStruct((batch, seq, NA_WIDTH), BF16)] * 3
    for _ in range(3):
        for dil in DILATIONS:
            out_specs.append(pl.BlockSpec((None, dil, tm // dil, GROUP_WIDTH), res))
            out_shape.append(jax.ShapeDtypeStruct((batch, dil, seq // dil, GROUP_WIDTH), BF16))
    blocks = 2 * _nbytes((tm, d), F32) + _nbytes((d, QKV_TOTAL), w_in.dtype) + 2 * _nbytes((tm, QKV_TOTAL), BF16)
    stage = (GROUP_LANE_BLOCKS, SUB_TM, V7X_LANES)
    n_stage = N_STAGED_OUTS * (tm // SUB_TM)
    scratch = n_stage * _nbytes(stage, F32) + _nbytes((d, QKV_TOTAL), BF16)
    temps = 2 * _nbytes((SUB_TM, d), F32)
    outs = pl.pallas_call(
        _qkv_kernel,
        grid=(batch, seq // tm),
        in_specs=[pl.BlockSpec((None, tm, d), tok), _resident((1, d)), _resident((d, QKV_TOTAL))],
        out_specs=out_specs,
        out_shape=out_shape,
        scratch_shapes=[pltpu.VMEM((d, QKV_TOTAL), BF16)] + [pltpu.VMEM(stage, F32)] * n_stage,
        compiler_params=pltpu.CompilerParams(
            dimension_semantics=("arbitrary", "arbitrary"),
            vmem_limit_bytes=_vmem_limit(blocks, scratch, temps),
        ),
        name="qkv",
    )(x, g, w_in)
    qa, ka, va = outs[:3]
    qb, kb, vb = (outs[3 + w * NB_GROUPS:3 + (w + 1) * NB_GROUPS] for w in range(3))
    return qa, ka, va, qb, kb, vb


NA_HEADS_PER_STEP = V7X_LANES // HEAD_DIM
NA_SPAN = NA_ROWS * GRID_W
NA_VARIANTS = NA_ROWS
NA_RPB_ROWS = 2 * NA_ROWS - 1
NA_RPB_COLS = 2 * NA_COLS - 1
NA_UNROLL = 16


def _nbr_build_bias(rep_ref, tbl_ref):
    width = NA_RPB_ROWS * GRID_W
    qc = lax.broadcasted_iota(jnp.int32, (GRID_W, width), 0)
    kc = lax.broadcasted_iota(jnp.int32, (GRID_W, width), 1) % GRID_W
    diff = kc - qc
    col0 = jnp.clip(qc - NA_COLS // 2, 0, GRID_W - NA_COLS)
    col_ok = (kc >= col0) & (kc < col0 + NA_COLS)
    for h in range(NA_HEADS_PER_STEP):
        w = jnp.full((GRID_W, width), -jnp.inf, F32)
        for k in range(NA_RPB_COLS):
            w = jnp.where(diff == k - (NA_COLS - 1), rep_ref[h, k:k + 1, :], w)
        w = jnp.where(col_ok, w * LOG2E, -jnp.inf)
        for variant in range(NA_VARIANTS):
            tbl_ref[h, variant] = w[:, variant * GRID_W:variant * GRID_W + NA_SPAN]


def _nbr_kernel(q_ref, k_ref, v_ref, rep_ref, o_ref, tbl_ref, s_scr, p_scr, *, rows):
    hp = NA_HEADS_PER_STEP
    ones = jnp.ones((NA_SPAN, V7X_LANES), BF16)

    @pl.when(pl.program_id(1) == 0)
    def _():
        _nbr_build_bias(rep_ref, tbl_ref)

    lane_head = lax.broadcasted_iota(jnp.int32, (GRID_W, V7X_LANES), 1) // HEAD_DIM

    def scores(r):
        row0 = jnp.clip(r - NA_ROWS // 2, 0, rows - NA_ROWS)
        variant = row0 - r + (NA_ROWS - 1)
        q = q_ref[pl.ds(pl.multiple_of(r * GRID_W, GRID_W), GRID_W), :]
        kw = k_ref[pl.ds(pl.multiple_of(row0 * GRID_W, GRID_W), NA_SPAN), :]
        qs = jnp.concatenate([jnp.where(lane_head == h, q, jnp.zeros_like(q)) for h in range(hp)], axis=0)
        return _dot_nt(qs, kw) + tbl_ref[:, variant].reshape(hp * GRID_W, NA_SPAN)

    def probs(s):
        return (jnp.exp2(s - jnp.max(s, axis=-1, keepdims=True)).astype(BF16),)

    def finish(r, p):
        row0 = jnp.clip(r - NA_ROWS // 2, 0, rows - NA_ROWS)
        vw = v_ref[pl.ds(pl.multiple_of(row0 * GRID_W, GRID_W), NA_SPAN), :]
        o = _dot(p, jnp.concatenate([vw, ones], axis=1))
        out, den = o[:GRID_W, :V7X_LANES], o[:GRID_W, V7X_LANES:]
        for h in range(1, hp):
            rows_h = slice(h * GRID_W, (h + 1) * GRID_W)
            out = jnp.where(lane_head == h, o[rows_h, :V7X_LANES], out)
            den = jnp.where(lane_head == h, o[rows_h, V7X_LANES:], den)
        o_ref[pl.ds(pl.multiple_of(r * GRID_W, GRID_W), GRID_W), :] = (out * (1.0 / den)).astype(BF16)

    _pipelined_attention(rows // NA_UNROLL, NA_UNROLL, scores, probs, finish, s_scr, p_scr, ())


def _nbr_attention(q, k, v, rpb):
    batch, seq, width = q.shape
    rows = seq // GRID_W
    n_pairs = width // V7X_LANES
    rep = jnp.repeat(rpb.astype(F32).transpose(0, 2, 1), GRID_W, axis=-1)
    tok = lambda p, b: (b, 0, p)
    tbl_shape = (NA_HEADS_PER_STEP, NA_VARIANTS, GRID_W, NA_SPAN)
    rep_block = (NA_HEADS_PER_STEP, NA_RPB_COLS, NA_RPB_ROWS * GRID_W)
    blocks = 2 * 4 * _nbytes((seq, V7X_LANES), BF16) + 2 * _nbytes((NA_HEADS_PER_STEP, 32, 1024), F32)
    tile = (2, NA_UNROLL, NA_HEADS_PER_STEP * GRID_W, NA_SPAN)
    scratch = _nbytes(tbl_shape, F32) + _nbytes(tile, F32) + _nbytes(tile, BF16)
    temps = NA_UNROLL * _nbytes(tile[2:], F32)
    return pl.pallas_call(
        functools.partial(_nbr_kernel, rows=rows),
        grid=(n_pairs, batch),
        in_specs=[
            pl.BlockSpec((None, seq, V7X_LANES), tok),
            pl.BlockSpec((None, seq, V7X_LANES), tok),
            pl.BlockSpec((None, seq, V7X_LANES), tok),
            pl.BlockSpec(rep_block, lambda p, b: (p, 0, 0)),
        ],
        out_specs=pl.BlockSpec((None, seq, V7X_LANES), tok),
        out_shape=jax.ShapeDtypeStruct((batch, seq, width), BF16),
        scratch_shapes=[pltpu.VMEM(tbl_shape, F32), pltpu.VMEM(tile, F32), pltpu.VMEM(tile, BF16)],
        compiler_params=pltpu.CompilerParams(
            dimension_semantics=("arbitrary", "arbitrary"),
            vmem_limit_bytes=_vmem_limit(blocks, scratch, temps),
        ),
        name="nbr_attn",
    )(q, k, v, rep)


DIL_HALF = 64
DIL_TQ = 128
DIL_TK = DIL_TQ + 2 * DIL_HALF
DIL_VARIANTS = 3
DIL_STEP_ROWS = 4096
DIL_UNROLL = 4


def _dil_bias_table(group, dilation):
    heads = jnp.arange(NB_HEADS_PER_GROUP, dtype=F32) + group * NB_HEADS_PER_GROUP
    slopes = jnp.exp2(-ALIBI_MAX_EXP * (heads + 1.0) / NB_HEADS)
    offs = jnp.array([0, -DIL_HALF, -2 * DIL_HALF], jnp.int32)
    rel = offs[:, None, None] + jnp.arange(DIL_TK)[None, None, :] - jnp.arange(DIL_TQ)[None, :, None]
    dist = (dilation * jnp.abs(rel)).astype(F32)
    bias = -(slopes[None, :, None, None] * dist[:, None]) * LOG2E
    bias = jnp.where((jnp.abs(rel) <= DIL_HALF)[:, None], bias, -jnp.inf)
    return bias.reshape(DIL_VARIANTS, NB_HEADS_PER_GROUP * DIL_TQ, DIL_TK)


def _dil_kernel(q_ref, k_ref, v_ref, bias_ref, o_ref, lse_ref, s_scr, p_scr, m_scr, *, seq_len, stretch):
    nh = NB_HEADS_PER_GROUP
    hpb = V7X_LANES // HEAD_DIM
    ones = jnp.ones((DIL_TK, V7X_LANES), BF16)
    group_lane_head = lax.broadcasted_iota(jnp.int32, (DIL_TQ, GROUP_WIDTH), 1) // HEAD_DIM
    lane_head = lax.broadcasted_iota(jnp.int32, (DIL_TQ, V7X_LANES), 1) // HEAD_DIM
    base = pl.program_id(2) * stretch
    tiles_per_residue = stretch // DIL_TQ

    def window(t):
        r, j = t // tiles_per_residue, t % tiles_per_residue
        qloc = pl.multiple_of(j * DIL_TQ, DIL_TQ)
        qs = base + qloc
        ws = jnp.clip(qs - DIL_HALF, 0, seq_len - DIL_TK)
        variant = (qs - ws) // DIL_HALF
        return r, qloc, pl.multiple_of(ws, DIL_HALF), variant

    def scores(t):
        r, qloc, ws, variant = window(t)
        q = q_ref[r, pl.ds(qloc, DIL_TQ), :]
        kw = k_ref[r, pl.ds(ws, DIL_TK), :]
        qst = jnp.concatenate([jnp.where(group_lane_head == h, q, jnp.zeros_like(q)) for h in range(nh)],
                              axis=0)
        return _dot_nt(qst, kw) + bias_ref[variant]

    def probs(s):
        m = jnp.max(s, axis=-1, keepdims=True)
        return jnp.exp2(s - m).astype(BF16), jnp.broadcast_to(m, (s.shape[0], V7X_LANES))

    def finish(t, p, m):
        r, qloc, ws, _ = window(t)
        vw = v_ref[r, pl.ds(ws, DIL_TK), :]
        outs, lses = [], []
        for c in range(GROUP_LANE_BLOCKS):
            rows_c = slice(c * hpb * DIL_TQ, (c + 1) * hpb * DIL_TQ)
            v_aug = jnp.concatenate([vw[:, c * V7X_LANES:(c + 1) * V7X_LANES], ones], axis=1)
            o = _dot(p[rows_c], v_aug)
            m_c = m[rows_c]
            out_c, den_c, max_c = o[:DIL_TQ, :V7X_LANES], o[:DIL_TQ, V7X_LANES:], m_c[:DIL_TQ]
            for h in range(1, hpb):
                rows_h = slice(h * DIL_TQ, (h + 1) * DIL_TQ)
                out_c = jnp.where(lane_head == h, o[rows_h, :V7X_LANES], out_c)
                den_c = jnp.where(lane_head == h, o[rows_h, V7X_LANES:], den_c)
                max_c = jnp.where(lane_head == h, m_c[rows_h], max_c)
            outs.append(out_c * (1.0 / den_c))
            lses.append(max_c * LN2 + jnp.log(den_c))
        o_ref[r, pl.ds(qloc, DIL_TQ), :] = jnp.concatenate(outs, axis=1).astype(BF16)
        lse_ref[r, pl.ds(qloc, DIL_TQ), :] = jnp.concatenate(lses, axis=1)

    n_tiles = q_ref.shape[0] * tiles_per_residue
    _pipelined_attention(n_tiles // DIL_UNROLL, DIL_UNROLL, scores, probs, finish, s_scr, p_scr, (m_scr,))


def _dil_attention(q, k, v, group):
    batch, dilation, seq_len, width = q.shape
    stretch = min(seq_len, DIL_STEP_ROWS)
    res_blk = DIL_STEP_ROWS // stretch
    assert dilation % res_blk == 0 and seq_len % stretch == 0
    qmap = lambda b, r, s: (b, r, s, 0)
    kvmap = lambda b, r, s: (b, r, 0, 0)
    bias = _dil_bias_table(group, dilation)
    tile = (2, DIL_UNROLL, NB_HEADS_PER_GROUP * DIL_TQ, DIL_TK)
    stat = tile[:3] + (V7X_LANES,)
    kv_buffers = 2
    kv_spec = pl.BlockSpec((None, res_blk, seq_len, width), kvmap)
    blocks = (2 * (2 * _nbytes((DIL_STEP_ROWS, width), BF16) + _nbytes((DIL_STEP_ROWS, width), F32))
              + kv_buffers * 2 * _nbytes((res_blk * seq_len, width), BF16) + _nbytes(bias.shape, F32))
    scratch = _nbytes(tile, F32) + _nbytes(tile, BF16) + _nbytes(stat, F32)
    temps = DIL_UNROLL * _nbytes(tile[2:], F32)
    return pl.pallas_call(
        functools.partial(_dil_kernel, seq_len=seq_len, stretch=stretch),
        grid=(batch, dilation // res_blk, seq_len // stretch),
        in_specs=[
            pl.BlockSpec((None, res_blk, stretch, width), qmap),
            kv_spec,
            kv_spec,
            _resident(bias.shape),
        ],
        out_specs=[pl.BlockSpec((None, res_blk, stretch, width), qmap)] * 2,
        out_shape=[jax.ShapeDtypeStruct(q.shape, BF16), jax.ShapeDtypeStruct(q.shape, F32)],
        scratch_shapes=[pltpu.VMEM(tile, F32), pltpu.VMEM(tile, BF16), pltpu.VMEM(stat, F32)],
        compiler_params=pltpu.CompilerParams(
            dimension_semantics=("arbitrary", "arbitrary", "arbitrary"),
            vmem_limit_bytes=_vmem_limit(blocks, scratch, temps),
        ),
        name=f"dil_attn_g{group}",
    )(q, k, v, bias)


MERGE_TM = 1024
MERGE_CN = 256


def _to_token_order(src_ref, scr_ref, dil, k):
    per_res = SUB_TM // dil
    res_rows = slice(k * per_res, (k + 1) * per_res)
    if dil == 1:
        return src_ref[0, res_rows].astype(F32)
    for r in range(dil):
        for c in range(GROUP_LANE_BLOCKS):
            scr_ref[c, pl.ds(k * SUB_TM + r, per_res, stride=dil), :] = (
                src_ref[r, res_rows, c * V7X_LANES:(c + 1) * V7X_LANES].astype(F32))
    rows = slice(k * SUB_TM, (k + 1) * SUB_TM)
    return jnp.concatenate([scr_ref[c, rows] for c in range(GROUP_LANE_BLOCKS)], axis=-1)


def _merge_kernel(x_ref, pre_g_ref, wgate_ref, oa_ref, wa_ref, *refs):
    ng = NB_GROUPS
    o_refs, lse_refs = refs[:ng], refs[ng:2 * ng]
    wb_ref, wout_ref, post_g_ref, out_ref, h_ref, ob_ref, mg_ref = refs[2 * ng:2 * ng + 7]
    scr_refs = iter(refs[2 * ng + 7:])
    o_scr = [next(scr_refs) if dil > 1 else None for dil in DILATIONS]
    lse_scr = [next(scr_refs) if dil > 1 else None for dil in DILATIONS]
    subs = _sub_tiles(x_ref.shape[0])
    for k, rows in enumerate(subs):
        h_ref[rows] = _rmsnorm_f32(x_ref[rows], pre_g_ref[...]).astype(BF16)
        os_ = [_to_token_order(r, s, d, k) for r, s, d in zip(o_refs, o_scr, DILATIONS)]
        lses = [_to_token_order(r, s, d, k) for r, s, d in zip(lse_refs, lse_scr, DILATIONS)]
        mx = jnp.maximum(jnp.maximum(lses[0], lses[1]), lses[2])
        es = [jnp.exp(l - mx) for l in lses]
        inv = 1.0 / (es[0] + es[1] + es[2])
        for g in range(ng):
            ob_ref[rows, g * GROUP_WIDTH:(g + 1) * GROUP_WIDTH] = (os_[g] * (es[g] * inv)).astype(BF16)
    for rows in subs:
        for n in range(D_MODEL // MERGE_CN):
            cols = slice(n * MERGE_CN, (n + 1) * MERGE_CN)
            cols_b = slice(D_MODEL + n * MERGE_CN, D_MODEL + (n + 1) * MERGE_CN)
            h = h_ref[rows]
            ga = _dot(h, wgate_ref[:, cols].astype(BF16))
            gb = _dot(h, wgate_ref[:, cols_b].astype(BF16))
            ya = _dot(oa_ref[rows], wa_ref[:, cols])
            yb = _dot(ob_ref[rows], wb_ref[:, cols])
            mg_ref[rows, cols] = (jax.nn.sigmoid(ga) * ya + jax.nn.sigmoid(gb) * yb).astype(BF16)
    for rows in subs:
        out_ref[rows] = _dot(mg_ref[rows], wout_ref[...])
    for rows in subs:
        out_ref[rows] = x_ref[rows] + _rmsnorm_f32(out_ref[rows], post_g_ref[...])


def _merge(x, pre_g, wgate, oa, wa, obs, lses, wb, wout, post_g):
    batch, seq, d = x.shape
    tm = MERGE_TM
    tok = lambda b, i: (b, i, 0)
    res = lambda b, i: (b, 0, i, 0)
    tokspec = lambda w: pl.BlockSpec((None, tm, w), tok)
    resspecs = [pl.BlockSpec((None, dil, tm // dil, GROUP_WIDTH), res) for dil in DILATIONS]
    blocks = (2 * 2 * _nbytes((tm, d), F32) + 2 * _nbytes((tm, NA_WIDTH), BF16)
              + 2 * 3 * (_nbytes((tm, GROUP_WIDTH), BF16) + _nbytes((tm, GROUP_WIDTH), F32))
              + _nbytes((NA_WIDTH, d), wa.dtype) + _nbytes((d + NB_WIDTH, d), wb.dtype) + _nbytes((d, 2 * d), F32))
    work = [pltpu.VMEM((tm, d), BF16), pltpu.VMEM((tm, NB_WIDTH), BF16), pltpu.VMEM((tm, d), BF16)]
    n_interleave = 2 * sum(dil > 1 for dil in DILATIONS)
    scratch = (n_interleave * _nbytes((tm, GROUP_WIDTH), F32) + 2 * _nbytes((tm, d), BF16)
               + _nbytes((tm, NB_WIDTH), BF16))
    temps = 2 * _nbytes((SUB_TM, d), F32)
    return pl.pallas_call(
        _merge_kernel,
        grid=(batch, seq // tm),
        in_specs=[
            tokspec(d), _resident((1, d)),
            pl.BlockSpec((pl.Element(d), pl.Element(2 * d)), lambda *_: (0, QKV_TOTAL),
                         pipeline_mode=pl.Buffered(1)),
            tokspec(NA_WIDTH), _resident((NA_WIDTH, d)),
            *resspecs, *resspecs,
            _resident((NB_WIDTH, d)), _resident((d, d)), _resident((1, d)),
        ],
        out_specs=tokspec(d),
        out_shape=jax.ShapeDtypeStruct((batch, seq, d), F32),
        scratch_shapes=work + [pltpu.VMEM((GROUP_LANE_BLOCKS, tm, V7X_LANES), F32)] * n_interleave,
        compiler_params=pltpu.CompilerParams(
            dimension_semantics=("arbitrary", "arbitrary"),
            vmem_limit_bytes=_vmem_limit(blocks, scratch, temps),
        ),
        name="merge",
    )(x, pre_g, wgate, oa, wa, *obs, *lses, wb, wout, post_g)


def kernel(x, ffn1_pre_g, ffn1_w_gate, ffn1_w_up, ffn1_w_down, ffn1_post_g, mix_pre_g, w_in, na_rpb, w_branch_a, w_branch_b, w_out, mix_post_g, ffn2_pre_g, ffn2_w_gate, ffn2_w_up, ffn2_w_down, ffn2_post_g):
    batch, seq, d = x.shape
    depth = ffn1_pre_g.shape[0]
    for window, dilation in DIL_PAIRS:
        assert window // (2 * dilation) == DIL_HALF
    for l in range(depth):
        x = _ffn(x.reshape(batch * seq, d), ffn1_pre_g[l][None], ffn1_w_gate[l], ffn1_w_up[l],
                 ffn1_w_down[l], ffn1_post_g[l][None])
        x = x.reshape(batch, seq, d)
        qa, ka, va, qb, kb, vb = _qkv(x, mix_pre_g[l][None], w_in[l])
        oa = _nbr_attention(qa, ka, va, na_rpb[l])
        obs, lses = [], []
        for g in range(NB_GROUPS):
            o_g, lse_g = _dil_attention(qb[g], kb[g], vb[g], g)
            obs.append(o_g)
            lses.append(lse_g)
        x = _merge(x, mix_pre_g[l][None], w_in[l],
                   oa, w_branch_a[l].astype(BF16), obs, lses, w_branch_b[l].astype(BF16),
                   w_out[l].astype(BF16), mix_post_g[l][None])
        x = _ffn(x.reshape(batch * seq, d), ffn2_pre_g[l][None], ffn2_w_gate[l], ffn2_w_up[l],
                 ffn2_w_down[l], ffn2_post_g[l][None])
        x = x.reshape(batch, seq, d)
    return x
```

```python
import functools
import math

import jax
import jax.numpy as jnp
from jax import lax
from jax.experimental import pallas as pl
from jax.experimental.pallas import tpu as pltpu

D_MODEL = 1024
HEAD_DIM = 64
NA_HEADS = 8
NA_WIDTH = NA_HEADS * HEAD_DIM
NA_ROWS = 8
NA_COLS = 16
GRID_W = 64
DIL_PAIRS = ((128, 1), (512, 4), (2048, 16))
DILATIONS = tuple(d for _, d in DIL_PAIRS)
NB_GROUPS = len(DIL_PAIRS)
NB_HEADS_PER_GROUP = 4
NB_HEADS = NB_GROUPS * NB_HEADS_PER_GROUP
NB_WIDTH = NB_HEADS * HEAD_DIM
GROUP_WIDTH = NB_HEADS_PER_GROUP * HEAD_DIM
ALIBI_MAX_EXP = 8.0
D_FF = 2816
NORM_EPS = 1e-6
ATTN_SCALE = HEAD_DIM ** -0.5
LOG2E = math.log2(math.e)
LN2 = math.log(2.0)
Q_SCALE = ATTN_SCALE * LOG2E

V7X_LANES = 128
GROUP_LANE_BLOCKS = GROUP_WIDTH // V7X_LANES
V7X_VMEM_BYTES = 64 * 1024 * 1024
V7X_VMEM_RESERVE = 6 * 1024 * 1024

BF16 = jnp.bfloat16
F32 = jnp.float32


def _vmem_limit(block_bytes, scratch_bytes, temp_bytes):
    need = block_bytes + scratch_bytes + temp_bytes
    budget = V7X_VMEM_BYTES - V7X_VMEM_RESERVE
    assert need <= budget, need
    return budget


def _nbytes(shape, dtype):
    return math.prod(shape) * jnp.dtype(dtype).itemsize


def _resident(shape):
    nd = len(shape)
    return pl.BlockSpec(shape, lambda *_: (0,) * nd, pipeline_mode=pl.Buffered(1))


def _rmsnorm_f32(x, g):
    return x * lax.rsqrt(jnp.mean(x * x, axis=-1, keepdims=True) + NORM_EPS) * g


def _dot(a, b):
    return jnp.dot(a, b, preferred_element_type=F32)


def _dot_nt(a, b):
    return lax.dot_general(a, b, (((1,), (1,)), ((), ())), preferred_element_type=F32)


def _pipelined_attention(n_groups, unroll, scores, probs, finish, s_scr, p_scr, stat_scrs):
    assert n_groups % 2 == 0 and n_groups >= 2
    tiles = lambda i: [i * unroll + u for u in range(unroll)]

    def stage_scores(i, slot):
        for u, t in enumerate(tiles(i)):
            s_scr[slot, u] = scores(t)

    def stage_probs(slot):
        for u in range(unroll):
            p, *stats = probs(s_scr[slot, u])
            p_scr[slot, u] = p
            for ref, stat in zip(stat_scrs, stats):
                ref[slot, u] = stat

    def stage_finish(i, slot):
        for u, t in enumerate(tiles(i)):
            finish(t, p_scr[slot, u], *[ref[slot, u] for ref in stat_scrs])

    def step(i, parity):
        stage_scores(i + 1, 1 - parity)
        stage_finish(i - 1, 1 - parity)
        stage_probs(parity)

    stage_scores(0, 0)
    stage_scores(1, 1)
    stage_probs(0)

    def body(i, carry):
        @pl.when(i % 2 == 1)
        def _():
            step(i, 1)

        @pl.when(i % 2 == 0)
        def _():
            step(i, 0)

        return carry

    lax.fori_loop(1, n_groups - 1, body, 0)
    stage_finish(n_groups - 2, 0)
    stage_probs(1)
    stage_finish(n_groups - 1, 1)


FFN_TM = 512
SUB_TM = 512
FFN_CK = 256


def _sub_tiles(tm):
    return [slice(k * SUB_TM, (k + 1) * SUB_TM) for k in range(tm // SUB_TM)]


FFN_CHUNKS = [slice(c * FFN_CK, (c + 1) * FFN_CK) for c in range(D_FF // FFN_CK)]


def _ffn_weight_copies(wg_hbm, wu_hbm, wd_hbm, wg_ref, wu_ref, wd_ref, sem):
    gate_up = [(pltpu.make_async_copy(wg_hbm.at[:, cols], wg_ref.at[:, cols], sem.at[0, c]),
                pltpu.make_async_copy(wu_hbm.at[:, cols], wu_ref.at[:, cols], sem.at[1, c]))
               for c, cols in enumerate(FFN_CHUNKS)]
    down = [pltpu.make_async_copy(wd_hbm.at[cols, :], wd_ref.at[cols, :], sem.at[2, c])
            for c, cols in enumerate(FFN_CHUNKS)]
    return gate_up, down


def _ffn_body(x_ref, pre_g_ref, wg_ref, wu_ref, wd_ref, post_g_ref, o_ref, h_ref, a_ref, wait_gate_up, wait_down):
    h_ref[...] = _rmsnorm_f32(x_ref[...], pre_g_ref[...]).astype(BF16)
    for c, cols in enumerate(FFN_CHUNKS):
        wait_gate_up(c)
        h = h_ref[...]
        g = _dot(h, wg_ref[:, cols].astype(BF16))
        u = _dot(h, wu_ref[:, cols].astype(BF16))
        a_ref[:, cols] = (g * jax.nn.sigmoid(g) * u).astype(BF16)
    f = None
    for c, cols in enumerate(FFN_CHUNKS):
        wait_down(c)
        part = _dot(a_ref[:, cols], wd_ref[cols, :].astype(BF16))
        f = part if f is None else f + part
    o_ref[...] = x_ref[...] + 0.5 * _rmsnorm_f32(f, post_g_ref[...])


def _ffn_kernel(x_ref, pre_g_ref, wg_hbm, wu_hbm, wd_hbm, post_g_ref, o_ref, h_ref, a_ref,
                wg_ref, wu_ref, wd_ref, sem):
    refs = (x_ref, pre_g_ref, wg_ref, wu_ref, wd_ref, post_g_ref, o_ref, h_ref, a_ref)
    first = pl.program_id(0) == 0

    @pl.when(first)
    def _():
        gate_up, down = _ffn_weight_copies(wg_hbm, wu_hbm, wd_hbm, wg_ref, wu_ref, wd_ref, sem)
        for cg, cu in gate_up:
            cg.start()
            cu.start()
        for cd in down:
            cd.start()

        def wait_gate_up(c):
            gate_up[c][0].wait()
            gate_up[c][1].wait()

        _ffn_body(*refs, wait_gate_up, lambda c: down[c].wait())

    @pl.when(jnp.logical_not(first))
    def _():
        _ffn_body(*refs, lambda c: None, lambda c: None)


def _ffn(x, pre_g, wg, wu, wd, post_g):
    n, d = x.shape
    tm = FFN_TM
    row = lambda i: (i, 0)
    n_chunks = len(FFN_CHUNKS)
    blocks = 2 * 2 * _nbytes((tm, d), F32) + 2 * _nbytes((1, d), F32)
    scratch = _nbytes((tm, d), BF16) + _nbytes((tm, D_FF), BF16) + 3 * _nbytes((d, D_FF), wg.dtype)
    temps = 4 * _nbytes((tm, d), F32)
    hbm = pl.BlockSpec(memory_space=pl.ANY)
    return pl.pallas_call(
        _ffn_kernel,
        grid=(n // tm,),
        in_specs=[pl.BlockSpec((tm, d), row), _resident((1, d)), hbm, hbm, hbm, _resident((1, d))],
        out_specs=pl.BlockSpec((tm, d), row),
        out_shape=jax.ShapeDtypeStruct((n, d), F32),
        scratch_shapes=[
            pltpu.VMEM((tm, d), BF16), pltpu.VMEM((tm, D_FF), BF16),
            pltpu.VMEM((d, D_FF), wg.dtype), pltpu.VMEM((d, D_FF), wu.dtype), pltpu.VMEM((D_FF, d), wd.dtype),
            pltpu.SemaphoreType.DMA((3, n_chunks)),
        ],
        compiler_params=pltpu.CompilerParams(
            dimension_semantics=("arbitrary",),
            vmem_limit_bytes=_vmem_limit(blocks, scratch, temps),
        ),
        name="ffn",
    )(x, pre_g, wg, wu, wd, post_g)


QKV_TM = 1024
QKV_TOTAL = 3 * NA_WIDTH + 3 * NB_WIDTH
N_DIL_OUTS = 3 * NB_GROUPS
STAGED_OUTS = [i for i in range(N_DIL_OUTS) if DILATIONS[i % NB_GROUPS] > 1]
N_STAGED_OUTS = len(STAGED_OUTS)


def _qkv_kernel(x_ref, g_ref, w32_ref, *refs):
    na_refs, dil_refs = refs[:3], refs[3:3 + N_DIL_OUTS]
    w_ref, y_refs = refs[3 + N_DIL_OUTS], refs[4 + N_DIL_OUTS:]

    @pl.when((pl.program_id(0) == 0) & (pl.program_id(1) == 0))
    def _():
        for c in range(QKV_TOTAL // GROUP_WIDTH):
            cols = slice(c * GROUP_WIDTH, (c + 1) * GROUP_WIDTH)
            w_ref[:, cols] = w32_ref[:, cols].astype(BF16)

    for k, rows in enumerate(_sub_tiles(x_ref.shape[0])):
        h = _rmsnorm_f32(x_ref[rows], g_ref[...]).astype(BF16)
        for idx, o_ref in enumerate(dil_refs):
            which, g = divmod(idx, NB_GROUPS)
            dil = DILATIONS[g]
            off = 3 * NA_WIDTH + which * NB_WIDTH + g * GROUP_WIDTH
            y = _dot(h, w_ref[:, off:off + GROUP_WIDTH])
            if which == 0:
                y = y * Q_SCALE
            if dil == 1:
                o_ref[0, rows] = y.astype(BF16)
                continue
            y_ref = y_refs[k * N_STAGED_OUTS + STAGED_OUTS.index(idx)]
            per_res = SUB_TM // dil
            for c in range(GROUP_LANE_BLOCKS):
                y_ref[c] = y[:, c * V7X_LANES:(c + 1) * V7X_LANES]
            for r in range(dil):
                for c in range(GROUP_LANE_BLOCKS):
                    o_ref[r, k * per_res:(k + 1) * per_res, c * V7X_LANES:(c + 1) * V7X_LANES] = (
                        y_ref[c, pl.ds(r, per_res, stride=dil), :].astype(BF16))
        for idx, o_ref in enumerate(na_refs):
            y = _dot(h, w_ref[:, idx * NA_WIDTH:(idx + 1) * NA_WIDTH])
            if idx == 0:
                y = y * Q_SCALE
            o_ref[rows] = y.astype(BF16)


def _qkv(x, g, w_in):
    batch, seq, d = x.shape
    tm = QKV_TM
    tok = lambda b, i: (b, i, 0)
    res = lambda b, i: (b, 0, i, 0)
    out_specs = [pl.BlockSpec((None, tm, NA_WIDTH), tok)] * 3
    out_shape = [jax.ShapeDtypeStruct((batch, seq, NA_WIDTH), BF16)] * 3
    for _ in range(3):
        for dil in DILATIONS:
            out_specs.append(pl.BlockSpec((None, dil, tm // dil, GROUP_WIDTH), res))
            out_shape.append(jax.ShapeDtypeStruct((batch, dil, seq // dil, GROUP_WIDTH), BF16))
    blocks = 2 * _nbytes((tm, d), F32) + _nbytes((d, QKV_TOTAL), w_in.dtype) + 2 * _nbytes((tm, QKV_TOTAL), BF16)
    stage = (GROUP_LANE_BLOCKS, SUB_TM, V7X_LANES)
    n_stage = N_STAGED_OUTS * (tm // SUB_TM)
    scratch = n_stage * _nbytes(stage, F32) + _nbytes((d, QKV_TOTAL), BF16)
    temps = 2 * _nbytes((SUB_TM, d), F32)
    outs = pl.pallas_call(
        _qkv_kernel,
        grid=(batch, seq // tm),
        in_specs=[pl.BlockSpec((None, tm, d), tok), _resident((1, d)), _resident((d, QKV_TOTAL))],
        out_specs=out_specs,
        out_shape=out_shape,
        scratch_shapes=[pltpu.VMEM((d, QKV_TOTAL), BF16)] + [pltpu.VMEM(stage, F32)] * n_stage,
        compiler_params=pltpu.CompilerParams(
            dimension_semantics=("arbitrary", "arbitrary"),
            vmem_limit_bytes=_vmem_limit(blocks, scratch, temps),
        ),
        name="qkv",
    )(x, g, w_in)
    qa, ka, va = outs[:3]
    qb, kb, vb = (outs[3 + w * NB_GROUPS:3 + (w + 1) * NB_GROUPS] for w in range(3))
    return qa, ka, va, qb, kb, vb


NA_HEADS_PER_STEP = V7X_LANES // HEAD_DIM
NA_SPAN = NA_ROWS * GRID_W
NA_VARIANTS = NA_ROWS
NA_RPB_ROWS = 2 * NA_ROWS - 1
NA_RPB_COLS = 2 * NA_COLS - 1
NA_UNROLL = 32


def _nbr_build_bias(rep_ref, tbl_ref):
    width = NA_RPB_ROWS * GRID_W
    qc = lax.broadcasted_iota(jnp.int32, (GRID_W, width), 0)
    kc = lax.broadcasted_iota(jnp.int32, (GRID_W, width), 1) % GRID_W
    diff = kc - qc
    col0 = jnp.clip(qc - NA_COLS // 2, 0, GRID_W - NA_COLS)
    col_ok = (kc >= col0) & (kc < col0 + NA_COLS)
    for h in range(NA_HEADS_PER_STEP):
        w = jnp.full((GRID_W, width), -jnp.inf, F32)
        for k in range(NA_RPB_COLS):
            w = jnp.where(diff == k - (NA_COLS - 1), rep_ref[h, k:k + 1, :], w)
        w = jnp.where(col_ok, w * LOG2E, -jnp.inf)
        for variant in range(NA_VARIANTS):
            tbl_ref[h, variant] = w[:, variant * GRID_W:variant * GRID_W + NA_SPAN]


def _nbr_kernel(q_ref, k_ref, v_ref, rep_ref, o_ref, tbl_ref, s_scr, p_scr, *, rows):
    hp = NA_HEADS_PER_STEP
    ones = jnp.ones((NA_SPAN, V7X_LANES), BF16)

    @pl.when(pl.program_id(1) == 0)
    def _():
        _nbr_build_bias(rep_ref, tbl_ref)

    lane_head = lax.broadcasted_iota(jnp.int32, (GRID_W, V7X_LANES), 1) // HEAD_DIM

    def scores(r):
        row0 = jnp.clip(r - NA_ROWS // 2, 0, rows - NA_ROWS)
        variant = row0 - r + (NA_ROWS - 1)
        q = q_ref[pl.ds(pl.multiple_of(r * GRID_W, GRID_W), GRID_W), :]
        kw = k_ref[pl.ds(pl.multiple_of(row0 * GRID_W, GRID_W), NA_SPAN), :]
        qs = jnp.concatenate([jnp.where(lane_head == h, q, jnp.zeros_like(q)) for h in range(hp)], axis=0)
        return _dot_nt(qs, kw) + tbl_ref[:, variant].reshape(hp * GRID_W, NA_SPAN)

    def probs(s):
        return (jnp.exp2(s - jnp.max(s, axis=-1, keepdims=True)).astype(BF16),)

    def finish(r, p):
        row0 = jnp.clip(r - NA_ROWS // 2, 0, rows - NA_ROWS)
        vw = v_ref[pl.ds(pl.multiple_of(row0 * GRID_W, GRID_W), NA_SPAN), :]
        o = _dot(p, jnp.concatenate([vw, ones], axis=1))
        out, den = o[:GRID_W, :V7X_LANES], o[:GRID_W, V7X_LANES:]
        for h in range(1, hp):
            rows_h = slice(h * GRID_W, (h + 1) * GRID_W)
            out = jnp.where(lane_head == h, o[rows_h, :V7X_LANES], out)
            den = jnp.where(lane_head == h, o[rows_h, V7X_LANES:], den)
        o_ref[pl.ds(pl.multiple_of(r * GRID_W, GRID_W), GRID_W), :] = (out * (1.0 / den)).astype(BF16)

    _pipelined_attention(rows // NA_UNROLL, NA_UNROLL, scores, probs, finish, s_scr, p_scr, ())


def _nbr_attention(q, k, v, rpb):
    batch, seq, width = q.shape
    rows = seq // GRID_W
    n_pairs = width // V7X_LANES
    rep = jnp.repeat(rpb.astype(F32).transpose(0, 2, 1), GRID_W, axis=-1)
    tok = lambda p, b: (b, 0, p)
    tbl_shape = (NA_HEADS_PER_STEP, NA_VARIANTS, GRID_W, NA_SPAN)
    rep_block = (NA_HEADS_PER_STEP, NA_RPB_COLS, NA_RPB_ROWS * GRID_W)
    blocks = 2 * 4 * _nbytes((seq, V7X_LANES), BF16) + 2 * _nbytes((NA_HEADS_PER_STEP, 32, 1024), F32)
    tile = (2, NA_UNROLL, NA_HEADS_PER_STEP * GRID_W, NA_SPAN)
    scratch = _nbytes(tbl_shape, F32) + _nbytes(tile, F32) + _nbytes(tile, BF16)
    temps = NA_UNROLL * _nbytes(tile[2:], F32)
    return pl.pallas_call(
        functools.partial(_nbr_kernel, rows=rows),
        grid=(n_pairs, batch),
        in_specs=[
            pl.BlockSpec((None, seq, V7X_LANES), tok),
            pl.BlockSpec((None, seq, V7X_LANES), tok),
            pl.BlockSpec((None, seq, V7X_LANES), tok),
            pl.BlockSpec(rep_block, lambda p, b: (p, 0, 0)),
        ],
        out_specs=pl.BlockSpec((None, seq, V7X_LANES), tok),
        out_shape=jax.ShapeDtypeStruct((batch, seq, width), BF16),
        scratch_shapes=[pltpu.VMEM(tbl_shape, F32), pltpu.VMEM(tile, F32), pltpu.VMEM(tile, BF16)],
        compiler_params=pltpu.CompilerParams(
            dimension_semantics=("arbitrary", "arbitrary"),
            vmem_limit_bytes=_vmem_limit(blocks, scratch, temps),
        ),
        name="nbr_attn",
    )(q, k, v, rep)


DIL_HALF = 64
DIL_TQ = 128
DIL_TK = DIL_TQ + 2 * DIL_HALF
DIL_VARIANTS = 3
DIL_STEP_ROWS = 4096
DIL_UNROLL = 8


def _dil_bias_table(group, dilation):
    heads = jnp.arange(NB_HEADS_PER_GROUP, dtype=F32) + group * NB_HEADS_PER_GROUP
    slopes = jnp.exp2(-ALIBI_MAX_EXP * (heads + 1.0) / NB_HEADS)
    offs = jnp.array([0, -DIL_HALF, -2 * DIL_HALF], jnp.int32)
    rel = offs[:, None, None] + jnp.arange(DIL_TK)[None, None, :] - jnp.arange(DIL_TQ)[None, :, None]
    dist = (dilation * jnp.abs(rel)).astype(F32)
    bias = -(slopes[None, :, None, None] * dist[:, None]) * LOG2E
    bias = jnp.where((jnp.abs(rel) <= DIL_HALF)[:, None], bias, -jnp.inf)
    return bias.reshape(DIL_VARIANTS, NB_HEADS_PER_GROUP * DIL_TQ, DIL_TK)


def _dil_kernel(q_ref, k_ref, v_ref, bias_ref, o_ref, lse_ref, s_scr, p_scr, m_scr, *, seq_len, stretch):
    nh = NB_HEADS_PER_GROUP
    hpb = V7X_LANES // HEAD_DIM
    ones = jnp.ones((DIL_TK, V7X_LANES), BF16)
    group_lane_head = lax.broadcasted_iota(jnp.int32, (DIL_TQ, GROUP_WIDTH), 1) // HEAD_DIM
    lane_head = lax.broadcasted_iota(jnp.int32, (DIL_TQ, V7X_LANES), 1) // HEAD_DIM
    base = pl.program_id(2) * stretch
    tiles_per_residue = stretch // DIL_TQ

    def window(t):
        r, j = t // tiles_per_residue, t % tiles_per_residue
        qloc = pl.multiple_of(j * DIL_TQ, DIL_TQ)
        qs = base + qloc
        ws = jnp.clip(qs - DIL_HALF, 0, seq_len - DIL_TK)
        variant = (qs - ws) // DIL_HALF
        return r, qloc, pl.multiple_of(ws, DIL_HALF), variant

    def scores(t):
        r, qloc, ws, variant = window(t)
        q = q_ref[r, pl.ds(qloc, DIL_TQ), :]
        kw = k_ref[r, pl.ds(ws, DIL_TK), :]
        qst = jnp.concatenate([jnp.where(group_lane_head == h, q, jnp.zeros_like(q)) for h in range(nh)],
                              axis=0)
        return _dot_nt(qst, kw) + bias_ref[variant]

    def probs(s):
        m = jnp.max(s, axis=-1, keepdims=True)
        return jnp.exp2(s - m).astype(BF16), jnp.broadcast_to(m, (s.shape[0], V7X_LANES))

    def finish(t, p, m):
        r, qloc, ws, _ = window(t)
        vw = v_ref[r, pl.ds(ws, DIL_TK), :]
        outs, lses = [], []
        for c in range(GROUP_LANE_BLOCKS):
            rows_c = slice(c * hpb * DIL_TQ, (c + 1) * hpb * DIL_TQ)
            v_aug = jnp.concatenate([vw[:, c * V7X_LANES:(c + 1) * V7X_LANES], ones], axis=1)
            o = _dot(p[rows_c], v_aug)
            m_c = m[rows_c]
            out_c, den_c, max_c = o[:DIL_TQ, :V7X_LANES], o[:DIL_TQ, V7X_LANES:], m_c[:DIL_TQ]
            for h in range(1, hpb):
                rows_h = slice(h * DIL_TQ, (h + 1) * DIL_TQ)
                out_c = jnp.where(lane_head == h, o[rows_h, :V7X_LANES], out_c)
                den_c = jnp.where(lane_head == h, o[rows_h, V7X_LANES:], den_c)
                max_c = jnp.where(lane_head == h, m_c[rows_h], max_c)
            outs.append(out_c * (1.0 / den_c))
            lses.append(max_c * LN2 + jnp.log(den_c))
        o_ref[r, pl.ds(qloc, DIL_TQ), :] = jnp.concatenate(outs, axis=1).astype(BF16)
        lse_ref[r, pl.ds(qloc, DIL_TQ), :] = jnp.concatenate(lses, axis=1)

    n_tiles = q_ref.shape[0] * tiles_per_residue
    _pipelined_attention(n_tiles // DIL_UNROLL, DIL_UNROLL, scores, probs, finish, s_scr, p_scr, (m_scr,))


def _dil_attention(q, k, v, group):
    batch, dilation, seq_len, width = q.shape
    stretch = min(seq_len, DIL_STEP_ROWS)
    res_blk = DIL_STEP_ROWS // stretch
    assert dilation % res_blk == 0 and seq_len % stretch == 0
    qmap = lambda b, r, s: (b, r, s, 0)
    kvmap = lambda b, r, s: (b, r, 0, 0)
    bias = _dil_bias_table(group, dilation)
    tile = (2, DIL_UNROLL, NB_HEADS_PER_GROUP * DIL_TQ, DIL_TK)
    stat = tile[:3] + (V7X_LANES,)
    kv_buffers = 2
    kv_spec = pl.BlockSpec((None, res_blk, seq_len, width), kvmap)
    blocks = (2 * (2 * _nbytes((DIL_STEP_ROWS, width), BF16) + _nbytes((DIL_STEP_ROWS, width), F32))
              + kv_buffers * 2 * _nbytes((res_blk * seq_len, width), BF16) + _nbytes(bias.shape, F32))
    scratch = _nbytes(tile, F32) + _nbytes(tile, BF16) + _nbytes(stat, F32)
    temps = DIL_UNROLL * _nbytes(tile[2:], F32)
    return pl.pallas_call(
        functools.partial(_dil_kernel, seq_len=seq_len, stretch=stretch),
        grid=(batch, dilation // res_blk, seq_len // stretch),
        in_specs=[
            pl.BlockSpec((None, res_blk, stretch, width), qmap),
            kv_spec,
            kv_spec,
            _resident(bias.shape),
        ],
        out_specs=[pl.BlockSpec((None, res_blk, stretch, width), qmap)] * 2,
        out_shape=[jax.ShapeDtypeStruct(q.shape, BF16), jax.ShapeDtypeStruct(q.shape, F32)],
        scratch_shapes=[pltpu.VMEM(tile, F32), pltpu.VMEM(tile, BF16), pltpu.VMEM(stat, F32)],
        compiler_params=pltpu.CompilerParams(
            dimension_semantics=("arbitrary", "arbitrary", "arbitrary"),
            vmem_limit_bytes=_vmem_limit(blocks, scratch, temps),
        ),
        name=f"dil_attn_g{group}",
    )(q, k, v, bias)


MERGE_TM = 1024
MERGE_CN = 256


def _to_token_order(src_ref, scr_ref, dil, k):
    per_res = SUB_TM // dil
    res_rows = slice(k * per_res, (k + 1) * per_res)
    if dil == 1:
        return src_ref[0, res_rows].astype(F32)
    for r in range(dil):
        for c in range(GROUP_LANE_BLOCKS):
            scr_ref[c, pl.ds(k * SUB_TM + r, per_res, stride=dil), :] = (
                src_ref[r, res_rows, c * V7X_LANES:(c + 1) * V7X_LANES].astype(F32))
    rows = slice(k * SUB_TM, (k + 1) * SUB_TM)
    return jnp.concatenate([scr_ref[c, rows] for c in range(GROUP_LANE_BLOCKS)], axis=-1)


def _merge_kernel(x_ref, pre_g_ref, wgate_ref, oa_ref, wa_ref, *refs):
    ng = NB_GROUPS
    o_refs, lse_refs = refs[:ng], refs[ng:2 * ng]
    wb_ref, wout_ref, post_g_ref, out_ref, h_ref, ob_ref, mg_ref = refs[2 * ng:2 * ng + 7]
    scr_refs = iter(refs[2 * ng + 7:])
    o_scr = [next(scr_refs) if dil > 1 else None for dil in DILATIONS]
    lse_scr = [next(scr_refs) if dil > 1 else None for dil in DILATIONS]
    subs = _sub_tiles(x_ref.shape[0])
    for k, rows in enumerate(subs):
        h_ref[rows] = _rmsnorm_f32(x_ref[rows], pre_g_ref[...]).astype(BF16)
        os_ = [_to_token_order(r, s, d, k) for r, s, d in zip(o_refs, o_scr, DILATIONS)]
        lses = [_to_token_order(r, s, d, k) for r, s, d in zip(lse_refs, lse_scr, DILATIONS)]
        mx = jnp.maximum(jnp.maximum(lses[0], lses[1]), lses[2])
        es = [jnp.exp(l - mx) for l in lses]
        inv = 1.0 / (es[0] + es[1] + es[2])
        for g in range(ng):
            ob_ref[rows, g * GROUP_WIDTH:(g + 1) * GROUP_WIDTH] = (os_[g] * (es[g] * inv)).astype(BF16)
    for rows in subs:
        for n in range(D_MODEL // MERGE_CN):
            cols = slice(n * MERGE_CN, (n + 1) * MERGE_CN)
            cols_b = slice(D_MODEL + n * MERGE_CN, D_MODEL + (n + 1) * MERGE_CN)
            h = h_ref[rows]
            ga = _dot(h, wgate_ref[:, cols].astype(BF16))
            gb = _dot(h, wgate_ref[:, cols_b].astype(BF16))
            ya = _dot(oa_ref[rows], wa_ref[:, cols])
            yb = _dot(ob_ref[rows], wb_ref[:, cols])
            mg_ref[rows, cols] = (jax.nn.sigmoid(ga) * ya + jax.nn.sigmoid(gb) * yb).astype(BF16)
    for rows in subs:
        out_ref[rows] = _dot(mg_ref[rows], wout_ref[...])
    for rows in subs:
        out_ref[rows] = x_ref[rows] + _rmsnorm_f32(out_ref[rows], post_g_ref[...])


def _merge(x, pre_g, wgate, oa, wa, obs, lses, wb, wout, post_g):
    batch, seq, d = x.shape
    tm = MERGE_TM
    tok = lambda b, i: (b, i, 0)
    res = lambda b, i: (b, 0, i, 0)
    tokspec = lambda w: pl.BlockSpec((None, tm, w), tok)
    resspecs = [pl.BlockSpec((None, dil, tm // dil, GROUP_WIDTH), res) for dil in DILATIONS]
    blocks = (2 * 2 * _nbytes((tm, d), F32) + 2 * _nbytes((tm, NA_WIDTH), BF16)
              + 2 * 3 * (_nbytes((tm, GROUP_WIDTH), BF16) + _nbytes((tm, GROUP_WIDTH), F32))
              + _nbytes((NA_WIDTH, d), wa.dtype) + _nbytes((d + NB_WIDTH, d), wb.dtype) + _nbytes((d, 2 * d), F32))
    work = [pltpu.VMEM((tm, d), BF16), pltpu.VMEM((tm, NB_WIDTH), BF16), pltpu.VMEM((tm, d), BF16)]
    n_interleave = 2 * sum(dil > 1 for dil in DILATIONS)
    scratch = (n_interleave * _nbytes((tm, GROUP_WIDTH), F32) + 2 * _nbytes((tm, d), BF16)
               + _nbytes((tm, NB_WIDTH), BF16))
    temps = 2 * _nbytes((SUB_TM, d), F32)
    return pl.pallas_call(
        _merge_kernel,
        grid=(batch, seq // tm),
        in_specs=[
            tokspec(d), _resident((1, d)),
            pl.BlockSpec((pl.Element(d), pl.Element(2 * d)), lambda *_: (0, QKV_TOTAL),
                         pipeline_mode=pl.Buffered(1)),
            tokspec(NA_WIDTH), _resident((NA_WIDTH, d)),
            *resspecs, *resspecs,
            _resident((NB_WIDTH, d)), _resident((d, d)), _resident((1, d)),
        ],
        out_specs=tokspec(d),
        out_shape=jax.ShapeDtypeStruct((batch, seq, d), F32),
        scratch_shapes=work + [pltpu.VMEM((GROUP_LANE_BLOCKS, tm, V7X_LANES), F32)] * n_interleave,
        compiler_params=pltpu.CompilerParams(
            dimension_semantics=("arbitrary", "arbitrary"),
            vmem_limit_bytes=_vmem_limit(blocks, scratch, temps),
        ),
        name="merge",
    )(x, pre_g, wgate, oa, wa, *obs, *lses, wb, wout, post_g)


def kernel(x, ffn1_pre_g, ffn1_w_gate, ffn1_w_up, ffn1_w_down, ffn1_post_g, mix_pre_g, w_in, na_rpb, w_branch_a, w_branch_b, w_out, mix_post_g, ffn2_pre_g, ffn2_w_gate, ffn2_w_up, ffn2_w_down, ffn2_post_g):
    batch, seq, d = x.shape
    depth = ffn1_pre_g.shape[0]
    for window, dilation in DIL_PAIRS:
        assert window // (2 * dilation) == DIL_HALF
    for l in range(depth):
        x = _ffn(x.reshape(batch * seq, d), ffn1_pre_g[l][None], ffn1_w_gate[l], ffn1_w_up[l],
                 ffn1_w_down[l], ffn1_post_g[l][None])
        x = x.reshape(batch, seq, d)
        qa, ka, va, qb, kb, vb = _qkv(x, mix_pre_g[l][None], w_in[l])
        oa = _nbr_attention(qa, ka, va, na_rpb[l])
        obs, lses = [], []
        for g in range(NB_GROUPS):
            o_g, lse_g = _dil_attention(qb[g], kb[g], vb[g], g)
            obs.append(o_g)
            lses.append(lse_g)
        x = _merge(x, mix_pre_g[l][None], w_in[l],
                   oa, w_branch_a[l].astype(BF16), obs, lses, w_branch_b[l].astype(BF16),
                   w_out[l].astype(BF16), mix_post_g[l][None])
        x = _ffn(x.reshape(batch * seq, d), ffn2_pre_g[l][None], ffn2_w_gate[l], ffn2_w_up[l],
                 ffn2_w_down[l], ffn2_post_g[l][None])
        x = x.reshape(batch, seq, d)
    return x
```

```python
import functools
import math

import jax
import jax.numpy as jnp
from jax import lax
from jax.experimental import pallas as pl
from jax.experimental.pallas import tpu as pltpu

D_MODEL = 1024
HEAD_DIM = 64
NA_HEADS = 8
NA_WIDTH = NA_HEADS * HEAD_DIM
NA_ROWS = 8
NA_COLS = 16
GRID_W = 64
DIL_PAIRS = ((128, 1), (512, 4), (2048, 16))
DILATIONS = tuple(d for _, d in DIL_PAIRS)
NB_GROUPS = len(DIL_PAIRS)
NB_HEADS_PER_GROUP = 4
NB_HEADS = NB_GROUPS * NB_HEADS_PER_GROUP
NB_WIDTH = NB_HEADS * HEAD_DIM
GROUP_WIDTH = NB_HEADS_PER_GROUP * HEAD_DIM
ALIBI_MAX_EXP = 8.0
D_FF = 2816
NORM_EPS = 1e-6
ATTN_SCALE = HEAD_DIM ** -0.5
LOG2E = math.log2(math.e)
LN2 = math.log(2.0)
Q_SCALE = ATTN_SCALE * LOG2E

V7X_LANES = 128
GROUP_LANE_BLOCKS = GROUP_WIDTH // V7X_LANES
V7X_VMEM_BYTES = 64 * 1024 * 1024
V7X_VMEM_RESERVE = 6 * 1024 * 1024

BF16 = jnp.bfloat16
F32 = jnp.float32


def _vmem_limit(block_bytes, scratch_bytes, temp_bytes):
    need = block_bytes + scratch_bytes + temp_bytes
    budget = V7X_VMEM_BYTES - V7X_VMEM_RESERVE
    assert need <= budget, need
    return budget


def _nbytes(shape, dtype):
    return math.prod(shape) * jnp.dtype(dtype).itemsize


def _resident(shape):
    nd = len(shape)
    return pl.BlockSpec(shape, lambda *_: (0,) * nd, pipeline_mode=pl.Buffered(1))


def _rmsnorm_f32(x, g):
    return x * lax.rsqrt(jnp.mean(x * x, axis=-1, keepdims=True) + NORM_EPS) * g


def _dot(a, b):
    return jnp.dot(a, b, preferred_element_type=F32)


def _dot_nt(a, b):
    return lax.dot_general(a, b, (((1,), (1,)), ((), ())), preferred_element_type=F32)


def _pipelined_attention(n_groups, unroll, scores, probs, finish, s_scr, p_scr, stat_scrs):
    assert n_groups % 2 == 0 and n_groups >= 2
    tiles = lambda i: [i * unroll + u for u in range(unroll)]

    def stage_scores(i, slot):
        for u, t in enumerate(tiles(i)):
            s_scr[slot, u] = scores(t)

    def stage_probs(slot):
        for u in range(unroll):
            p, *stats = probs(s_scr[slot, u])
            p_scr[slot, u] = p
            for ref, stat in zip(stat_scrs, stats):
                ref[slot, u] = stat

    def stage_finish(i, slot):
        for u, t in enumerate(tiles(i)):
            finish(t, p_scr[slot, u], *[ref[slot, u] for ref in stat_scrs])

    def step(i, parity):
        stage_scores(i + 1, 1 - parity)
        stage_finish(i - 1, 1 - parity)
        stage_probs(parity)

    stage_scores(0, 0)
    stage_scores(1, 1)
    stage_probs(0)

    def body(i, carry):
        @pl.when(i % 2 == 1)
        def _():
            step(i, 1)

        @pl.when(i % 2 == 0)
        def _():
            step(i, 0)

        return carry

    lax.fori_loop(1, n_groups - 1, body, 0)
    stage_finish(n_groups - 2, 0)
    stage_probs(1)
    stage_finish(n_groups - 1, 1)


FFN_TM = 512
SUB_TM = 512
FFN_CK = 256


def _sub_tiles(tm):
    return [slice(k * SUB_TM, (k + 1) * SUB_TM) for k in range(tm // SUB_TM)]


FFN_CHUNKS = [slice(c * FFN_CK, (c + 1) * FFN_CK) for c in range(D_FF // FFN_CK)]


def _ffn_weight_copies(wg_hbm, wu_hbm, wd_hbm, wg_ref, wu_ref, wd_ref, sem):
    gate_up = [(pltpu.make_async_copy(wg_hbm.at[:, cols], wg_ref.at[:, cols], sem.at[0, c]),
                pltpu.make_async_copy(wu_hbm.at[:, cols], wu_ref.at[:, cols], sem.at[1, c]))
               for c, cols in enumerate(FFN_CHUNKS)]
    down = [pltpu.make_async_copy(wd_hbm.at[cols, :], wd_ref.at[cols, :], sem.at[2, c])
            for c, cols in enumerate(FFN_CHUNKS)]
    return gate_up, down


def _ffn_body(x_ref, pre_g_ref, wg_ref, wu_ref, wd_ref, post_g_ref, o_ref, h_ref, a_ref, wait_gate_up, wait_down):
    h_ref[...] = _rmsnorm_f32(x_ref[...], pre_g_ref[...]).astype(BF16)
    for c, cols in enumerate(FFN_CHUNKS):
        wait_gate_up(c)
        h = h_ref[...]
        g = _dot(h, wg_ref[:, cols].astype(BF16))
        u = _dot(h, wu_ref[:, cols].astype(BF16))
        a_ref[:, cols] = (g * jax.nn.sigmoid(g) * u).astype(BF16)
    f = None
    for c, cols in enumerate(FFN_CHUNKS):
        wait_down(c)
        part = _dot(a_ref[:, cols], wd_ref[cols, :].astype(BF16))
        f = part if f is None else f + part
    o_ref[...] = x_ref[...] + 0.5 * _rmsnorm_f32(f, post_g_ref[...])


def _ffn_kernel(x_ref, pre_g_ref, wg_hbm, wu_hbm, wd_hbm, post_g_ref, o_ref, h_ref, a_ref,
                wg_ref, wu_ref, wd_ref, sem):
    refs = (x_ref, pre_g_ref, wg_ref, wu_ref, wd_ref, post_g_ref, o_ref, h_ref, a_ref)
    first = pl.program_id(0) == 0

    @pl.when(first)
    def _():
        gate_up, down = _ffn_weight_copies(wg_hbm, wu_hbm, wd_hbm, wg_ref, wu_ref, wd_ref, sem)
        for cg, cu in gate_up:
            cg.start()
            cu.start()
        for cd in down:
            cd.start()

        def wait_gate_up(c):
            gate_up[c][0].wait()
            gate_up[c][1].wait()

        _ffn_body(*refs, wait_gate_up, lambda c: down[c].wait())

    @pl.when(jnp.logical_not(first))
    def _():
        _ffn_body(*refs, lambda c: None, lambda c: None)


def _ffn(x, pre_g, wg, wu, wd, post_g):
    n, d = x.shape
    tm = FFN_TM
    row = lambda i: (i, 0)
    n_chunks = len(FFN_CHUNKS)
    blocks = 2 * 2 * _nbytes((tm, d), F32) + 2 * _nbytes((1, d), F32)
    scratch = _nbytes((tm, d), BF16) + _nbytes((tm, D_FF), BF16) + 3 * _nbytes((d, D_FF), wg.dtype)
    temps = 4 * _nbytes((tm, d), F32)
    hbm = pl.BlockSpec(memory_space=pl.ANY)
    return pl.pallas_call(
        _ffn_kernel,
        grid=(n // tm,),
        in_specs=[pl.BlockSpec((tm, d), row), _resident((1, d)), hbm, hbm, hbm, _resident((1, d))],
        out_specs=pl.BlockSpec((tm, d), row),
        out_shape=jax.ShapeDtypeStruct((n, d), F32),
        scratch_shapes=[
            pltpu.VMEM((tm, d), BF16), pltpu.VMEM((tm, D_FF), BF16),
            pltpu.VMEM((d, D_FF), wg.dtype), pltpu.VMEM((d, D_FF), wu.dtype), pltpu.VMEM((D_FF, d), wd.dtype),
            pltpu.SemaphoreType.DMA((3, n_chunks)),
        ],
        compiler_params=pltpu.CompilerParams(
            dimension_semantics=("arbitrary",),
            vmem_limit_bytes=_vmem_limit(blocks, scratch, temps),
        ),
        name="ffn",
    )(x, pre_g, wg, wu, wd, post_g)


QKV_TM = 1024
QKV_TOTAL = 3 * NA_WIDTH + 3 * NB_WIDTH
N_DIL_OUTS = 3 * NB_GROUPS
STAGED_OUTS = [i for i in range(N_DIL_OUTS) if DILATIONS[i % NB_GROUPS] > 1]
N_STAGED_OUTS = len(STAGED_OUTS)


def _qkv_kernel(x_ref, g_ref, w32_ref, *refs):
    na_refs, dil_refs = refs[:3], refs[3:3 + N_DIL_OUTS]
    w_ref, y_refs = refs[3 + N_DIL_OUTS], refs[4 + N_DIL_OUTS:]

    @pl.when((pl.program_id(0) == 0) & (pl.program_id(1) == 0))
    def _():
        for c in range(QKV_TOTAL // GROUP_WIDTH):
            cols = slice(c * GROUP_WIDTH, (c + 1) * GROUP_WIDTH)
            w_ref[:, cols] = w32_ref[:, cols].astype(BF16)

    for k, rows in enumerate(_sub_tiles(x_ref.shape[0])):
        h = _rmsnorm_f32(x_ref[rows], g_ref[...]).astype(BF16)
        for idx, o_ref in enumerate(dil_refs):
            which, g = divmod(idx, NB_GROUPS)
            dil = DILATIONS[g]
            off = 3 * NA_WIDTH + which * NB_WIDTH + g * GROUP_WIDTH
            y = _dot(h, w_ref[:, off:off + GROUP_WIDTH])
            if which == 0:
                y = y * Q_SCALE
            if dil == 1:
                o_ref[0, rows] = y.astype(BF16)
                continue
            y_ref = y_refs[k * N_STAGED_OUTS + STAGED_OUTS.index(idx)]
            per_res = SUB_TM // dil
            for c in range(GROUP_LANE_BLOCKS):
                y_ref[c] = y[:, c * V7X_LANES:(c + 1) * V7X_LANES]
            for r in range(dil):
                for c in range(GROUP_LANE_BLOCKS):
                    o_ref[r, k * per_res:(k + 1) * per_res, c * V7X_LANES:(c + 1) * V7X_LANES] = (
                        y_ref[c, pl.ds(r, per_res, stride=dil), :].astype(BF16))
        for idx, o_ref in enumerate(na_refs):
            y = _dot(h, w_ref[:, idx * NA_WIDTH:(idx + 1) * NA_WIDTH])
            if idx == 0:
                y = y * Q_SCALE
            o_ref[rows] = y.astype(BF16)


def _qkv(x, g, w_in):
    batch, seq, d = x.shape
    tm = QKV_TM
    tok = lambda b, i: (b, i, 0)
    res = lambda b, i: (b, 0, i, 0)
    out_specs = [pl.BlockSpec((None, tm, NA_WIDTH), tok)] * 3
    out_shape = [jax.ShapeDtypeStruct((batch, seq, NA_WIDTH), BF16)] * 3
    for _ in range(3):
        for dil in DILATIONS:
            out_specs.append(pl.BlockSpec((None, dil, tm // dil, GROUP_WIDTH), res))
            out_shape.append(jax.ShapeDtypeStruct((batch, dil, seq // dil, GROUP_WIDTH), BF16))
    blocks = 2 * _nbytes((tm, d), F32) + _nbytes((d, QKV_TOTAL), w_in.dtype) + 2 * _nbytes((tm, QKV_TOTAL), BF16)
    stage = (GROUP_LANE_BLOCKS, SUB_TM, V7X_LANES)
    n_stage = N_STAGED_OUTS * (tm // SUB_TM)
    scratch = n_stage * _nbytes(stage, F32) + _nbytes((d, QKV_TOTAL), BF16)
    temps = 2 * _nbytes((SUB_TM, d), F32)
    outs = pl.pallas_call(
        _qkv_kernel,
        grid=(batch, seq // tm),
        in_specs=[pl.BlockSpec((None, tm, d), tok), _resident((1, d)), _resident((d, QKV_TOTAL))],
        out_specs=out_specs,
        out_shape=out_shape,
        scratch_shapes=[pltpu.VMEM((d, QKV_TOTAL), BF16)] + [pltpu.VMEM(stage, F32)] * n_stage,
        compiler_params=pltpu.CompilerParams(
            dimension_semantics=("arbitrary", "arbitrary"),
            vmem_limit_bytes=_vmem_limit(blocks, scratch, temps),
        ),
        name="qkv",
    )(x, g, w_in)
    qa, ka, va = outs[:3]
    qb, kb, vb = (outs[3 + w * NB_GROUPS:3 + (w + 1) * NB_GROUPS] for w in range(3))
    return qa, ka, va, qb, kb, vb


NA_HEADS_PER_STEP = V7X_LANES // HEAD_DIM
NA_SPAN = NA_ROWS * GRID_W
NA_VARIANTS = NA_ROWS
NA_RPB_ROWS = 2 * NA_ROWS - 1
NA_RPB_COLS = 2 * NA_COLS - 1
NA_UNROLL = 32


def _nbr_build_bias(rep_ref, tbl_ref):
    width = NA_RPB_ROWS * GRID_W
    qc = lax.broadcasted_iota(jnp.int32, (GRID_W, width), 0)
    kc = lax.broadcasted_iota(jnp.int32, (GRID_W, width), 1) % GRID_W
    diff = kc - qc
    col0 = jnp.clip(qc - NA_COLS // 2, 0, GRID_W - NA_COLS)
    col_ok = (kc >= col0) & (kc < col0 + NA_COLS)
    for h in range(NA_HEADS_PER_STEP):
        w = jnp.full((GRID_W, width), -jnp.inf, F32)
        for k in range(NA_RPB_COLS):
            w = jnp.where(diff == k - (NA_COLS - 1), rep_ref[h, k:k + 1, :], w)
        w = jnp.where(col_ok, w * LOG2E, -jnp.inf)
        for variant in range(NA_VARIANTS):
            tbl_ref[h, variant] = w[:, variant * GRID_W:variant * GRID_W + NA_SPAN]


def _nbr_kernel(q_ref, k_ref, v_ref, rep_ref, o_ref, tbl_ref, s_scr, p_scr, *, rows):
    hp = NA_HEADS_PER_STEP
    ones = jnp.ones((NA_SPAN, V7X_LANES), BF16)

    @pl.when(pl.program_id(1) == 0)
    def _():
        _nbr_build_bias(rep_ref, tbl_ref)

    lane_head = lax.broadcasted_iota(jnp.int32, (GRID_W, V7X_LANES), 1) // HEAD_DIM

    def scores(r):
        row0 = jnp.clip(r - NA_ROWS // 2, 0, rows - NA_ROWS)
        variant = row0 - r + (NA_ROWS - 1)
        q = q_ref[pl.ds(pl.multiple_of(r * GRID_W, GRID_W), GRID_W), :]
        kw = k_ref[pl.ds(pl.multiple_of(row0 * GRID_W, GRID_W), NA_SPAN), :]
        qs = jnp.concatenate([jnp.where(lane_head == h, q, jnp.zeros_like(q)) for h in range(hp)], axis=0)
        return _dot_nt(qs, kw) + tbl_ref[:, variant].reshape(hp * GRID_W, NA_SPAN)

    def probs(s):
        return (jnp.exp2(s - jnp.max(s, axis=-1, keepdims=True)).astype(BF16),)

    def finish(r, p):
        row0 = jnp.clip(r - NA_ROWS // 2, 0, rows - NA_ROWS)
        vw = v_ref[pl.ds(pl.multiple_of(row0 * GRID_W, GRID_W), NA_SPAN), :]
        o = _dot(p, jnp.concatenate([vw, ones], axis=1))
        out, den = o[:GRID_W, :V7X_LANES], o[:GRID_W, V7X_LANES:]
        for h in range(1, hp):
            rows_h = slice(h * GRID_W, (h + 1) * GRID_W)
            out = jnp.where(lane_head == h, o[rows_h, :V7X_LANES], out)
            den = jnp.where(lane_head == h, o[rows_h, V7X_LANES:], den)
        o_ref[pl.ds(pl.multiple_of(r * GRID_W, GRID_W), GRID_W), :] = (out * (1.0 / den)).astype(BF16)

    _pipelined_attention(rows // NA_UNROLL, NA_UNROLL, scores, probs, finish, s_scr, p_scr, ())


def _nbr_attention(q, k, v, rpb):
    batch, seq, width = q.shape
    rows = seq // GRID_W
    n_pairs = width // V7X_LANES
    rep = jnp.repeat(rpb.astype(F32).transpose(0, 2, 1), GRID_W, axis=-1)
    tok = lambda p, b: (b, 0, p)
    tbl_shape = (NA_HEADS_PER_STEP, NA_VARIANTS, GRID_W, NA_SPAN)
    rep_block = (NA_HEADS_PER_STEP, NA_RPB_COLS, NA_RPB_ROWS * GRID_W)
    blocks = 2 * 4 * _nbytes((seq, V7X_LANES), BF16) + 2 * _nbytes((NA_HEADS_PER_STEP, 32, 1024), F32)
    tile = (2, NA_UNROLL, NA_HEADS_PER_STEP * GRID_W, NA_SPAN)
    scratch = _nbytes(tbl_shape, F32) + _nbytes(tile, F32) + _nbytes(tile, BF16)
    temps = NA_UNROLL * _nbytes(tile[2:], F32)
    return pl.pallas_call(
        functools.partial(_nbr_kernel, rows=rows),
        grid=(n_pairs, batch),
        in_specs=[
            pl.BlockSpec((None, seq, V7X_LANES), tok),
            pl.BlockSpec((None, seq, V7X_LANES), tok),
            pl.BlockSpec((None, seq, V7X_LANES), tok),
            pl.BlockSpec(rep_block, lambda p, b: (p, 0, 0)),
        ],
        out_specs=pl.BlockSpec((None, seq, V7X_LANES), tok),
        out_shape=jax.ShapeDtypeStruct((batch, seq, width), BF16),
        scratch_shapes=[pltpu.VMEM(tbl_shape, F32), pltpu.VMEM(tile, F32), pltpu.VMEM(tile, BF16)],
        compiler_params=pltpu.CompilerParams(
            dimension_semantics=("arbitrary", "arbitrary"),
            vmem_limit_bytes=_vmem_limit(blocks, scratch, temps),
        ),
        name="nbr_attn",
    )(q, k, v, rep)


DIL_HALF = 64
DIL_TQ = 128
DIL_TK = DIL_TQ + 2 * DIL_HALF
DIL_VARIANTS = 3
DIL_STEP_ROWS = 4096
DIL_UNROLL = 4


def _dil_bias_table(group, dilation):
    heads = jnp.arange(NB_HEADS_PER_GROUP, dtype=F32) + group * NB_HEADS_PER_GROUP
    slopes = jnp.exp2(-ALIBI_MAX_EXP * (heads + 1.0) / NB_HEADS)
    offs = jnp.array([0, -DIL_HALF, -2 * DIL_HALF], jnp.int32)
    rel = offs[:, None, None] + jnp.arange(DIL_TK)[None, None, :] - jnp.arange(DIL_TQ)[None, :, None]
    dist = (dilation * jnp.abs(rel)).astype(F32)
    bias = -(slopes[None, :, None, None] * dist[:, None]) * LOG2E
    bias = jnp.where((jnp.abs(rel) <= DIL_HALF)[:, None], bias, -jnp.inf)
    return bias.reshape(DIL_VARIANTS, NB_HEADS_PER_GROUP * DIL_TQ, DIL_TK)


def _dil_kernel(q_ref, k_ref, v_ref, bias_ref, o_ref, lse_ref, s_scr, p_scr, m_scr, *, seq_len, stretch):
    nh = NB_HEADS_PER_GROUP
    hpb = V7X_LANES // HEAD_DIM
    ones = jnp.ones((DIL_TK, V7X_LANES), BF16)
    group_lane_head = lax.broadcasted_iota(jnp.int32, (DIL_TQ, GROUP_WIDTH), 1) // HEAD_DIM
    lane_head = lax.broadcasted_iota(jnp.int32, (DIL_TQ, V7X_LANES), 1) // HEAD_DIM
    base = pl.program_id(2) * stretch
    tiles_per_residue = stretch // DIL_TQ

    def window(t):
        r, j = t // tiles_per_residue, t % tiles_per_residue
        qloc = pl.multiple_of(j * DIL_TQ, DIL_TQ)
        qs = base + qloc
        ws = jnp.clip(qs - DIL_HALF, 0, seq_len - DIL_TK)
        variant = (qs - ws) // DIL_HALF
        return r, qloc, pl.multiple_of(ws, DIL_HALF), variant

    def scores(t):
        r, qloc, ws, variant = window(t)
        q = q_ref[r, pl.ds(qloc, DIL_TQ), :]
        kw = k_ref[r, pl.ds(ws, DIL_TK), :]
        qst = jnp.concatenate([jnp.where(group_lane_head == h, q, jnp.zeros_like(q)) for h in range(nh)],
                              axis=0)
        return _dot_nt(qst, kw) + bias_ref[variant]

    def probs(s):
        m = jnp.max(s, axis=-1, keepdims=True)
        return jnp.exp2(s - m).astype(BF16), jnp.broadcast_to(m, (s.shape[0], V7X_LANES))

    def finish(t, p, m):
        r, qloc, ws, _ = window(t)
        vw = v_ref[r, pl.ds(ws, DIL_TK), :]
        outs, lses = [], []
        for c in range(GROUP_LANE_BLOCKS):
            rows_c = slice(c * hpb * DIL_TQ, (c + 1) * hpb * DIL_TQ)
            v_aug = jnp.concatenate([vw[:, c * V7X_LANES:(c + 1) * V7X_LANES], ones], axis=1)
            o = _dot(p[rows_c], v_aug)
            m_c = m[rows_c]
            out_c, den_c, max_c = o[:DIL_TQ, :V7X_LANES], o[:DIL_TQ, V7X_LANES:], m_c[:DIL_TQ]
            for h in range(1, hpb):
                rows_h = slice(h * DIL_TQ, (h + 1) * DIL_TQ)
                out_c = jnp.where(lane_head == h, o[rows_h, :V7X_LANES], out_c)
                den_c = jnp.where(lane_head == h, o[rows_h, V7X_LANES:], den_c)
                max_c = jnp.where(lane_head == h, m_c[rows_h], max_c)
            outs.append(out_c * (1.0 / den_c))
            lses.append(max_c * LN2 + jnp.log(den_c))
        o_ref[r, pl.ds(qloc, DIL_TQ), :] = jnp.concatenate(outs, axis=1).astype(BF16)
        lse_ref[r, pl.ds(qloc, DIL_TQ), :] = jnp.concatenate(lses, axis=1)

    n_tiles = q_ref.shape[0] * tiles_per_residue
    _pipelined_attention(n_tiles // DIL_UNROLL, DIL_UNROLL, scores, probs, finish, s_scr, p_scr, (m_scr,))


def _dil_attention(q, k, v, group):
    batch, dilation, seq_len, width = q.shape
    stretch = min(seq_len, DIL_STEP_ROWS)
    res_blk = DIL_STEP_ROWS // stretch
    assert dilation % res_blk == 0 and seq_len % stretch == 0
    qmap = lambda b, r, s: (b, r, s, 0)
    kvmap = lambda b, r, s: (b, r, 0, 0)
    bias = _dil_bias_table(group, dilation)
    tile = (2, DIL_UNROLL, NB_HEADS_PER_GROUP * DIL_TQ, DIL_TK)
    stat = tile[:3] + (V7X_LANES,)
    kv_buffers = 2
    kv_spec = pl.BlockSpec((None, res_blk, seq_len, width), kvmap)
    blocks = (2 * (2 * _nbytes((DIL_STEP_ROWS, width), BF16) + _nbytes((DIL_STEP_ROWS, width), F32))
              + kv_buffers * 2 * _nbytes((res_blk * seq_len, width), BF16) + _nbytes(bias.shape, F32))
    scratch = _nbytes(tile, F32) + _nbytes(tile, BF16) + _nbytes(stat, F32)
    temps = DIL_UNROLL * _nbytes(tile[2:], F32)
    return pl.pallas_call(
        functools.partial(_dil_kernel, seq_len=seq_len, stretch=stretch),
        grid=(batch, dilation // res_blk, seq_len // stretch),
        in_specs=[
            pl.BlockSpec((None, res_blk, stretch, width), qmap),
            kv_spec,
            kv_spec,
            _resident(bias.shape),
        ],
        out_specs=[pl.BlockSpec((None, res_blk, stretch, width), qmap)] * 2,
        out_shape=[jax.ShapeDtypeStruct(q.shape, BF16), jax.ShapeDtypeStruct(q.shape, F32)],
        scratch_shapes=[pltpu.VMEM(tile, F32), pltpu.VMEM(tile, BF16), pltpu.VMEM(stat, F32)],
        compiler_params=pltpu.CompilerParams(
            dimension_semantics=("arbitrary", "arbitrary", "arbitrary"),
            vmem_limit_bytes=_vmem_limit(blocks, scratch, temps),
        ),
        name=f"dil_attn_g{group}",
    )(q, k, v, bias)


MERGE_TM = 1024
MERGE_CN = 256


def _to_token_order(src_ref, scr_ref, dil, k):
    per_res = SUB_TM // dil
    res_rows = slice(k * per_res, (k + 1) * per_res)
    if dil == 1:
        return src_ref[0, res_rows].astype(F32)
    for r in range(dil):
        for c in range(GROUP_LANE_BLOCKS):
            scr_ref[c, pl.ds(k * SUB_TM + r, per_res, stride=dil), :] = (
                src_ref[r, res_rows, c * V7X_LANES:(c + 1) * V7X_LANES].astype(F32))
    rows = slice(k * SUB_TM, (k + 1) * SUB_TM)
    return jnp.concatenate([scr_ref[c, rows] for c in range(GROUP_LANE_BLOCKS)], axis=-1)


def _merge_kernel(x_ref, pre_g_ref, wgate_ref, oa_ref, wa_ref, *refs):
    ng = NB_GROUPS
    o_refs, lse_refs = refs[:ng], refs[ng:2 * ng]
    wb_ref, wout_ref, post_g_ref, out_ref, h_ref, ob_ref, mg_ref = refs[2 * ng:2 * ng + 7]
    scr_refs = iter(refs[2 * ng + 7:])
    o_scr = [next(scr_refs) if dil > 1 else None for dil in DILATIONS]
    lse_scr = [next(scr_refs) if dil > 1 else None for dil in DILATIONS]
    subs = _sub_tiles(x_ref.shape[0])
    for k, rows in enumerate(subs):
        h_ref[rows] = _rmsnorm_f32(x_ref[rows], pre_g_ref[...]).astype(BF16)
        os_ = [_to_token_order(r, s, d, k) for r, s, d in zip(o_refs, o_scr, DILATIONS)]
        lses = [_to_token_order(r, s, d, k) for r, s, d in zip(lse_refs, lse_scr, DILATIONS)]
        mx = jnp.maximum(jnp.maximum(lses[0], lses[1]), lses[2])
        es = [jnp.exp(l - mx) for l in lses]
        inv = 1.0 / (es[0] + es[1] + es[2])
        for g in range(ng):
            ob_ref[rows, g * GROUP_WIDTH:(g + 1) * GROUP_WIDTH] = (os_[g] * (es[g] * inv)).astype(BF16)
    for rows in subs:
        for n in range(D_MODEL // MERGE_CN):
            cols = slice(n * MERGE_CN, (n + 1) * MERGE_CN)
            cols_b = slice(D_MODEL + n * MERGE_CN, D_MODEL + (n + 1) * MERGE_CN)
            h = h_ref[rows]
            ga = _dot(h, wgate_ref[:, cols].astype(BF16))
            gb = _dot(h, wgate_ref[:, cols_b].astype(BF16))
            ya = _dot(oa_ref[rows], wa_ref[:, cols])
            yb = _dot(ob_ref[rows], wb_ref[:, cols])
            mg_ref[rows, cols] = (jax.nn.sigmoid(ga) * ya + jax.nn.sigmoid(gb) * yb).astype(BF16)
    for rows in subs:
        out_ref[rows] = _dot(mg_ref[rows], wout_ref[...])
    for rows in subs:
        out_ref[rows] = x_ref[rows] + _rmsnorm_f32(out_ref[rows], post_g_ref[...])


def _merge(x, pre_g, wgate, oa, wa, obs, lses, wb, wout, post_g):
    batch, seq, d = x.shape
    tm = MERGE_TM
    tok = lambda b, i: (b, i, 0)
    res = lambda b, i: (b, 0, i, 0)
    tokspec = lambda w: pl.BlockSpec((None, tm, w), tok)
    resspecs = [pl.BlockSpec((None, dil, tm // dil, GROUP_WIDTH), res) for dil in DILATIONS]
    blocks = (2 * 2 * _nbytes((tm, d), F32) + 2 * _nbytes((tm, NA_WIDTH), BF16)
              + 2 * 3 * (_nbytes((tm, GROUP_WIDTH), BF16) + _nbytes((tm, GROUP_WIDTH), F32))
              + _nbytes((NA_WIDTH, d), wa.dtype) + _nbytes((d + NB_WIDTH, d), wb.dtype) + _nbytes((d, 2 * d), F32))
    work = [pltpu.VMEM((tm, d), BF16), pltpu.VMEM((tm, NB_WIDTH), BF16), pltpu.VMEM((tm, d), BF16)]
    n_interleave = 2 * sum(dil > 1 for dil in DILATIONS)
    scratch = (n_interleave * _nbytes((tm, GROUP_WIDTH), F32) + 2 * _nbytes((tm, d), BF16)
               + _nbytes((tm, NB_WIDTH), BF16))
    temps = 2 * _nbytes((SUB_TM, d), F32)
    return pl.pallas_call(
        _merge_kernel,
        grid=(batch, seq // tm),
        in_specs=[
            tokspec(d), _resident((1, d)),
            pl.BlockSpec((pl.Element(d), pl.Element(2 * d)), lambda *_: (0, QKV_TOTAL),
                         pipeline_mode=pl.Buffered(1)),
            tokspec(NA_WIDTH), _resident((NA_WIDTH, d)),
            *resspecs, *resspecs,
            _resident((NB_WIDTH, d)), _resident((d, d)), _resident((1, d)),
        ],
        out_specs=tokspec(d),
        out_shape=jax.ShapeDtypeStruct((batch, seq, d), F32),
        scratch_shapes=work + [pltpu.VMEM((GROUP_LANE_BLOCKS, tm, V7X_LANES), F32)] * n_interleave,
        compiler_params=pltpu.CompilerParams(
            dimension_semantics=("arbitrary", "arbitrary"),
            vmem_limit_bytes=_vmem_limit(blocks, scratch, temps),
        ),
        name="merge",
    )(x, pre_g, wgate, oa, wa, *obs, *lses, wb, wout, post_g)


def kernel(x, ffn1_pre_g, ffn1_w_gate, ffn1_w_up, ffn1_w_down, ffn1_post_g, mix_pre_g, w_in, na_rpb, w_branch_a, w_branch_b, w_out, mix_post_g, ffn2_pre_g, ffn2_w_gate, ffn2_w_up, ffn2_w_down, ffn2_post_g):
    batch, seq, d = x.shape
    depth = ffn1_pre_g.shape[0]
    for window, dilation in DIL_PAIRS:
        assert window // (2 * dilation) == DIL_HALF
    for l in range(depth):
        x = _ffn(x.reshape(batch * seq, d), ffn1_pre_g[l][None], ffn1_w_gate[l], ffn1_w_up[l],
                 ffn1_w_down[l], ffn1_post_g[l][None])
        x = x.reshape(batch, seq, d)
        qa, ka, va, qb, kb, vb = _qkv(x, mix_pre_g[l][None], w_in[l])
        oa = _nbr_attention(qa, ka, va, na_rpb[l])
        obs, lses = [], []
        for g in range(NB_GROUPS):
            o_g, lse_g = _dil_attention(qb[g], kb[g], vb[g], g)
            obs.append(o_g)
            lses.append(lse_g)
        x = _merge(x, mix_pre_g[l][None], w_in[l],
                   oa, w_branch_a[l].astype(BF16), obs, lses, w_branch_b[l].astype(BF16),
                   w_out[l].astype(BF16), mix_post_g[l][None])
        x = _ffn(x.reshape(batch * seq, d), ffn2_pre_g[l][None], ffn2_w_gate[l], ffn2_w_up[l],
                 ffn2_w_down[l], ffn2_post_g[l][None])
        x = x.reshape(batch, seq, d)
    return x
```

```python
import functools
import math

import jax
import jax.numpy as jnp
from jax import lax
from jax.experimental import pallas as pl
from jax.experimental.pallas import tpu as pltpu

D_MODEL = 1024
HEAD_DIM = 64
NA_HEADS = 8
NA_WIDTH = NA_HEADS * HEAD_DIM
NA_ROWS = 8
NA_COLS = 16
GRID_W = 64
DIL_PAIRS = ((128, 1), (512, 4), (2048, 16))
DILATIONS = tuple(d for _, d in DIL_PAIRS)
NB_GROUPS = len(DIL_PAIRS)
NB_HEADS_PER_GROUP = 4
NB_HEADS = NB_GROUPS * NB_HEADS_PER_GROUP
NB_WIDTH = NB_HEADS * HEAD_DIM
GROUP_WIDTH = NB_HEADS_PER_GROUP * HEAD_DIM
ALIBI_MAX_EXP = 8.0
D_FF = 2816
NORM_EPS = 1e-6
ATTN_SCALE = HEAD_DIM ** -0.5
LOG2E = math.log2(math.e)
LN2 = math.log(2.0)
Q_SCALE = ATTN_SCALE * LOG2E

V7X_LANES = 128
GROUP_LANE_BLOCKS = GROUP_WIDTH // V7X_LANES
V7X_VMEM_BYTES = 64 * 1024 * 1024
V7X_VMEM_RESERVE = 6 * 1024 * 1024

BF16 = jnp.bfloat16
F32 = jnp.float32


def _vmem_limit(block_bytes, scratch_bytes, temp_bytes):
    need = block_bytes + scratch_bytes + temp_bytes
    budget = V7X_VMEM_BYTES - V7X_VMEM_RESERVE
    assert need <= budget, need
    return budget


def _nbytes(shape, dtype):
    return math.prod(shape) * jnp.dtype(dtype).itemsize


def _resident(shape):
    nd = len(shape)
    return pl.BlockSpec(shape, lambda *_: (0,) * nd, pipeline_mode=pl.Buffered(1))


def _rmsnorm_f32(x, g):
    return x * lax.rsqrt(jnp.mean(x * x, axis=-1, keepdims=True) + NORM_EPS) * g


def _dot(a, b):
    return jnp.dot(a, b, preferred_element_type=F32)


def _dot_nt(a, b):
    return lax.dot_general(a, b, (((1,), (1,)), ((), ())), preferred_element_type=F32)


def _pipelined_attention(n_groups, unroll, scores, probs, finish, s_scr, p_scr, stat_scrs):
    assert n_groups % 2 == 0 and n_groups >= 2
    tiles = lambda i: [i * unroll + u for u in range(unroll)]

    def stage_scores(i, slot):
        for u, t in enumerate(tiles(i)):
            s_scr[slot, u] = scores(t)

    def stage_probs(slot):
        for u in range(unroll):
            p, *stats = probs(s_scr[slot, u])
            p_scr[slot, u] = p
            for ref, stat in zip(stat_scrs, stats):
                ref[slot, u] = stat

    def stage_finish(i, slot):
        for u, t in enumerate(tiles(i)):
            finish(t, p_scr[slot, u], *[ref[slot, u] for ref in stat_scrs])

    def step(i, parity):
        stage_scores(i + 1, 1 - parity)
        stage_finish(i - 1, 1 - parity)
        stage_probs(parity)

    stage_scores(0, 0)
    stage_scores(1, 1)
    stage_probs(0)

    def body(i, carry):
        @pl.when(i % 2 == 1)
        def _():
            step(i, 1)

        @pl.when(i % 2 == 0)
        def _():
            step(i, 0)

        return carry

    lax.fori_loop(1, n_groups - 1, body, 0)
    stage_finish(n_groups - 2, 0)
    stage_probs(1)
    stage_finish(n_groups - 1, 1)


FFN_TM = 512
SUB_TM = 512
FFN_CK = 256


def _sub_tiles(tm):
    return [slice(k * SUB_TM, (k + 1) * SUB_TM) for k in range(tm // SUB_TM)]


FFN_CHUNKS = [slice(c * FFN_CK, (c + 1) * FFN_CK) for c in range(D_FF // FFN_CK)]


def _ffn_weight_copies(wg_hbm, wu_hbm, wd_hbm, wg_ref, wu_ref, wd_ref, sem):
    gate_up = [(pltpu.make_async_copy(wg_hbm.at[:, cols], wg_ref.at[:, cols], sem.at[0, c]),
                pltpu.make_async_copy(wu_hbm.at[:, cols], wu_ref.at[:, cols], sem.at[1, c]))
               for c, cols in enumerate(FFN_CHUNKS)]
    down = [pltpu.make_async_copy(wd_hbm.at[cols, :], wd_ref.at[cols, :], sem.at[2, c])
            for c, cols in enumerate(FFN_CHUNKS)]
    return gate_up, down


def _ffn_body(x_ref, pre_g_ref, wg_ref, wu_ref, wd_ref, post_g_ref, o_ref, h_ref, a_ref, wait_gate_up, wait_down):
    h_ref[...] = _rmsnorm_f32(x_ref[...], pre_g_ref[...]).astype(BF16)
    for c, cols in enumerate(FFN_CHUNKS):
        wait_gate_up(c)
        h = h_ref[...]
        g = _dot(h, wg_ref[:, cols].astype(BF16))
        u = _dot(h, wu_ref[:, cols].astype(BF16))
        a_ref[:, cols] = (g * jax.nn.sigmoid(g) * u).astype(BF16)
    f = None
    for c, cols in enumerate(FFN_CHUNKS):
        wait_down(c)
        part = _dot(a_ref[:, cols], wd_ref[cols, :].astype(BF16))
        f = part if f is None else f + part
    o_ref[...] = x_ref[...] + 0.5 * _rmsnorm_f32(f, post_g_ref[...])


def _ffn_kernel(x_ref, pre_g_ref, wg_hbm, wu_hbm, wd_hbm, post_g_ref, o_ref, h_ref, a_ref,
                wg_ref, wu_ref, wd_ref, sem):
    refs = (x_ref, pre_g_ref, wg_ref, wu_ref, wd_ref, post_g_ref, o_ref, h_ref, a_ref)
    first = pl.program_id(0) == 0

    @pl.when(first)
    def _():
        gate_up, down = _ffn_weight_copies(wg_hbm, wu_hbm, wd_hbm, wg_ref, wu_ref, wd_ref, sem)
        for cg, cu in gate_up:
            cg.start()
            cu.start()
        for cd in down:
            cd.start()

        def wait_gate_up(c):
            gate_up[c][0].wait()
            gate_up[c][1].wait()

        _ffn_body(*refs, wait_gate_up, lambda c: down[c].wait())

    @pl.when(jnp.logical_not(first))
    def _():
        _ffn_body(*refs, lambda c: None, lambda c: None)


def _ffn(x, pre_g, wg, wu, wd, post_g):
    n, d = x.shape
    tm = FFN_TM
    row = lambda i: (i, 0)
    n_chunks = len(FFN_CHUNKS)
    blocks = 2 * 2 * _nbytes((tm, d), F32) + 2 * _nbytes((1, d), F32)
    scratch = _nbytes((tm, d), BF16) + _nbytes((tm, D_FF), BF16) + 3 * _nbytes((d, D_FF), wg.dtype)
    temps = 4 * _nbytes((tm, d), F32)
    hbm = pl.BlockSpec(memory_space=pl.ANY)
    return pl.pallas_call(
        _ffn_kernel,
        grid=(n // tm,),
        in_specs=[pl.BlockSpec((tm, d), row), _resident((1, d)), hbm, hbm, hbm, _resident((1, d))],
        out_specs=pl.BlockSpec((tm, d), row),
        out_shape=jax.ShapeDtypeStruct((n, d), F32),
        scratch_shapes=[
            pltpu.VMEM((tm, d), BF16), pltpu.VMEM((tm, D_FF), BF16),
            pltpu.VMEM((d, D_FF), wg.dtype), pltpu.VMEM((d, D_FF), wu.dtype), pltpu.VMEM((D_FF, d), wd.dtype),
            pltpu.SemaphoreType.DMA((3, n_chunks)),
        ],
        compiler_params=pltpu.CompilerParams(
            dimension_semantics=("arbitrary",),
            vmem_limit_bytes=_vmem_limit(blocks, scratch, temps),
        ),
        name="ffn",
    )(x, pre_g, wg, wu, wd, post_g)


QKV_TM = 1024
QKV_TOTAL = 3 * NA_WIDTH + 3 * NB_WIDTH
N_DIL_OUTS = 3 * NB_GROUPS
STAGED_OUTS = [i for i in range(N_DIL_OUTS) if DILATIONS[i % NB_GROUPS] > 1]
N_STAGED_OUTS = len(STAGED_OUTS)
SPLIT_STRIDE = 4
SPLIT_OUTS = [i for i in STAGED_OUTS if DILATIONS[i % NB_GROUPS] == SPLIT_STRIDE ** 2]
N_SPLIT_OUTS = len(SPLIT_OUTS)


def _qkv_kernel(x_ref, g_ref, w32_ref, *refs):
    na_refs, dil_refs = refs[:3], refs[3:3 + N_DIL_OUTS]
    w_ref = refs[3 + N_DIL_OUTS]
    n_stage = N_STAGED_OUTS * (x_ref.shape[0] // SUB_TM)
    y_refs, t_refs = refs[4 + N_DIL_OUTS:4 + N_DIL_OUTS + n_stage], refs[4 + N_DIL_OUTS + n_stage:]

    @pl.when((pl.program_id(0) == 0) & (pl.program_id(1) == 0))
    def _():
        for c in range(QKV_TOTAL // GROUP_WIDTH):
            cols = slice(c * GROUP_WIDTH, (c + 1) * GROUP_WIDTH)
            w_ref[:, cols] = w32_ref[:, cols].astype(BF16)

    for k, rows in enumerate(_sub_tiles(x_ref.shape[0])):
        h = _rmsnorm_f32(x_ref[rows], g_ref[...]).astype(BF16)
        for idx, o_ref in enumerate(dil_refs):
            which, g = divmod(idx, NB_GROUPS)
            dil = DILATIONS[g]
            off = 3 * NA_WIDTH + which * NB_WIDTH + g * GROUP_WIDTH
            y = _dot(h, w_ref[:, off:off + GROUP_WIDTH])
            if which == 0:
                y = y * Q_SCALE
            if dil == 1:
                o_ref[0, rows] = y.astype(BF16)
                continue
            y_ref = y_refs[k * N_STAGED_OUTS + STAGED_OUTS.index(idx)]
            per_res = SUB_TM // dil
            for c in range(GROUP_LANE_BLOCKS):
                y_ref[c] = y[:, c * V7X_LANES:(c + 1) * V7X_LANES]
            if dil == SPLIT_STRIDE ** 2:
                t_ref = t_refs[k * N_SPLIT_OUTS + SPLIT_OUTS.index(idx)]
                quarter = SUB_TM // SPLIT_STRIDE
                for c in range(GROUP_LANE_BLOCKS):
                    for r4 in range(SPLIT_STRIDE):
                        t_ref[c, r4 * quarter:(r4 + 1) * quarter] = y_ref[c, pl.ds(r4, quarter, stride=SPLIT_STRIDE), :]
                for r in range(dil):
                    start = (r % SPLIT_STRIDE) * quarter + r // SPLIT_STRIDE
                    for c in range(GROUP_LANE_BLOCKS):
                        o_ref[r, k * per_res:(k + 1) * per_res, c * V7X_LANES:(c + 1) * V7X_LANES] = (
                            t_ref[c, pl.ds(start, per_res, stride=SPLIT_STRIDE), :].astype(BF16))
                continue
            for r in range(dil):
                for c in range(GROUP_LANE_BLOCKS):
                    o_ref[r, k * per_res:(k + 1) * per_res, c * V7X_LANES:(c + 1) * V7X_LANES] = (
                        y_ref[c, pl.ds(r, per_res, stride=dil), :].astype(BF16))
        for idx, o_ref in enumerate(na_refs):
            y = _dot(h, w_ref[:, idx * NA_WIDTH:(idx + 1) * NA_WIDTH])
            if idx == 0:
                y = y * Q_SCALE
            o_ref[rows] = y.astype(BF16)


def _qkv(x, g, w_in):
    batch, seq, d = x.shape
    tm = QKV_TM
    tok = lambda b, i: (b, i, 0)
    res = lambda b, i: (b, 0, i, 0)
    out_specs = [pl.BlockSpec((None, tm, NA_WIDTH), tok)] * 3
    out_shape = [jax.ShapeDtypeStruct((batch, seq, NA_WIDTH), BF16)] * 3
    for _ in range(3):
        for dil in DILATIONS:
            out_specs.append(pl.BlockSpec((None, dil, tm // dil, GROUP_WIDTH), res))
            out_shape.append(jax.ShapeDtypeStruct((batch, dil, seq // dil, GROUP_WIDTH), BF16))
    blocks = 2 * _nbytes((tm, d), F32) + _nbytes((d, QKV_TOTAL), w_in.dtype) + 2 * _nbytes((tm, QKV_TOTAL), BF16)
    stage = (GROUP_LANE_BLOCKS, SUB_TM, V7X_LANES)
    n_stage = N_STAGED_OUTS * (tm // SUB_TM)
    n_split = N_SPLIT_OUTS * (tm // SUB_TM)
    scratch = (n_stage + n_split) * _nbytes(stage, F32) + _nbytes((d, QKV_TOTAL), BF16)
    temps = _nbytes((SUB_TM, d), F32)
    outs = pl.pallas_call(
        _qkv_kernel,
        grid=(batch, seq // tm),
        in_specs=[pl.BlockSpec((None, tm, d), tok), _resident((1, d)), _resident((d, QKV_TOTAL))],
        out_specs=out_specs,
        out_shape=out_shape,
        scratch_shapes=[pltpu.VMEM((d, QKV_TOTAL), BF16)] + [pltpu.VMEM(stage, F32)] * (n_stage + n_split),
        compiler_params=pltpu.CompilerParams(
            dimension_semantics=("arbitrary", "arbitrary"),
            vmem_limit_bytes=_vmem_limit(blocks, scratch, temps),
        ),
        name="qkv",
    )(x, g, w_in)
    qa, ka, va = outs[:3]
    qb, kb, vb = (outs[3 + w * NB_GROUPS:3 + (w + 1) * NB_GROUPS] for w in range(3))
    return qa, ka, va, qb, kb, vb


NA_HEADS_PER_STEP = V7X_LANES // HEAD_DIM
NA_SPAN = NA_ROWS * GRID_W
NA_VARIANTS = NA_ROWS
NA_RPB_ROWS = 2 * NA_ROWS - 1
NA_RPB_COLS = 2 * NA_COLS - 1
NA_UNROLL = 16


def _nbr_build_bias(rep_ref, tbl_ref):
    width = NA_RPB_ROWS * GRID_W
    qc = lax.broadcasted_iota(jnp.int32, (GRID_W, width), 0)
    kc = lax.broadcasted_iota(jnp.int32, (GRID_W, width), 1) % GRID_W
    diff = kc - qc
    col0 = jnp.clip(qc - NA_COLS // 2, 0, GRID_W - NA_COLS)
    col_ok = (kc >= col0) & (kc < col0 + NA_COLS)
    for h in range(NA_HEADS_PER_STEP):
        w = jnp.full((GRID_W, width), -jnp.inf, F32)
        for k in range(NA_RPB_COLS):
            w = jnp.where(diff == k - (NA_COLS - 1), rep_ref[h, k:k + 1, :], w)
        w = jnp.where(col_ok, w * LOG2E, -jnp.inf)
        for variant in range(NA_VARIANTS):
            tbl_ref[h, variant] = w[:, variant * GRID_W:variant * GRID_W + NA_SPAN]


def _nbr_kernel(q_ref, k_ref, v_ref, rep_ref, o_ref, tbl_ref, s_scr, p_scr, *, rows):
    hp = NA_HEADS_PER_STEP
    ones = jnp.ones((NA_SPAN, V7X_LANES), BF16)

    @pl.when(pl.program_id(1) == 0)
    def _():
        _nbr_build_bias(rep_ref, tbl_ref)

    lane_head = lax.broadcasted_iota(jnp.int32, (GRID_W, V7X_LANES), 1) // HEAD_DIM

    def scores(r):
        row0 = jnp.clip(r - NA_ROWS // 2, 0, rows - NA_ROWS)
        variant = row0 - r + (NA_ROWS - 1)
        q = q_ref[pl.ds(pl.multiple_of(r * GRID_W, GRID_W), GRID_W), :]
        kw = k_ref[pl.ds(pl.multiple_of(row0 * GRID_W, GRID_W), NA_SPAN), :]
        qs = jnp.concatenate([jnp.where(lane_head == h, q, jnp.zeros_like(q)) for h in range(hp)], axis=0)
        return _dot_nt(qs, kw) + tbl_ref[:, variant].reshape(hp * GRID_W, NA_SPAN)

    def probs(s):
        return (jnp.exp2(s - jnp.max(s, axis=-1, keepdims=True)).astype(BF16),)

    def finish(r, p):
        row0 = jnp.clip(r - NA_ROWS // 2, 0, rows - NA_ROWS)
        vw = v_ref[pl.ds(pl.multiple_of(row0 * GRID_W, GRID_W), NA_SPAN), :]
        o = _dot(p, jnp.concatenate([vw, ones], axis=1))
        out, den = o[:GRID_W, :V7X_LANES], o[:GRID_W, V7X_LANES:]
        for h in range(1, hp):
            rows_h = slice(h * GRID_W, (h + 1) * GRID_W)
            out = jnp.where(lane_head == h, o[rows_h, :V7X_LANES], out)
            den = jnp.where(lane_head == h, o[rows_h, V7X_LANES:], den)
        o_ref[pl.ds(pl.multiple_of(r * GRID_W, GRID_W), GRID_W), :] = (out * (1.0 / den)).astype(BF16)

    _pipelined_attention(rows // NA_UNROLL, NA_UNROLL, scores, probs, finish, s_scr, p_scr, ())


def _nbr_attention(q, k, v, rpb):
    batch, seq, width = q.shape
    rows = seq // GRID_W
    n_pairs = width // V7X_LANES
    rep = jnp.repeat(rpb.astype(F32).transpose(0, 2, 1), GRID_W, axis=-1)
    tok = lambda p, b: (b, 0, p)
    tbl_shape = (NA_HEADS_PER_STEP, NA_VARIANTS, GRID_W, NA_SPAN)
    rep_block = (NA_HEADS_PER_STEP, NA_RPB_COLS, NA_RPB_ROWS * GRID_W)
    blocks = 2 * 4 * _nbytes((seq, V7X_LANES), BF16) + 2 * _nbytes((NA_HEADS_PER_STEP, 32, 1024), F32)
    tile = (2, NA_UNROLL, NA_HEADS_PER_STEP * GRID_W, NA_SPAN)
    scratch = _nbytes(tbl_shape, F32) + _nbytes(tile, F32) + _nbytes(tile, BF16)
    temps = NA_UNROLL * _nbytes(tile[2:], F32)
    return pl.pallas_call(
        functools.partial(_nbr_kernel, rows=rows),
        grid=(n_pairs, batch),
        in_specs=[
            pl.BlockSpec((None, seq, V7X_LANES), tok),
            pl.BlockSpec((None, seq, V7X_LANES), tok),
            pl.BlockSpec((None, seq, V7X_LANES), tok),
            pl.BlockSpec(rep_block, lambda p, b: (p, 0, 0)),
        ],
        out_specs=pl.BlockSpec((None, seq, V7X_LANES), tok),
        out_shape=jax.ShapeDtypeStruct((batch, seq, width), BF16),
        scratch_shapes=[pltpu.VMEM(tbl_shape, F32), pltpu.VMEM(tile, F32), pltpu.VMEM(tile, BF16)],
        compiler_params=pltpu.CompilerParams(
            dimension_semantics=("arbitrary", "arbitrary"),
            vmem_limit_bytes=_vmem_limit(blocks, scratch, temps),
        ),
        name="nbr_attn",
    )(q, k, v, rep)


DIL_HALF = 64
DIL_TQ = 128
DIL_TK = DIL_TQ + 2 * DIL_HALF
DIL_VARIANTS = 3
DIL_STEP_ROWS = 4096
DIL_UNROLL = 8


def _dil_bias_table(group, dilation):
    heads = jnp.arange(NB_HEADS_PER_GROUP, dtype=F32) + group * NB_HEADS_PER_GROUP
    slopes = jnp.exp2(-ALIBI_MAX_EXP * (heads + 1.0) / NB_HEADS)
    offs = jnp.array([0, -DIL_HALF, -2 * DIL_HALF], jnp.int32)
    rel = offs[:, None, None] + jnp.arange(DIL_TK)[None, None, :] - jnp.arange(DIL_TQ)[None, :, None]
    dist = (dilation * jnp.abs(rel)).astype(F32)
    bias = -(slopes[None, :, None, None] * dist[:, None]) * LOG2E
    bias = jnp.where((jnp.abs(rel) <= DIL_HALF)[:, None], bias, -jnp.inf)
    return bias.reshape(DIL_VARIANTS, NB_HEADS_PER_GROUP * DIL_TQ, DIL_TK)


def _dil_kernel(q_ref, k_ref, v_ref, bias_ref, o_ref, lse_ref, s_scr, p_scr, m_scr, *, seq_len, stretch):
    nh = NB_HEADS_PER_GROUP
    hpb = V7X_LANES // HEAD_DIM
    ones = jnp.ones((DIL_TK, V7X_LANES), BF16)
    group_lane_head = lax.broadcasted_iota(jnp.int32, (DIL_TQ, GROUP_WIDTH), 1) // HEAD_DIM
    lane_head = lax.broadcasted_iota(jnp.int32, (DIL_TQ, V7X_LANES), 1) // HEAD_DIM
    base = pl.program_id(2) * stretch
    tiles_per_residue = stretch // DIL_TQ

    def window(t):
        r, j = t // tiles_per_residue, t % tiles_per_residue
        qloc = pl.multiple_of(j * DIL_TQ, DIL_TQ)
        qs = base + qloc
        ws = jnp.clip(qs - DIL_HALF, 0, seq_len - DIL_TK)
        variant = (qs - ws) // DIL_HALF
        return r, qloc, pl.multiple_of(ws, DIL_HALF), variant

    def scores(t):
        r, qloc, ws, variant = window(t)
        q = q_ref[r, pl.ds(qloc, DIL_TQ), :]
        kw = k_ref[r, pl.ds(ws, DIL_TK), :]
        qst = jnp.concatenate([jnp.where(group_lane_head == h, q, jnp.zeros_like(q)) for h in range(nh)],
                              axis=0)
        return _dot_nt(qst, kw) + bias_ref[variant]

    def probs(s):
        m = jnp.max(s, axis=-1, keepdims=True)
        return jnp.exp2(s - m).astype(BF16), jnp.broadcast_to(m, (s.shape[0], V7X_LANES))

    def finish(t, p, m):
        r, qloc, ws, _ = window(t)
        vw = v_ref[r, pl.ds(ws, DIL_TK), :]
        outs, lses = [], []
        for c in range(GROUP_LANE_BLOCKS):
            rows_c = slice(c * hpb * DIL_TQ, (c + 1) * hpb * DIL_TQ)
            v_aug = jnp.concatenate([vw[:, c * V7X_LANES:(c + 1) * V7X_LANES], ones], axis=1)
            o = _dot(p[rows_c], v_aug)
            m_c = m[rows_c]
            out_c, den_c, max_c = o[:DIL_TQ, :V7X_LANES], o[:DIL_TQ, V7X_LANES:], m_c[:DIL_TQ]
            for h in range(1, hpb):
                rows_h = slice(h * DIL_TQ, (h + 1) * DIL_TQ)
                out_c = jnp.where(lane_head == h, o[rows_h, :V7X_LANES], out_c)
                den_c = jnp.where(lane_head == h, o[rows_h, V7X_LANES:], den_c)
                max_c = jnp.where(lane_head == h, m_c[rows_h], max_c)
            outs.append(out_c * (1.0 / den_c))
            lses.append(max_c * LN2 + jnp.log(den_c))
        o_ref[r, pl.ds(qloc, DIL_TQ), :] = jnp.concatenate(outs, axis=1).astype(BF16)
        lse_ref[r, pl.ds(qloc, DIL_TQ), :] = jnp.concatenate(lses, axis=1)

    n_tiles = q_ref.shape[0] * tiles_per_residue
    _pipelined_attention(n_tiles // DIL_UNROLL, DIL_UNROLL, scores, probs, finish, s_scr, p_scr, (m_scr,))


def _dil_attention(q, k, v, group):
    batch, dilation, seq_len, width = q.shape
    stretch = min(seq_len, DIL_STEP_ROWS)
    res_blk = DIL_STEP_ROWS // stretch
    assert dilation % res_blk == 0 and seq_len % stretch == 0
    qmap = lambda b, r, s: (b, r, s, 0)
    kvmap = lambda b, r, s: (b, r, 0, 0)
    bias = _dil_bias_table(group, dilation)
    tile = (2, DIL_UNROLL, NB_HEADS_PER_GROUP * DIL_TQ, DIL_TK)
    stat = tile[:3] + (V7X_LANES,)
    kv_buffers = 2
    kv_spec = pl.BlockSpec((None, res_blk, seq_len, width), kvmap)
    blocks = (2 * (2 * _nbytes((DIL_STEP_ROWS, width), BF16) + _nbytes((DIL_STEP_ROWS, width), F32))
              + kv_buffers * 2 * _nbytes((res_blk * seq_len, width), BF16) + _nbytes(bias.shape, F32))
    scratch = _nbytes(tile, F32) + _nbytes(tile, BF16) + _nbytes(stat, F32)
    temps = DIL_UNROLL * _nbytes(tile[2:], F32)
    return pl.pallas_call(
        functools.partial(_dil_kernel, seq_len=seq_len, stretch=stretch),
        grid=(batch, dilation // res_blk, seq_len // stretch),
        in_specs=[
            pl.BlockSpec((None, res_blk, stretch, width), qmap),
            kv_spec,
            kv_spec,
            _resident(bias.shape),
        ],
        out_specs=[pl.BlockSpec((None, res_blk, stretch, width), qmap)] * 2,
        out_shape=[jax.ShapeDtypeStruct(q.shape, BF16), jax.ShapeDtypeStruct(q.shape, F32)],
        scratch_shapes=[pltpu.VMEM(tile, F32), pltpu.VMEM(tile, BF16), pltpu.VMEM(stat, F32)],
        compiler_params=pltpu.CompilerParams(
            dimension_semantics=("arbitrary", "arbitrary", "arbitrary"),
            vmem_limit_bytes=_vmem_limit(blocks, scratch, temps),
        ),
        name=f"dil_attn_g{group}",
    )(q, k, v, bias)


MERGE_TM = 1024
MERGE_CN = 256


def _to_token_order(src_ref, scr_ref, dil, k):
    per_res = SUB_TM // dil
    res_rows = slice(k * per_res, (k + 1) * per_res)
    if dil == 1:
        return src_ref[0, res_rows].astype(F32)
    for r in range(dil):
        for c in range(GROUP_LANE_BLOCKS):
            scr_ref[c, pl.ds(k * SUB_TM + r, per_res, stride=dil), :] = (
                src_ref[r, res_rows, c * V7X_LANES:(c + 1) * V7X_LANES].astype(F32))
    rows = slice(k * SUB_TM, (k + 1) * SUB_TM)
    return jnp.concatenate([scr_ref[c, rows] for c in range(GROUP_LANE_BLOCKS)], axis=-1)


def _merge_kernel(x_ref, pre_g_ref, wgate_ref, oa_ref, wa_ref, *refs):
    ng = NB_GROUPS
    o_refs, lse_refs = refs[:ng], refs[ng:2 * ng]
    wb_ref, wout_ref, post_g_ref, out_ref, h_ref, ob_ref, mg_ref = refs[2 * ng:2 * ng + 7]
    scr_refs = iter(refs[2 * ng + 7:])
    o_scr = [next(scr_refs) if dil > 1 else None for dil in DILATIONS]
    lse_scr = [next(scr_refs) if dil > 1 else None for dil in DILATIONS]
    subs = _sub_tiles(x_ref.shape[0])
    for k, rows in enumerate(subs):
        h_ref[rows] = _rmsnorm_f32(x_ref[rows], pre_g_ref[...]).astype(BF16)
        os_ = [_to_token_order(r, s, d, k) for r, s, d in zip(o_refs, o_scr, DILATIONS)]
        lses = [_to_token_order(r, s, d, k) for r, s, d in zip(lse_refs, lse_scr, DILATIONS)]
        mx = jnp.maximum(jnp.maximum(lses[0], lses[1]), lses[2])
        es = [jnp.exp(l - mx) for l in lses]
        inv = 1.0 / (es[0] + es[1] + es[2])
        for g in range(ng):
            ob_ref[rows, g * GROUP_WIDTH:(g + 1) * GROUP_WIDTH] = (os_[g] * (es[g] * inv)).astype(BF16)
    for rows in subs:
        for n in range(D_MODEL // MERGE_CN):
            cols = slice(n * MERGE_CN, (n + 1) * MERGE_CN)
            cols_b = slice(D_MODEL + n * MERGE_CN, D_MODEL + (n + 1) * MERGE_CN)
            h = h_ref[rows]
            ga = _dot(h, wgate_ref[:, cols].astype(BF16))
            gb = _dot(h, wgate_ref[:, cols_b].astype(BF16))
            ya = _dot(oa_ref[rows], wa_ref[:, cols])
            yb = _dot(ob_ref[rows], wb_ref[:, cols])
            mg_ref[rows, cols] = (jax.nn.sigmoid(ga) * ya + jax.nn.sigmoid(gb) * yb).astype(BF16)
    for rows in subs:
        out_ref[rows] = _dot(mg_ref[rows], wout_ref[...])
    for rows in subs:
        out_ref[rows] = x_ref[rows] + _rmsnorm_f32(out_ref[rows], post_g_ref[...])


def _merge(x, pre_g, wgate, oa, wa, obs, lses, wb, wout, post_g):
    batch, seq, d = x.shape
    tm = MERGE_TM
    tok = lambda b, i: (b, i, 0)
    res = lambda b, i: (b, 0, i, 0)
    tokspec = lambda w: pl.BlockSpec((None, tm, w), tok)
    resspecs = [pl.BlockSpec((None, dil, tm // dil, GROUP_WIDTH), res) for dil in DILATIONS]
    blocks = (2 * 2 * _nbytes((tm, d), F32) + 2 * _nbytes((tm, NA_WIDTH), BF16)
              + 2 * 3 * (_nbytes((tm, GROUP_WIDTH), BF16) + _nbytes((tm, GROUP_WIDTH), F32))
              + _nbytes((NA_WIDTH, d), wa.dtype) + _nbytes((d + NB_WIDTH, d), wb.dtype) + _nbytes((d, 2 * d), F32))
    work = [pltpu.VMEM((tm, d), BF16), pltpu.VMEM((tm, NB_WIDTH), BF16), pltpu.VMEM((tm, d), BF16)]
    n_interleave = 2 * sum(dil > 1 for dil in DILATIONS)
    scratch = (n_interleave * _nbytes((tm, GROUP_WIDTH), F32) + 2 * _nbytes((tm, d), BF16)
               + _nbytes((tm, NB_WIDTH), BF16))
    temps = 2 * _nbytes((SUB_TM, d), F32)
    return pl.pallas_call(
        _merge_kernel,
        grid=(batch, seq // tm),
        in_specs=[
            tokspec(d), _resident((1, d)),
            pl.BlockSpec((pl.Element(d), pl.Element(2 * d)), lambda *_: (0, QKV_TOTAL),
                         pipeline_mode=pl.Buffered(1)),
            tokspec(NA_WIDTH), _resident((NA_WIDTH, d)),
            *resspecs, *resspecs,
            _resident((NB_WIDTH, d)), _resident((d, d)), _resident((1, d)),
        ],
        out_specs=tokspec(d),
        out_shape=jax.ShapeDtypeStruct((batch, seq, d), F32),
        scratch_shapes=work + [pltpu.VMEM((GROUP_LANE_BLOCKS, tm, V7X_LANES), F32)] * n_interleave,
        compiler_params=pltpu.CompilerParams(
            dimension_semantics=("arbitrary", "arbitrary"),
            vmem_limit_bytes=_vmem_limit(blocks, scratch, temps),
        ),
        name="merge",
    )(x, pre_g, wgate, oa, wa, *obs, *lses, wb, wout, post_g)


def kernel(x, ffn1_pre_g, ffn1_w_gate, ffn1_w_up, ffn1_w_down, ffn1_post_g, mix_pre_g, w_in, na_rpb, w_branch_a, w_branch_b, w_out, mix_post_g, ffn2_pre_g, ffn2_w_gate, ffn2_w_up, ffn2_w_down, ffn2_post_g):
    batch, seq, d = x.shape
    depth = ffn1_pre_g.shape[0]
    for window, dilation in DIL_PAIRS:
        assert window // (2 * dilation) == DIL_HALF
    for l in range(depth):
        x = _ffn(x.reshape(batch * seq, d), ffn1_pre_g[l][None], ffn1_w_gate[l], ffn1_w_up[l],
                 ffn1_w_down[l], ffn1_post_g[l][None])
        x = x.reshape(batch, seq, d)
        qa, ka, va, qb, kb, vb = _qkv(x, mix_pre_g[l][None], w_in[l])
        oa = _nbr_attention(qa, ka, va, na_rpb[l])
        obs, lses = [], []
        for g in range(NB_GROUPS):
            o_g, lse_g = _dil_attention(qb[g], kb[g], vb[g], g)
            obs.append(o_g)
            lses.append(lse_g)
        x = _merge(x, mix_pre_g[l][None], w_in[l],
                   oa, w_branch_a[l].astype(BF16), obs, lses, w_branch_b[l].astype(BF16),
                   w_out[l].astype(BF16), mix_post_g[l][None])
        x = _ffn(x.reshape(batch * seq, d), ffn2_pre_g[l][None], ffn2_w_gate[l], ffn2_w_up[l],
                 ffn2_w_down[l], ffn2_post_g[l][None])
        x = x.reshape(batch, seq, d)
    return x
```

```python
import functools
import math

import jax
import jax.numpy as jnp
from jax import lax
from jax.experimental import pallas as pl
from jax.experimental.pallas import tpu as pltpu

D_MODEL = 1024
HEAD_DIM = 64
NA_HEADS = 8
NA_WIDTH = NA_HEADS * HEAD_DIM
NA_ROWS = 8
NA_COLS = 16
GRID_W = 64
DIL_PAIRS = ((128, 1), (512, 4), (2048, 16))
DILATIONS = tuple(d for _, d in DIL_PAIRS)
NB_GROUPS = len(DIL_PAIRS)
NB_HEADS_PER_GROUP = 4
NB_HEADS = NB_GROUPS * NB_HEADS_PER_GROUP
NB_WIDTH = NB_HEADS * HEAD_DIM
GROUP_WIDTH = NB_HEADS_PER_GROUP * HEAD_DIM
ALIBI_MAX_EXP = 8.0
D_FF = 2816
NORM_EPS = 1e-6
ATTN_SCALE = HEAD_DIM ** -0.5
LOG2E = math.log2(math.e)
LN2 = math.log(2.0)
Q_SCALE = ATTN_SCALE * LOG2E

V7X_LANES = 128
GROUP_LANE_BLOCKS = GROUP_WIDTH // V7X_LANES
V7X_VMEM_BYTES = 64 * 1024 * 1024
V7X_VMEM_RESERVE = 6 * 1024 * 1024

BF16 = jnp.bfloat16
F32 = jnp.float32


def _vmem_limit(block_bytes, scratch_bytes, temp_bytes):
    need = block_bytes + scratch_bytes + temp_bytes
    budget = V7X_VMEM_BYTES - V7X_VMEM_RESERVE
    assert need <= budget, need
    return budget


def _nbytes(shape, dtype):
    return math.prod(shape) * jnp.dtype(dtype).itemsize


def _resident(shape):
    nd = len(shape)
    return pl.BlockSpec(shape, lambda *_: (0,) * nd, pipeline_mode=pl.Buffered(1))


def _rmsnorm_f32(x, g):
    return x * lax.rsqrt(jnp.mean(x * x, axis=-1, keepdims=True) + NORM_EPS) * g


def _dot(a, b):
    return jnp.dot(a, b, preferred_element_type=F32)


def _dot_nt(a, b):
    return lax.dot_general(a, b, (((1,), (1,)), ((), ())), preferred_element_type=F32)


def _pipelined_attention(n_groups, unroll, scores, probs, finish, s_scr, p_scr, stat_scrs):
    assert n_groups % 2 == 0 and n_groups >= 2
    tiles = lambda i: [i * unroll + u for u in range(unroll)]

    def stage_scores(i, slot):
        for u, t in enumerate(tiles(i)):
            s_scr[slot, u] = scores(t)

    def stage_probs(slot):
        for u in range(unroll):
            p, *stats = probs(s_scr[slot, u])
            p_scr[slot, u] = p
            for ref, stat in zip(stat_scrs, stats):
                ref[slot, u] = stat

    def stage_finish(i, slot):
        for u, t in enumerate(tiles(i)):
            finish(t, p_scr[slot, u], *[ref[slot, u] for ref in stat_scrs])

    def step(i, parity):
        stage_scores(i + 1, 1 - parity)
        stage_finish(i - 1, 1 - parity)
        stage_probs(parity)

    stage_scores(0, 0)
    stage_scores(1, 1)
    stage_probs(0)

    def body(i, carry):
        @pl.when(i % 2 == 1)
        def _():
            step(i, 1)

        @pl.when(i % 2 == 0)
        def _():
            step(i, 0)

        return carry

    lax.fori_loop(1, n_groups - 1, body, 0)
    stage_finish(n_groups - 2, 0)
    stage_probs(1)
    stage_finish(n_groups - 1, 1)


FFN_TM = 512
SUB_TM = 512
FFN_CK = 256


def _sub_tiles(tm):
    return [slice(k * SUB_TM, (k + 1) * SUB_TM) for k in range(tm // SUB_TM)]


FFN_CHUNKS = [slice(c * FFN_CK, (c + 1) * FFN_CK) for c in range(D_FF // FFN_CK)]


def _ffn_weight_copies(wg_hbm, wu_hbm, wd_hbm, wg_ref, wu_ref, wd_ref, sem):
    gate_up = [(pltpu.make_async_copy(wg_hbm.at[:, cols], wg_ref.at[:, cols], sem.at[0, c]),
                pltpu.make_async_copy(wu_hbm.at[:, cols], wu_ref.at[:, cols], sem.at[1, c]))
               for c, cols in enumerate(FFN_CHUNKS)]
    down = [pltpu.make_async_copy(wd_hbm.at[cols, :], wd_ref.at[cols, :], sem.at[2, c])
            for c, cols in enumerate(FFN_CHUNKS)]
    return gate_up, down


def _ffn_body(x_ref, pre_g_ref, wg_ref, wu_ref, wd_ref, post_g_ref, o_ref, h_ref, a_ref, wait_gate_up, wait_down):
    h_ref[...] = _rmsnorm_f32(x_ref[...], pre_g_ref[...]).astype(BF16)
    for c, cols in enumerate(FFN_CHUNKS):
        wait_gate_up(c)
        h = h_ref[...]
        g = _dot(h, wg_ref[:, cols].astype(BF16))
        u = _dot(h, wu_ref[:, cols].astype(BF16))
        a_ref[:, cols] = (g * jax.nn.sigmoid(g) * u).astype(BF16)
    f = None
    for c, cols in enumerate(FFN_CHUNKS):
        wait_down(c)
        part = _dot(a_ref[:, cols], wd_ref[cols, :].astype(BF16))
        f = part if f is None else f + part
    o_ref[...] = x_ref[...] + 0.5 * _rmsnorm_f32(f, post_g_ref[...])


def _ffn_kernel(x_ref, pre_g_ref, wg_hbm, wu_hbm, wd_hbm, post_g_ref, o_ref, h_ref, a_ref,
                wg_ref, wu_ref, wd_ref, sem):
    refs = (x_ref, pre_g_ref, wg_ref, wu_ref, wd_ref, post_g_ref, o_ref, h_ref, a_ref)
    first = pl.program_id(0) == 0

    @pl.when(first)
    def _():
        gate_up, down = _ffn_weight_copies(wg_hbm, wu_hbm, wd_hbm, wg_ref, wu_ref, wd_ref, sem)
        for cg, cu in gate_up:
            cg.start()
            cu.start()
        for cd in down:
            cd.start()

        def wait_gate_up(c):
            gate_up[c][0].wait()
            gate_up[c][1].wait()

        _ffn_body(*refs, wait_gate_up, lambda c: down[c].wait())

    @pl.when(jnp.logical_not(first))
    def _():
        _ffn_body(*refs, lambda c: None, lambda c: None)


def _ffn(x, pre_g, wg, wu, wd, post_g):
    n, d = x.shape
    tm = FFN_TM
    row = lambda i: (i, 0)
    n_chunks = len(FFN_CHUNKS)
    blocks = 2 * 2 * _nbytes((tm, d), F32) + 2 * _nbytes((1, d), F32)
    scratch = _nbytes((tm, d), BF16) + _nbytes((tm, D_FF), BF16) + 3 * _nbytes((d, D_FF), wg.dtype)
    temps = 4 * _nbytes((tm, d), F32)
    hbm = pl.BlockSpec(memory_space=pl.ANY)
    return pl.pallas_call(
        _ffn_kernel,
        grid=(n // tm,),
        in_specs=[pl.BlockSpec((tm, d), row), _resident((1, d)), hbm, hbm, hbm, _resident((1, d))],
        out_specs=pl.BlockSpec((tm, d), row),
        out_shape=jax.ShapeDtypeStruct((n, d), F32),
        scratch_shapes=[
            pltpu.VMEM((tm, d), BF16), pltpu.VMEM((tm, D_FF), BF16),
            pltpu.VMEM((d, D_FF), wg.dtype), pltpu.VMEM((d, D_FF), wu.dtype), pltpu.VMEM((D_FF, d), wd.dtype),
            pltpu.SemaphoreType.DMA((3, n_chunks)),
        ],
        compiler_params=pltpu.CompilerParams(
            dimension_semantics=("arbitrary",),
            vmem_limit_bytes=_vmem_limit(blocks, scratch, temps),
        ),
        name="ffn",
    )(x, pre_g, wg, wu, wd, post_g)


QKV_TM = 1024
QKV_TOTAL = 3 * NA_WIDTH + 3 * NB_WIDTH
N_DIL_OUTS = 3 * NB_GROUPS
STAGED_OUTS = [i for i in range(N_DIL_OUTS) if DILATIONS[i % NB_GROUPS] > 1]
N_STAGED_OUTS = len(STAGED_OUTS)
SPLIT_STRIDE = 4
SPLIT_OUTS = [i for i in STAGED_OUTS if DILATIONS[i % NB_GROUPS] == SPLIT_STRIDE ** 2]
N_SPLIT_OUTS = len(SPLIT_OUTS)


def _qkv_kernel(x_ref, g_ref, w32_ref, *refs):
    na_refs, dil_refs = refs[:3], refs[3:3 + N_DIL_OUTS]
    w_ref = refs[3 + N_DIL_OUTS]
    n_stage = N_STAGED_OUTS * (x_ref.shape[0] // SUB_TM)
    y_refs, t_refs = refs[4 + N_DIL_OUTS:4 + N_DIL_OUTS + n_stage], refs[4 + N_DIL_OUTS + n_stage:]

    @pl.when((pl.program_id(0) == 0) & (pl.program_id(1) == 0))
    def _():
        for c in range(QKV_TOTAL // GROUP_WIDTH):
            cols = slice(c * GROUP_WIDTH, (c + 1) * GROUP_WIDTH)
            w_ref[:, cols] = w32_ref[:, cols].astype(BF16)

    for k, rows in enumerate(_sub_tiles(x_ref.shape[0])):
        h = _rmsnorm_f32(x_ref[rows], g_ref[...]).astype(BF16)
        for idx, o_ref in enumerate(dil_refs):
            which, g = divmod(idx, NB_GROUPS)
            dil = DILATIONS[g]
            off = 3 * NA_WIDTH + which * NB_WIDTH + g * GROUP_WIDTH
            y = _dot(h, w_ref[:, off:off + GROUP_WIDTH])
            if which == 0:
                y = y * Q_SCALE
            if dil == 1:
                o_ref[0, rows] = y.astype(BF16)
                continue
            y_ref = y_refs[k * N_STAGED_OUTS + STAGED_OUTS.index(idx)]
            per_res = SUB_TM // dil
            for c in range(GROUP_LANE_BLOCKS):
                y_ref[c] = y[:, c * V7X_LANES:(c + 1) * V7X_LANES]
            if dil == SPLIT_STRIDE ** 2:
                t_ref = t_refs[k * N_SPLIT_OUTS + SPLIT_OUTS.index(idx)]
                quarter = SUB_TM // SPLIT_STRIDE
                for c in range(GROUP_LANE_BLOCKS):
                    for r4 in range(SPLIT_STRIDE):
                        t_ref[c, r4 * quarter:(r4 + 1) * quarter] = y_ref[c, pl.ds(r4, quarter, stride=SPLIT_STRIDE), :]
                for r in range(dil):
                    start = (r % SPLIT_STRIDE) * quarter + r // SPLIT_STRIDE
                    for c in range(GROUP_LANE_BLOCKS):
                        o_ref[r, k * per_res:(k + 1) * per_res, c * V7X_LANES:(c + 1) * V7X_LANES] = (
                            t_ref[c, pl.ds(start, per_res, stride=SPLIT_STRIDE), :].astype(BF16))
                continue
            for r in range(dil):
                for c in range(GROUP_LANE_BLOCKS):
                    o_ref[r, k * per_res:(k + 1) * per_res, c * V7X_LANES:(c + 1) * V7X_LANES] = (
                        y_ref[c, pl.ds(r, per_res, stride=dil), :].astype(BF16))
        for idx, o_ref in enumerate(na_refs):
            y = _dot(h, w_ref[:, idx * NA_WIDTH:(idx + 1) * NA_WIDTH])
            if idx == 0:
                y = y * Q_SCALE
            o_ref[rows] = y.astype(BF16)


def _qkv(x, g, w_in):
    batch, seq, d = x.shape
    tm = QKV_TM
    tok = lambda b, i: (b, i, 0)
    res = lambda b, i: (b, 0, i, 0)
    out_specs = [pl.BlockSpec((None, tm, NA_WIDTH), tok)] * 3
    out_shape = [jax.ShapeDtypeStruct((batch, seq, NA_WIDTH), BF16)] * 3
    for _ in range(3):
        for dil in DILATIONS:
            out_specs.append(pl.BlockSpec((None, dil, tm // dil, GROUP_WIDTH), res))
            out_shape.append(jax.ShapeDtypeStruct((batch, dil, seq // dil, GROUP_WIDTH), BF16))
    blocks = 2 * _nbytes((tm, d), F32) + _nbytes((d, QKV_TOTAL), w_in.dtype) + 2 * _nbytes((tm, QKV_TOTAL), BF16)
    stage = (GROUP_LANE_BLOCKS, SUB_TM, V7X_LANES)
    n_stage = N_STAGED_OUTS * (tm // SUB_TM)
    n_split = N_SPLIT_OUTS * (tm // SUB_TM)
    scratch = (n_stage + n_split) * _nbytes(stage, F32) + _nbytes((d, QKV_TOTAL), BF16)
    temps = _nbytes((SUB_TM, d), F32)
    outs = pl.pallas_call(
        _qkv_kernel,
        grid=(batch, seq // tm),
        in_specs=[pl.BlockSpec((None, tm, d), tok), _resident((1, d)), _resident((d, QKV_TOTAL))],
        out_specs=out_specs,
        out_shape=out_shape,
        scratch_shapes=[pltpu.VMEM((d, QKV_TOTAL), BF16)] + [pltpu.VMEM(stage, F32)] * (n_stage + n_split),
        compiler_params=pltpu.CompilerParams(
            dimension_semantics=("arbitrary", "arbitrary"),
            vmem_limit_bytes=_vmem_limit(blocks, scratch, temps),
        ),
        name="qkv",
    )(x, g, w_in)
    qa, ka, va = outs[:3]
    qb, kb, vb = (outs[3 + w * NB_GROUPS:3 + (w + 1) * NB_GROUPS] for w in range(3))
    return qa, ka, va, qb, kb, vb


NA_HEADS_PER_STEP = V7X_LANES // HEAD_DIM
NA_SPAN = NA_ROWS * GRID_W
NA_VARIANTS = NA_ROWS
NA_RPB_ROWS = 2 * NA_ROWS - 1
NA_RPB_COLS = 2 * NA_COLS - 1
NA_UNROLL = 16


def _nbr_build_bias(rep_ref, tbl_ref):
    width = NA_RPB_ROWS * GRID_W
    qc = lax.broadcasted_iota(jnp.int32, (GRID_W, width), 0)
    kc = lax.broadcasted_iota(jnp.int32, (GRID_W, width), 1) % GRID_W
    diff = kc - qc
    col0 = jnp.clip(qc - NA_COLS // 2, 0, GRID_W - NA_COLS)
    col_ok = (kc >= col0) & (kc < col0 + NA_COLS)
    for h in range(NA_HEADS_PER_STEP):
        w = jnp.full((GRID_W, width), -jnp.inf, F32)
        for k in range(NA_RPB_COLS):
            w = jnp.where(diff == k - (NA_COLS - 1), rep_ref[h, k:k + 1, :], w)
        w = jnp.where(col_ok, w * LOG2E, -jnp.inf)
        for variant in range(NA_VARIANTS):
            tbl_ref[h, variant] = w[:, variant * GRID_W:variant * GRID_W + NA_SPAN]


def _nbr_kernel(q_ref, k_ref, v_ref, rep_ref, o_ref, tbl_ref, s_scr, p_scr, *, rows):
    hp = NA_HEADS_PER_STEP
    ones = jnp.ones((NA_SPAN, V7X_LANES), BF16)

    @pl.when(pl.program_id(1) == 0)
    def _():
        _nbr_build_bias(rep_ref, tbl_ref)

    lane_head = lax.broadcasted_iota(jnp.int32, (GRID_W, V7X_LANES), 1) // HEAD_DIM

    def scores(r):
        row0 = jnp.clip(r - NA_ROWS // 2, 0, rows - NA_ROWS)
        variant = row0 - r + (NA_ROWS - 1)
        q = q_ref[pl.ds(pl.multiple_of(r * GRID_W, GRID_W), GRID_W), :]
        kw = k_ref[pl.ds(pl.multiple_of(row0 * GRID_W, GRID_W), NA_SPAN), :]
        qs = jnp.concatenate([jnp.where(lane_head == h, q, jnp.zeros_like(q)) for h in range(hp)], axis=0)
        return _dot_nt(qs, kw) + tbl_ref[:, variant].reshape(hp * GRID_W, NA_SPAN)

    def probs(s):
        return (jnp.exp2(s - jnp.max(s, axis=-1, keepdims=True)).astype(BF16),)

    def finish(r, p):
        row0 = jnp.clip(r - NA_ROWS // 2, 0, rows - NA_ROWS)
        vw = v_ref[pl.ds(pl.multiple_of(row0 * GRID_W, GRID_W), NA_SPAN), :]
        o = _dot(p, jnp.concatenate([vw, ones], axis=1))
        out, den = o[:GRID_W, :V7X_LANES], o[:GRID_W, V7X_LANES:]
        for h in range(1, hp):
            rows_h = slice(h * GRID_W, (h + 1) * GRID_W)
            out = jnp.where(lane_head == h, o[rows_h, :V7X_LANES], out)
            den = jnp.where(lane_head == h, o[rows_h, V7X_LANES:], den)
        o_ref[pl.ds(pl.multiple_of(r * GRID_W, GRID_W), GRID_W), :] = (out * (1.0 / den)).astype(BF16)

    _pipelined_attention(rows // NA_UNROLL, NA_UNROLL, scores, probs, finish, s_scr, p_scr, ())


def _nbr_attention(q, k, v, rpb):
    batch, seq, width = q.shape
    rows = seq // GRID_W
    n_pairs = width // V7X_LANES
    rep = jnp.repeat(rpb.astype(F32).transpose(0, 2, 1), GRID_W, axis=-1)
    tok = lambda p, b: (b, 0, p)
    tbl_shape = (NA_HEADS_PER_STEP, NA_VARIANTS, GRID_W, NA_SPAN)
    rep_block = (NA_HEADS_PER_STEP, NA_RPB_COLS, NA_RPB_ROWS * GRID_W)
    blocks = 2 * 4 * _nbytes((seq, V7X_LANES), BF16) + 2 * _nbytes((NA_HEADS_PER_STEP, 32, 1024), F32)
    tile = (2, NA_UNROLL, NA_HEADS_PER_STEP * GRID_W, NA_SPAN)
    scratch = _nbytes(tbl_shape, F32) + _nbytes(tile, F32) + _nbytes(tile, BF16)
    temps = NA_UNROLL * _nbytes(tile[2:], F32)
    return pl.pallas_call(
        functools.partial(_nbr_kernel, rows=rows),
        grid=(n_pairs, batch),
        in_specs=[
            pl.BlockSpec((None, seq, V7X_LANES), tok),
            pl.BlockSpec((None, seq, V7X_LANES), tok),
            pl.BlockSpec((None, seq, V7X_LANES), tok),
            pl.BlockSpec(rep_block, lambda p, b: (p, 0, 0)),
        ],
        out_specs=pl.BlockSpec((None, seq, V7X_LANES), tok),
        out_shape=jax.ShapeDtypeStruct((batch, seq, width), BF16),
        scratch_shapes=[pltpu.VMEM(tbl_shape, F32), pltpu.VMEM(tile, F32), pltpu.VMEM(tile, BF16)],
        compiler_params=pltpu.CompilerParams(
            dimension_semantics=("arbitrary", "arbitrary"),
            vmem_limit_bytes=_vmem_limit(blocks, scratch, temps),
        ),
        name="nbr_attn",
    )(q, k, v, rep)


DIL_HALF = 64
DIL_TQ = 128
DIL_TK = DIL_TQ + 2 * DIL_HALF
DIL_VARIANTS = 3
DIL_STEP_ROWS = 4096
DIL_UNROLL = 8


def _dil_bias_table(group, dilation):
    heads = jnp.arange(NB_HEADS_PER_GROUP, dtype=F32) + group * NB_HEADS_PER_GROUP
    slopes = jnp.exp2(-ALIBI_MAX_EXP * (heads + 1.0) / NB_HEADS)
    offs = jnp.array([0, -DIL_HALF, -2 * DIL_HALF], jnp.int32)
    rel = offs[:, None, None] + jnp.arange(DIL_TK)[None, None, :] - jnp.arange(DIL_TQ)[None, :, None]
    dist = (dilation * jnp.abs(rel)).astype(F32)
    bias = -(slopes[None, :, None, None] * dist[:, None]) * LOG2E
    bias = jnp.where((jnp.abs(rel) <= DIL_HALF)[:, None], bias, -jnp.inf)
    return bias.reshape(DIL_VARIANTS, NB_HEADS_PER_GROUP * DIL_TQ, DIL_TK)


def _dil_kernel(q_ref, k_ref, v_ref, bias_ref, o_ref, lse_ref, s_scr, p_scr, m_scr, *, seq_len, stretch):
    nh = NB_HEADS_PER_GROUP
    hpb = V7X_LANES // HEAD_DIM
    ones = jnp.ones((DIL_TK, V7X_LANES), BF16)
    group_lane_head = lax.broadcasted_iota(jnp.int32, (DIL_TQ, GROUP_WIDTH), 1) // HEAD_DIM
    lane_head = lax.broadcasted_iota(jnp.int32, (DIL_TQ, V7X_LANES), 1) // HEAD_DIM
    base = pl.program_id(2) * stretch
    tiles_per_residue = stretch // DIL_TQ

    def window(t):
        r, j = t // tiles_per_residue, t % tiles_per_residue
        qloc = pl.multiple_of(j * DIL_TQ, DIL_TQ)
        qs = base + qloc
        ws = jnp.clip(qs - DIL_HALF, 0, seq_len - DIL_TK)
        variant = (qs - ws) // DIL_HALF
        return r, qloc, pl.multiple_of(ws, DIL_HALF), variant

    def scores(t):
        r, qloc, ws, variant = window(t)
        q = q_ref[r, pl.ds(qloc, DIL_TQ), :]
        kw = k_ref[r, pl.ds(ws, DIL_TK), :]
        qst = jnp.concatenate([jnp.where(group_lane_head == h, q, jnp.zeros_like(q)) for h in range(nh)],
                              axis=0)
        return _dot_nt(qst, kw) + bias_ref[variant]

    def probs(s):
        m = jnp.max(s, axis=-1, keepdims=True)
        return jnp.exp2(s - m).astype(BF16), jnp.broadcast_to(m, (s.shape[0], V7X_LANES))

    def finish(t, p, m):
        r, qloc, ws, _ = window(t)
        vw = v_ref[r, pl.ds(ws, DIL_TK), :]
        outs, lses = [], []
        for c in range(GROUP_LANE_BLOCKS):
            rows_c = slice(c * hpb * DIL_TQ, (c + 1) * hpb * DIL_TQ)
            v_aug = jnp.concatenate([vw[:, c * V7X_LANES:(c + 1) * V7X_LANES], ones], axis=1)
            o = _dot(p[rows_c], v_aug)
            m_c = m[rows_c]
            out_c, den_c, max_c = o[:DIL_TQ, :V7X_LANES], o[:DIL_TQ, V7X_LANES:], m_c[:DIL_TQ]
            for h in range(1, hpb):
                rows_h = slice(h * DIL_TQ, (h + 1) * DIL_TQ)
                out_c = jnp.where(lane_head == h, o[rows_h, :V7X_LANES], out_c)
                den_c = jnp.where(lane_head == h, o[rows_h, V7X_LANES:], den_c)
                max_c = jnp.where(lane_head == h, m_c[rows_h], max_c)
            outs.append(out_c * (1.0 / den_c))
            lses.append(max_c * LN2 + jnp.log(den_c))
        o_ref[r, pl.ds(qloc, DIL_TQ), :] = jnp.concatenate(outs, axis=1).astype(BF16)
        lse_ref[r, pl.ds(qloc, DIL_TQ), :] = jnp.concatenate(lses, axis=1)

    n_tiles = q_ref.shape[0] * tiles_per_residue
    _pipelined_attention(n_tiles // DIL_UNROLL, DIL_UNROLL, scores, probs, finish, s_scr, p_scr, (m_scr,))


def _dil_attention(q, k, v, group):
    batch, dilation, seq_len, width = q.shape
    stretch = min(seq_len, DIL_STEP_ROWS)
    res_blk = DIL_STEP_ROWS // stretch
    assert dilation % res_blk == 0 and seq_len % stretch == 0
    qmap = lambda b, r, s: (b, r, s, 0)
    kvmap = lambda b, r, s: (b, r, 0, 0)
    bias = _dil_bias_table(group, dilation)
    tile = (2, DIL_UNROLL, NB_HEADS_PER_GROUP * DIL_TQ, DIL_TK)
    stat = tile[:3] + (V7X_LANES,)
    kv_buffers = 2
    kv_spec = pl.BlockSpec((None, res_blk, seq_len, width), kvmap)
    blocks = (2 * (2 * _nbytes((DIL_STEP_ROWS, width), BF16) + _nbytes((DIL_STEP_ROWS, width), F32))
              + kv_buffers * 2 * _nbytes((res_blk * seq_len, width), BF16) + _nbytes(bias.shape, F32))
    scratch = _nbytes(tile, F32) + _nbytes(tile, BF16) + _nbytes(stat, F32)
    temps = DIL_UNROLL * _nbytes(tile[2:], F32)
    return pl.pallas_call(
        functools.partial(_dil_kernel, seq_len=seq_len, stretch=stretch),
        grid=(batch, dilation // res_blk, seq_len // stretch),
        in_specs=[
            pl.BlockSpec((None, res_blk, stretch, width), qmap),
            kv_spec,
            kv_spec,
            _resident(bias.shape),
        ],
        out_specs=[pl.BlockSpec((None, res_blk, stretch, width), qmap)] * 2,
        out_shape=[jax.ShapeDtypeStruct(q.shape, BF16), jax.ShapeDtypeStruct(q.shape, F32)],
        scratch_shapes=[pltpu.VMEM(tile, F32), pltpu.VMEM(tile, BF16), pltpu.VMEM(stat, F32)],
        compiler_params=pltpu.CompilerParams(
            dimension_semantics=("arbitrary", "arbitrary", "arbitrary"),
            vmem_limit_bytes=_vmem_limit(blocks, scratch, temps),
        ),
        name=f"dil_attn_g{group}",
    )(q, k, v, bias)


MERGE_TM = 1024
MERGE_CN = 256


def _to_token_order(src_ref, scr_ref, tmp_ref, dil, k):
    per_res = SUB_TM // dil
    res_rows = slice(k * per_res, (k + 1) * per_res)
    if dil == 1:
        return src_ref[0, res_rows].astype(F32)
    rows = slice(k * SUB_TM, (k + 1) * SUB_TM)
    if dil == SPLIT_STRIDE ** 2:
        quarter = SUB_TM // SPLIT_STRIDE
        for r in range(dil):
            start = k * SUB_TM + (r % SPLIT_STRIDE) * quarter + r // SPLIT_STRIDE
            for c in range(GROUP_LANE_BLOCKS):
                tmp_ref[c, pl.ds(start, per_res, stride=SPLIT_STRIDE), :] = (
                    src_ref[r, res_rows, c * V7X_LANES:(c + 1) * V7X_LANES].astype(F32))
        for c in range(GROUP_LANE_BLOCKS):
            for r4 in range(SPLIT_STRIDE):
                lo = k * SUB_TM + r4 * quarter
                scr_ref[c, pl.ds(k * SUB_TM + r4, quarter, stride=SPLIT_STRIDE), :] = tmp_ref[c, lo:lo + quarter]
        return jnp.concatenate([scr_ref[c, rows] for c in range(GROUP_LANE_BLOCKS)], axis=-1)
    for r in range(dil):
        for c in range(GROUP_LANE_BLOCKS):
            scr_ref[c, pl.ds(k * SUB_TM + r, per_res, stride=dil), :] = (
                src_ref[r, res_rows, c * V7X_LANES:(c + 1) * V7X_LANES].astype(F32))
    return jnp.concatenate([scr_ref[c, rows] for c in range(GROUP_LANE_BLOCKS)], axis=-1)


def _merge_kernel(x_ref, pre_g_ref, wgate_ref, oa_ref, wa_ref, *refs):
    ng = NB_GROUPS
    o_refs, lse_refs = refs[:ng], refs[ng:2 * ng]
    wb_ref, wout_ref, post_g_ref, out_ref, h_ref, ob_ref, mg_ref = refs[2 * ng:2 * ng + 7]
    scr_refs = iter(refs[2 * ng + 7:])
    o_scr = [next(scr_refs) if dil > 1 else None for dil in DILATIONS]
    lse_scr = [next(scr_refs) if dil > 1 else None for dil in DILATIONS]
    o_tmp = [next(scr_refs) if dil == SPLIT_STRIDE ** 2 else None for dil in DILATIONS]
    lse_tmp = [next(scr_refs) if dil == SPLIT_STRIDE ** 2 else None for dil in DILATIONS]
    subs = _sub_tiles(x_ref.shape[0])
    for k, rows in enumerate(subs):
        h_ref[rows] = _rmsnorm_f32(x_ref[rows], pre_g_ref[...]).astype(BF16)
        os_ = [_to_token_order(r, s, t, d, k) for r, s, t, d in zip(o_refs, o_scr, o_tmp, DILATIONS)]
        lses = [_to_token_order(r, s, t, d, k) for r, s, t, d in zip(lse_refs, lse_scr, lse_tmp, DILATIONS)]
        mx = jnp.maximum(jnp.maximum(lses[0], lses[1]), lses[2])
        es = [jnp.exp(l - mx) for l in lses]
        inv = 1.0 / (es[0] + es[1] + es[2])
        for g in range(ng):
            ob_ref[rows, g * GROUP_WIDTH:(g + 1) * GROUP_WIDTH] = (os_[g] * (es[g] * inv)).astype(BF16)
    for rows in subs:
        for n in range(D_MODEL // MERGE_CN):
            cols = slice(n * MERGE_CN, (n + 1) * MERGE_CN)
            cols_b = slice(D_MODEL + n * MERGE_CN, D_MODEL + (n + 1) * MERGE_CN)
            h = h_ref[rows]
            ga = _dot(h, wgate_ref[:, cols].astype(BF16))
            gb = _dot(h, wgate_ref[:, cols_b].astype(BF16))
            ya = _dot(oa_ref[rows], wa_ref[:, cols])
            yb = _dot(ob_ref[rows], wb_ref[:, cols])
            mg_ref[rows, cols] = (jax.nn.sigmoid(ga) * ya + jax.nn.sigmoid(gb) * yb).astype(BF16)
    for rows in subs:
        out_ref[rows] = _dot(mg_ref[rows], wout_ref[...])
    for rows in subs:
        out_ref[rows] = x_ref[rows] + _rmsnorm_f32(out_ref[rows], post_g_ref[...])


def _merge(x, pre_g, wgate, oa, wa, obs, lses, wb, wout, post_g):
    batch, seq, d = x.shape
    tm = MERGE_TM
    tok = lambda b, i: (b, i, 0)
    res = lambda b, i: (b, 0, i, 0)
    tokspec = lambda w: pl.BlockSpec((None, tm, w), tok)
    resspecs = [pl.BlockSpec((None, dil, tm // dil, GROUP_WIDTH), res) for dil in DILATIONS]
    blocks = (2 * 2 * _nbytes((tm, d), F32) + 2 * _nbytes((tm, NA_WIDTH), BF16)
              + 2 * 3 * (_nbytes((tm, GROUP_WIDTH), BF16) + _nbytes((tm, GROUP_WIDTH), F32))
              + _nbytes((NA_WIDTH, d), wa.dtype) + _nbytes((d + NB_WIDTH, d), wb.dtype) + _nbytes((d, 2 * d), F32))
    work = [pltpu.VMEM((tm, d), BF16), pltpu.VMEM((tm, NB_WIDTH), BF16), pltpu.VMEM((tm, d), BF16)]
    n_interleave = 2 * sum(dil > 1 for dil in DILATIONS) + 2 * sum(dil == SPLIT_STRIDE ** 2 for dil in DILATIONS)
    scratch = (n_interleave * _nbytes((tm, GROUP_WIDTH), F32) + 2 * _nbytes((tm, d), BF16)
               + _nbytes((tm, NB_WIDTH), BF16))
    temps = 2 * _nbytes((SUB_TM, d), F32)
    return pl.pallas_call(
        _merge_kernel,
        grid=(batch, seq // tm),
        in_specs=[
            tokspec(d), _resident((1, d)),
            pl.BlockSpec((pl.Element(d), pl.Element(2 * d)), lambda *_: (0, QKV_TOTAL),
                         pipeline_mode=pl.Buffered(1)),
            tokspec(NA_WIDTH), _resident((NA_WIDTH, d)),
            *resspecs, *resspecs,
            _resident((NB_WIDTH, d)), _resident((d, d)), _resident((1, d)),
        ],
        out_specs=tokspec(d),
        out_shape=jax.ShapeDtypeStruct((batch, seq, d), F32),
        scratch_shapes=work + [pltpu.VMEM((GROUP_LANE_BLOCKS, tm, V7X_LANES), F32)] * n_interleave,
        compiler_params=pltpu.CompilerParams(
            dimension_semantics=("arbitrary", "arbitrary"),
            vmem_limit_bytes=_vmem_limit(blocks, scratch, temps),
        ),
        name="merge",
    )(x, pre_g, wgate, oa, wa, *obs, *lses, wb, wout, post_g)


def kernel(x, ffn1_pre_g, ffn1_w_gate, ffn1_w_up, ffn1_w_down, ffn1_post_g, mix_pre_g, w_in, na_rpb, w_branch_a, w_branch_b, w_out, mix_post_g, ffn2_pre_g, ffn2_w_gate, ffn2_w_up, ffn2_w_down, ffn2_post_g):
    batch, seq, d = x.shape
    depth = ffn1_pre_g.shape[0]
    for window, dilation in DIL_PAIRS:
        assert window // (2 * dilation) == DIL_HALF
    for l in range(depth):
        x = _ffn(x.reshape(batch * seq, d), ffn1_pre_g[l][None], ffn1_w_gate[l], ffn1_w_up[l],
                 ffn1_w_down[l], ffn1_post_g[l][None])
        x = x.reshape(batch, seq, d)
        qa, ka, va, qb, kb, vb = _qkv(x, mix_pre_g[l][None], w_in[l])
        oa = _nbr_attention(qa, ka, va, na_rpb[l])
        obs, lses = [], []
        for g in range(NB_GROUPS):
            o_g, lse_g = _dil_attention(qb[g], kb[g], vb[g], g)
            obs.append(o_g)
            lses.append(lse_g)
        x = _merge(x, mix_pre_g[l][None], w_in[l],
                   oa, w_branch_a[l].astype(BF16), obs, lses, w_branch_b[l].astype(BF16),
                   w_out[l].astype(BF16), mix_post_g[l][None])
        x = _ffn(x.reshape(batch * seq, d), ffn2_pre_g[l][None], ffn2_w_gate[l], ffn2_w_up[l],
                 ffn2_w_down[l], ffn2_post_g[l][None])
        x = x.reshape(batch, seq, d)
    return x
```

```python
import functools
import math

import jax
import jax.numpy as jnp
from jax import lax
from jax.experimental import pallas as pl
from jax.experimental.pallas import tpu as pltpu

D_MODEL = 1024
HEAD_DIM = 64
NA_HEADS = 8
NA_WIDTH = NA_HEADS * HEAD_DIM
NA_ROWS = 8
NA_COLS = 16
GRID_W = 64
DIL_PAIRS = ((128, 1), (512, 4), (2048, 16))
DILATIONS = tuple(d for _, d in DIL_PAIRS)
NB_GROUPS = len(DIL_PAIRS)
NB_HEADS_PER_GROUP = 4
NB_HEADS = NB_GROUPS * NB_HEADS_PER_GROUP
NB_WIDTH = NB_HEADS * HEAD_DIM
GROUP_WIDTH = NB_HEADS_PER_GROUP * HEAD_DIM
ALIBI_MAX_EXP = 8.0
D_FF = 2816
NORM_EPS = 1e-6
ATTN_SCALE = HEAD_DIM ** -0.5
LOG2E = math.log2(math.e)
LN2 = math.log(2.0)
Q_SCALE = ATTN_SCALE * LOG2E

V7X_LANES = 128
GROUP_LANE_BLOCKS = GROUP_WIDTH // V7X_LANES
V7X_VMEM_BYTES = 64 * 1024 * 1024
V7X_VMEM_RESERVE = 6 * 1024 * 1024

BF16 = jnp.bfloat16
F32 = jnp.float32


def _vmem_limit(block_bytes, scratch_bytes, temp_bytes):
    need = block_bytes + scratch_bytes + temp_bytes
    budget = V7X_VMEM_BYTES - V7X_VMEM_RESERVE
    assert need <= budget, need
    return budget


def _nbytes(shape, dtype):
    return math.prod(shape) * jnp.dtype(dtype).itemsize


def _resident(shape):
    nd = len(shape)
    return pl.BlockSpec(shape, lambda *_: (0,) * nd, pipeline_mode=pl.Buffered(1))


def _rmsnorm_f32(x, g):
    return x * lax.rsqrt(jnp.mean(x * x, axis=-1, keepdims=True) + NORM_EPS) * g


def _dot(a, b):
    return jnp.dot(a, b, preferred_element_type=F32)


def _dot_nt(a, b):
    return lax.dot_general(a, b, (((1,), (1,)), ((), ())), preferred_element_type=F32)


def _pipelined_attention(n_groups, unroll, scores, probs, finish, s_scr, p_scr, stat_scrs):
    assert n_groups % 2 == 0 and n_groups >= 2
    tiles = lambda i: [i * unroll + u for u in range(unroll)]

    def stage_scores(i, slot):
        for u, t in enumerate(tiles(i)):
            s_scr[slot, u] = scores(t)

    def stage_probs(slot):
        for u in range(unroll):
            p, *stats = probs(s_scr[slot, u])
            p_scr[slot, u] = p
            for ref, stat in zip(stat_scrs, stats):
                ref[slot, u] = stat

    def stage_finish(i, slot):
        for u, t in enumerate(tiles(i)):
            finish(t, p_scr[slot, u], *[ref[slot, u] for ref in stat_scrs])

    def step(i, parity):
        stage_scores(i + 1, 1 - parity)
        stage_finish(i - 1, 1 - parity)
        stage_probs(parity)

    stage_scores(0, 0)
    stage_scores(1, 1)
    stage_probs(0)

    def body(i, carry):
        @pl.when(i % 2 == 1)
        def _():
            step(i, 1)

        @pl.when(i % 2 == 0)
        def _():
            step(i, 0)

        return carry

    lax.fori_loop(1, n_groups - 1, body, 0)
    stage_finish(n_groups - 2, 0)
    stage_probs(1)
    stage_finish(n_groups - 1, 1)


FFN_TM = 512
SUB_TM = 512
FFN_CK = 256


def _sub_tiles(tm):
    return [slice(k * SUB_TM, (k + 1) * SUB_TM) for k in range(tm // SUB_TM)]


FFN_CHUNKS = [slice(c * FFN_CK, (c + 1) * FFN_CK) for c in range(D_FF // FFN_CK)]


def _ffn_weight_copies(wg_hbm, wu_hbm, wd_hbm, wg_ref, wu_ref, wd_ref, sem):
    gate_up = [(pltpu.make_async_copy(wg_hbm.at[:, cols], wg_ref.at[:, cols], sem.at[0, c]),
                pltpu.make_async_copy(wu_hbm.at[:, cols], wu_ref.at[:, cols], sem.at[1, c]))
               for c, cols in enumerate(FFN_CHUNKS)]
    down = [pltpu.make_async_copy(wd_hbm.at[cols, :], wd_ref.at[cols, :], sem.at[2, c])
            for c, cols in enumerate(FFN_CHUNKS)]
    return gate_up, down


def _ffn_body(x_ref, pre_g_ref, wg_ref, wu_ref, wd_ref, post_g_ref, o_ref, h_ref, a_ref, wait_gate_up, wait_down):
    h_ref[...] = _rmsnorm_f32(x_ref[...], pre_g_ref[...]).astype(BF16)
    for c, cols in enumerate(FFN_CHUNKS):
        wait_gate_up(c)
        h = h_ref[...]
        g = _dot(h, wg_ref[:, cols].astype(BF16))
        u = _dot(h, wu_ref[:, cols].astype(BF16))
        a_ref[:, cols] = (g * jax.nn.sigmoid(g) * u).astype(BF16)
    f = None
    for c, cols in enumerate(FFN_CHUNKS):
        wait_down(c)
        part = _dot(a_ref[:, cols], wd_ref[cols, :].astype(BF16))
        f = part if f is None else f + part
    o_ref[...] = x_ref[...] + 0.5 * _rmsnorm_f32(f, post_g_ref[...])


def _ffn_kernel(x_ref, pre_g_ref, wg_hbm, wu_hbm, wd_hbm, post_g_ref, o_ref, h_ref, a_ref,
                wg_ref, wu_ref, wd_ref, sem):
    refs = (x_ref, pre_g_ref, wg_ref, wu_ref, wd_ref, post_g_ref, o_ref, h_ref, a_ref)
    first = pl.program_id(0) == 0

    @pl.when(jnp.logical_not(first))
    def _():
        _ffn_body(*refs, lambda c: None, lambda c: None)

    @pl.when(first)
    def _():
        gate_up, down = _ffn_weight_copies(wg_hbm, wu_hbm, wd_hbm, wg_ref, wu_ref, wd_ref, sem)
        for cg, cu in gate_up:
            cg.start()
            cu.start()
        for cd in down:
            cd.start()

        def wait_gate_up(c):
            gate_up[c][0].wait()
            gate_up[c][1].wait()

        _ffn_body(*refs, wait_gate_up, lambda c: down[c].wait())


def _ffn(x, pre_g, wg, wu, wd, post_g):
    n, d = x.shape
    tm = FFN_TM
    row = lambda i: (i, 0)
    n_chunks = len(FFN_CHUNKS)
    blocks = 2 * 2 * _nbytes((tm, d), F32) + 2 * _nbytes((1, d), F32)
    scratch = _nbytes((tm, d), BF16) + _nbytes((tm, D_FF), BF16) + 3 * _nbytes((d, D_FF), wg.dtype)
    temps = 4 * _nbytes((tm, d), F32)
    hbm = pl.BlockSpec(memory_space=pl.ANY)
    return pl.pallas_call(
        _ffn_kernel,
        grid=(n // tm,),
        in_specs=[pl.BlockSpec((tm, d), row), _resident((1, d)), hbm, hbm, hbm, _resident((1, d))],
        out_specs=pl.BlockSpec((tm, d), row),
        out_shape=jax.ShapeDtypeStruct((n, d), F32),
        scratch_shapes=[
            pltpu.VMEM((tm, d), BF16), pltpu.VMEM((tm, D_FF), BF16),
            pltpu.VMEM((d, D_FF), wg.dtype), pltpu.VMEM((d, D_FF), wu.dtype), pltpu.VMEM((D_FF, d), wd.dtype),
            pltpu.SemaphoreType.DMA((3, n_chunks)),
        ],
        compiler_params=pltpu.CompilerParams(
            dimension_semantics=("arbitrary",),
            vmem_limit_bytes=_vmem_limit(blocks, scratch, temps),
        ),
        name="ffn",
    )(x, pre_g, wg, wu, wd, post_g)


QKV_TM = 1024
QKV_TOTAL = 3 * NA_WIDTH + 3 * NB_WIDTH
N_DIL_OUTS = 3 * NB_GROUPS
STAGED_OUTS = [i for i in range(N_DIL_OUTS) if DILATIONS[i % NB_GROUPS] > 1]
N_STAGED_OUTS = len(STAGED_OUTS)
SPLIT_STRIDE = 4
SPLIT_OUTS = [i for i in STAGED_OUTS if DILATIONS[i % NB_GROUPS] == SPLIT_STRIDE ** 2]
N_SPLIT_OUTS = len(SPLIT_OUTS)


def _qkv_kernel(x_ref, g_ref, w32_ref, *refs):
    na_refs, dil_refs = refs[:3], refs[3:3 + N_DIL_OUTS]
    w_ref = refs[3 + N_DIL_OUTS]
    n_stage = N_STAGED_OUTS * (x_ref.shape[0] // SUB_TM)
    y_refs, t_refs = refs[4 + N_DIL_OUTS:4 + N_DIL_OUTS + n_stage], refs[4 + N_DIL_OUTS + n_stage:]

    @pl.when((pl.program_id(0) == 0) & (pl.program_id(1) == 0))
    def _():
        for c in range(QKV_TOTAL // GROUP_WIDTH):
            cols = slice(c * GROUP_WIDTH, (c + 1) * GROUP_WIDTH)
            w_ref[:, cols] = w32_ref[:, cols].astype(BF16)

    for k, rows in enumerate(_sub_tiles(x_ref.shape[0])):
        h = _rmsnorm_f32(x_ref[rows], g_ref[...]).astype(BF16)
        for idx, o_ref in enumerate(dil_refs):
            which, g = divmod(idx, NB_GROUPS)
            dil = DILATIONS[g]
            off = 3 * NA_WIDTH + which * NB_WIDTH + g * GROUP_WIDTH
            y = _dot(h, w_ref[:, off:off + GROUP_WIDTH])
            if which == 0:
                y = y * Q_SCALE
            if dil == 1:
                o_ref[0, rows] = y.astype(BF16)
                continue
            y_ref = y_refs[k * N_STAGED_OUTS + STAGED_OUTS.index(idx)]
            per_res = SUB_TM // dil
            for c in range(GROUP_LANE_BLOCKS):
                y_ref[c] = y[:, c * V7X_LANES:(c + 1) * V7X_LANES]
            if dil == SPLIT_STRIDE ** 2:
                t_ref = t_refs[k * N_SPLIT_OUTS + SPLIT_OUTS.index(idx)]
                quarter = SUB_TM // SPLIT_STRIDE
                for c in range(GROUP_LANE_BLOCKS):
                    for r4 in range(SPLIT_STRIDE):
                        t_ref[c, r4 * quarter:(r4 + 1) * quarter] = y_ref[c, pl.ds(r4, quarter, stride=SPLIT_STRIDE), :]
                for r in range(dil):
                    start = (r % SPLIT_STRIDE) * quarter + r // SPLIT_STRIDE
                    for c in range(GROUP_LANE_BLOCKS):
                        o_ref[r, k * per_res:(k + 1) * per_res, c * V7X_LANES:(c + 1) * V7X_LANES] = (
                            t_ref[c, pl.ds(start, per_res, stride=SPLIT_STRIDE), :].astype(BF16))
                continue
            for r in range(dil):
                for c in range(GROUP_LANE_BLOCKS):
                    o_ref[r, k * per_res:(k + 1) * per_res, c * V7X_LANES:(c + 1) * V7X_LANES] = (
                        y_ref[c, pl.ds(r, per_res, stride=dil), :].astype(BF16))
        for idx, o_ref in enumerate(na_refs):
            y = _dot(h, w_ref[:, idx * NA_WIDTH:(idx + 1) * NA_WIDTH])
            if idx == 0:
                y = y * Q_SCALE
            o_ref[rows] = y.astype(BF16)


def _qkv(x, g, w_in):
    batch, seq, d = x.shape
    tm = QKV_TM
    tok = lambda b, i: (b, i, 0)
    res = lambda b, i: (b, 0, i, 0)
    out_specs = [pl.BlockSpec((None, tm, NA_WIDTH), tok)] * 3
    out_shape = [jax.ShapeDtypeStruct((batch, seq, NA_WIDTH), BF16)] * 3
    for _ in range(3):
        for dil in DILATIONS:
            out_specs.append(pl.BlockSpec((None, dil, tm // dil, GROUP_WIDTH), res))
            out_shape.append(jax.ShapeDtypeStruct((batch, dil, seq // dil, GROUP_WIDTH), BF16))
    blocks = 2 * _nbytes((tm, d), F32) + _nbytes((d, QKV_TOTAL), w_in.dtype) + 2 * _nbytes((tm, QKV_TOTAL), BF16)
    stage = (GROUP_LANE_BLOCKS, SUB_TM, V7X_LANES)
    n_stage = N_STAGED_OUTS * (tm // SUB_TM)
    n_split = N_SPLIT_OUTS * (tm // SUB_TM)
    scratch = (n_stage + n_split) * _nbytes(stage, F32) + _nbytes((d, QKV_TOTAL), BF16)
    temps = _nbytes((SUB_TM, d), F32)
    outs = pl.pallas_call(
        _qkv_kernel,
        grid=(batch, seq // tm),
        in_specs=[pl.BlockSpec((None, tm, d), tok), _resident((1, d)), _resident((d, QKV_TOTAL))],
        out_specs=out_specs,
        out_shape=out_shape,
        scratch_shapes=[pltpu.VMEM((d, QKV_TOTAL), BF16)] + [pltpu.VMEM(stage, F32)] * (n_stage + n_split),
        compiler_params=pltpu.CompilerParams(
            dimension_semantics=("arbitrary", "arbitrary"),
            vmem_limit_bytes=_vmem_limit(blocks, scratch, temps),
        ),
        name="qkv",
    )(x, g, w_in)
    qa, ka, va = outs[:3]
    qb, kb, vb = (outs[3 + w * NB_GROUPS:3 + (w + 1) * NB_GROUPS] for w in range(3))
    return qa, ka, va, qb, kb, vb


NA_HEADS_PER_STEP = V7X_LANES // HEAD_DIM
NA_SPAN = NA_ROWS * GRID_W
NA_VARIANTS = NA_ROWS
NA_RPB_ROWS = 2 * NA_ROWS - 1
NA_RPB_COLS = 2 * NA_COLS - 1
NA_UNROLL = 16


def _nbr_build_bias(rep_ref, tbl_ref):
    width = NA_RPB_ROWS * GRID_W
    qc = lax.broadcasted_iota(jnp.int32, (GRID_W, width), 0)
    kc = lax.broadcasted_iota(jnp.int32, (GRID_W, width), 1) % GRID_W
    diff = kc - qc
    col0 = jnp.clip(qc - NA_COLS // 2, 0, GRID_W - NA_COLS)
    col_ok = (kc >= col0) & (kc < col0 + NA_COLS)
    for h in range(NA_HEADS_PER_STEP):
        w = jnp.full((GRID_W, width), -jnp.inf, F32)
        for k in range(NA_RPB_COLS):
            w = jnp.where(diff == k - (NA_COLS - 1), rep_ref[h, k:k + 1, :], w)
        w = jnp.where(col_ok, w * LOG2E, -jnp.inf)
        for variant in range(NA_VARIANTS):
            tbl_ref[h, variant] = w[:, variant * GRID_W:variant * GRID_W + NA_SPAN]


def _nbr_kernel(q_ref, k_ref, v_ref, rep_ref, o_ref, tbl_ref, s_scr, p_scr, *, rows):
    hp = NA_HEADS_PER_STEP
    ones = jnp.ones((NA_SPAN, V7X_LANES), BF16)

    @pl.when(pl.program_id(1) == 0)
    def _():
        _nbr_build_bias(rep_ref, tbl_ref)

    lane_head = lax.broadcasted_iota(jnp.int32, (GRID_W, V7X_LANES), 1) // HEAD_DIM

    def scores(r):
        row0 = jnp.clip(r - NA_ROWS // 2, 0, rows - NA_ROWS)
        variant = row0 - r + (NA_ROWS - 1)
        q = q_ref[pl.ds(pl.multiple_of(r * GRID_W, GRID_W), GRID_W), :]
        kw = k_ref[pl.ds(pl.multiple_of(row0 * GRID_W, GRID_W), NA_SPAN), :]
        qs = jnp.concatenate([jnp.where(lane_head == h, q, jnp.zeros_like(q)) for h in range(hp)], axis=0)
        return _dot_nt(qs, kw) + tbl_ref[:, variant].reshape(hp * GRID_W, NA_SPAN)

    def probs(s):
        return (jnp.exp2(s - jnp.max(s, axis=-1, keepdims=True)).astype(BF16),)

    def finish(r, p):
        row0 = jnp.clip(r - NA_ROWS // 2, 0, rows - NA_ROWS)
        vw = v_ref[pl.ds(pl.multiple_of(row0 * GRID_W, GRID_W), NA_SPAN), :]
        o = _dot(p, jnp.concatenate([vw, ones], axis=1))
        out, den = o[:GRID_W, :V7X_LANES], o[:GRID_W, V7X_LANES:]
        for h in range(1, hp):
            rows_h = slice(h * GRID_W, (h + 1) * GRID_W)
            out = jnp.where(lane_head == h, o[rows_h, :V7X_LANES], out)
            den = jnp.where(lane_head == h, o[rows_h, V7X_LANES:], den)
        o_ref[pl.ds(pl.multiple_of(r * GRID_W, GRID_W), GRID_W), :] = (out * (1.0 / den)).astype(BF16)

    _pipelined_attention(rows // NA_UNROLL, NA_UNROLL, scores, probs, finish, s_scr, p_scr, ())


def _nbr_attention(q, k, v, rpb):
    batch, seq, width = q.shape
    rows = seq // GRID_W
    n_pairs = width // V7X_LANES
    rep = jnp.repeat(rpb.astype(F32).transpose(0, 2, 1), GRID_W, axis=-1)
    tok = lambda p, b: (b, 0, p)
    tbl_shape = (NA_HEADS_PER_STEP, NA_VARIANTS, GRID_W, NA_SPAN)
    rep_block = (NA_HEADS_PER_STEP, NA_RPB_COLS, NA_RPB_ROWS * GRID_W)
    blocks = 2 * 4 * _nbytes((seq, V7X_LANES), BF16) + 2 * _nbytes((NA_HEADS_PER_STEP, 32, 1024), F32)
    tile = (2, NA_UNROLL, NA_HEADS_PER_STEP * GRID_W, NA_SPAN)
    scratch = _nbytes(tbl_shape, F32) + _nbytes(tile, F32) + _nbytes(tile, BF16)
    temps = NA_UNROLL * _nbytes(tile[2:], F32)
    return pl.pallas_call(
        functools.partial(_nbr_kernel, rows=rows),
        grid=(n_pairs, batch),
        in_specs=[
            pl.BlockSpec((None, seq, V7X_LANES), tok),
            pl.BlockSpec((None, seq, V7X_LANES), tok),
            pl.BlockSpec((None, seq, V7X_LANES), tok),
            pl.BlockSpec(rep_block, lambda p, b: (p, 0, 0)),
        ],
        out_specs=pl.BlockSpec((None, seq, V7X_LANES), tok),
        out_shape=jax.ShapeDtypeStruct((batch, seq, width), BF16),
        scratch_shapes=[pltpu.VMEM(tbl_shape, F32), pltpu.VMEM(tile, F32), pltpu.VMEM(tile, BF16)],
        compiler_params=pltpu.CompilerParams(
            dimension_semantics=("arbitrary", "arbitrary"),
            vmem_limit_bytes=_vmem_limit(blocks, scratch, temps),
        ),
        name="nbr_attn",
    )(q, k, v, rep)


DIL_HALF = 64
DIL_TQ = 128
DIL_TK = DIL_TQ + 2 * DIL_HALF
DIL_VARIANTS = 3
DIL_STEP_ROWS = 4096
DIL_UNROLL = 8


def _dil_bias_table(group, dilation):
    heads = jnp.arange(NB_HEADS_PER_GROUP, dtype=F32) + group * NB_HEADS_PER_GROUP
    slopes = jnp.exp2(-ALIBI_MAX_EXP * (heads + 1.0) / NB_HEADS)
    offs = jnp.array([0, -DIL_HALF, -2 * DIL_HALF], jnp.int32)
    rel = offs[:, None, None] + jnp.arange(DIL_TK)[None, None, :] - jnp.arange(DIL_TQ)[None, :, None]
    dist = (dilation * jnp.abs(rel)).astype(F32)
    bias = -(slopes[None, :, None, None] * dist[:, None]) * LOG2E
    bias = jnp.where((jnp.abs(rel) <= DIL_HALF)[:, None], bias, -jnp.inf)
    return bias.reshape(DIL_VARIANTS, NB_HEADS_PER_GROUP * DIL_TQ, DIL_TK)


def _dil_kernel(q_ref, k_ref, v_ref, bias_ref, o_ref, lse_ref, s_scr, p_scr, m_scr, *, seq_len, stretch):
    nh = NB_HEADS_PER_GROUP
    hpb = V7X_LANES // HEAD_DIM
    ones = jnp.ones((DIL_TK, V7X_LANES), BF16)
    group_lane_head = lax.broadcasted_iota(jnp.int32, (DIL_TQ, GROUP_WIDTH), 1) // HEAD_DIM
    lane_head = lax.broadcasted_iota(jnp.int32, (DIL_TQ, V7X_LANES), 1) // HEAD_DIM
    base = pl.program_id(2) * stretch
    tiles_per_residue = stretch // DIL_TQ

    def window(t):
        r, j = t // tiles_per_residue, t % tiles_per_residue
        qloc = pl.multiple_of(j * DIL_TQ, DIL_TQ)
        qs = base + qloc
        ws = jnp.clip(qs - DIL_HALF, 0, seq_len - DIL_TK)
        variant = (qs - ws) // DIL_HALF
        return r, qloc, pl.multiple_of(ws, DIL_HALF), variant

    def scores(t):
        r, qloc, ws, variant = window(t)
        q = q_ref[r, pl.ds(qloc, DIL_TQ), :]
        kw = k_ref[r, pl.ds(ws, DIL_TK), :]
        qst = jnp.concatenate([jnp.where(group_lane_head == h, q, jnp.zeros_like(q)) for h in range(nh)],
                              axis=0)
        return _dot_nt(qst, kw) + bias_ref[variant]

    def probs(s):
        m = jnp.max(s, axis=-1, keepdims=True)
        return jnp.exp2(s - m).astype(BF16), jnp.broadcast_to(m, (s.shape[0], V7X_LANES))

    def finish(t, p, m):
        r, qloc, ws, _ = window(t)
        vw = v_ref[r, pl.ds(ws, DIL_TK), :]
        outs, lses = [], []
        for c in range(GROUP_LANE_BLOCKS):
            rows_c = slice(c * hpb * DIL_TQ, (c + 1) * hpb * DIL_TQ)
            v_aug = jnp.concatenate([vw[:, c * V7X_LANES:(c + 1) * V7X_LANES], ones], axis=1)
            o = _dot(p[rows_c], v_aug)
            m_c = m[rows_c]
            out_c, den_c, max_c = o[:DIL_TQ, :V7X_LANES], o[:DIL_TQ, V7X_LANES:], m_c[:DIL_TQ]
            for h in range(1, hpb):
                rows_h = slice(h * DIL_TQ, (h + 1) * DIL_TQ)
                out_c = jnp.where(lane_head == h, o[rows_h, :V7X_LANES], out_c)
                den_c = jnp.where(lane_head == h, o[rows_h, V7X_LANES:], den_c)
                max_c = jnp.where(lane_head == h, m_c[rows_h], max_c)
            outs.append(out_c * (1.0 / den_c))
            lses.append(max_c * LN2 + jnp.log(den_c))
        o_ref[r, pl.ds(qloc, DIL_TQ), :] = jnp.concatenate(outs, axis=1).astype(BF16)
        lse_ref[r, pl.ds(qloc, DIL_TQ), :] = jnp.concatenate(lses, axis=1)

    n_tiles = q_ref.shape[0] * tiles_per_residue
    _pipelined_attention(n_tiles // DIL_UNROLL, DIL_UNROLL, scores, probs, finish, s_scr, p_scr, (m_scr,))


def _dil_attention(q, k, v, group):
    batch, dilation, seq_len, width = q.shape
    stretch = min(seq_len, DIL_STEP_ROWS)
    res_blk = DIL_STEP_ROWS // stretch
    assert dilation % res_blk == 0 and seq_len % stretch == 0
    qmap = lambda b, r, s: (b, r, s, 0)
    kvmap = lambda b, r, s: (b, r, 0, 0)
    bias = _dil_bias_table(group, dilation)
    tile = (2, DIL_UNROLL, NB_HEADS_PER_GROUP * DIL_TQ, DIL_TK)
    stat = tile[:3] + (V7X_LANES,)
    kv_buffers = 2
    kv_spec = pl.BlockSpec((None, res_blk, seq_len, width), kvmap)
    blocks = (2 * (2 * _nbytes((DIL_STEP_ROWS, width), BF16) + _nbytes((DIL_STEP_ROWS, width), F32))
              + kv_buffers * 2 * _nbytes((res_blk * seq_len, width), BF16) + _nbytes(bias.shape, F32))
    scratch = _nbytes(tile, F32) + _nbytes(tile, BF16) + _nbytes(stat, F32)
    temps = DIL_UNROLL * _nbytes(tile[2:], F32)
    return pl.pallas_call(
        functools.partial(_dil_kernel, seq_len=seq_len, stretch=stretch),
        grid=(batch, dilation // res_blk, seq_len // stretch),
        in_specs=[
            pl.BlockSpec((None, res_blk, stretch, width), qmap),
            kv_spec,
            kv_spec,
            _resident(bias.shape),
        ],
        out_specs=[pl.BlockSpec((None, res_blk, stretch, width), qmap)] * 2,
        out_shape=[jax.ShapeDtypeStruct(q.shape, BF16), jax.ShapeDtypeStruct(q.shape, F32)],
        scratch_shapes=[pltpu.VMEM(tile, F32), pltpu.VMEM(tile, BF16), pltpu.VMEM(stat, F32)],
        compiler_params=pltpu.CompilerParams(
            dimension_semantics=("arbitrary", "arbitrary", "arbitrary"),
            vmem_limit_bytes=_vmem_limit(blocks, scratch, temps),
        ),
        name=f"dil_attn_g{group}",
    )(q, k, v, bias)


MERGE_TM = 1024
MERGE_CN = 256


def _to_token_order(src_ref, scr_ref, tmp_ref, dil, k):
    per_res = SUB_TM // dil
    res_rows = slice(k * per_res, (k + 1) * per_res)
    if dil == 1:
        return src_ref[0, res_rows].astype(F32)
    rows = slice(k * SUB_TM, (k + 1) * SUB_TM)
    if dil == SPLIT_STRIDE ** 2:
        quarter = SUB_TM // SPLIT_STRIDE
        for r in range(dil):
            start = k * SUB_TM + (r % SPLIT_STRIDE) * quarter + r // SPLIT_STRIDE
            for c in range(GROUP_LANE_BLOCKS):
                tmp_ref[c, pl.ds(start, per_res, stride=SPLIT_STRIDE), :] = (
                    src_ref[r, res_rows, c * V7X_LANES:(c + 1) * V7X_LANES].astype(F32))
        for c in range(GROUP_LANE_BLOCKS):
            for r4 in range(SPLIT_STRIDE):
                lo = k * SUB_TM + r4 * quarter
                scr_ref[c, pl.ds(k * SUB_TM + r4, quarter, stride=SPLIT_STRIDE), :] = tmp_ref[c, lo:lo + quarter]
        return jnp.concatenate([scr_ref[c, rows] for c in range(GROUP_LANE_BLOCKS)], axis=-1)
    for r in range(dil):
        for c in range(GROUP_LANE_BLOCKS):
            scr_ref[c, pl.ds(k * SUB_TM + r, per_res, stride=dil), :] = (
                src_ref[r, res_rows, c * V7X_LANES:(c + 1) * V7X_LANES].astype(F32))
    return jnp.concatenate([scr_ref[c, rows] for c in range(GROUP_LANE_BLOCKS)], axis=-1)


def _merge_kernel(x_ref, pre_g_ref, wgate_ref, oa_ref, wa_ref, *refs):
    ng = NB_GROUPS
    o_refs, lse_refs = refs[:ng], refs[ng:2 * ng]
    wb_ref, wout_ref, post_g_ref, out_ref, h_ref, ob_ref, mg_ref = refs[2 * ng:2 * ng + 7]
    scr_refs = iter(refs[2 * ng + 7:])
    o_scr = [next(scr_refs) if dil > 1 else None for dil in DILATIONS]
    lse_scr = [next(scr_refs) if dil > 1 else None for dil in DILATIONS]
    o_tmp = [next(scr_refs) if dil == SPLIT_STRIDE ** 2 else None for dil in DILATIONS]
    lse_tmp = [next(scr_refs) if dil == SPLIT_STRIDE ** 2 else None for dil in DILATIONS]
    subs = _sub_tiles(x_ref.shape[0])
    for k, rows in enumerate(subs):
        h_ref[rows] = _rmsnorm_f32(x_ref[rows], pre_g_ref[...]).astype(BF16)
        os_ = [_to_token_order(r, s, t, d, k) for r, s, t, d in zip(o_refs, o_scr, o_tmp, DILATIONS)]
        lses = [_to_token_order(r, s, t, d, k) for r, s, t, d in zip(lse_refs, lse_scr, lse_tmp, DILATIONS)]
        mx = jnp.maximum(jnp.maximum(lses[0], lses[1]), lses[2])
        es = [jnp.exp(l - mx) for l in lses]
        inv = 1.0 / (es[0] + es[1] + es[2])
        for g in range(ng):
            ob_ref[rows, g * GROUP_WIDTH:(g + 1) * GROUP_WIDTH] = (os_[g] * (es[g] * inv)).astype(BF16)
    for rows in subs:
        for n in range(D_MODEL // MERGE_CN):
            cols = slice(n * MERGE_CN, (n + 1) * MERGE_CN)
            cols_b = slice(D_MODEL + n * MERGE_CN, D_MODEL + (n + 1) * MERGE_CN)
            h = h_ref[rows]
            ga = _dot(h, wgate_ref[:, cols].astype(BF16))
            gb = _dot(h, wgate_ref[:, cols_b].astype(BF16))
            ya = _dot(oa_ref[rows], wa_ref[:, cols])
            yb = _dot(ob_ref[rows], wb_ref[:, cols])
            mg_ref[rows, cols] = (jax.nn.sigmoid(ga) * ya + jax.nn.sigmoid(gb) * yb).astype(BF16)
    for rows in subs:
        out_ref[rows] = _dot(mg_ref[rows], wout_ref[...])
    for rows in subs:
        out_ref[rows] = x_ref[rows] + _rmsnorm_f32(out_ref[rows], post_g_ref[...])


def _merge(x, pre_g, wgate, oa, wa, obs, lses, wb, wout, post_g):
    batch, seq, d = x.shape
    tm = MERGE_TM
    tok = lambda b, i: (b, i, 0)
    res = lambda b, i: (b, 0, i, 0)
    tokspec = lambda w: pl.BlockSpec((None, tm, w), tok)
    resspecs = [pl.BlockSpec((None, dil, tm // dil, GROUP_WIDTH), res) for dil in DILATIONS]
    blocks = (2 * 2 * _nbytes((tm, d), F32) + 2 * _nbytes((tm, NA_WIDTH), BF16)
              + 2 * 3 * (_nbytes((tm, GROUP_WIDTH), BF16) + _nbytes((tm, GROUP_WIDTH), F32))
              + _nbytes((NA_WIDTH, d), wa.dtype) + _nbytes((d + NB_WIDTH, d), wb.dtype) + _nbytes((d, 2 * d), F32))
    work = [pltpu.VMEM((tm, d), BF16), pltpu.VMEM((tm, NB_WIDTH), BF16), pltpu.VMEM((tm, d), BF16)]
    n_interleave = 2 * sum(dil > 1 for dil in DILATIONS) + 2 * sum(dil == SPLIT_STRIDE ** 2 for dil in DILATIONS)
    scratch = (n_interleave * _nbytes((tm, GROUP_WIDTH), F32) + 2 * _nbytes((tm, d), BF16)
               + _nbytes((tm, NB_WIDTH), BF16))
    temps = 2 * _nbytes((SUB_TM, d), F32)
    return pl.pallas_call(
        _merge_kernel,
        grid=(batch, seq // tm),
        in_specs=[
            tokspec(d), _resident((1, d)),
            pl.BlockSpec((pl.Element(d), pl.Element(2 * d)), lambda *_: (0, QKV_TOTAL),
                         pipeline_mode=pl.Buffered(1)),
            tokspec(NA_WIDTH), _resident((NA_WIDTH, d)),
            *resspecs, *resspecs,
            _resident((NB_WIDTH, d)), _resident((d, d)), _resident((1, d)),
        ],
        out_specs=tokspec(d),
        out_shape=jax.ShapeDtypeStruct((batch, seq, d), F32),
        scratch_shapes=work + [pltpu.VMEM((GROUP_LANE_BLOCKS, tm, V7X_LANES), F32)] * n_interleave,
        compiler_params=pltpu.CompilerParams(
            dimension_semantics=("arbitrary", "arbitrary"),
            vmem_limit_bytes=_vmem_limit(blocks, scratch, temps),
        ),
        name="merge",
    )(x, pre_g, wgate, oa, wa, *obs, *lses, wb, wout, post_g)


def kernel(x, ffn1_pre_g, ffn1_w_gate, ffn1_w_up, ffn1_w_down, ffn1_post_g, mix_pre_g, w_in, na_rpb, w_branch_a, w_branch_b, w_out, mix_post_g, ffn2_pre_g, ffn2_w_gate, ffn2_w_up, ffn2_w_down, ffn2_post_g):
    batch, seq, d = x.shape
    depth = ffn1_pre_g.shape[0]
    for window, dilation in DIL_PAIRS:
        assert window // (2 * dilation) == DIL_HALF
    for l in range(depth):
        x = _ffn(x.reshape(batch * seq, d), ffn1_pre_g[l][None], ffn1_w_gate[l], ffn1_w_up[l],
                 ffn1_w_down[l], ffn1_post_g[l][None])
        x = x.reshape(batch, seq, d)
        qa, ka, va, qb, kb, vb = _qkv(x, mix_pre_g[l][None], w_in[l])
        oa = _nbr_attention(qa, ka, va, na_rpb[l])
        obs, lses = [], []
        for g in range(NB_GROUPS):
            o_g, lse_g = _dil_attention(qb[g], kb[g], vb[g], g)
            obs.append(o_g)
            lses.append(lse_g)
        x = _merge(x, mix_pre_g[l][None], w_in[l],
                   oa, w_branch_a[l].astype(BF16), obs, lses, w_branch_b[l].astype(BF16),
                   w_out[l].astype(BF16), mix_post_g[l][None])
        x = _ffn(x.reshape(batch * seq, d), ffn2_pre_g[l][None], ffn2_w_gate[l], ffn2_w_up[l],
                 ffn2_w_down[l], ffn2_post_g[l][None])
        x = x.reshape(batch, seq, d)
    return x
```

```python
import functools
import math

import jax
import jax.numpy as jnp
from jax import lax
from jax.experimental import pallas as pl
from jax.experimental.pallas import tpu as pltpu

D_MODEL = 1024
HEAD_DIM = 64
NA_HEADS = 8
NA_WIDTH = NA_HEADS * HEAD_DIM
NA_ROWS = 8
NA_COLS = 16
GRID_W = 64
DIL_PAIRS = ((128, 1), (512, 4), (2048, 16))
DILATIONS = tuple(d for _, d in DIL_PAIRS)
NB_GROUPS = len(DIL_PAIRS)
NB_HEADS_PER_GROUP = 4
NB_HEADS = NB_GROUPS * NB_HEADS_PER_GROUP
NB_WIDTH = NB_HEADS * HEAD_DIM
GROUP_WIDTH = NB_HEADS_PER_GROUP * HEAD_DIM
ALIBI_MAX_EXP = 8.0
D_FF = 2816
NORM_EPS = 1e-6
ATTN_SCALE = HEAD_DIM ** -0.5
LOG2E = math.log2(math.e)
LN2 = math.log(2.0)
Q_SCALE = ATTN_SCALE * LOG2E

V7X_LANES = 128
GROUP_LANE_BLOCKS = GROUP_WIDTH // V7X_LANES
V7X_VMEM_BYTES = 64 * 1024 * 1024
V7X_VMEM_RESERVE = 6 * 1024 * 1024

BF16 = jnp.bfloat16
F32 = jnp.float32


def _vmem_limit(block_bytes, scratch_bytes, temp_bytes):
    need = block_bytes + scratch_bytes + temp_bytes
    budget = V7X_VMEM_BYTES - V7X_VMEM_RESERVE
    assert need <= budget, need
    return budget


def _nbytes(shape, dtype):
    return math.prod(shape) * jnp.dtype(dtype).itemsize


def _resident(shape):
    nd = len(shape)
    return pl.BlockSpec(shape, lambda *_: (0,) * nd, pipeline_mode=pl.Buffered(1))


def _rmsnorm_f32(x, g):
    return x * lax.rsqrt(jnp.mean(x * x, axis=-1, keepdims=True) + NORM_EPS) * g


def _dot(a, b):
    return jnp.dot(a, b, preferred_element_type=F32)


def _dot_nt(a, b):
    return lax.dot_general(a, b, (((1,), (1,)), ((), ())), preferred_element_type=F32)


def _pipelined_attention(n_groups, unroll, scores, probs, finish, s_scr, p_scr, stat_scrs):
    assert n_groups % 2 == 0 and n_groups >= 2
    tiles = lambda i: [i * unroll + u for u in range(unroll)]

    def stage_scores(i, slot):
        for u, t in enumerate(tiles(i)):
            s_scr[slot, u] = scores(t)

    def stage_probs(slot):
        for u in range(unroll):
            p, *stats = probs(s_scr[slot, u])
            p_scr[slot, u] = p
            for ref, stat in zip(stat_scrs, stats):
                ref[slot, u] = stat

    def stage_finish(i, slot):
        for u, t in enumerate(tiles(i)):
            finish(t, p_scr[slot, u], *[ref[slot, u] for ref in stat_scrs])

    def step(i, parity):
        stage_scores(i + 1, 1 - parity)
        stage_finish(i - 1, 1 - parity)
        stage_probs(parity)

    stage_scores(0, 0)
    stage_scores(1, 1)
    stage_probs(0)

    def body(i, carry):
        @pl.when(i % 2 == 1)
        def _():
            step(i, 1)

        @pl.when(i % 2 == 0)
        def _():
            step(i, 0)

        return carry

    lax.fori_loop(1, n_groups - 1, body, 0)
    stage_finish(n_groups - 2, 0)
    stage_probs(1)
    stage_finish(n_groups - 1, 1)


FFN_TM = 512
SUB_TM = 512
FFN_CK = 256


def _sub_tiles(tm):
    return [slice(k * SUB_TM, (k + 1) * SUB_TM) for k in range(tm // SUB_TM)]


FFN_CHUNKS = [slice(c * FFN_CK, (c + 1) * FFN_CK) for c in range(D_FF // FFN_CK)]


def _ffn_weight_copies(wg_hbm, wu_hbm, wd_hbm, wg_ref, wu_ref, wd_ref, sem):
    gate_up = [(pltpu.make_async_copy(wg_hbm.at[:, cols], wg_ref.at[:, cols], sem.at[0, c]),
                pltpu.make_async_copy(wu_hbm.at[:, cols], wu_ref.at[:, cols], sem.at[1, c]))
               for c, cols in enumerate(FFN_CHUNKS)]
    down = [pltpu.make_async_copy(wd_hbm.at[cols, :], wd_ref.at[cols, :], sem.at[2, c])
            for c, cols in enumerate(FFN_CHUNKS)]
    return gate_up, down


def _ffn_body(x_ref, pre_g_ref, wg_ref, wu_ref, wd_ref, post_g_ref, o_ref, h_ref, a_ref, wait_gate_up, wait_down):
    h_ref[...] = _rmsnorm_f32(x_ref[...], pre_g_ref[...]).astype(BF16)
    for c, cols in enumerate(FFN_CHUNKS):
        wait_gate_up(c)
        h = h_ref[...]
        g = _dot(h, wg_ref[:, cols].astype(BF16))
        u = _dot(h, wu_ref[:, cols].astype(BF16))
        a_ref[:, cols] = (g * jax.nn.sigmoid(g) * u).astype(BF16)
    f = None
    for c, cols in enumerate(FFN_CHUNKS):
        wait_down(c)
        part = _dot(a_ref[:, cols], wd_ref[cols, :].astype(BF16))
        f = part if f is None else f + part
    o_ref[...] = x_ref[...] + 0.5 * _rmsnorm_f32(f, post_g_ref[...])


def _ffn_kernel(x_ref, pre_g_ref, wg_hbm, wu_hbm, wd_hbm, post_g_ref, o_ref, h_ref, a_ref,
                wg_ref, wu_ref, wd_ref, sem):
    refs = (x_ref, pre_g_ref, wg_ref, wu_ref, wd_ref, post_g_ref, o_ref, h_ref, a_ref)
    first = pl.program_id(0) == 0

    @pl.when(jnp.logical_not(first))
    def _():
        _ffn_body(*refs, lambda c: None, lambda c: None)

    @pl.when(first)
    def _():
        gate_up, down = _ffn_weight_copies(wg_hbm, wu_hbm, wd_hbm, wg_ref, wu_ref, wd_ref, sem)
        for cg, cu in gate_up:
            cg.start()
            cu.start()
        for cd in down:
            cd.start()

        def wait_gate_up(c):
            gate_up[c][0].wait()
            gate_up[c][1].wait()

        _ffn_body(*refs, wait_gate_up, lambda c: down[c].wait())


def _ffn(x, pre_g, wg, wu, wd, post_g):
    n, d = x.shape
    tm = FFN_TM
    row = lambda i: (i, 0)
    n_chunks = len(FFN_CHUNKS)
    blocks = 2 * 2 * _nbytes((tm, d), F32) + 2 * _nbytes((1, d), F32)
    scratch = _nbytes((tm, d), BF16) + _nbytes((tm, D_FF), BF16) + 3 * _nbytes((d, D_FF), wg.dtype)
    temps = 4 * _nbytes((tm, d), F32)
    hbm = pl.BlockSpec(memory_space=pl.ANY)
    return pl.pallas_call(
        _ffn_kernel,
        grid=(n // tm,),
        in_specs=[pl.BlockSpec((tm, d), row), _resident((1, d)), hbm, hbm, hbm, _resident((1, d))],
        out_specs=pl.BlockSpec((tm, d), row),
        out_shape=jax.ShapeDtypeStruct((n, d), F32),
        scratch_shapes=[
            pltpu.VMEM((tm, d), BF16), pltpu.VMEM((tm, D_FF), BF16),
            pltpu.VMEM((d, D_FF), wg.dtype), pltpu.VMEM((d, D_FF), wu.dtype), pltpu.VMEM((D_FF, d), wd.dtype),
            pltpu.SemaphoreType.DMA((3, n_chunks)),
        ],
        compiler_params=pltpu.CompilerParams(
            dimension_semantics=("arbitrary",),
            vmem_limit_bytes=_vmem_limit(blocks, scratch, temps),
        ),
        name="ffn",
    )(x, pre_g, wg, wu, wd, post_g)


QKV_TM = 1024
QKV_TOTAL = 3 * NA_WIDTH + 3 * NB_WIDTH
N_DIL_OUTS = 3 * NB_GROUPS
STAGED_OUTS = [i for i in range(N_DIL_OUTS) if DILATIONS[i % NB_GROUPS] > 1]
N_STAGED_OUTS = len(STAGED_OUTS)
SPLIT_STRIDE = 4
SPLIT_OUTS = [i for i in STAGED_OUTS if DILATIONS[i % NB_GROUPS] == SPLIT_STRIDE ** 2]
N_SPLIT_OUTS = len(SPLIT_OUTS)


def _qkv_kernel(x_ref, g_ref, w32_ref, *refs):
    na_refs, dil_refs = refs[:3], refs[3:3 + N_DIL_OUTS]
    w_ref = refs[3 + N_DIL_OUTS]
    n_stage = N_STAGED_OUTS * (x_ref.shape[0] // SUB_TM)
    y_refs, t_refs = refs[4 + N_DIL_OUTS:4 + N_DIL_OUTS + n_stage], refs[4 + N_DIL_OUTS + n_stage:]

    @pl.when((pl.program_id(0) == 0) & (pl.program_id(1) == 0))
    def _():
        for c in range(QKV_TOTAL // GROUP_WIDTH):
            cols = slice(c * GROUP_WIDTH, (c + 1) * GROUP_WIDTH)
            w_ref[:, cols] = w32_ref[:, cols].astype(BF16)

    for k, rows in enumerate(_sub_tiles(x_ref.shape[0])):
        h = _rmsnorm_f32(x_ref[rows], g_ref[...]).astype(BF16)
        for idx, o_ref in enumerate(dil_refs):
            which, g = divmod(idx, NB_GROUPS)
            dil = DILATIONS[g]
            off = 3 * NA_WIDTH + which * NB_WIDTH + g * GROUP_WIDTH
            y = _dot(h, w_ref[:, off:off + GROUP_WIDTH])
            if which == 0:
                y = y * Q_SCALE
            if dil == 1:
                o_ref[0, rows] = y.astype(BF16)
                continue
            y_ref = y_refs[k * N_STAGED_OUTS + STAGED_OUTS.index(idx)]
            per_res = SUB_TM // dil
            for c in range(GROUP_LANE_BLOCKS):
                y_ref[c] = y[:, c * V7X_LANES:(c + 1) * V7X_LANES]
            if dil == SPLIT_STRIDE ** 2:
                t_ref = t_refs[k * N_SPLIT_OUTS + SPLIT_OUTS.index(idx)]
                quarter = SUB_TM // SPLIT_STRIDE
                for c in range(GROUP_LANE_BLOCKS):
                    for r4 in range(SPLIT_STRIDE):
                        t_ref[c, r4 * quarter:(r4 + 1) * quarter] = y_ref[c, pl.ds(r4, quarter, stride=SPLIT_STRIDE), :]
                for r in range(dil):
                    start = (r % SPLIT_STRIDE) * quarter + r // SPLIT_STRIDE
                    for c in range(GROUP_LANE_BLOCKS):
                        o_ref[r, k * per_res:(k + 1) * per_res, c * V7X_LANES:(c + 1) * V7X_LANES] = (
                            t_ref[c, pl.ds(start, per_res, stride=SPLIT_STRIDE), :].astype(BF16))
                continue
            for r in range(dil):
                for c in range(GROUP_LANE_BLOCKS):
                    o_ref[r, k * per_res:(k + 1) * per_res, c * V7X_LANES:(c + 1) * V7X_LANES] = (
                        y_ref[c, pl.ds(r, per_res, stride=dil), :].astype(BF16))
        for idx, o_ref in enumerate(na_refs):
            y = _dot(h, w_ref[:, idx * NA_WIDTH:(idx + 1) * NA_WIDTH])
            if idx == 0:
                y = y * Q_SCALE
            o_ref[rows] = y.astype(BF16)


def _qkv(x, g, w_in):
    batch, seq, d = x.shape
    tm = QKV_TM
    tok = lambda b, i: (b, i, 0)
    res = lambda b, i: (b, 0, i, 0)
    out_specs = [pl.BlockSpec((None, tm, NA_WIDTH), tok)] * 3
    out_shape = [jax.ShapeDtypeStruct((batch, seq, NA_WIDTH), BF16)] * 3
    for _ in range(3):
        for dil in DILATIONS:
            out_specs.append(pl.BlockSpec((None, dil, tm // dil, GROUP_WIDTH), res))
            out_shape.append(jax.ShapeDtypeStruct((batch, dil, seq // dil, GROUP_WIDTH), BF16))
    blocks = 2 * _nbytes((tm, d), F32) + _nbytes((d, QKV_TOTAL), w_in.dtype) + 2 * _nbytes((tm, QKV_TOTAL), BF16)
    stage = (GROUP_LANE_BLOCKS, SUB_TM, V7X_LANES)
    n_stage = N_STAGED_OUTS * (tm // SUB_TM)
    n_split = N_SPLIT_OUTS * (tm // SUB_TM)
    scratch = (n_stage + n_split) * _nbytes(stage, F32) + _nbytes((d, QKV_TOTAL), BF16)
    temps = _nbytes((SUB_TM, d), F32)
    outs = pl.pallas_call(
        _qkv_kernel,
        grid=(batch, seq // tm),
        in_specs=[pl.BlockSpec((None, tm, d), tok), _resident((1, d)), _resident((d, QKV_TOTAL))],
        out_specs=out_specs,
        out_shape=out_shape,
        scratch_shapes=[pltpu.VMEM((d, QKV_TOTAL), BF16)] + [pltpu.VMEM(stage, F32)] * (n_stage + n_split),
        compiler_params=pltpu.CompilerParams(
            dimension_semantics=("arbitrary", "arbitrary"),
            vmem_limit_bytes=_vmem_limit(blocks, scratch, temps),
        ),
        name="qkv",
    )(x, g, w_in)
    qa, ka, va = outs[:3]
    qb, kb, vb = (outs[3 + w * NB_GROUPS:3 + (w + 1) * NB_GROUPS] for w in range(3))
    return qa, ka, va, qb, kb, vb


NA_HEADS_PER_STEP = V7X_LANES // HEAD_DIM
NA_SPAN = NA_ROWS * GRID_W
NA_VARIANTS = NA_ROWS
NA_RPB_ROWS = 2 * NA_ROWS - 1
NA_RPB_COLS = 2 * NA_COLS - 1
NA_UNROLL = 16


NA_RPB_PAIRS = (NA_RPB_ROWS + 1) // 2


def _nbr_build_bias(rep_ref, tbl_ref):
    width = NA_RPB_PAIRS * V7X_LANES
    qc = lax.broadcasted_iota(jnp.int32, (GRID_W, width), 0)
    kc = lax.broadcasted_iota(jnp.int32, (GRID_W, width), 1) % GRID_W
    col0 = jnp.clip(qc - NA_COLS // 2, 0, GRID_W - NA_COLS)
    col_ok = (kc >= col0) & (kc < col0 + NA_COLS)
    for h in range(NA_HEADS_PER_STEP):
        blocks = []
        for p in range(NA_RPB_PAIRS):
            row = jnp.broadcast_to(rep_ref[h, p:p + 1, :], (GRID_W, V7X_LANES))
            blocks.append(pltpu.roll(row, V7X_LANES - (NA_COLS - 1), 1, stride=1, stride_axis=0))
        w = jnp.where(col_ok, jnp.concatenate(blocks, axis=1) * LOG2E, -jnp.inf)
        for variant in range(NA_VARIANTS):
            tbl_ref[h, variant] = w[:, variant * GRID_W:variant * GRID_W + NA_SPAN]


def _nbr_kernel(q_ref, k_ref, v_ref, rep_ref, o_ref, tbl_ref, s_scr, p_scr, *, rows):
    hp = NA_HEADS_PER_STEP
    ones = jnp.ones((NA_SPAN, V7X_LANES), BF16)

    @pl.when(pl.program_id(1) == 0)
    def _():
        _nbr_build_bias(rep_ref, tbl_ref)

    lane_head = lax.broadcasted_iota(jnp.int32, (GRID_W, V7X_LANES), 1) // HEAD_DIM

    def scores(r):
        row0 = jnp.clip(r - NA_ROWS // 2, 0, rows - NA_ROWS)
        variant = row0 - r + (NA_ROWS - 1)
        q = q_ref[pl.ds(pl.multiple_of(r * GRID_W, GRID_W), GRID_W), :]
        kw = k_ref[pl.ds(pl.multiple_of(row0 * GRID_W, GRID_W), NA_SPAN), :]
        qs = jnp.concatenate([jnp.where(lane_head == h, q, jnp.zeros_like(q)) for h in range(hp)], axis=0)
        return _dot_nt(qs, kw) + tbl_ref[:, variant].reshape(hp * GRID_W, NA_SPAN)

    def probs(s):
        return (jnp.exp2(s - jnp.max(s, axis=-1, keepdims=True)).astype(BF16),)

    def finish(r, p):
        row0 = jnp.clip(r - NA_ROWS // 2, 0, rows - NA_ROWS)
        vw = v_ref[pl.ds(pl.multiple_of(row0 * GRID_W, GRID_W), NA_SPAN), :]
        o = _dot(p, jnp.concatenate([vw, ones], axis=1))
        out, den = o[:GRID_W, :V7X_LANES], o[:GRID_W, V7X_LANES:]
        for h in range(1, hp):
            rows_h = slice(h * GRID_W, (h + 1) * GRID_W)
            out = jnp.where(lane_head == h, o[rows_h, :V7X_LANES], out)
            den = jnp.where(lane_head == h, o[rows_h, V7X_LANES:], den)
        o_ref[pl.ds(pl.multiple_of(r * GRID_W, GRID_W), GRID_W), :] = (out * (1.0 / den)).astype(BF16)

    _pipelined_attention(rows // NA_UNROLL, NA_UNROLL, scores, probs, finish, s_scr, p_scr, ())


def _nbr_attention(q, k, v, rpb):
    batch, seq, width = q.shape
    rows = seq // GRID_W
    n_pairs = width // V7X_LANES
    rep = jnp.pad(rpb.astype(F32), ((0, 0), (0, 2 * NA_RPB_PAIRS - NA_RPB_ROWS), (0, GRID_W - NA_RPB_COLS)))
    rep = rep.reshape(rpb.shape[0], NA_RPB_PAIRS, V7X_LANES)
    tok = lambda p, b: (b, 0, p)
    tbl_shape = (NA_HEADS_PER_STEP, NA_VARIANTS, GRID_W, NA_SPAN)
    rep_block = (NA_HEADS_PER_STEP, NA_RPB_PAIRS, V7X_LANES)
    blocks = 2 * 4 * _nbytes((seq, V7X_LANES), BF16) + 2 * _nbytes(rep_block, F32)
    tile = (2, NA_UNROLL, NA_HEADS_PER_STEP * GRID_W, NA_SPAN)
    scratch = _nbytes(tbl_shape, F32) + _nbytes(tile, F32) + _nbytes(tile, BF16)
    temps = NA_UNROLL * _nbytes(tile[2:], F32)
    return pl.pallas_call(
        functools.partial(_nbr_kernel, rows=rows),
        grid=(n_pairs, batch),
        in_specs=[
            pl.BlockSpec((None, seq, V7X_LANES), tok),
            pl.BlockSpec((None, seq, V7X_LANES), tok),
            pl.BlockSpec((None, seq, V7X_LANES), tok),
            pl.BlockSpec(rep_block, lambda p, b: (p, 0, 0)),
        ],
        out_specs=pl.BlockSpec((None, seq, V7X_LANES), tok),
        out_shape=jax.ShapeDtypeStruct((batch, seq, width), BF16),
        scratch_shapes=[pltpu.VMEM(tbl_shape, F32), pltpu.VMEM(tile, F32), pltpu.VMEM(tile, BF16)],
        compiler_params=pltpu.CompilerParams(
            dimension_semantics=("arbitrary", "arbitrary"),
            vmem_limit_bytes=_vmem_limit(blocks, scratch, temps),
        ),
        name="nbr_attn",
    )(q, k, v, rep)


DIL_HALF = 64
DIL_TQ = 128
DIL_TK = DIL_TQ + 2 * DIL_HALF
DIL_VARIANTS = 3
DIL_STEP_ROWS = 4096
DIL_UNROLL = 8


def _dil_bias_table(group, dilation):
    heads = jnp.arange(NB_HEADS_PER_GROUP, dtype=F32) + group * NB_HEADS_PER_GROUP
    slopes = jnp.exp2(-ALIBI_MAX_EXP * (heads + 1.0) / NB_HEADS)
    offs = jnp.array([0, -DIL_HALF, -2 * DIL_HALF], jnp.int32)
    rel = offs[:, None, None] + jnp.arange(DIL_TK)[None, None, :] - jnp.arange(DIL_TQ)[None, :, None]
    dist = (dilation * jnp.abs(rel)).astype(F32)
    bias = -(slopes[None, :, None, None] * dist[:, None]) * LOG2E
    bias = jnp.where((jnp.abs(rel) <= DIL_HALF)[:, None], bias, -jnp.inf)
    return bias.reshape(DIL_VARIANTS, NB_HEADS_PER_GROUP * DIL_TQ, DIL_TK)


def _dil_kernel(q_ref, k_ref, v_ref, bias_ref, o_ref, lse_ref, s_scr, p_scr, m_scr, *, seq_len, stretch):
    nh = NB_HEADS_PER_GROUP
    hpb = V7X_LANES // HEAD_DIM
    ones = jnp.ones((DIL_TK, V7X_LANES), BF16)
    group_lane_head = lax.broadcasted_iota(jnp.int32, (DIL_TQ, GROUP_WIDTH), 1) // HEAD_DIM
    lane_head = lax.broadcasted_iota(jnp.int32, (DIL_TQ, V7X_LANES), 1) // HEAD_DIM
    base = pl.program_id(2) * stretch
    tiles_per_residue = stretch // DIL_TQ

    def window(t):
        r, j = t // tiles_per_residue, t % tiles_per_residue
        qloc = pl.multiple_of(j * DIL_TQ, DIL_TQ)
        qs = base + qloc
        ws = jnp.clip(qs - DIL_HALF, 0, seq_len - DIL_TK)
        variant = (qs - ws) // DIL_HALF
        return r, qloc, pl.multiple_of(ws, DIL_HALF), variant

    def scores(t):
        r, qloc, ws, variant = window(t)
        q = q_ref[r, pl.ds(qloc, DIL_TQ), :]
        kw = k_ref[r, pl.ds(ws, DIL_TK), :]
        qst = jnp.concatenate([jnp.where(group_lane_head == h, q, jnp.zeros_like(q)) for h in range(nh)],
                              axis=0)
        return _dot_nt(qst, kw) + bias_ref[variant]

    def probs(s):
        m = jnp.max(s, axis=-1, keepdims=True)
        return jnp.exp2(s - m).astype(BF16), jnp.broadcast_to(m, (s.shape[0], V7X_LANES))

    def finish(t, p, m):
        r, qloc, ws, _ = window(t)
        vw = v_ref[r, pl.ds(ws, DIL_TK), :]
        outs, lses = [], []
        for c in range(GROUP_LANE_BLOCKS):
            rows_c = slice(c * hpb * DIL_TQ, (c + 1) * hpb * DIL_TQ)
            v_aug = jnp.concatenate([vw[:, c * V7X_LANES:(c + 1) * V7X_LANES], ones], axis=1)
            o = _dot(p[rows_c], v_aug)
            m_c = m[rows_c]
            out_c, den_c, max_c = o[:DIL_TQ, :V7X_LANES], o[:DIL_TQ, V7X_LANES:], m_c[:DIL_TQ]
            for h in range(1, hpb):
                rows_h = slice(h * DIL_TQ, (h + 1) * DIL_TQ)
                out_c = jnp.where(lane_head == h, o[rows_h, :V7X_LANES], out_c)
                den_c = jnp.where(lane_head == h, o[rows_h, V7X_LANES:], den_c)
                max_c = jnp.where(lane_head == h, m_c[rows_h], max_c)
            outs.append(out_c * (1.0 / den_c))
            lses.append(max_c * LN2 + jnp.log(den_c))
        o_ref[r, pl.ds(qloc, DIL_TQ), :] = jnp.concatenate(outs, axis=1).astype(BF16)
        lse_ref[r, pl.ds(qloc, DIL_TQ), :] = jnp.concatenate(lses, axis=1)

    n_tiles = q_ref.shape[0] * tiles_per_residue
    _pipelined_attention(n_tiles // DIL_UNROLL, DIL_UNROLL, scores, probs, finish, s_scr, p_scr, (m_scr,))


def _dil_attention(q, k, v, group):
    batch, dilation, seq_len, width = q.shape
    stretch = min(seq_len, DIL_STEP_ROWS)
    res_blk = DIL_STEP_ROWS // stretch
    assert dilation % res_blk == 0 and seq_len % stretch == 0
    qmap = lambda b, r, s: (b, r, s, 0)
    kvmap = lambda b, r, s: (b, r, 0, 0)
    bias = _dil_bias_table(group, dilation)
    tile = (2, DIL_UNROLL, NB_HEADS_PER_GROUP * DIL_TQ, DIL_TK)
    stat = tile[:3] + (V7X_LANES,)
    kv_buffers = 2
    kv_spec = pl.BlockSpec((None, res_blk, seq_len, width), kvmap)
    blocks = (2 * (2 * _nbytes((DIL_STEP_ROWS, width), BF16) + _nbytes((DIL_STEP_ROWS, width), F32))
              + kv_buffers * 2 * _nbytes((res_blk * seq_len, width), BF16) + _nbytes(bias.shape, F32))
    scratch = _nbytes(tile, F32) + _nbytes(tile, BF16) + _nbytes(stat, F32)
    temps = DIL_UNROLL * _nbytes(tile[2:], F32)
    return pl.pallas_call(
        functools.partial(_dil_kernel, seq_len=seq_len, stretch=stretch),
        grid=(batch, dilation // res_blk, seq_len // stretch),
        in_specs=[
            pl.BlockSpec((None, res_blk, stretch, width), qmap),
            kv_spec,
            kv_spec,
            _resident(bias.shape),
        ],
        out_specs=[pl.BlockSpec((None, res_blk, stretch, width), qmap)] * 2,
        out_shape=[jax.ShapeDtypeStruct(q.shape, BF16), jax.ShapeDtypeStruct(q.shape, F32)],
        scratch_shapes=[pltpu.VMEM(tile, F32), pltpu.VMEM(tile, BF16), pltpu.VMEM(stat, F32)],
        compiler_params=pltpu.CompilerParams(
            dimension_semantics=("arbitrary", "arbitrary", "arbitrary"),
            vmem_limit_bytes=_vmem_limit(blocks, scratch, temps),
        ),
        name=f"dil_attn_g{group}",
    )(q, k, v, bias)


MERGE_TM = 1024
MERGE_CN = 256


def _to_token_order(src_ref, scr_ref, tmp_ref, dil, k):
    per_res = SUB_TM // dil
    res_rows = slice(k * per_res, (k + 1) * per_res)
    if dil == 1:
        return src_ref[0, res_rows].astype(F32)
    rows = slice(k * SUB_TM, (k + 1) * SUB_TM)
    if dil == SPLIT_STRIDE ** 2:
        quarter = SUB_TM // SPLIT_STRIDE
        for r in range(dil):
            start = k * SUB_TM + (r % SPLIT_STRIDE) * quarter + r // SPLIT_STRIDE
            for c in range(GROUP_LANE_BLOCKS):
                tmp_ref[c, pl.ds(start, per_res, stride=SPLIT_STRIDE), :] = (
                    src_ref[r, res_rows, c * V7X_LANES:(c + 1) * V7X_LANES].astype(F32))
        for c in range(GROUP_LANE_BLOCKS):
            for r4 in range(SPLIT_STRIDE):
                lo = k * SUB_TM + r4 * quarter
                scr_ref[c, pl.ds(k * SUB_TM + r4, quarter, stride=SPLIT_STRIDE), :] = tmp_ref[c, lo:lo + quarter]
        return jnp.concatenate([scr_ref[c, rows] for c in range(GROUP_LANE_BLOCKS)], axis=-1)
    for r in range(dil):
        for c in range(GROUP_LANE_BLOCKS):
            scr_ref[c, pl.ds(k * SUB_TM + r, per_res, stride=dil), :] = (
                src_ref[r, res_rows, c * V7X_LANES:(c + 1) * V7X_LANES].astype(F32))
    return jnp.concatenate([scr_ref[c, rows] for c in range(GROUP_LANE_BLOCKS)], axis=-1)


def _merge_kernel(x_ref, pre_g_ref, wgate_ref, oa_ref, wa_ref, *refs):
    ng = NB_GROUPS
    o_refs, lse_refs = refs[:ng], refs[ng:2 * ng]
    wb_ref, wout_ref, post_g_ref, out_ref, h_ref, ob_ref, mg_ref = refs[2 * ng:2 * ng + 7]
    scr_refs = iter(refs[2 * ng + 7:])
    o_scr = [next(scr_refs) if dil > 1 else None for dil in DILATIONS]
    lse_scr = [next(scr_refs) if dil > 1 else None for dil in DILATIONS]
    o_tmp = [next(scr_refs) if dil == SPLIT_STRIDE ** 2 else None for dil in DILATIONS]
    lse_tmp = [next(scr_refs) if dil == SPLIT_STRIDE ** 2 else None for dil in DILATIONS]
    subs = _sub_tiles(x_ref.shape[0])
    for k, rows in enumerate(subs):
        h_ref[rows] = _rmsnorm_f32(x_ref[rows], pre_g_ref[...]).astype(BF16)
        os_ = [_to_token_order(r, s, t, d, k) for r, s, t, d in zip(o_refs, o_scr, o_tmp, DILATIONS)]
        lses = [_to_token_order(r, s, t, d, k) for r, s, t, d in zip(lse_refs, lse_scr, lse_tmp, DILATIONS)]
        mx = jnp.maximum(jnp.maximum(lses[0], lses[1]), lses[2])
        es = [jnp.exp(l - mx) for l in lses]
        inv = 1.0 / (es[0] + es[1] + es[2])
        for g in range(ng):
            ob_ref[rows, g * GROUP_WIDTH:(g + 1) * GROUP_WIDTH] = (os_[g] * (es[g] * inv)).astype(BF16)
    for rows in subs:
        for n in range(D_MODEL // MERGE_CN):
            cols = slice(n * MERGE_CN, (n + 1) * MERGE_CN)
            cols_b = slice(D_MODEL + n * MERGE_CN, D_MODEL + (n + 1) * MERGE_CN)
            h = h_ref[rows]
            ga = _dot(h, wgate_ref[:, cols].astype(BF16))
            gb = _dot(h, wgate_ref[:, cols_b].astype(BF16))
            ya = _dot(oa_ref[rows], wa_ref[:, cols])
            yb = _dot(ob_ref[rows], wb_ref[:, cols])
            mg_ref[rows, cols] = (jax.nn.sigmoid(ga) * ya + jax.nn.sigmoid(gb) * yb).astype(BF16)
    for rows in subs:
        out_ref[rows] = _dot(mg_ref[rows], wout_ref[...])
    for rows in subs:
        out_ref[rows] = x_ref[rows] + _rmsnorm_f32(out_ref[rows], post_g_ref[...])


def _merge(x, pre_g, wgate, oa, wa, obs, lses, wb, wout, post_g):
    batch, seq, d = x.shape
    tm = MERGE_TM
    tok = lambda b, i: (b, i, 0)
    res = lambda b, i: (b, 0, i, 0)
    tokspec = lambda w: pl.BlockSpec((None, tm, w), tok)
    resspecs = [pl.BlockSpec((None, dil, tm // dil, GROUP_WIDTH), res) for dil in DILATIONS]
    blocks = (2 * 2 * _nbytes((tm, d), F32) + 2 * _nbytes((tm, NA_WIDTH), BF16)
              + 2 * 3 * (_nbytes((tm, GROUP_WIDTH), BF16) + _nbytes((tm, GROUP_WIDTH), F32))
              + _nbytes((NA_WIDTH, d), wa.dtype) + _nbytes((d + NB_WIDTH, d), wb.dtype) + _nbytes((d, 2 * d), F32))
    work = [pltpu.VMEM((tm, d), BF16), pltpu.VMEM((tm, NB_WIDTH), BF16), pltpu.VMEM((tm, d), BF16)]
    n_interleave = 2 * sum(dil > 1 for dil in DILATIONS) + 2 * sum(dil == SPLIT_STRIDE ** 2 for dil in DILATIONS)
    scratch = (n_interleave * _nbytes((tm, GROUP_WIDTH), F32) + 2 * _nbytes((tm, d), BF16)
               + _nbytes((tm, NB_WIDTH), BF16))
    temps = 2 * _nbytes((SUB_TM, d), F32)
    return pl.pallas_call(
        _merge_kernel,
        grid=(batch, seq // tm),
        in_specs=[
            tokspec(d), _resident((1, d)),
            pl.BlockSpec((pl.Element(d), pl.Element(2 * d)), lambda *_: (0, QKV_TOTAL),
                         pipeline_mode=pl.Buffered(1)),
            tokspec(NA_WIDTH), _resident((NA_WIDTH, d)),
            *resspecs, *resspecs,
            _resident((NB_WIDTH, d)), _resident((d, d)), _resident((1, d)),
        ],
        out_specs=tokspec(d),
        out_shape=jax.ShapeDtypeStruct((batch, seq, d), F32),
        scratch_shapes=work + [pltpu.VMEM((GROUP_LANE_BLOCKS, tm, V7X_LANES), F32)] * n_interleave,
        compiler_params=pltpu.CompilerParams(
            dimension_semantics=("arbitrary", "arbitrary"),
            vmem_limit_bytes=_vmem_limit(blocks, scratch, temps),
        ),
        name="merge",
    )(x, pre_g, wgate, oa, wa, *obs, *lses, wb, wout, post_g)


def kernel(x, ffn1_pre_g, ffn1_w_gate, ffn1_w_up, ffn1_w_down, ffn1_post_g, mix_pre_g, w_in, na_rpb, w_branch_a, w_branch_b, w_out, mix_post_g, ffn2_pre_g, ffn2_w_gate, ffn2_w_up, ffn2_w_down, ffn2_post_g):
    batch, seq, d = x.shape
    depth = ffn1_pre_g.shape[0]
    for window, dilation in DIL_PAIRS:
        assert window // (2 * dilation) == DIL_HALF
    for l in range(depth):
        x = _ffn(x.reshape(batch * seq, d), ffn1_pre_g[l][None], ffn1_w_gate[l], ffn1_w_up[l],
                 ffn1_w_down[l], ffn1_post_g[l][None])
        x = x.reshape(batch, seq, d)
        qa, ka, va, qb, kb, vb = _qkv(x, mix_pre_g[l][None], w_in[l])
        oa = _nbr_attention(qa, ka, va, na_rpb[l])
        obs, lses = [], []
        for g in range(NB_GROUPS):
            o_g, lse_g = _dil_attention(qb[g], kb[g], vb[g], g)
            obs.append(o_g)
            lses.append(lse_g)
        x = _merge(x, mix_pre_g[l][None], w_in[l],
                   oa, w_branch_a[l].astype(BF16), obs, lses, w_branch_b[l].astype(BF16),
                   w_out[l].astype(BF16), mix_post_g[l][None])
        x = _ffn(x.reshape(batch * seq, d), ffn2_pre_g[l][None], ffn2_w_gate[l], ffn2_w_up[l],
                 ffn2_w_down[l], ffn2_post_g[l][None])
        x = x.reshape(batch, seq, d)
    return x
```

```python
import functools
import math

import jax
import jax.numpy as jnp
from jax import lax
from jax.experimental import pallas as pl
from jax.experimental.pallas import tpu as pltpu

D_MODEL = 1024
HEAD_DIM = 64
NA_HEADS = 8
NA_WIDTH = NA_HEADS * HEAD_DIM
NA_ROWS = 8
NA_COLS = 16
GRID_W = 64
DIL_PAIRS = ((128, 1), (512, 4), (2048, 16))
DILATIONS = tuple(d for _, d in DIL_PAIRS)
NB_GROUPS = len(DIL_PAIRS)
NB_HEADS_PER_GROUP = 4
NB_HEADS = NB_GROUPS * NB_HEADS_PER_GROUP
NB_WIDTH = NB_HEADS * HEAD_DIM
GROUP_WIDTH = NB_HEADS_PER_GROUP * HEAD_DIM
ALIBI_MAX_EXP = 8.0
D_FF = 2816
NORM_EPS = 1e-6
ATTN_SCALE = HEAD_DIM ** -0.5
LOG2E = math.log2(math.e)
LN2 = math.log(2.0)
Q_SCALE = ATTN_SCALE * LOG2E

V7X_LANES = 128
GROUP_LANE_BLOCKS = GROUP_WIDTH // V7X_LANES
V7X_VMEM_BYTES = 64 * 1024 * 1024
V7X_VMEM_RESERVE = 6 * 1024 * 1024

BF16 = jnp.bfloat16
F32 = jnp.float32


def _vmem_limit(block_bytes, scratch_bytes, temp_bytes):
    need = block_bytes + scratch_bytes + temp_bytes
    budget = V7X_VMEM_BYTES - V7X_VMEM_RESERVE
    assert need <= budget, need
    return budget


def _nbytes(shape, dtype):
    return math.prod(shape) * jnp.dtype(dtype).itemsize


def _resident(shape):
    nd = len(shape)
    return pl.BlockSpec(shape, lambda *_: (0,) * nd, pipeline_mode=pl.Buffered(1))


def _rmsnorm_f32(x, g):
    return x * lax.rsqrt(jnp.mean(x * x, axis=-1, keepdims=True) + NORM_EPS) * g


def _dot(a, b):
    return jnp.dot(a, b, preferred_element_type=F32)


def _dot_nt(a, b):
    return lax.dot_general(a, b, (((1,), (1,)), ((), ())), preferred_element_type=F32)


def _pipelined_attention(n_groups, unroll, scores, probs, finish, s_scr, p_scr, stat_scrs):
    assert n_groups % 2 == 0 and n_groups >= 2
    tiles = lambda i: [i * unroll + u for u in range(unroll)]

    def stage_scores(i, slot):
        for u, t in enumerate(tiles(i)):
            s_scr[slot, u] = scores(t)

    def stage_probs(slot):
        for u in range(unroll):
            p, *stats = probs(s_scr[slot, u])
            p_scr[slot, u] = p
            for ref, stat in zip(stat_scrs, stats):
                ref[slot, u] = stat

    def stage_finish(i, slot):
        for u, t in enumerate(tiles(i)):
            finish(t, p_scr[slot, u], *[ref[slot, u] for ref in stat_scrs])

    def step(i, parity):
        stage_scores(i + 1, 1 - parity)
        stage_finish(i - 1, 1 - parity)
        stage_probs(parity)

    stage_scores(0, 0)
    stage_scores(1, 1)
    stage_probs(0)

    def body(i, carry):
        @pl.when(i % 2 == 1)
        def _():
            step(i, 1)

        @pl.when(i % 2 == 0)
        def _():
            step(i, 0)

        return carry

    lax.fori_loop(1, n_groups - 1, body, 0)
    stage_finish(n_groups - 2, 0)
    stage_probs(1)
    stage_finish(n_groups - 1, 1)


FFN_TM = 512
SUB_TM = 512
FFN_CK = 256


def _sub_tiles(tm):
    return [slice(k * SUB_TM, (k + 1) * SUB_TM) for k in range(tm // SUB_TM)]


FFN_CHUNKS = [slice(c * FFN_CK, (c + 1) * FFN_CK) for c in range(D_FF // FFN_CK)]


def _ffn_weight_copies(wg_hbm, wu_hbm, wd_hbm, wg_ref, wu_ref, wd_ref, sem):
    gate_up = [(pltpu.make_async_copy(wg_hbm.at[:, cols], wg_ref.at[:, cols], sem.at[0, c]),
                pltpu.make_async_copy(wu_hbm.at[:, cols], wu_ref.at[:, cols], sem.at[1, c]))
               for c, cols in enumerate(FFN_CHUNKS)]
    down = [pltpu.make_async_copy(wd_hbm.at[cols, :], wd_ref.at[cols, :], sem.at[2, c])
            for c, cols in enumerate(FFN_CHUNKS)]
    return gate_up, down


def _ffn_body(x_ref, pre_g_ref, wg_ref, wu_ref, wd_ref, post_g_ref, o_ref, h_ref, a_ref, wait_gate_up, wait_down):
    h_ref[...] = _rmsnorm_f32(x_ref[...], pre_g_ref[...]).astype(BF16)
    for c, cols in enumerate(FFN_CHUNKS):
        wait_gate_up(c)
        h = h_ref[...]
        g = _dot(h, wg_ref[:, cols].astype(BF16))
        u = _dot(h, wu_ref[:, cols].astype(BF16))
        a_ref[:, cols] = (g * jax.nn.sigmoid(g) * u).astype(BF16)
    f = None
    for c, cols in enumerate(FFN_CHUNKS):
        wait_down(c)
        part = _dot(a_ref[:, cols], wd_ref[cols, :].astype(BF16))
        f = part if f is None else f + part
    o_ref[...] = x_ref[...] + 0.5 * _rmsnorm_f32(f, post_g_ref[...])


def _ffn_kernel(x_ref, pre_g_ref, wg_hbm, wu_hbm, wd_hbm, post_g_ref, o_ref, h_ref, a_ref,
                wg_ref, wu_ref, wd_ref, sem):
    refs = (x_ref, pre_g_ref, wg_ref, wu_ref, wd_ref, post_g_ref, o_ref, h_ref, a_ref)
    first = pl.program_id(0) == 0

    @pl.when(jnp.logical_not(first))
    def _():
        _ffn_body(*refs, lambda c: None, lambda c: None)

    @pl.when(first)
    def _():
        gate_up, down = _ffn_weight_copies(wg_hbm, wu_hbm, wd_hbm, wg_ref, wu_ref, wd_ref, sem)
        for cg, cu in gate_up:
            cg.start()
            cu.start()
        for cd in down:
            cd.start()

        def wait_gate_up(c):
            gate_up[c][0].wait()
            gate_up[c][1].wait()

        _ffn_body(*refs, wait_gate_up, lambda c: down[c].wait())


def _ffn(x, pre_g, wg, wu, wd, post_g):
    n, d = x.shape
    tm = FFN_TM
    row = lambda i: (i, 0)
    n_chunks = len(FFN_CHUNKS)
    blocks = 2 * 2 * _nbytes((tm, d), F32) + 2 * _nbytes((1, d), F32)
    scratch = _nbytes((tm, d), BF16) + _nbytes((tm, D_FF), BF16) + 3 * _nbytes((d, D_FF), wg.dtype)
    temps = 4 * _nbytes((tm, d), F32)
    hbm = pl.BlockSpec(memory_space=pl.ANY)
    return pl.pallas_call(
        _ffn_kernel,
        grid=(n // tm,),
        in_specs=[pl.BlockSpec((tm, d), row), _resident((1, d)), hbm, hbm, hbm, _resident((1, d))],
        out_specs=pl.BlockSpec((tm, d), row),
        out_shape=jax.ShapeDtypeStruct((n, d), F32),
        scratch_shapes=[
            pltpu.VMEM((tm, d), BF16), pltpu.VMEM((tm, D_FF), BF16),
            pltpu.VMEM((d, D_FF), wg.dtype), pltpu.VMEM((d, D_FF), wu.dtype), pltpu.VMEM((D_FF, d), wd.dtype),
            pltpu.SemaphoreType.DMA((3, n_chunks)),
        ],
        compiler_params=pltpu.CompilerParams(
            dimension_semantics=("arbitrary",),
            vmem_limit_bytes=_vmem_limit(blocks, scratch, temps),
        ),
        name="ffn",
    )(x, pre_g, wg, wu, wd, post_g)


QKV_TM = 1024
QKV_TOTAL = 3 * NA_WIDTH + 3 * NB_WIDTH
N_DIL_OUTS = 3 * NB_GROUPS
STAGED_OUTS = [i for i in range(N_DIL_OUTS) if DILATIONS[i % NB_GROUPS] > 1]
N_STAGED_OUTS = len(STAGED_OUTS)
SPLIT_STRIDE = 4
SPLIT_OUTS = [i for i in STAGED_OUTS if DILATIONS[i % NB_GROUPS] == SPLIT_STRIDE ** 2]
N_SPLIT_OUTS = len(SPLIT_OUTS)


def _qkv_kernel(x_ref, g_ref, w32_ref, *refs):
    na_refs, dil_refs = refs[:3], refs[3:3 + N_DIL_OUTS]
    w_ref = refs[3 + N_DIL_OUTS]
    n_stage = N_STAGED_OUTS * (x_ref.shape[0] // SUB_TM)
    y_refs, t_refs = refs[4 + N_DIL_OUTS:4 + N_DIL_OUTS + n_stage], refs[4 + N_DIL_OUTS + n_stage:]

    @pl.when((pl.program_id(0) == 0) & (pl.program_id(1) == 0))
    def _():
        for c in range(QKV_TOTAL // GROUP_WIDTH):
            cols = slice(c * GROUP_WIDTH, (c + 1) * GROUP_WIDTH)
            w_ref[:, cols] = w32_ref[:, cols].astype(BF16)

    for k, rows in enumerate(_sub_tiles(x_ref.shape[0])):
        h = _rmsnorm_f32(x_ref[rows], g_ref[...]).astype(BF16)
        for idx, o_ref in enumerate(dil_refs):
            which, g = divmod(idx, NB_GROUPS)
            dil = DILATIONS[g]
            off = 3 * NA_WIDTH + which * NB_WIDTH + g * GROUP_WIDTH
            y = _dot(h, w_ref[:, off:off + GROUP_WIDTH])
            if which == 0:
                y = y * Q_SCALE
            if dil == 1:
                o_ref[0, rows] = y.astype(BF16)
                continue
            y_ref = y_refs[k * N_STAGED_OUTS + STAGED_OUTS.index(idx)]
            per_res = SUB_TM // dil
            for c in range(GROUP_LANE_BLOCKS):
                y_ref[c] = y[:, c * V7X_LANES:(c + 1) * V7X_LANES]
            if dil == SPLIT_STRIDE ** 2:
                t_ref = t_refs[k * N_SPLIT_OUTS + SPLIT_OUTS.index(idx)]
                quarter = SUB_TM // SPLIT_STRIDE
                for c in range(GROUP_LANE_BLOCKS):
                    for r4 in range(SPLIT_STRIDE):
                        t_ref[c, r4 * quarter:(r4 + 1) * quarter] = y_ref[c, pl.ds(r4, quarter, stride=SPLIT_STRIDE), :]
                for r in range(dil):
                    start = (r % SPLIT_STRIDE) * quarter + r // SPLIT_STRIDE
                    for c in range(GROUP_LANE_BLOCKS):
                        o_ref[r, k * per_res:(k + 1) * per_res, c * V7X_LANES:(c + 1) * V7X_LANES] = (
                            t_ref[c, pl.ds(start, per_res, stride=SPLIT_STRIDE), :].astype(BF16))
                continue
            for r in range(dil):
                for c in range(GROUP_LANE_BLOCKS):
                    o_ref[r, k * per_res:(k + 1) * per_res, c * V7X_LANES:(c + 1) * V7X_LANES] = (
                        y_ref[c, pl.ds(r, per_res, stride=dil), :].astype(BF16))
        for idx, o_ref in enumerate(na_refs):
            y = _dot(h, w_ref[:, idx * NA_WIDTH:(idx + 1) * NA_WIDTH])
            if idx == 0:
                y = y * Q_SCALE
            o_ref[rows] = y.astype(BF16)


def _qkv(x, g, w_in):
    batch, seq, d = x.shape
    tm = QKV_TM
    tok = lambda b, i: (b, i, 0)
    res = lambda b, i: (b, 0, i, 0)
    out_specs = [pl.BlockSpec((None, tm, NA_WIDTH), tok)] * 3
    out_shape = [jax.ShapeDtypeStruct((batch, seq, NA_WIDTH), BF16)] * 3
    for _ in range(3):
        for dil in DILATIONS:
            out_specs.append(pl.BlockSpec((None, dil, tm // dil, GROUP_WIDTH), res))
            out_shape.append(jax.ShapeDtypeStruct((batch, dil, seq // dil, GROUP_WIDTH), BF16))
    blocks = 2 * _nbytes((tm, d), F32) + _nbytes((d, QKV_TOTAL), w_in.dtype) + 2 * _nbytes((tm, QKV_TOTAL), BF16)
    stage = (GROUP_LANE_BLOCKS, SUB_TM, V7X_LANES)
    n_stage = N_STAGED_OUTS * (tm // SUB_TM)
    n_split = N_SPLIT_OUTS * (tm // SUB_TM)
    scratch = (n_stage + n_split) * _nbytes(stage, F32) + _nbytes((d, QKV_TOTAL), BF16)
    temps = _nbytes((SUB_TM, d), F32)
    outs = pl.pallas_call(
        _qkv_kernel,
        grid=(batch, seq // tm),
        in_specs=[pl.BlockSpec((None, tm, d), tok), _resident((1, d)), _resident((d, QKV_TOTAL))],
        out_specs=out_specs,
        out_shape=out_shape,
        scratch_shapes=[pltpu.VMEM((d, QKV_TOTAL), BF16)] + [pltpu.VMEM(stage, F32)] * (n_stage + n_split),
        compiler_params=pltpu.CompilerParams(
            dimension_semantics=("arbitrary", "arbitrary"),
            vmem_limit_bytes=_vmem_limit(blocks, scratch, temps),
        ),
        name="qkv",
    )(x, g, w_in)
    qa, ka, va = outs[:3]
    qb, kb, vb = (outs[3 + w * NB_GROUPS:3 + (w + 1) * NB_GROUPS] for w in range(3))
    return qa, ka, va, qb, kb, vb


NA_HEADS_PER_STEP = V7X_LANES // HEAD_DIM
NA_SPAN = NA_ROWS * GRID_W
NA_VARIANTS = NA_ROWS
NA_RPB_ROWS = 2 * NA_ROWS - 1
NA_RPB_COLS = 2 * NA_COLS - 1
NA_UNROLL = 16


NA_RPB_PAIRS = (NA_RPB_ROWS + 1) // 2


def _nbr_build_bias(rep_ref, tbl_ref):
    width = NA_RPB_PAIRS * V7X_LANES
    qc = lax.broadcasted_iota(jnp.int32, (GRID_W, width), 0)
    kc = lax.broadcasted_iota(jnp.int32, (GRID_W, width), 1) % GRID_W
    col0 = jnp.clip(qc - NA_COLS // 2, 0, GRID_W - NA_COLS)
    col_ok = (kc >= col0) & (kc < col0 + NA_COLS)
    for h in range(NA_HEADS_PER_STEP):
        blocks = []
        for p in range(NA_RPB_PAIRS):
            row = jnp.broadcast_to(rep_ref[h, p:p + 1, :], (GRID_W, V7X_LANES))
            blocks.append(pltpu.roll(row, V7X_LANES - (NA_COLS - 1), 1, stride=1, stride_axis=0))
        w = jnp.where(col_ok, jnp.concatenate(blocks, axis=1) * LOG2E, -jnp.inf)
        for variant in range(NA_VARIANTS):
            tbl_ref[h, variant] = w[:, variant * GRID_W:variant * GRID_W + NA_SPAN]


def _nbr_kernel(q_ref, k_ref, v_ref, rep_ref, *refs, rows, n_side):
    side_hbm, o_ref, side_out = refs[:n_side], refs[n_side], refs[n_side + 1:2 * n_side + 1]
    tbl_ref, s_scr, p_scr = refs[2 * n_side + 1:2 * n_side + 4]
    side_buf, side_sem = refs[2 * n_side + 4:3 * n_side + 4], refs[3 * n_side + 4]
    hp = NA_HEADS_PER_STEP
    ones = jnp.ones((NA_SPAN, V7X_LANES), BF16)
    step = pl.program_id(0) * pl.num_programs(1) + pl.program_id(1)
    side_copies = [pltpu.make_async_copy(src, buf, side_sem.at[i])
                   for i, (src, buf) in enumerate(zip(side_hbm, side_buf))]

    @pl.when(step == 0)
    def _():
        for cp in side_copies:
            cp.start()

    @pl.when(pl.program_id(1) == 0)
    def _():
        _nbr_build_bias(rep_ref, tbl_ref)

    lane_head = lax.broadcasted_iota(jnp.int32, (GRID_W, V7X_LANES), 1) // HEAD_DIM

    def scores(r):
        row0 = jnp.clip(r - NA_ROWS // 2, 0, rows - NA_ROWS)
        variant = row0 - r + (NA_ROWS - 1)
        q = q_ref[pl.ds(pl.multiple_of(r * GRID_W, GRID_W), GRID_W), :]
        kw = k_ref[pl.ds(pl.multiple_of(row0 * GRID_W, GRID_W), NA_SPAN), :]
        qs = jnp.concatenate([jnp.where(lane_head == h, q, jnp.zeros_like(q)) for h in range(hp)], axis=0)
        return _dot_nt(qs, kw) + tbl_ref[:, variant].reshape(hp * GRID_W, NA_SPAN)

    def probs(s):
        return (jnp.exp2(s - jnp.max(s, axis=-1, keepdims=True)).astype(BF16),)

    def finish(r, p):
        row0 = jnp.clip(r - NA_ROWS // 2, 0, rows - NA_ROWS)
        vw = v_ref[pl.ds(pl.multiple_of(row0 * GRID_W, GRID_W), NA_SPAN), :]
        o = _dot(p, jnp.concatenate([vw, ones], axis=1))
        out, den = o[:GRID_W, :V7X_LANES], o[:GRID_W, V7X_LANES:]
        for h in range(1, hp):
            rows_h = slice(h * GRID_W, (h + 1) * GRID_W)
            out = jnp.where(lane_head == h, o[rows_h, :V7X_LANES], out)
            den = jnp.where(lane_head == h, o[rows_h, V7X_LANES:], den)
        o_ref[pl.ds(pl.multiple_of(r * GRID_W, GRID_W), GRID_W), :] = (out * (1.0 / den)).astype(BF16)

    _pipelined_attention(rows // NA_UNROLL, NA_UNROLL, scores, probs, finish, s_scr, p_scr, ())

    @pl.when(step == pl.num_programs(0) * pl.num_programs(1) - 1)
    def _():
        for cp, buf, out in zip(side_copies, side_buf, side_out):
            cp.wait()
            out[...] = buf[...].astype(BF16)


def _nbr_attention(q, k, v, rpb, side_weights):
    batch, seq, width = q.shape
    rows = seq // GRID_W
    n_pairs = width // V7X_LANES
    rep = jnp.pad(rpb.astype(F32), ((0, 0), (0, 2 * NA_RPB_PAIRS - NA_RPB_ROWS), (0, GRID_W - NA_RPB_COLS)))
    rep = rep.reshape(rpb.shape[0], NA_RPB_PAIRS, V7X_LANES)
    tok = lambda p, b: (b, 0, p)
    tbl_shape = (NA_HEADS_PER_STEP, NA_VARIANTS, GRID_W, NA_SPAN)
    rep_block = (NA_HEADS_PER_STEP, NA_RPB_PAIRS, V7X_LANES)
    n_side = len(side_weights)
    side_bytes = sum(_nbytes(w.shape, F32) for w in side_weights)
    blocks = 2 * 4 * _nbytes((seq, V7X_LANES), BF16) + 2 * _nbytes(rep_block, F32) + side_bytes
    tile = (2, NA_UNROLL, NA_HEADS_PER_STEP * GRID_W, NA_SPAN)
    scratch = _nbytes(tbl_shape, F32) + _nbytes(tile, F32) + _nbytes(tile, BF16) + side_bytes
    temps = NA_UNROLL * _nbytes(tile[2:], F32)
    outs = pl.pallas_call(
        functools.partial(_nbr_kernel, rows=rows, n_side=n_side),
        grid=(n_pairs, batch),
        in_specs=[
            pl.BlockSpec((None, seq, V7X_LANES), tok),
            pl.BlockSpec((None, seq, V7X_LANES), tok),
            pl.BlockSpec((None, seq, V7X_LANES), tok),
            pl.BlockSpec(rep_block, lambda p, b: (p, 0, 0)),
        ] + [pl.BlockSpec(memory_space=pl.ANY)] * n_side,
        out_specs=[pl.BlockSpec((None, seq, V7X_LANES), tok)]
                  + [pl.BlockSpec(w.shape, lambda p, b: (0, 0)) for w in side_weights],
        out_shape=[jax.ShapeDtypeStruct((batch, seq, width), BF16)]
                  + [jax.ShapeDtypeStruct(w.shape, BF16) for w in side_weights],
        scratch_shapes=([pltpu.VMEM(tbl_shape, F32), pltpu.VMEM(tile, F32), pltpu.VMEM(tile, BF16)]
                        + [pltpu.VMEM(w.shape, F32) for w in side_weights]
                        + [pltpu.SemaphoreType.DMA((n_side,))]),
        compiler_params=pltpu.CompilerParams(
            dimension_semantics=("arbitrary", "arbitrary"),
            vmem_limit_bytes=_vmem_limit(blocks, scratch, temps),
        ),
        name="nbr_attn",
    )(q, k, v, rep, *side_weights)
    return outs[0], outs[1:]


DIL_HALF = 64
DIL_TQ = 128
DIL_TK = DIL_TQ + 2 * DIL_HALF
DIL_VARIANTS = 3
DIL_STEP_ROWS = 4096
DIL_UNROLL = 8


def _dil_slopes(group):
    heads = jnp.arange(NB_HEADS_PER_GROUP, dtype=F32) + group * NB_HEADS_PER_GROUP
    return jnp.exp2(-ALIBI_MAX_EXP * (heads + 1.0) / NB_HEADS)


def _dil_build_bias(slope_ref, bias_ref, dilation):
    qq = lax.broadcasted_iota(jnp.int32, (DIL_TQ, DIL_TK), 0)
    kk = lax.broadcasted_iota(jnp.int32, (DIL_TQ, DIL_TK), 1)
    for variant in range(DIL_VARIANTS):
        rel = kk - qq - variant * DIL_HALF
        dist = (dilation * jnp.abs(rel)).astype(F32)
        in_band = jnp.abs(rel) <= DIL_HALF
        for h in range(NB_HEADS_PER_GROUP):
            bias = -(slope_ref[h] * dist) * LOG2E
            bias_ref[variant, h * DIL_TQ:(h + 1) * DIL_TQ, :] = jnp.where(in_band, bias, -jnp.inf)


def _dil_kernel(q_ref, k_ref, v_ref, slope_ref, o_ref, lse_ref, bias_ref, s_scr, p_scr, m_scr,
                *, seq_len, stretch, dilation):
    nh = NB_HEADS_PER_GROUP
    hpb = V7X_LANES // HEAD_DIM
    ones = jnp.ones((DIL_TK, V7X_LANES), BF16)

    @pl.when((pl.program_id(0) == 0) & (pl.program_id(1) == 0) & (pl.program_id(2) == 0))
    def _():
        _dil_build_bias(slope_ref, bias_ref, dilation)

    group_lane_head = lax.broadcasted_iota(jnp.int32, (DIL_TQ, GROUP_WIDTH), 1) // HEAD_DIM
    lane_head = lax.broadcasted_iota(jnp.int32, (DIL_TQ, V7X_LANES), 1) // HEAD_DIM
    base = pl.program_id(2) * stretch
    tiles_per_residue = stretch // DIL_TQ

    def window(t):
        r, j = t // tiles_per_residue, t % tiles_per_residue
        qloc = pl.multiple_of(j * DIL_TQ, DIL_TQ)
        qs = base + qloc
        ws = jnp.clip(qs - DIL_HALF, 0, seq_len - DIL_TK)
        variant = (qs - ws) // DIL_HALF
        return r, qloc, pl.multiple_of(ws, DIL_HALF), variant

    def scores(t):
        r, qloc, ws, variant = window(t)
        q = q_ref[r, pl.ds(qloc, DIL_TQ), :]
        kw = k_ref[r, pl.ds(ws, DIL_TK), :]
        qst = jnp.concatenate([jnp.where(group_lane_head == h, q, jnp.zeros_like(q)) for h in range(nh)],
                              axis=0)
        return _dot_nt(qst, kw) + bias_ref[variant]

    def probs(s):
        m = jnp.max(s, axis=-1, keepdims=True)
        return jnp.exp2(s - m).astype(BF16), jnp.broadcast_to(m, (s.shape[0], V7X_LANES))

    def finish(t, p, m):
        r, qloc, ws, _ = window(t)
        vw = v_ref[r, pl.ds(ws, DIL_TK), :]
        outs, lses = [], []
        for c in range(GROUP_LANE_BLOCKS):
            rows_c = slice(c * hpb * DIL_TQ, (c + 1) * hpb * DIL_TQ)
            v_aug = jnp.concatenate([vw[:, c * V7X_LANES:(c + 1) * V7X_LANES], ones], axis=1)
            o = _dot(p[rows_c], v_aug)
            m_c = m[rows_c]
            out_c, den_c, max_c = o[:DIL_TQ, :V7X_LANES], o[:DIL_TQ, V7X_LANES:], m_c[:DIL_TQ]
            for h in range(1, hpb):
                rows_h = slice(h * DIL_TQ, (h + 1) * DIL_TQ)
                out_c = jnp.where(lane_head == h, o[rows_h, :V7X_LANES], out_c)
                den_c = jnp.where(lane_head == h, o[rows_h, V7X_LANES:], den_c)
                max_c = jnp.where(lane_head == h, m_c[rows_h], max_c)
            outs.append(out_c * (1.0 / den_c))
            lses.append(max_c * LN2 + jnp.log(den_c))
        o_ref[r, pl.ds(qloc, DIL_TQ), :] = jnp.concatenate(outs, axis=1).astype(BF16)
        lse_ref[r, pl.ds(qloc, DIL_TQ), :] = jnp.concatenate(lses, axis=1)

    n_tiles = q_ref.shape[0] * tiles_per_residue
    _pipelined_attention(n_tiles // DIL_UNROLL, DIL_UNROLL, scores, probs, finish, s_scr, p_scr, (m_scr,))


def _dil_attention(q, k, v, group):
    batch, dilation, seq_len, width = q.shape
    stretch = min(seq_len, DIL_STEP_ROWS)
    res_blk = DIL_STEP_ROWS // stretch
    assert dilation % res_blk == 0 and seq_len % stretch == 0
    qmap = lambda b, r, s: (b, r, s, 0)
    kvmap = lambda b, r, s: (b, r, 0, 0)
    bias_shape = (DIL_VARIANTS, NB_HEADS_PER_GROUP * DIL_TQ, DIL_TK)
    tile = (2, DIL_UNROLL, NB_HEADS_PER_GROUP * DIL_TQ, DIL_TK)
    stat = tile[:3] + (V7X_LANES,)
    kv_buffers = 2
    kv_spec = pl.BlockSpec((None, res_blk, seq_len, width), kvmap)
    blocks = (2 * (2 * _nbytes((DIL_STEP_ROWS, width), BF16) + _nbytes((DIL_STEP_ROWS, width), F32))
              + kv_buffers * 2 * _nbytes((res_blk * seq_len, width), BF16))
    scratch = _nbytes(bias_shape, F32) + _nbytes(tile, F32) + _nbytes(tile, BF16) + _nbytes(stat, F32)
    temps = DIL_UNROLL * _nbytes(tile[2:], F32)
    return pl.pallas_call(
        functools.partial(_dil_kernel, seq_len=seq_len, stretch=stretch, dilation=dilation),
        grid=(batch, dilation // res_blk, seq_len // stretch),
        in_specs=[
            pl.BlockSpec((None, res_blk, stretch, width), qmap),
            kv_spec,
            kv_spec,
            pl.BlockSpec(memory_space=pltpu.SMEM),
        ],
        out_specs=[pl.BlockSpec((None, res_blk, stretch, width), qmap)] * 2,
        out_shape=[jax.ShapeDtypeStruct(q.shape, BF16), jax.ShapeDtypeStruct(q.shape, F32)],
        scratch_shapes=[pltpu.VMEM(bias_shape, F32), pltpu.VMEM(tile, F32), pltpu.VMEM(tile, BF16),
                        pltpu.VMEM(stat, F32)],
        compiler_params=pltpu.CompilerParams(
            dimension_semantics=("arbitrary", "arbitrary", "arbitrary"),
            vmem_limit_bytes=_vmem_limit(blocks, scratch, temps),
        ),
        name=f"dil_attn_g{group}",
    )(q, k, v, _dil_slopes(group))


MERGE_TM = 1024
MERGE_CN = 256


def _to_token_order(src_ref, scr_ref, tmp_ref, dil, k):
    per_res = SUB_TM // dil
    res_rows = slice(k * per_res, (k + 1) * per_res)
    if dil == 1:
        return src_ref[0, res_rows].astype(F32)
    rows = slice(k * SUB_TM, (k + 1) * SUB_TM)
    if dil == SPLIT_STRIDE ** 2:
        quarter = SUB_TM // SPLIT_STRIDE
        for r in range(dil):
            start = k * SUB_TM + (r % SPLIT_STRIDE) * quarter + r // SPLIT_STRIDE
            for c in range(GROUP_LANE_BLOCKS):
                tmp_ref[c, pl.ds(start, per_res, stride=SPLIT_STRIDE), :] = (
                    src_ref[r, res_rows, c * V7X_LANES:(c + 1) * V7X_LANES].astype(F32))
        for c in range(GROUP_LANE_BLOCKS):
            for r4 in range(SPLIT_STRIDE):
                lo = k * SUB_TM + r4 * quarter
                scr_ref[c, pl.ds(k * SUB_TM + r4, quarter, stride=SPLIT_STRIDE), :] = tmp_ref[c, lo:lo + quarter]
        return jnp.concatenate([scr_ref[c, rows] for c in range(GROUP_LANE_BLOCKS)], axis=-1)
    for r in range(dil):
        for c in range(GROUP_LANE_BLOCKS):
            scr_ref[c, pl.ds(k * SUB_TM + r, per_res, stride=dil), :] = (
                src_ref[r, res_rows, c * V7X_LANES:(c + 1) * V7X_LANES].astype(F32))
    return jnp.concatenate([scr_ref[c, rows] for c in range(GROUP_LANE_BLOCKS)], axis=-1)


def _merge_kernel(x_ref, pre_g_ref, wgate_ref, oa_ref, wa_ref, *refs):
    ng = NB_GROUPS
    o_refs, lse_refs = refs[:ng], refs[ng:2 * ng]
    wb_ref, wout_ref, post_g_ref, out_ref, h_ref, ob_ref, mg_ref = refs[2 * ng:2 * ng + 7]
    scr_refs = iter(refs[2 * ng + 7:])
    o_scr = [next(scr_refs) if dil > 1 else None for dil in DILATIONS]
    lse_scr = [next(scr_refs) if dil > 1 else None for dil in DILATIONS]
    o_tmp = [next(scr_refs) if dil == SPLIT_STRIDE ** 2 else None for dil in DILATIONS]
    lse_tmp = [next(scr_refs) if dil == SPLIT_STRIDE ** 2 else None for dil in DILATIONS]
    subs = _sub_tiles(x_ref.shape[0])
    for k, rows in enumerate(subs):
        h_ref[rows] = _rmsnorm_f32(x_ref[rows], pre_g_ref[...]).astype(BF16)
        os_ = [_to_token_order(r, s, t, d, k) for r, s, t, d in zip(o_refs, o_scr, o_tmp, DILATIONS)]
        lses = [_to_token_order(r, s, t, d, k) for r, s, t, d in zip(lse_refs, lse_scr, lse_tmp, DILATIONS)]
        mx = jnp.maximum(jnp.maximum(lses[0], lses[1]), lses[2])
        es = [jnp.exp(l - mx) for l in lses]
        inv = 1.0 / (es[0] + es[1] + es[2])
        for g in range(ng):
            ob_ref[rows, g * GROUP_WIDTH:(g + 1) * GROUP_WIDTH] = (os_[g] * (es[g] * inv)).astype(BF16)
    for rows in subs:
        for n in range(D_MODEL // MERGE_CN):
            cols = slice(n * MERGE_CN, (n + 1) * MERGE_CN)
            cols_b = slice(D_MODEL + n * MERGE_CN, D_MODEL + (n + 1) * MERGE_CN)
            h = h_ref[rows]
            ga = _dot(h, wgate_ref[:, cols].astype(BF16))
            gb = _dot(h, wgate_ref[:, cols_b].astype(BF16))
            ya = _dot(oa_ref[rows], wa_ref[:, cols])
            yb = _dot(ob_ref[rows], wb_ref[:, cols])
            mg_ref[rows, cols] = (jax.nn.sigmoid(ga) * ya + jax.nn.sigmoid(gb) * yb).astype(BF16)
    for rows in subs:
        out_ref[rows] = _dot(mg_ref[rows], wout_ref[...])
    for rows in subs:
        out_ref[rows] = x_ref[rows] + _rmsnorm_f32(out_ref[rows], post_g_ref[...])


def _merge(x, pre_g, wgate, oa, wa, obs, lses, wb, wout, post_g):
    batch, seq, d = x.shape
    tm = MERGE_TM
    tok = lambda b, i: (b, i, 0)
    res = lambda b, i: (b, 0, i, 0)
    tokspec = lambda w: pl.BlockSpec((None, tm, w), tok)
    resspecs = [pl.BlockSpec((None, dil, tm // dil, GROUP_WIDTH), res) for dil in DILATIONS]
    blocks = (2 * 2 * _nbytes((tm, d), F32) + 2 * _nbytes((tm, NA_WIDTH), BF16)
              + 2 * 3 * (_nbytes((tm, GROUP_WIDTH), BF16) + _nbytes((tm, GROUP_WIDTH), F32))
              + _nbytes((NA_WIDTH, d), wa.dtype) + _nbytes((d + NB_WIDTH, d), wb.dtype) + _nbytes((d, 2 * d), F32))
    work = [pltpu.VMEM((tm, d), BF16), pltpu.VMEM((tm, NB_WIDTH), BF16), pltpu.VMEM((tm, d), BF16)]
    n_interleave = 2 * sum(dil > 1 for dil in DILATIONS) + 2 * sum(dil == SPLIT_STRIDE ** 2 for dil in DILATIONS)
    scratch = (n_interleave * _nbytes((tm, GROUP_WIDTH), F32) + 2 * _nbytes((tm, d), BF16)
               + _nbytes((tm, NB_WIDTH), BF16))
    temps = 2 * _nbytes((SUB_TM, d), F32)
    return pl.pallas_call(
        _merge_kernel,
        grid=(batch, seq // tm),
        in_specs=[
            tokspec(d), _resident((1, d)),
            pl.BlockSpec((pl.Element(d), pl.Element(2 * d)), lambda *_: (0, QKV_TOTAL),
                         pipeline_mode=pl.Buffered(1)),
            tokspec(NA_WIDTH), _resident((NA_WIDTH, d)),
            *resspecs, *resspecs,
            _resident((NB_WIDTH, d)), _resident((d, d)), _resident((1, d)),
        ],
        out_specs=tokspec(d),
        out_shape=jax.ShapeDtypeStruct((batch, seq, d), F32),
        scratch_shapes=work + [pltpu.VMEM((GROUP_LANE_BLOCKS, tm, V7X_LANES), F32)] * n_interleave,
        compiler_params=pltpu.CompilerParams(
            dimension_semantics=("arbitrary", "arbitrary"),
            vmem_limit_bytes=_vmem_limit(blocks, scratch, temps),
        ),
        name="merge",
    )(x, pre_g, wgate, oa, wa, *obs, *lses, wb, wout, post_g)


def kernel(x, ffn1_pre_g, ffn1_w_gate, ffn1_w_up, ffn1_w_down, ffn1_post_g, mix_pre_g, w_in, na_rpb, w_branch_a, w_branch_b, w_out, mix_post_g, ffn2_pre_g, ffn2_w_gate, ffn2_w_up, ffn2_w_down, ffn2_post_g):
    batch, seq, d = x.shape
    depth = ffn1_pre_g.shape[0]
    for window, dilation in DIL_PAIRS:
        assert window // (2 * dilation) == DIL_HALF
    for l in range(depth):
        x = _ffn(x.reshape(batch * seq, d), ffn1_pre_g[l][None], ffn1_w_gate[l], ffn1_w_up[l],
                 ffn1_w_down[l], ffn1_post_g[l][None])
        x = x.reshape(batch, seq, d)
        qa, ka, va, qb, kb, vb = _qkv(x, mix_pre_g[l][None], w_in[l])
        oa, (wout, wb, wa) = _nbr_attention(qa, ka, va, na_rpb[l], (w_out[l], w_branch_b[l], w_branch_a[l]))
        obs, lses = [], []
        for g in range(NB_GROUPS):
            o_g, lse_g = _dil_attention(qb[g], kb[g], vb[g], g)
            obs.append(o_g)
            lses.append(lse_g)
        x = _merge(x, mix_pre_g[l][None], w_in[l], oa, wa, obs, lses, wb, wout, mix_post_g[l][None])
        x = _ffn(x.reshape(batch * seq, d), ffn2_pre_g[l][None], ffn2_w_gate[l], ffn2_w_up[l],
                 ffn2_w_down[l], ffn2_post_g[l][None])
        x = x.reshape(batch, seq, d)
    return x
```

```python
import functools
import math

import jax
import jax.numpy as jnp
from jax import lax
from jax.experimental import pallas as pl
from jax.experimental.pallas import tpu as pltpu

D_MODEL = 1024
HEAD_DIM = 64
NA_HEADS = 8
NA_WIDTH = NA_HEADS * HEAD_DIM
NA_ROWS = 8
NA_COLS = 16
GRID_W = 64
DIL_PAIRS = ((128, 1), (512, 4), (2048, 16))
DILATIONS = tuple(d for _, d in DIL_PAIRS)
NB_GROUPS = len(DIL_PAIRS)
NB_HEADS_PER_GROUP = 4
NB_HEADS = NB_GROUPS * NB_HEADS_PER_GROUP
NB_WIDTH = NB_HEADS * HEAD_DIM
GROUP_WIDTH = NB_HEADS_PER_GROUP * HEAD_DIM
ALIBI_MAX_EXP = 8.0
D_FF = 2816
NORM_EPS = 1e-6
ATTN_SCALE = HEAD_DIM ** -0.5
LOG2E = math.log2(math.e)
LN2 = math.log(2.0)
Q_SCALE = ATTN_SCALE * LOG2E

V7X_LANES = 128
GROUP_LANE_BLOCKS = GROUP_WIDTH // V7X_LANES
V7X_VMEM_BYTES = 64 * 1024 * 1024
V7X_VMEM_RESERVE = 6 * 1024 * 1024

BF16 = jnp.bfloat16
F32 = jnp.float32


def _vmem_limit(block_bytes, scratch_bytes, temp_bytes):
    need = block_bytes + scratch_bytes + temp_bytes
    budget = V7X_VMEM_BYTES - V7X_VMEM_RESERVE
    assert need <= budget, need
    return budget


def _nbytes(shape, dtype):
    return math.prod(shape) * jnp.dtype(dtype).itemsize


def _resident(shape):
    nd = len(shape)
    return pl.BlockSpec(shape, lambda *_: (0,) * nd, pipeline_mode=pl.Buffered(1))


def _rmsnorm_f32(x, g):
    return x * lax.rsqrt(jnp.mean(x * x, axis=-1, keepdims=True) + NORM_EPS) * g


def _dot(a, b):
    return jnp.dot(a, b, preferred_element_type=F32)


def _dot_nt(a, b):
    return lax.dot_general(a, b, (((1,), (1,)), ((), ())), preferred_element_type=F32)


def _pipelined_attention(n_groups, unroll, scores, probs, finish, s_scr, p_scr, stat_scrs):
    assert n_groups % 2 == 0 and n_groups >= 2
    tiles = lambda i: [i * unroll + u for u in range(unroll)]

    def stage_scores(i, slot):
        for u, t in enumerate(tiles(i)):
            s_scr[slot, u] = scores(t)

    def stage_probs(slot):
        for u in range(unroll):
            p, *stats = probs(s_scr[slot, u])
            p_scr[slot, u] = p
            for ref, stat in zip(stat_scrs, stats):
                ref[slot, u] = stat

    def stage_finish(i, slot):
        for u, t in enumerate(tiles(i)):
            finish(t, p_scr[slot, u], *[ref[slot, u] for ref in stat_scrs])

    def step(i, parity):
        stage_scores(i + 1, 1 - parity)
        stage_finish(i - 1, 1 - parity)
        stage_probs(parity)

    stage_scores(0, 0)
    stage_scores(1, 1)
    stage_probs(0)

    def body(i, carry):
        @pl.when(i % 2 == 1)
        def _():
            step(i, 1)

        @pl.when(i % 2 == 0)
        def _():
            step(i, 0)

        return carry

    lax.fori_loop(1, n_groups - 1, body, 0)
    stage_finish(n_groups - 2, 0)
    stage_probs(1)
    stage_finish(n_groups - 1, 1)


FFN_TM = 512
SUB_TM = 512
FFN_CK = 256


def _sub_tiles(tm):
    return [slice(k * SUB_TM, (k + 1) * SUB_TM) for k in range(tm // SUB_TM)]


FFN_CHUNKS = [slice(c * FFN_CK, (c + 1) * FFN_CK) for c in range(D_FF // FFN_CK)]


def _ffn_weight_copies(wg_hbm, wu_hbm, wd_hbm, wg_ref, wu_ref, wd_ref, sem):
    gate_up = [(pltpu.make_async_copy(wg_hbm.at[:, cols], wg_ref.at[:, cols], sem.at[0, c]),
                pltpu.make_async_copy(wu_hbm.at[:, cols], wu_ref.at[:, cols], sem.at[1, c]))
               for c, cols in enumerate(FFN_CHUNKS)]
    down = [pltpu.make_async_copy(wd_hbm.at[cols, :], wd_ref.at[cols, :], sem.at[2, c])
            for c, cols in enumerate(FFN_CHUNKS)]
    return gate_up, down


def _ffn_body(x_ref, pre_g_ref, wg_ref, wu_ref, wd_ref, post_g_ref, o_ref, h_ref, a_ref, wait_gate_up, wait_down):
    h_ref[...] = _rmsnorm_f32(x_ref[...], pre_g_ref[...]).astype(BF16)
    for c, cols in enumerate(FFN_CHUNKS):
        wait_gate_up(c)
        h = h_ref[...]
        g = _dot(h, wg_ref[:, cols].astype(BF16))
        u = _dot(h, wu_ref[:, cols].astype(BF16))
        a_ref[:, cols] = (g * jax.nn.sigmoid(g) * u).astype(BF16)
    f = None
    for c, cols in enumerate(FFN_CHUNKS):
        wait_down(c)
        part = _dot(a_ref[:, cols], wd_ref[cols, :].astype(BF16))
        f = part if f is None else f + part
    o_ref[...] = x_ref[...] + 0.5 * _rmsnorm_f32(f, post_g_ref[...])


def _ffn_kernel(x_ref, pre_g_ref, wg_hbm, wu_hbm, wd_hbm, post_g_ref, o_ref, h_ref, a_ref,
                wg_ref, wu_ref, wd_ref, sem):
    refs = (x_ref, pre_g_ref, wg_ref, wu_ref, wd_ref, post_g_ref, o_ref, h_ref, a_ref)
    first = pl.program_id(0) == 0

    @pl.when(jnp.logical_not(first))
    def _():
        _ffn_body(*refs, lambda c: None, lambda c: None)

    @pl.when(first)
    def _():
        gate_up, down = _ffn_weight_copies(wg_hbm, wu_hbm, wd_hbm, wg_ref, wu_ref, wd_ref, sem)
        for cg, cu in gate_up:
            cg.start()
            cu.start()
        for cd in down:
            cd.start()

        def wait_gate_up(c):
            gate_up[c][0].wait()
            gate_up[c][1].wait()

        _ffn_body(*refs, wait_gate_up, lambda c: down[c].wait())


def _ffn(x, pre_g, wg, wu, wd, post_g):
    n, d = x.shape
    tm = FFN_TM
    row = lambda i: (i, 0)
    n_chunks = len(FFN_CHUNKS)
    blocks = 2 * 2 * _nbytes((tm, d), F32) + 2 * _nbytes((1, d), F32)
    scratch = _nbytes((tm, d), BF16) + _nbytes((tm, D_FF), BF16) + 3 * _nbytes((d, D_FF), wg.dtype)
    temps = 4 * _nbytes((tm, d), F32)
    hbm = pl.BlockSpec(memory_space=pl.ANY)
    return pl.pallas_call(
        _ffn_kernel,
        grid=(n // tm,),
        in_specs=[pl.BlockSpec((tm, d), row), _resident((1, d)), hbm, hbm, hbm, _resident((1, d))],
        out_specs=pl.BlockSpec((tm, d), row),
        out_shape=jax.ShapeDtypeStruct((n, d), F32),
        scratch_shapes=[
            pltpu.VMEM((tm, d), BF16), pltpu.VMEM((tm, D_FF), BF16),
            pltpu.VMEM((d, D_FF), wg.dtype), pltpu.VMEM((d, D_FF), wu.dtype), pltpu.VMEM((D_FF, d), wd.dtype),
            pltpu.SemaphoreType.DMA((3, n_chunks)),
        ],
        compiler_params=pltpu.CompilerParams(
            dimension_semantics=("arbitrary",),
            vmem_limit_bytes=_vmem_limit(blocks, scratch, temps),
        ),
        name="ffn",
    )(x, pre_g, wg, wu, wd, post_g)


QKV_TM = 1024
QKV_TOTAL = 3 * NA_WIDTH + 3 * NB_WIDTH
N_DIL_OUTS = 3 * NB_GROUPS
STAGED_OUTS = [i for i in range(N_DIL_OUTS) if DILATIONS[i % NB_GROUPS] > 1]
N_STAGED_OUTS = len(STAGED_OUTS)
SPLIT_STRIDE = 4
SPLIT_OUTS = [i for i in STAGED_OUTS if DILATIONS[i % NB_GROUPS] == SPLIT_STRIDE ** 2]
N_SPLIT_OUTS = len(SPLIT_OUTS)


def _qkv_kernel(x_ref, g_ref, w32_ref, *refs):
    na_refs, dil_refs = refs[:3], refs[3:3 + N_DIL_OUTS]
    w_ref = refs[3 + N_DIL_OUTS]
    n_stage = N_STAGED_OUTS * (x_ref.shape[0] // SUB_TM)
    y_refs, t_refs = refs[4 + N_DIL_OUTS:4 + N_DIL_OUTS + n_stage], refs[4 + N_DIL_OUTS + n_stage:]

    @pl.when((pl.program_id(0) == 0) & (pl.program_id(1) == 0))
    def _():
        for c in range(QKV_TOTAL // GROUP_WIDTH):
            cols = slice(c * GROUP_WIDTH, (c + 1) * GROUP_WIDTH)
            w_ref[:, cols] = w32_ref[:, cols].astype(BF16)

    for k, rows in enumerate(_sub_tiles(x_ref.shape[0])):
        h = _rmsnorm_f32(x_ref[rows], g_ref[...]).astype(BF16)
        for idx, o_ref in enumerate(dil_refs):
            which, g = divmod(idx, NB_GROUPS)
            dil = DILATIONS[g]
            off = 3 * NA_WIDTH + which * NB_WIDTH + g * GROUP_WIDTH
            y = _dot(h, w_ref[:, off:off + GROUP_WIDTH])
            if which == 0:
                y = y * Q_SCALE
            if dil == 1:
                o_ref[0, rows] = y.astype(BF16)
                continue
            y_ref = y_refs[k * N_STAGED_OUTS + STAGED_OUTS.index(idx)]
            per_res = SUB_TM // dil
            for c in range(GROUP_LANE_BLOCKS):
                y_ref[c] = y[:, c * V7X_LANES:(c + 1) * V7X_LANES]
            if dil == SPLIT_STRIDE ** 2:
                t_ref = t_refs[k * N_SPLIT_OUTS + SPLIT_OUTS.index(idx)]
                quarter = SUB_TM // SPLIT_STRIDE
                for c in range(GROUP_LANE_BLOCKS):
                    for r4 in range(SPLIT_STRIDE):
                        t_ref[c, r4 * quarter:(r4 + 1) * quarter] = y_ref[c, pl.ds(r4, quarter, stride=SPLIT_STRIDE), :]
                for r in range(dil):
                    start = (r % SPLIT_STRIDE) * quarter + r // SPLIT_STRIDE
                    for c in range(GROUP_LANE_BLOCKS):
                        o_ref[r, k * per_res:(k + 1) * per_res, c * V7X_LANES:(c + 1) * V7X_LANES] = (
                            t_ref[c, pl.ds(start, per_res, stride=SPLIT_STRIDE), :].astype(BF16))
                continue
            for r in range(dil):
                for c in range(GROUP_LANE_BLOCKS):
                    o_ref[r, k * per_res:(k + 1) * per_res, c * V7X_LANES:(c + 1) * V7X_LANES] = (
                        y_ref[c, pl.ds(r, per_res, stride=dil), :].astype(BF16))
        for idx, o_ref in enumerate(na_refs):
            y = _dot(h, w_ref[:, idx * NA_WIDTH:(idx + 1) * NA_WIDTH])
            if idx == 0:
                y = y * Q_SCALE
            o_ref[rows] = y.astype(BF16)


def _qkv(x, g, w_in):
    batch, seq, d = x.shape
    tm = QKV_TM
    tok = lambda b, i: (b, i, 0)
    res = lambda b, i: (b, 0, i, 0)
    out_specs = [pl.BlockSpec((None, tm, NA_WIDTH), tok)] * 3
    out_shape = [jax.ShapeDtypeStruct((batch, seq, NA_WIDTH), BF16)] * 3
    for _ in range(3):
        for dil in DILATIONS:
            out_specs.append(pl.BlockSpec((None, dil, tm // dil, GROUP_WIDTH), res))
            out_shape.append(jax.ShapeDtypeStruct((batch, dil, seq // dil, GROUP_WIDTH), BF16))
    blocks = 2 * _nbytes((tm, d), F32) + _nbytes((d, QKV_TOTAL), w_in.dtype) + 2 * _nbytes((tm, QKV_TOTAL), BF16)
    stage = (GROUP_LANE_BLOCKS, SUB_TM, V7X_LANES)
    n_stage = N_STAGED_OUTS * (tm // SUB_TM)
    n_split = N_SPLIT_OUTS * (tm // SUB_TM)
    scratch = (n_stage + n_split) * _nbytes(stage, F32) + _nbytes((d, QKV_TOTAL), BF16)
    temps = _nbytes((SUB_TM, d), F32)
    outs = pl.pallas_call(
        _qkv_kernel,
        grid=(batch, seq // tm),
        in_specs=[pl.BlockSpec((None, tm, d), tok), _resident((1, d)), _resident((d, QKV_TOTAL))],
        out_specs=out_specs,
        out_shape=out_shape,
        scratch_shapes=[pltpu.VMEM((d, QKV_TOTAL), BF16)] + [pltpu.VMEM(stage, F32)] * (n_stage + n_split),
        compiler_params=pltpu.CompilerParams(
            dimension_semantics=("arbitrary", "arbitrary"),
            vmem_limit_bytes=_vmem_limit(blocks, scratch, temps),
        ),
        name="qkv",
    )(x, g, w_in)
    qa, ka, va = outs[:3]
    qb, kb, vb = (outs[3 + w * NB_GROUPS:3 + (w + 1) * NB_GROUPS] for w in range(3))
    return qa, ka, va, qb, kb, vb


NA_HEADS_PER_STEP = V7X_LANES // HEAD_DIM
NA_SPAN = NA_ROWS * GRID_W
NA_VARIANTS = NA_ROWS
NA_RPB_ROWS = 2 * NA_ROWS - 1
NA_RPB_COLS = 2 * NA_COLS - 1
NA_UNROLL = 32


NA_RPB_PAIRS = (NA_RPB_ROWS + 1) // 2


def _nbr_build_bias(rep_ref, tbl_ref):
    width = NA_RPB_PAIRS * V7X_LANES
    qc = lax.broadcasted_iota(jnp.int32, (GRID_W, width), 0)
    kc = lax.broadcasted_iota(jnp.int32, (GRID_W, width), 1) % GRID_W
    col0 = jnp.clip(qc - NA_COLS // 2, 0, GRID_W - NA_COLS)
    col_ok = (kc >= col0) & (kc < col0 + NA_COLS)
    for h in range(NA_HEADS_PER_STEP):
        blocks = []
        for p in range(NA_RPB_PAIRS):
            row = jnp.broadcast_to(rep_ref[h, p:p + 1, :], (GRID_W, V7X_LANES))
            blocks.append(pltpu.roll(row, V7X_LANES - (NA_COLS - 1), 1, stride=1, stride_axis=0))
        w = jnp.where(col_ok, jnp.concatenate(blocks, axis=1) * LOG2E, -jnp.inf)
        for variant in range(NA_VARIANTS):
            tbl_ref[h, variant] = w[:, variant * GRID_W:variant * GRID_W + NA_SPAN]


def _nbr_kernel(q_ref, k_ref, v_ref, rep_ref, o_ref, tbl_ref, s_scr, p_scr, *, rows):
    hp = NA_HEADS_PER_STEP
    ones = jnp.ones((NA_SPAN, V7X_LANES), BF16)

    @pl.when(pl.program_id(1) == 0)
    def _():
        _nbr_build_bias(rep_ref, tbl_ref)

    lane_head = lax.broadcasted_iota(jnp.int32, (GRID_W, V7X_LANES), 1) // HEAD_DIM

    def scores(r):
        row0 = jnp.clip(r - NA_ROWS // 2, 0, rows - NA_ROWS)
        variant = row0 - r + (NA_ROWS - 1)
        q = q_ref[pl.ds(pl.multiple_of(r * GRID_W, GRID_W), GRID_W), :]
        kw = k_ref[pl.ds(pl.multiple_of(row0 * GRID_W, GRID_W), NA_SPAN), :]
        qs = jnp.concatenate([jnp.where(lane_head == h, q, jnp.zeros_like(q)) for h in range(hp)], axis=0)
        return _dot_nt(qs, kw) + tbl_ref[:, variant].reshape(hp * GRID_W, NA_SPAN)

    def probs(s):
        return (jnp.exp2(s - jnp.max(s, axis=-1, keepdims=True)).astype(BF16),)

    def finish(r, p):
        row0 = jnp.clip(r - NA_ROWS // 2, 0, rows - NA_ROWS)
        vw = v_ref[pl.ds(pl.multiple_of(row0 * GRID_W, GRID_W), NA_SPAN), :]
        o = _dot(p, jnp.concatenate([vw, ones], axis=1))
        out, den = o[:GRID_W, :V7X_LANES], o[:GRID_W, V7X_LANES:]
        for h in range(1, hp):
            rows_h = slice(h * GRID_W, (h + 1) * GRID_W)
            out = jnp.where(lane_head == h, o[rows_h, :V7X_LANES], out)
            den = jnp.where(lane_head == h, o[rows_h, V7X_LANES:], den)
        o_ref[pl.ds(pl.multiple_of(r * GRID_W, GRID_W), GRID_W), :] = (out * (1.0 / den)).astype(BF16)

    _pipelined_attention(rows // NA_UNROLL, NA_UNROLL, scores, probs, finish, s_scr, p_scr, ())


def _nbr_attention(q, k, v, rpb):
    batch, seq, width = q.shape
    rows = seq // GRID_W
    n_pairs = width // V7X_LANES
    rep = jnp.pad(rpb.astype(F32), ((0, 0), (0, 2 * NA_RPB_PAIRS - NA_RPB_ROWS), (0, GRID_W - NA_RPB_COLS)))
    rep = rep.reshape(rpb.shape[0], NA_RPB_PAIRS, V7X_LANES)
    tok = lambda p, b: (b, 0, p)
    tbl_shape = (NA_HEADS_PER_STEP, NA_VARIANTS, GRID_W, NA_SPAN)
    rep_block = (NA_HEADS_PER_STEP, NA_RPB_PAIRS, V7X_LANES)
    blocks = 2 * 4 * _nbytes((seq, V7X_LANES), BF16) + 2 * _nbytes(rep_block, F32)
    tile = (2, NA_UNROLL, NA_HEADS_PER_STEP * GRID_W, NA_SPAN)
    scratch = _nbytes(tbl_shape, F32) + _nbytes(tile, F32) + _nbytes(tile, BF16)
    temps = NA_UNROLL * _nbytes(tile[2:], F32)
    return pl.pallas_call(
        functools.partial(_nbr_kernel, rows=rows),
        grid=(n_pairs, batch),
        in_specs=[
            pl.BlockSpec((None, seq, V7X_LANES), tok),
            pl.BlockSpec((None, seq, V7X_LANES), tok),
            pl.BlockSpec((None, seq, V7X_LANES), tok),
            pl.BlockSpec(rep_block, lambda p, b: (p, 0, 0)),
        ],
        out_specs=pl.BlockSpec((None, seq, V7X_LANES), tok),
        out_shape=jax.ShapeDtypeStruct((batch, seq, width), BF16),
        scratch_shapes=[pltpu.VMEM(tbl_shape, F32), pltpu.VMEM(tile, F32), pltpu.VMEM(tile, BF16)],
        compiler_params=pltpu.CompilerParams(
            dimension_semantics=("arbitrary", "arbitrary"),
            vmem_limit_bytes=_vmem_limit(blocks, scratch, temps),
        ),
        name="nbr_attn",
    )(q, k, v, rep)


DIL_HALF = 64
DIL_TQ = 128
DIL_TK = DIL_TQ + 2 * DIL_HALF
DIL_VARIANTS = 3
DIL_STEP_ROWS = 4096
DIL_UNROLL = 8


def _dil_slopes(group):
    heads = jnp.arange(NB_HEADS_PER_GROUP, dtype=F32) + group * NB_HEADS_PER_GROUP
    return jnp.exp2(-ALIBI_MAX_EXP * (heads + 1.0) / NB_HEADS)


def _dil_build_bias(slope_ref, bias_ref, dilation):
    qq = lax.broadcasted_iota(jnp.int32, (DIL_TQ, DIL_TK), 0)
    kk = lax.broadcasted_iota(jnp.int32, (DIL_TQ, DIL_TK), 1)
    for variant in range(DIL_VARIANTS):
        rel = kk - qq - variant * DIL_HALF
        dist = (dilation * jnp.abs(rel)).astype(F32)
        in_band = jnp.abs(rel) <= DIL_HALF
        for h in range(NB_HEADS_PER_GROUP):
            bias = -(slope_ref[h] * dist) * LOG2E
            bias_ref[variant, h * DIL_TQ:(h + 1) * DIL_TQ, :] = jnp.where(in_band, bias, -jnp.inf)


def _dil_kernel(q_ref, k_ref, v_ref, slope_ref, o_ref, lse_ref, bias_ref, s_scr, p_scr, m_scr,
                *, seq_len, stretch, dilation):
    nh = NB_HEADS_PER_GROUP
    hpb = V7X_LANES // HEAD_DIM
    ones = jnp.ones((DIL_TK, V7X_LANES), BF16)

    @pl.when((pl.program_id(0) == 0) & (pl.program_id(1) == 0) & (pl.program_id(2) == 0))
    def _():
        _dil_build_bias(slope_ref, bias_ref, dilation)

    group_lane_head = lax.broadcasted_iota(jnp.int32, (DIL_TQ, GROUP_WIDTH), 1) // HEAD_DIM
    lane_head = lax.broadcasted_iota(jnp.int32, (DIL_TQ, V7X_LANES), 1) // HEAD_DIM
    base = pl.program_id(2) * stretch
    tiles_per_residue = stretch // DIL_TQ

    def window(t):
        r, j = t // tiles_per_residue, t % tiles_per_residue
        qloc = pl.multiple_of(j * DIL_TQ, DIL_TQ)
        qs = base + qloc
        ws = jnp.clip(qs - DIL_HALF, 0, seq_len - DIL_TK)
        variant = (qs - ws) // DIL_HALF
        return r, qloc, pl.multiple_of(ws, DIL_HALF), variant

    def scores(t):
        r, qloc, ws, variant = window(t)
        q = q_ref[r, pl.ds(qloc, DIL_TQ), :]
        kw = k_ref[r, pl.ds(ws, DIL_TK), :]
        qst = jnp.concatenate([jnp.where(group_lane_head == h, q, jnp.zeros_like(q)) for h in range(nh)],
                              axis=0)
        return _dot_nt(qst, kw) + bias_ref[variant]

    def probs(s):
        m = jnp.max(s, axis=-1, keepdims=True)
        return jnp.exp2(s - m).astype(BF16), jnp.broadcast_to(m, (s.shape[0], V7X_LANES))

    def finish(t, p, m):
        r, qloc, ws, _ = window(t)
        vw = v_ref[r, pl.ds(ws, DIL_TK), :]
        outs, lses = [], []
        for c in range(GROUP_LANE_BLOCKS):
            rows_c = slice(c * hpb * DIL_TQ, (c + 1) * hpb * DIL_TQ)
            v_aug = jnp.concatenate([vw[:, c * V7X_LANES:(c + 1) * V7X_LANES], ones], axis=1)
            o = _dot(p[rows_c], v_aug)
            m_c = m[rows_c]
            out_c, den_c, max_c = o[:DIL_TQ, :V7X_LANES], o[:DIL_TQ, V7X_LANES:], m_c[:DIL_TQ]
            for h in range(1, hpb):
                rows_h = slice(h * DIL_TQ, (h + 1) * DIL_TQ)
                out_c = jnp.where(lane_head == h, o[rows_h, :V7X_LANES], out_c)
                den_c = jnp.where(lane_head == h, o[rows_h, V7X_LANES:], den_c)
                max_c = jnp.where(lane_head == h, m_c[rows_h], max_c)
            outs.append(out_c * (1.0 / den_c))
            lses.append(max_c * LN2 + jnp.log(den_c))
        o_ref[r, pl.ds(qloc, DIL_TQ), :] = jnp.concatenate(outs, axis=1).astype(BF16)
        lse_ref[r, pl.ds(qloc, DIL_TQ), :] = jnp.concatenate(lses, axis=1)

    n_tiles = q_ref.shape[0] * tiles_per_residue
    _pipelined_attention(n_tiles // DIL_UNROLL, DIL_UNROLL, scores, probs, finish, s_scr, p_scr, (m_scr,))


def _dil_attention(q, k, v, group):
    batch, dilation, seq_len, width = q.shape
    stretch = min(seq_len, DIL_STEP_ROWS)
    res_blk = DIL_STEP_ROWS // stretch
    assert dilation % res_blk == 0 and seq_len % stretch == 0
    qmap = lambda b, r, s: (b, r, s, 0)
    kvmap = lambda b, r, s: (b, r, 0, 0)
    bias_shape = (DIL_VARIANTS, NB_HEADS_PER_GROUP * DIL_TQ, DIL_TK)
    tile = (2, DIL_UNROLL, NB_HEADS_PER_GROUP * DIL_TQ, DIL_TK)
    stat = tile[:3] + (V7X_LANES,)
    kv_buffers = 2
    kv_spec = pl.BlockSpec((None, res_blk, seq_len, width), kvmap)
    blocks = (2 * (2 * _nbytes((DIL_STEP_ROWS, width), BF16) + _nbytes((DIL_STEP_ROWS, width), F32))
              + kv_buffers * 2 * _nbytes((res_blk * seq_len, width), BF16))
    scratch = _nbytes(bias_shape, F32) + _nbytes(tile, F32) + _nbytes(tile, BF16) + _nbytes(stat, F32)
    temps = DIL_UNROLL * _nbytes(tile[2:], F32)
    return pl.pallas_call(
        functools.partial(_dil_kernel, seq_len=seq_len, stretch=stretch, dilation=dilation),
        grid=(batch, dilation // res_blk, seq_len // stretch),
        in_specs=[
            pl.BlockSpec((None, res_blk, stretch, width), qmap),
            kv_spec,
            kv_spec,
            pl.BlockSpec(memory_space=pltpu.SMEM),
        ],
        out_specs=[pl.BlockSpec((None, res_blk, stretch, width), qmap)] * 2,
        out_shape=[jax.ShapeDtypeStruct(q.shape, BF16), jax.ShapeDtypeStruct(q.shape, F32)],
        scratch_shapes=[pltpu.VMEM(bias_shape, F32), pltpu.VMEM(tile, F32), pltpu.VMEM(tile, BF16),
                        pltpu.VMEM(stat, F32)],
        compiler_params=pltpu.CompilerParams(
            dimension_semantics=("arbitrary", "arbitrary", "arbitrary"),
            vmem_limit_bytes=_vmem_limit(blocks, scratch, temps),
        ),
        name=f"dil_attn_g{group}",
    )(q, k, v, _dil_slopes(group))


MERGE_TM = 1024
MERGE_CN = 256


def _to_token_order(src_ref, scr_ref, tmp_ref, dil, k):
    per_res = SUB_TM // dil
    res_rows = slice(k * per_res, (k + 1) * per_res)
    if dil == 1:
        return src_ref[0, res_rows].astype(F32)
    rows = slice(k * SUB_TM, (k + 1) * SUB_TM)
    if dil == SPLIT_STRIDE ** 2:
        quarter = SUB_TM // SPLIT_STRIDE
        for r in range(dil):
            start = k * SUB_TM + (r % SPLIT_STRIDE) * quarter + r // SPLIT_STRIDE
            for c in range(GROUP_LANE_BLOCKS):
                tmp_ref[c, pl.ds(start, per_res, stride=SPLIT_STRIDE), :] = (
                    src_ref[r, res_rows, c * V7X_LANES:(c + 1) * V7X_LANES].astype(F32))
        for c in range(GROUP_LANE_BLOCKS):
            for r4 in range(SPLIT_STRIDE):
                lo = k * SUB_TM + r4 * quarter
                scr_ref[c, pl.ds(k * SUB_TM + r4, quarter, stride=SPLIT_STRIDE), :] = tmp_ref[c, lo:lo + quarter]
        return jnp.concatenate([scr_ref[c, rows] for c in range(GROUP_LANE_BLOCKS)], axis=-1)
    for r in range(dil):
        for c in range(GROUP_LANE_BLOCKS):
            scr_ref[c, pl.ds(k * SUB_TM + r, per_res, stride=dil), :] = (
                src_ref[r, res_rows, c * V7X_LANES:(c + 1) * V7X_LANES].astype(F32))
    return jnp.concatenate([scr_ref[c, rows] for c in range(GROUP_LANE_BLOCKS)], axis=-1)


def _merge_kernel(x_ref, pre_g_ref, wgate_ref, oa_ref, wa_ref, *refs):
    ng = NB_GROUPS
    o_refs, lse_refs = refs[:ng], refs[ng:2 * ng]
    wb_ref, wout_ref, post_g_ref, out_ref, h_ref, ob_ref, mg_ref = refs[2 * ng:2 * ng + 7]
    scr_refs = iter(refs[2 * ng + 7:])
    o_scr = [next(scr_refs) if dil > 1 else None for dil in DILATIONS]
    lse_scr = [next(scr_refs) if dil > 1 else None for dil in DILATIONS]
    o_tmp = [next(scr_refs) if dil == SPLIT_STRIDE ** 2 else None for dil in DILATIONS]
    lse_tmp = [next(scr_refs) if dil == SPLIT_STRIDE ** 2 else None for dil in DILATIONS]
    subs = _sub_tiles(x_ref.shape[0])
    for k, rows in enumerate(subs):
        h_ref[rows] = _rmsnorm_f32(x_ref[rows], pre_g_ref[...]).astype(BF16)
        os_ = [_to_token_order(r, s, t, d, k) for r, s, t, d in zip(o_refs, o_scr, o_tmp, DILATIONS)]
        lses = [_to_token_order(r, s, t, d, k) for r, s, t, d in zip(lse_refs, lse_scr, lse_tmp, DILATIONS)]
        mx = jnp.maximum(jnp.maximum(lses[0], lses[1]), lses[2])
        es = [jnp.exp(l - mx) for l in lses]
        inv = 1.0 / (es[0] + es[1] + es[2])
        for g in range(ng):
            ob_ref[rows, g * GROUP_WIDTH:(g + 1) * GROUP_WIDTH] = (os_[g] * (es[g] * inv)).astype(BF16)
    for rows in subs:
        for n in range(D_MODEL // MERGE_CN):
            cols = slice(n * MERGE_CN, (n + 1) * MERGE_CN)
            cols_b = slice(D_MODEL + n * MERGE_CN, D_MODEL + (n + 1) * MERGE_CN)
            h = h_ref[rows]
            ga = _dot(h, wgate_ref[:, cols].astype(BF16))
            gb = _dot(h, wgate_ref[:, cols_b].astype(BF16))
            ya = _dot(oa_ref[rows], wa_ref[:, cols])
            yb = _dot(ob_ref[rows], wb_ref[:, cols])
            mg_ref[rows, cols] = (jax.nn.sigmoid(ga) * ya + jax.nn.sigmoid(gb) * yb).astype(BF16)
    for rows in subs:
        out_ref[rows] = _dot(mg_ref[rows], wout_ref[...])
    for rows in subs:
        out_ref[rows] = x_ref[rows] + _rmsnorm_f32(out_ref[rows], post_g_ref[...])


def _merge(x, pre_g, wgate, oa, wa, obs, lses, wb, wout, post_g):
    batch, seq, d = x.shape
    tm = MERGE_TM
    tok = lambda b, i: (b, i, 0)
    res = lambda b, i: (b, 0, i, 0)
    tokspec = lambda w: pl.BlockSpec((None, tm, w), tok)
    resspecs = [pl.BlockSpec((None, dil, tm // dil, GROUP_WIDTH), res) for dil in DILATIONS]
    blocks = (2 * 2 * _nbytes((tm, d), F32) + 2 * _nbytes((tm, NA_WIDTH), BF16)
              + 2 * 3 * (_nbytes((tm, GROUP_WIDTH), BF16) + _nbytes((tm, GROUP_WIDTH), F32))
              + _nbytes((NA_WIDTH, d), wa.dtype) + _nbytes((d + NB_WIDTH, d), wb.dtype) + _nbytes((d, 2 * d), F32))
    work = [pltpu.VMEM((tm, d), BF16), pltpu.VMEM((tm, NB_WIDTH), BF16), pltpu.VMEM((tm, d), BF16)]
    n_interleave = 2 * sum(dil > 1 for dil in DILATIONS) + 2 * sum(dil == SPLIT_STRIDE ** 2 for dil in DILATIONS)
    scratch = (n_interleave * _nbytes((tm, GROUP_WIDTH), F32) + 2 * _nbytes((tm, d), BF16)
               + _nbytes((tm, NB_WIDTH), BF16))
    temps = 2 * _nbytes((SUB_TM, d), F32)
    return pl.pallas_call(
        _merge_kernel,
        grid=(batch, seq // tm),
        in_specs=[
            tokspec(d), _resident((1, d)),
            pl.BlockSpec((pl.Element(d), pl.Element(2 * d)), lambda *_: (0, QKV_TOTAL),
                         pipeline_mode=pl.Buffered(1)),
            tokspec(NA_WIDTH), _resident((NA_WIDTH, d)),
            *resspecs, *resspecs,
            _resident((NB_WIDTH, d)), _resident((d, d)), _resident((1, d)),
        ],
        out_specs=tokspec(d),
        out_shape=jax.ShapeDtypeStruct((batch, seq, d), F32),
        scratch_shapes=work + [pltpu.VMEM((GROUP_LANE_BLOCKS, tm, V7X_LANES), F32)] * n_interleave,
        compiler_params=pltpu.CompilerParams(
            dimension_semantics=("arbitrary", "arbitrary"),
            vmem_limit_bytes=_vmem_limit(blocks, scratch, temps),
        ),
        name="merge",
    )(x, pre_g, wgate, oa, wa, *obs, *lses, wb, wout, post_g)


def kernel(x, ffn1_pre_g, ffn1_w_gate, ffn1_w_up, ffn1_w_down, ffn1_post_g, mix_pre_g, w_in, na_rpb, w_branch_a, w_branch_b, w_out, mix_post_g, ffn2_pre_g, ffn2_w_gate, ffn2_w_up, ffn2_w_down, ffn2_post_g):
    batch, seq, d = x.shape
    depth = ffn1_pre_g.shape[0]
    for window, dilation in DIL_PAIRS:
        assert window // (2 * dilation) == DIL_HALF
    for l in range(depth):
        x = _ffn(x.reshape(batch * seq, d), ffn1_pre_g[l][None], ffn1_w_gate[l], ffn1_w_up[l],
                 ffn1_w_down[l], ffn1_post_g[l][None])
        x = x.reshape(batch, seq, d)
        qa, ka, va, qb, kb, vb = _qkv(x, mix_pre_g[l][None], w_in[l])
        oa = _nbr_attention(qa, ka, va, na_rpb[l])
        obs, lses = [], []
        for g in range(NB_GROUPS):
            o_g, lse_g = _dil_attention(qb[g], kb[g], vb[g], g)
            obs.append(o_g)
            lses.append(lse_g)
        x = _merge(x, mix_pre_g[l][None], w_in[l],
                   oa, w_branch_a[l].astype(BF16), obs, lses, w_branch_b[l].astype(BF16),
                   w_out[l].astype(BF16), mix_post_g[l][None])
        x = _ffn(x.reshape(batch * seq, d), ffn2_pre_g[l][None], ffn2_w_gate[l], ffn2_w_up[l],
                 ffn2_w_down[l], ffn2_post_g[l][None])
        x = x.reshape(batch, seq, d)
    return x
```

```python
import functools
import math

import jax
import jax.numpy as jnp
from jax import lax
from jax.experimental import pallas as pl
from jax.experimental.pallas import tpu as pltpu

D_MODEL = 1024
HEAD_DIM = 64
NA_HEADS = 8
NA_WIDTH = NA_HEADS * HEAD_DIM
NA_ROWS = 8
NA_COLS = 16
GRID_W = 64
DIL_PAIRS = ((128, 1), (512, 4), (2048, 16))
DILATIONS = tuple(d for _, d in DIL_PAIRS)
NB_GROUPS = len(DIL_PAIRS)
NB_HEADS_PER_GROUP = 4
NB_HEADS = NB_GROUPS * NB_HEADS_PER_GROUP
NB_WIDTH = NB_HEADS * HEAD_DIM
GROUP_WIDTH = NB_HEADS_PER_GROUP * HEAD_DIM
ALIBI_MAX_EXP = 8.0
D_FF = 2816
NORM_EPS = 1e-6
ATTN_SCALE = HEAD_DIM ** -0.5
LOG2E = math.log2(math.e)
LN2 = math.log(2.0)
Q_SCALE = ATTN_SCALE * LOG2E

V7X_LANES = 128
GROUP_LANE_BLOCKS = GROUP_WIDTH // V7X_LANES
V7X_VMEM_BYTES = 64 * 1024 * 1024
V7X_VMEM_RESERVE = 6 * 1024 * 1024

BF16 = jnp.bfloat16
F32 = jnp.float32


def _vmem_limit(block_bytes, scratch_bytes, temp_bytes):
    need = block_bytes + scratch_bytes + temp_bytes
    budget = V7X_VMEM_BYTES - V7X_VMEM_RESERVE
    assert need <= budget, need
    return budget


def _nbytes(shape, dtype):
    return math.prod(shape) * jnp.dtype(dtype).itemsize


def _resident(shape):
    nd = len(shape)
    return pl.BlockSpec(shape, lambda *_: (0,) * nd, pipeline_mode=pl.Buffered(1))


def _rmsnorm_f32(x, g):
    return x * lax.rsqrt(jnp.mean(x * x, axis=-1, keepdims=True) + NORM_EPS) * g


def _sigmoid(x):
    return 0.5 * jnp.tanh(0.5 * x) + 0.5


def _dot(a, b):
    return jnp.dot(a, b, preferred_element_type=F32)


def _dot_nt(a, b):
    return lax.dot_general(a, b, (((1,), (1,)), ((), ())), preferred_element_type=F32)


def _pipelined_attention(n_groups, unroll, scores, probs, finish, s_scr, p_scr, stat_scrs):
    assert n_groups % 2 == 0 and n_groups >= 2
    tiles = lambda i: [i * unroll + u for u in range(unroll)]

    def stage_scores(i, slot):
        for u, t in enumerate(tiles(i)):
            s_scr[slot, u] = scores(t)

    def stage_probs(slot):
        for u in range(unroll):
            p, *stats = probs(s_scr[slot, u])
            p_scr[slot, u] = p
            for ref, stat in zip(stat_scrs, stats):
                ref[slot, u] = stat

    def stage_finish(i, slot):
        for u, t in enumerate(tiles(i)):
            finish(t, p_scr[slot, u], *[ref[slot, u] for ref in stat_scrs])

    def step(i, parity):
        stage_scores(i + 1, 1 - parity)
        stage_finish(i - 1, 1 - parity)
        stage_probs(parity)

    stage_scores(0, 0)
    stage_scores(1, 1)
    stage_probs(0)

    def body(i, carry):
        @pl.when(i % 2 == 1)
        def _():
            step(i, 1)

        @pl.when(i % 2 == 0)
        def _():
            step(i, 0)

        return carry

    lax.fori_loop(1, n_groups - 1, body, 0)
    stage_finish(n_groups - 2, 0)
    stage_probs(1)
    stage_finish(n_groups - 1, 1)


FFN_TM = 512
SUB_TM = 512
FFN_CK = 256


def _sub_tiles(tm):
    return [slice(k * SUB_TM, (k + 1) * SUB_TM) for k in range(tm // SUB_TM)]


FFN_CHUNKS = [slice(c * FFN_CK, (c + 1) * FFN_CK) for c in range(D_FF // FFN_CK)]


def _ffn_weight_copies(wg_hbm, wu_hbm, wd_hbm, wg_ref, wu_ref, wd_ref, sem):
    gate_up = [(pltpu.make_async_copy(wg_hbm.at[:, cols], wg_ref.at[:, cols], sem.at[0, c]),
                pltpu.make_async_copy(wu_hbm.at[:, cols], wu_ref.at[:, cols], sem.at[1, c]))
               for c, cols in enumerate(FFN_CHUNKS)]
    down = [pltpu.make_async_copy(wd_hbm.at[cols, :], wd_ref.at[cols, :], sem.at[2, c])
            for c, cols in enumerate(FFN_CHUNKS)]
    return gate_up, down


def _ffn_body(x_ref, pre_g_ref, wg_ref, wu_ref, wd_ref, post_g_ref, o_ref, h_ref, a_ref, wait_gate_up, wait_down):
    h_ref[...] = _rmsnorm_f32(x_ref[...], pre_g_ref[...]).astype(BF16)
    for c, cols in enumerate(FFN_CHUNKS):
        wait_gate_up(c)
        h = h_ref[...]
        g = _dot(h, wg_ref[:, cols].astype(BF16))
        u = _dot(h, wu_ref[:, cols].astype(BF16))
        a_ref[:, cols] = (g * _sigmoid(g) * u).astype(BF16)
    f = None
    for c, cols in enumerate(FFN_CHUNKS):
        wait_down(c)
        part = _dot(a_ref[:, cols], wd_ref[cols, :].astype(BF16))
        f = part if f is None else f + part
    o_ref[...] = x_ref[...] + 0.5 * _rmsnorm_f32(f, post_g_ref[...])


def _ffn_kernel(x_ref, pre_g_ref, wg_hbm, wu_hbm, wd_hbm, post_g_ref, o_ref, h_ref, a_ref,
                wg_ref, wu_ref, wd_ref, sem):
    refs = (x_ref, pre_g_ref, wg_ref, wu_ref, wd_ref, post_g_ref, o_ref, h_ref, a_ref)
    first = pl.program_id(0) == 0

    @pl.when(jnp.logical_not(first))
    def _():
        _ffn_body(*refs, lambda c: None, lambda c: None)

    @pl.when(first)
    def _():
        gate_up, down = _ffn_weight_copies(wg_hbm, wu_hbm, wd_hbm, wg_ref, wu_ref, wd_ref, sem)
        for cg, cu in gate_up:
            cg.start()
            cu.start()
        for cd in down:
            cd.start()

        def wait_gate_up(c):
            gate_up[c][0].wait()
            gate_up[c][1].wait()

        _ffn_body(*refs, wait_gate_up, lambda c: down[c].wait())


def _ffn(x, pre_g, wg, wu, wd, post_g):
    n, d = x.shape
    tm = FFN_TM
    row = lambda i: (i, 0)
    n_chunks = len(FFN_CHUNKS)
    blocks = 2 * 2 * _nbytes((tm, d), F32) + 2 * _nbytes((1, d), F32)
    scratch = _nbytes((tm, d), BF16) + _nbytes((tm, D_FF), BF16) + 3 * _nbytes((d, D_FF), wg.dtype)
    temps = 4 * _nbytes((tm, d), F32)
    hbm = pl.BlockSpec(memory_space=pl.ANY)
    return pl.pallas_call(
        _ffn_kernel,
        grid=(n // tm,),
        in_specs=[pl.BlockSpec((tm, d), row), _resident((1, d)), hbm, hbm, hbm, _resident((1, d))],
        out_specs=pl.BlockSpec((tm, d), row),
        out_shape=jax.ShapeDtypeStruct((n, d), F32),
        scratch_shapes=[
            pltpu.VMEM((tm, d), BF16), pltpu.VMEM((tm, D_FF), BF16),
            pltpu.VMEM((d, D_FF), wg.dtype), pltpu.VMEM((d, D_FF), wu.dtype), pltpu.VMEM((D_FF, d), wd.dtype),
            pltpu.SemaphoreType.DMA((3, n_chunks)),
        ],
        compiler_params=pltpu.CompilerParams(
            dimension_semantics=("arbitrary",),
            vmem_limit_bytes=_vmem_limit(blocks, scratch, temps),
        ),
        name="ffn",
    )(x, pre_g, wg, wu, wd, post_g)


QKV_TM = 1024
QKV_TOTAL = 3 * NA_WIDTH + 3 * NB_WIDTH
N_DIL_OUTS = 3 * NB_GROUPS
STAGED_OUTS = [i for i in range(N_DIL_OUTS) if DILATIONS[i % NB_GROUPS] > 1]
N_STAGED_OUTS = len(STAGED_OUTS)
SPLIT_STRIDE = 4
SPLIT_OUTS = [i for i in STAGED_OUTS if DILATIONS[i % NB_GROUPS] == SPLIT_STRIDE ** 2]
N_SPLIT_OUTS = len(SPLIT_OUTS)


def _qkv_kernel(x_ref, g_ref, w32_ref, *refs):
    na_refs, dil_refs = refs[:3], refs[3:3 + N_DIL_OUTS]
    w_ref = refs[3 + N_DIL_OUTS]
    n_stage = N_STAGED_OUTS * (x_ref.shape[0] // SUB_TM)
    y_refs, t_refs = refs[4 + N_DIL_OUTS:4 + N_DIL_OUTS + n_stage], refs[4 + N_DIL_OUTS + n_stage:]

    @pl.when((pl.program_id(0) == 0) & (pl.program_id(1) == 0))
    def _():
        for c in range(QKV_TOTAL // GROUP_WIDTH):
            cols = slice(c * GROUP_WIDTH, (c + 1) * GROUP_WIDTH)
            w = w32_ref[:, cols]
            if cols.stop <= NA_WIDTH or 3 * NA_WIDTH <= cols.start < 3 * NA_WIDTH + NB_WIDTH:
                w = w * Q_SCALE
            w_ref[:, cols] = w.astype(BF16)

    for k, rows in enumerate(_sub_tiles(x_ref.shape[0])):
        h = _rmsnorm_f32(x_ref[rows], g_ref[...]).astype(BF16)
        for idx, o_ref in enumerate(dil_refs):
            which, g = divmod(idx, NB_GROUPS)
            dil = DILATIONS[g]
            off = 3 * NA_WIDTH + which * NB_WIDTH + g * GROUP_WIDTH
            y = _dot(h, w_ref[:, off:off + GROUP_WIDTH])
            if dil == 1:
                o_ref[0, rows] = y.astype(BF16)
                continue
            y_ref = y_refs[k * N_STAGED_OUTS + STAGED_OUTS.index(idx)]
            per_res = SUB_TM // dil
            for c in range(GROUP_LANE_BLOCKS):
                y_ref[c] = y[:, c * V7X_LANES:(c + 1) * V7X_LANES]
            if dil == SPLIT_STRIDE ** 2:
                t_ref = t_refs[k * N_SPLIT_OUTS + SPLIT_OUTS.index(idx)]
                quarter = SUB_TM // SPLIT_STRIDE
                for c in range(GROUP_LANE_BLOCKS):
                    for r4 in range(SPLIT_STRIDE):
                        t_ref[c, r4 * quarter:(r4 + 1) * quarter] = y_ref[c, pl.ds(r4, quarter, stride=SPLIT_STRIDE), :]
                for r in range(dil):
                    start = (r % SPLIT_STRIDE) * quarter + r // SPLIT_STRIDE
                    for c in range(GROUP_LANE_BLOCKS):
                        o_ref[r, k * per_res:(k + 1) * per_res, c * V7X_LANES:(c + 1) * V7X_LANES] = (
                            t_ref[c, pl.ds(start, per_res, stride=SPLIT_STRIDE), :].astype(BF16))
                continue
            for r in range(dil):
                for c in range(GROUP_LANE_BLOCKS):
                    o_ref[r, k * per_res:(k + 1) * per_res, c * V7X_LANES:(c + 1) * V7X_LANES] = (
                        y_ref[c, pl.ds(r, per_res, stride=dil), :].astype(BF16))
        for idx, o_ref in enumerate(na_refs):
            y = _dot(h, w_ref[:, idx * NA_WIDTH:(idx + 1) * NA_WIDTH])
            o_ref[rows] = y.astype(BF16)


def _qkv(x, g, w_in):
    batch, seq, d = x.shape
    tm = QKV_TM
    tok = lambda b, i: (b, i, 0)
    res = lambda b, i: (b, 0, i, 0)
    out_specs = [pl.BlockSpec((None, tm, NA_WIDTH), tok)] * 3
    out_shape = [jax.ShapeDtypeStruct((batch, seq, NA_WIDTH), BF16)] * 3
    for _ in range(3):
        for dil in DILATIONS:
            out_specs.append(pl.BlockSpec((None, dil, tm // dil, GROUP_WIDTH), res))
            out_shape.append(jax.ShapeDtypeStruct((batch, dil, seq // dil, GROUP_WIDTH), BF16))
    blocks = 2 * _nbytes((tm, d), F32) + _nbytes((d, QKV_TOTAL), w_in.dtype) + 2 * _nbytes((tm, QKV_TOTAL), BF16)
    stage = (GROUP_LANE_BLOCKS, SUB_TM, V7X_LANES)
    n_stage = N_STAGED_OUTS * (tm // SUB_TM)
    n_split = N_SPLIT_OUTS * (tm // SUB_TM)
    scratch = (n_stage + n_split) * _nbytes(stage, F32) + _nbytes((d, QKV_TOTAL), BF16)
    temps = _nbytes((SUB_TM, d), F32)
    outs = pl.pallas_call(
        _qkv_kernel,
        grid=(batch, seq // tm),
        in_specs=[pl.BlockSpec((None, tm, d), tok), _resident((1, d)), _resident((d, QKV_TOTAL))],
        out_specs=out_specs,
        out_shape=out_shape,
        scratch_shapes=[pltpu.VMEM((d, QKV_TOTAL), BF16)] + [pltpu.VMEM(stage, F32)] * (n_stage + n_split),
        compiler_params=pltpu.CompilerParams(
            dimension_semantics=("arbitrary", "arbitrary"),
            vmem_limit_bytes=_vmem_limit(blocks, scratch, temps),
        ),
        name="qkv",
    )(x, g, w_in)
    qa, ka, va = outs[:3]
    qb, kb, vb = (outs[3 + w * NB_GROUPS:3 + (w + 1) * NB_GROUPS] for w in range(3))
    return qa, ka, va, qb, kb, vb


NA_HEADS_PER_STEP = V7X_LANES // HEAD_DIM
NA_SPAN = NA_ROWS * GRID_W
NA_VARIANTS = NA_ROWS
NA_RPB_ROWS = 2 * NA_ROWS - 1
NA_RPB_COLS = 2 * NA_COLS - 1
NA_UNROLL = 32


NA_RPB_PAIRS = (NA_RPB_ROWS + 1) // 2


def _nbr_build_bias(rep_ref, tbl_ref):
    width = NA_RPB_PAIRS * V7X_LANES
    qc = lax.broadcasted_iota(jnp.int32, (GRID_W, width), 0)
    kc = lax.broadcasted_iota(jnp.int32, (GRID_W, width), 1) % GRID_W
    col0 = jnp.clip(qc - NA_COLS // 2, 0, GRID_W - NA_COLS)
    col_ok = (kc >= col0) & (kc < col0 + NA_COLS)
    for h in range(NA_HEADS_PER_STEP):
        blocks = []
        for p in range(NA_RPB_PAIRS):
            row = jnp.broadcast_to(rep_ref[h, p:p + 1, :], (GRID_W, V7X_LANES))
            blocks.append(pltpu.roll(row, V7X_LANES - (NA_COLS - 1), 1, stride=1, stride_axis=0))
        w = jnp.where(col_ok, jnp.concatenate(blocks, axis=1) * LOG2E, -jnp.inf)
        for variant in range(NA_VARIANTS):
            tbl_ref[h, variant] = w[:, variant * GRID_W:variant * GRID_W + NA_SPAN]


def _nbr_kernel(q_ref, k_ref, v_ref, rep_ref, o_ref, tbl_ref, s_scr, p_scr, *, rows):
    hp = NA_HEADS_PER_STEP
    ones = jnp.ones((NA_SPAN, V7X_LANES), BF16)

    @pl.when(pl.program_id(1) == 0)
    def _():
        _nbr_build_bias(rep_ref, tbl_ref)

    lane_head = lax.broadcasted_iota(jnp.int32, (GRID_W, V7X_LANES), 1) // HEAD_DIM

    def scores(r):
        row0 = jnp.clip(r - NA_ROWS // 2, 0, rows - NA_ROWS)
        variant = row0 - r + (NA_ROWS - 1)
        q = q_ref[pl.ds(pl.multiple_of(r * GRID_W, GRID_W), GRID_W), :]
        kw = k_ref[pl.ds(pl.multiple_of(row0 * GRID_W, GRID_W), NA_SPAN), :]
        qs = jnp.concatenate([jnp.where(lane_head == h, q, jnp.zeros_like(q)) for h in range(hp)], axis=0)
        return _dot_nt(qs, kw) + tbl_ref[:, variant].reshape(hp * GRID_W, NA_SPAN)

    def probs(s):
        return (jnp.exp2(s - jnp.max(s, axis=-1, keepdims=True)).astype(BF16),)

    def finish(r, p):
        row0 = jnp.clip(r - NA_ROWS // 2, 0, rows - NA_ROWS)
        vw = v_ref[pl.ds(pl.multiple_of(row0 * GRID_W, GRID_W), NA_SPAN), :]
        o = _dot(p, jnp.concatenate([vw, ones], axis=1))
        out, den = o[:GRID_W, :V7X_LANES], o[:GRID_W, V7X_LANES:]
        for h in range(1, hp):
            rows_h = slice(h * GRID_W, (h + 1) * GRID_W)
            out = jnp.where(lane_head == h, o[rows_h, :V7X_LANES], out)
            den = jnp.where(lane_head == h, o[rows_h, V7X_LANES:], den)
        o_ref[pl.ds(pl.multiple_of(r * GRID_W, GRID_W), GRID_W), :] = (out * (1.0 / den)).astype(BF16)

    _pipelined_attention(rows // NA_UNROLL, NA_UNROLL, scores, probs, finish, s_scr, p_scr, ())


def _nbr_attention(q, k, v, rpb):
    batch, seq, width = q.shape
    rows = seq // GRID_W
    n_pairs = width // V7X_LANES
    rep = jnp.pad(rpb.astype(F32), ((0, 0), (0, 2 * NA_RPB_PAIRS - NA_RPB_ROWS), (0, GRID_W - NA_RPB_COLS)))
    rep = rep.reshape(rpb.shape[0], NA_RPB_PAIRS, V7X_LANES)
    tok = lambda p, b: (b, 0, p)
    tbl_shape = (NA_HEADS_PER_STEP, NA_VARIANTS, GRID_W, NA_SPAN)
    rep_block = (NA_HEADS_PER_STEP, NA_RPB_PAIRS, V7X_LANES)
    blocks = 2 * 4 * _nbytes((seq, V7X_LANES), BF16) + 2 * _nbytes(rep_block, F32)
    tile = (2, NA_UNROLL, NA_HEADS_PER_STEP * GRID_W, NA_SPAN)
    scratch = _nbytes(tbl_shape, F32) + _nbytes(tile, F32) + _nbytes(tile, BF16)
    temps = NA_UNROLL * _nbytes(tile[2:], F32)
    return pl.pallas_call(
        functools.partial(_nbr_kernel, rows=rows),
        grid=(n_pairs, batch),
        in_specs=[
            pl.BlockSpec((None, seq, V7X_LANES), tok),
            pl.BlockSpec((None, seq, V7X_LANES), tok),
            pl.BlockSpec((None, seq, V7X_LANES), tok),
            pl.BlockSpec(rep_block, lambda p, b: (p, 0, 0)),
        ],
        out_specs=pl.BlockSpec((None, seq, V7X_LANES), tok),
        out_shape=jax.ShapeDtypeStruct((batch, seq, width), BF16),
        scratch_shapes=[pltpu.VMEM(tbl_shape, F32), pltpu.VMEM(tile, F32), pltpu.VMEM(tile, BF16)],
        compiler_params=pltpu.CompilerParams(
            dimension_semantics=("arbitrary", "arbitrary"),
            vmem_limit_bytes=_vmem_limit(blocks, scratch, temps),
        ),
        name="nbr_attn",
    )(q, k, v, rep)


DIL_HALF = 64
DIL_TQ = 128
DIL_TK = DIL_TQ + 2 * DIL_HALF
DIL_VARIANTS = 3
DIL_STEP_ROWS = 4096
DIL_UNROLL = 8


def _dil_slopes(group):
    heads = jnp.arange(NB_HEADS_PER_GROUP, dtype=F32) + group * NB_HEADS_PER_GROUP
    return jnp.exp2(-ALIBI_MAX_EXP * (heads + 1.0) / NB_HEADS)


def _dil_build_bias(slope_ref, bias_ref, dilation):
    qq = lax.broadcasted_iota(jnp.int32, (DIL_TQ, DIL_TK), 0)
    kk = lax.broadcasted_iota(jnp.int32, (DIL_TQ, DIL_TK), 1)
    for variant in range(DIL_VARIANTS):
        rel = kk - qq - variant * DIL_HALF
        dist = (dilation * jnp.abs(rel)).astype(F32)
        in_band = jnp.abs(rel) <= DIL_HALF
        for h in range(NB_HEADS_PER_GROUP):
            bias = -(slope_ref[h] * dist) * LOG2E
            bias_ref[variant, h * DIL_TQ:(h + 1) * DIL_TQ, :] = jnp.where(in_band, bias, -jnp.inf)


def _dil_kernel(q_ref, k_ref, v_ref, slope_ref, o_ref, lse_ref, bias_ref, s_scr, p_scr, m_scr,
                *, seq_len, stretch, dilation):
    nh = NB_HEADS_PER_GROUP
    hpb = V7X_LANES // HEAD_DIM
    ones = jnp.ones((DIL_TK, V7X_LANES), BF16)

    @pl.when((pl.program_id(0) == 0) & (pl.program_id(1) == 0) & (pl.program_id(2) == 0))
    def _():
        _dil_build_bias(slope_ref, bias_ref, dilation)

    group_lane_head = lax.broadcasted_iota(jnp.int32, (DIL_TQ, GROUP_WIDTH), 1) // HEAD_DIM
    lane_head = lax.broadcasted_iota(jnp.int32, (DIL_TQ, V7X_LANES), 1) // HEAD_DIM
    base = pl.program_id(2) * stretch
    tiles_per_residue = stretch // DIL_TQ

    def window(t):
        r, j = t // tiles_per_residue, t % tiles_per_residue
        qloc = pl.multiple_of(j * DIL_TQ, DIL_TQ)
        qs = base + qloc
        ws = jnp.clip(qs - DIL_HALF, 0, seq_len - DIL_TK)
        variant = (qs - ws) // DIL_HALF
        return r, qloc, pl.multiple_of(ws, DIL_HALF), variant

    def scores(t):
        r, qloc, ws, variant = window(t)
        q = q_ref[r, pl.ds(qloc, DIL_TQ), :]
        kw = k_ref[r, pl.ds(ws, DIL_TK), :]
        qst = jnp.concatenate([jnp.where(group_lane_head == h, q, jnp.zeros_like(q)) for h in range(nh)],
                              axis=0)
        return _dot_nt(qst, kw) + bias_ref[variant]

    def probs(s):
        m = jnp.max(s, axis=-1, keepdims=True)
        return jnp.exp2(s - m).astype(BF16), jnp.broadcast_to(m, (s.shape[0], V7X_LANES))

    def finish(t, p, m):
        r, qloc, ws, _ = window(t)
        vw = v_ref[r, pl.ds(ws, DIL_TK), :]
        outs, lses = [], []
        for c in range(GROUP_LANE_BLOCKS):
            rows_c = slice(c * hpb * DIL_TQ, (c + 1) * hpb * DIL_TQ)
            v_aug = jnp.concatenate([vw[:, c * V7X_LANES:(c + 1) * V7X_LANES], ones], axis=1)
            o = _dot(p[rows_c], v_aug)
            m_c = m[rows_c]
            out_c, den_c, max_c = o[:DIL_TQ, :V7X_LANES], o[:DIL_TQ, V7X_LANES:], m_c[:DIL_TQ]
            for h in range(1, hpb):
                rows_h = slice(h * DIL_TQ, (h + 1) * DIL_TQ)
                out_c = jnp.where(lane_head == h, o[rows_h, :V7X_LANES], out_c)
                den_c = jnp.where(lane_head == h, o[rows_h, V7X_LANES:], den_c)
                max_c = jnp.where(lane_head == h, m_c[rows_h], max_c)
            outs.append(out_c * (1.0 / den_c))
            lses.append(max_c * LN2 + jnp.log(den_c))
        o_ref[r, pl.ds(qloc, DIL_TQ), :] = jnp.concatenate(outs, axis=1).astype(BF16)
        lse_ref[r, pl.ds(qloc, DIL_TQ), :] = jnp.concatenate(lses, axis=1)

    n_tiles = q_ref.shape[0] * tiles_per_residue
    _pipelined_attention(n_tiles // DIL_UNROLL, DIL_UNROLL, scores, probs, finish, s_scr, p_scr, (m_scr,))


def _dil_attention(q, k, v, group):
    batch, dilation, seq_len, width = q.shape
    stretch = min(seq_len, DIL_STEP_ROWS)
    res_blk = DIL_STEP_ROWS // stretch
    assert dilation % res_blk == 0 and seq_len % stretch == 0
    qmap = lambda b, r, s: (b, r, s, 0)
    kvmap = lambda b, r, s: (b, r, 0, 0)
    bias_shape = (DIL_VARIANTS, NB_HEADS_PER_GROUP * DIL_TQ, DIL_TK)
    tile = (2, DIL_UNROLL, NB_HEADS_PER_GROUP * DIL_TQ, DIL_TK)
    stat = tile[:3] + (V7X_LANES,)
    kv_buffers = 2
    kv_spec = pl.BlockSpec((None, res_blk, seq_len, width), kvmap)
    blocks = (2 * (2 * _nbytes((DIL_STEP_ROWS, width), BF16) + _nbytes((DIL_STEP_ROWS, width), F32))
              + kv_buffers * 2 * _nbytes((res_blk * seq_len, width), BF16))
    scratch = _nbytes(bias_shape, F32) + _nbytes(tile, F32) + _nbytes(tile, BF16) + _nbytes(stat, F32)
    temps = DIL_UNROLL * _nbytes(tile[2:], F32)
    return pl.pallas_call(
        functools.partial(_dil_kernel, seq_len=seq_len, stretch=stretch, dilation=dilation),
        grid=(batch, dilation // res_blk, seq_len // stretch),
        in_specs=[
            pl.BlockSpec((None, res_blk, stretch, width), qmap),
            kv_spec,
            kv_spec,
            pl.BlockSpec(memory_space=pltpu.SMEM),
        ],
        out_specs=[pl.BlockSpec((None, res_blk, stretch, width), qmap)] * 2,
        out_shape=[jax.ShapeDtypeStruct(q.shape, BF16), jax.ShapeDtypeStruct(q.shape, F32)],
        scratch_shapes=[pltpu.VMEM(bias_shape, F32), pltpu.VMEM(tile, F32), pltpu.VMEM(tile, BF16),
                        pltpu.VMEM(stat, F32)],
        compiler_params=pltpu.CompilerParams(
            dimension_semantics=("arbitrary", "arbitrary", "arbitrary"),
            vmem_limit_bytes=_vmem_limit(blocks, scratch, temps),
        ),
        name=f"dil_attn_g{group}",
    )(q, k, v, _dil_slopes(group))


MERGE_TM = 1024
MERGE_CN = 256


def _to_token_order(src_ref, scr_ref, tmp_ref, dil, k):
    per_res = SUB_TM // dil
    res_rows = slice(k * per_res, (k + 1) * per_res)
    if dil == 1:
        return src_ref[0, res_rows].astype(F32)
    rows = slice(k * SUB_TM, (k + 1) * SUB_TM)
    if dil == SPLIT_STRIDE ** 2:
        quarter = SUB_TM // SPLIT_STRIDE
        for r in range(dil):
            start = k * SUB_TM + (r % SPLIT_STRIDE) * quarter + r // SPLIT_STRIDE
            for c in range(GROUP_LANE_BLOCKS):
                tmp_ref[c, pl.ds(start, per_res, stride=SPLIT_STRIDE), :] = (
                    src_ref[r, res_rows, c * V7X_LANES:(c + 1) * V7X_LANES].astype(F32))
        for c in range(GROUP_LANE_BLOCKS):
            for r4 in range(SPLIT_STRIDE):
                lo = k * SUB_TM + r4 * quarter
                scr_ref[c, pl.ds(k * SUB_TM + r4, quarter, stride=SPLIT_STRIDE), :] = tmp_ref[c, lo:lo + quarter]
        return jnp.concatenate([scr_ref[c, rows] for c in range(GROUP_LANE_BLOCKS)], axis=-1)
    for r in range(dil):
        for c in range(GROUP_LANE_BLOCKS):
            scr_ref[c, pl.ds(k * SUB_TM + r, per_res, stride=dil), :] = (
                src_ref[r, res_rows, c * V7X_LANES:(c + 1) * V7X_LANES].astype(F32))
    return jnp.concatenate([scr_ref[c, rows] for c in range(GROUP_LANE_BLOCKS)], axis=-1)


def _merge_kernel(x_ref, pre_g_ref, wgate_ref, oa_ref, wa_ref, *refs):
    ng = NB_GROUPS
    o_refs, lse_refs = refs[:ng], refs[ng:2 * ng]
    wb_ref, wout_ref, post_g_ref, out_ref, h_ref, ob_ref, mg_ref = refs[2 * ng:2 * ng + 7]
    scr_refs = iter(refs[2 * ng + 7:])
    o_scr = [next(scr_refs) if dil > 1 else None for dil in DILATIONS]
    lse_scr = [next(scr_refs) if dil > 1 else None for dil in DILATIONS]
    o_tmp = [next(scr_refs) if dil == SPLIT_STRIDE ** 2 else None for dil in DILATIONS]
    lse_tmp = [next(scr_refs) if dil == SPLIT_STRIDE ** 2 else None for dil in DILATIONS]
    subs = _sub_tiles(x_ref.shape[0])
    for k, rows in enumerate(subs):
        h_ref[rows] = _rmsnorm_f32(x_ref[rows], pre_g_ref[...]).astype(BF16)
        os_ = [_to_token_order(r, s, t, d, k) for r, s, t, d in zip(o_refs, o_scr, o_tmp, DILATIONS)]
        lses = [_to_token_order(r, s, t, d, k) for r, s, t, d in zip(lse_refs, lse_scr, lse_tmp, DILATIONS)]
        mx = jnp.maximum(jnp.maximum(lses[0], lses[1]), lses[2])
        es = [jnp.exp(l - mx) for l in lses]
        inv = 1.0 / (es[0] + es[1] + es[2])
        for g in range(ng):
            ob_ref[rows, g * GROUP_WIDTH:(g + 1) * GROUP_WIDTH] = (os_[g] * (es[g] * inv)).astype(BF16)
    for rows in subs:
        for n in range(D_MODEL // MERGE_CN):
            cols = slice(n * MERGE_CN, (n + 1) * MERGE_CN)
            cols_b = slice(D_MODEL + n * MERGE_CN, D_MODEL + (n + 1) * MERGE_CN)
            h = h_ref[rows]
            ga = _dot(h, wgate_ref[:, cols].astype(BF16))
            gb = _dot(h, wgate_ref[:, cols_b].astype(BF16))
            ya = _dot(oa_ref[rows], wa_ref[:, cols])
            yb = _dot(ob_ref[rows], wb_ref[:, cols])
            mg_ref[rows, cols] = (_sigmoid(ga) * ya + _sigmoid(gb) * yb).astype(BF16)
    for rows in subs:
        out_ref[rows] = _dot(mg_ref[rows], wout_ref[...])
    for rows in subs:
        out_ref[rows] = x_ref[rows] + _rmsnorm_f32(out_ref[rows], post_g_ref[...])


def _merge(x, pre_g, wgate, oa, wa, obs, lses, wb, wout, post_g):
    batch, seq, d = x.shape
    tm = MERGE_TM
    tok = lambda b, i: (b, i, 0)
    res = lambda b, i: (b, 0, i, 0)
    tokspec = lambda w: pl.BlockSpec((None, tm, w), tok)
    resspecs = [pl.BlockSpec((None, dil, tm // dil, GROUP_WIDTH), res) for dil in DILATIONS]
    blocks = (2 * 2 * _nbytes((tm, d), F32) + 2 * _nbytes((tm, NA_WIDTH), BF16)
              + 2 * 3 * (_nbytes((tm, GROUP_WIDTH), BF16) + _nbytes((tm, GROUP_WIDTH), F32))
              + _nbytes((NA_WIDTH, d), wa.dtype) + _nbytes((d + NB_WIDTH, d), wb.dtype) + _nbytes((d, 2 * d), F32))
    work = [pltpu.VMEM((tm, d), BF16), pltpu.VMEM((tm, NB_WIDTH), BF16), pltpu.VMEM((tm, d), BF16)]
    n_interleave = 2 * sum(dil > 1 for dil in DILATIONS) + 2 * sum(dil == SPLIT_STRIDE ** 2 for dil in DILATIONS)
    scratch = (n_interleave * _nbytes((tm, GROUP_WIDTH), F32) + 2 * _nbytes((tm, d), BF16)
               + _nbytes((tm, NB_WIDTH), BF16))
    temps = 2 * _nbytes((SUB_TM, d), F32)
    return pl.pallas_call(
        _merge_kernel,
        grid=(batch, seq // tm),
        in_specs=[
            tokspec(d), _resident((1, d)),
            pl.BlockSpec((pl.Element(d), pl.Element(2 * d)), lambda *_: (0, QKV_TOTAL),
                         pipeline_mode=pl.Buffered(1)),
            tokspec(NA_WIDTH), _resident((NA_WIDTH, d)),
            *resspecs, *resspecs,
            _resident((NB_WIDTH, d)), _resident((d, d)), _resident((1, d)),
        ],
        out_specs=tokspec(d),
        out_shape=jax.ShapeDtypeStruct((batch, seq, d), F32),
        scratch_shapes=work + [pltpu.VMEM((GROUP_LANE_BLOCKS, tm, V7X_LANES), F32)] * n_interleave,
        compiler_params=pltpu.CompilerParams(
            dimension_semantics=("arbitrary", "arbitrary"),
            vmem_limit_bytes=_vmem_limit(blocks, scratch, temps),
        ),
        name="merge",
    )(x, pre_g, wgate, oa, wa, *obs, *lses, wb, wout, post_g)


def kernel(x, ffn1_pre_g, ffn1_w_gate, ffn1_w_up, ffn1_w_down, ffn1_post_g, mix_pre_g, w_in, na_rpb, w_branch_a, w_branch_b, w_out, mix_post_g, ffn2_pre_g, ffn2_w_gate, ffn2_w_up, ffn2_w_down, ffn2_post_g):
    batch, seq, d = x.shape
    depth = ffn1_pre_g.shape[0]
    for window, dilation in DIL_PAIRS:
        assert window // (2 * dilation) == DIL_HALF
    for l in range(depth):
        x = _ffn(x.reshape(batch * seq, d), ffn1_pre_g[l][None], ffn1_w_gate[l], ffn1_w_up[l],
                 ffn1_w_down[l], ffn1_post_g[l][None])
        x = x.reshape(batch, seq, d)
        qa, ka, va, qb, kb, vb = _qkv(x, mix_pre_g[l][None], w_in[l])
        oa = _nbr_attention(qa, ka, va, na_rpb[l])
        obs, lses = [], []
        for g in range(NB_GROUPS):
            o_g, lse_g = _dil_attention(qb[g], kb[g], vb[g], g)
            obs.append(o_g)
            lses.append(lse_g)
        x = _merge(x, mix_pre_g[l][None], w_in[l],
                   oa, w_branch_a[l].astype(BF16), obs, lses, w_branch_b[l].astype(BF16),
                   w_out[l].astype(BF16), mix_post_g[l][None])
        x = _ffn(x.reshape(batch * seq, d), ffn2_pre_g[l][None], ffn2_w_gate[l], ffn2_w_up[l],
                 ffn2_w_down[l], ffn2_post_g[l][None])
        x = x.reshape(batch, seq, d)
    return x
```

```python
import functools
import math

import jax
import jax.numpy as jnp
from jax import lax
from jax.experimental import pallas as pl
from jax.experimental.pallas import tpu as pltpu

D_MODEL = 1024
HEAD_DIM = 64
NA_HEADS = 8
NA_WIDTH = NA_HEADS * HEAD_DIM
NA_ROWS = 8
NA_COLS = 16
GRID_W = 64
DIL_PAIRS = ((128, 1), (512, 4), (2048, 16))
DILATIONS = tuple(d for _, d in DIL_PAIRS)
NB_GROUPS = len(DIL_PAIRS)
NB_HEADS_PER_GROUP = 4
NB_HEADS = NB_GROUPS * NB_HEADS_PER_GROUP
NB_WIDTH = NB_HEADS * HEAD_DIM
GROUP_WIDTH = NB_HEADS_PER_GROUP * HEAD_DIM
ALIBI_MAX_EXP = 8.0
D_FF = 2816
NORM_EPS = 1e-6
ATTN_SCALE = HEAD_DIM ** -0.5
LOG2E = math.log2(math.e)
LN2 = math.log(2.0)
Q_SCALE = ATTN_SCALE * LOG2E

V7X_LANES = 128
GROUP_LANE_BLOCKS = GROUP_WIDTH // V7X_LANES
V7X_VMEM_BYTES = 64 * 1024 * 1024
V7X_VMEM_RESERVE = 6 * 1024 * 1024

BF16 = jnp.bfloat16
F32 = jnp.float32


def _vmem_limit(block_bytes, scratch_bytes, temp_bytes):
    need = block_bytes + scratch_bytes + temp_bytes
    budget = V7X_VMEM_BYTES - V7X_VMEM_RESERVE
    assert need <= budget, need
    return budget


def _nbytes(shape, dtype):
    return math.prod(shape) * jnp.dtype(dtype).itemsize


def _resident(shape):
    nd = len(shape)
    return pl.BlockSpec(shape, lambda *_: (0,) * nd, pipeline_mode=pl.Buffered(1))


def _rmsnorm_f32(x, g):
    return x * lax.rsqrt(jnp.mean(x * x, axis=-1, keepdims=True) + NORM_EPS) * g


def _sigmoid(x):
    return 0.5 * jnp.tanh(0.5 * x) + 0.5


def _dot(a, b):
    return jnp.dot(a, b, preferred_element_type=F32)


def _dot_nt(a, b):
    return lax.dot_general(a, b, (((1,), (1,)), ((), ())), preferred_element_type=F32)


def _pipelined_attention(n_groups, unroll, scores, probs, finish, s_scr, p_scr, stat_scrs):
    assert n_groups % 2 == 0 and n_groups >= 2
    tiles = lambda i: [i * unroll + u for u in range(unroll)]

    def stage_scores(i, slot):
        for u, t in enumerate(tiles(i)):
            s_scr[slot, u] = scores(t)

    def stage_probs(slot):
        for u in range(unroll):
            p, *stats = probs(s_scr[slot, u])
            p_scr[slot, u] = p
            for ref, stat in zip(stat_scrs, stats):
                ref[slot, u] = stat

    def stage_finish(i, slot):
        for u, t in enumerate(tiles(i)):
            finish(t, p_scr[slot, u], *[ref[slot, u] for ref in stat_scrs])

    def step(i, parity):
        stage_scores(i + 1, 1 - parity)
        stage_finish(i - 1, 1 - parity)
        stage_probs(parity)

    stage_scores(0, 0)
    stage_scores(1, 1)
    stage_probs(0)

    def body(i, carry):
        @pl.when(i % 2 == 1)
        def _():
            step(i, 1)

        @pl.when(i % 2 == 0)
        def _():
            step(i, 0)

        return carry

    lax.fori_loop(1, n_groups - 1, body, 0)
    stage_finish(n_groups - 2, 0)
    stage_probs(1)
    stage_finish(n_groups - 1, 1)


FFN_TM = 512
SUB_TM = 512
FFN_CK = 256


def _sub_tiles(tm):
    return [slice(k * SUB_TM, (k + 1) * SUB_TM) for k in range(tm // SUB_TM)]


FFN_CHUNKS = [slice(c * FFN_CK, (c + 1) * FFN_CK) for c in range(D_FF // FFN_CK)]


def _ffn_weight_copies(wg_hbm, wu_hbm, wd_hbm, wg_ref, wu_ref, wd_ref, sem):
    gate_up = [(pltpu.make_async_copy(wg_hbm.at[:, cols], wg_ref.at[:, cols], sem.at[0, c]),
                pltpu.make_async_copy(wu_hbm.at[:, cols], wu_ref.at[:, cols], sem.at[1, c]))
               for c, cols in enumerate(FFN_CHUNKS)]
    down = [pltpu.make_async_copy(wd_hbm.at[cols, :], wd_ref.at[cols, :], sem.at[2, c])
            for c, cols in enumerate(FFN_CHUNKS)]
    return gate_up, down


def _ffn_body(x_ref, pre_g_ref, wg_ref, wu_ref, wd_ref, post_g_ref, o_ref, h_ref, a_ref, wait_gate_up, wait_down):
    h_ref[...] = _rmsnorm_f32(x_ref[...], pre_g_ref[...]).astype(BF16)
    for c, cols in enumerate(FFN_CHUNKS):
        wait_gate_up(c)
        h = h_ref[...]
        g = _dot(h, wg_ref[:, cols].astype(BF16))
        u = _dot(h, wu_ref[:, cols].astype(BF16))
        a_ref[:, cols] = (g * _sigmoid(g) * u).astype(BF16)
    f = None
    for c, cols in enumerate(FFN_CHUNKS):
        wait_down(c)
        part = _dot(a_ref[:, cols], wd_ref[cols, :].astype(BF16))
        f = part if f is None else f + part
    o_ref[...] = x_ref[...] + 0.5 * _rmsnorm_f32(f, post_g_ref[...])


def _ffn_kernel(x_ref, pre_g_ref, wg_hbm, wu_hbm, wd_hbm, post_g_ref, o_ref, h_ref, a_ref,
                wg_ref, wu_ref, wd_ref, sem):
    refs = (x_ref, pre_g_ref, wg_ref, wu_ref, wd_ref, post_g_ref, o_ref, h_ref, a_ref)
    first = pl.program_id(0) == 0

    @pl.when(jnp.logical_not(first))
    def _():
        _ffn_body(*refs, lambda c: None, lambda c: None)

    @pl.when(first)
    def _():
        gate_up, down = _ffn_weight_copies(wg_hbm, wu_hbm, wd_hbm, wg_ref, wu_ref, wd_ref, sem)
        for cg, cu in gate_up:
            cg.start()
            cu.start()
        for cd in down:
            cd.start()

        def wait_gate_up(c):
            gate_up[c][0].wait()
            gate_up[c][1].wait()

        _ffn_body(*refs, wait_gate_up, lambda c: down[c].wait())


def _ffn(x, pre_g, wg, wu, wd, post_g):
    n, d = x.shape
    tm = FFN_TM
    row = lambda i: (i, 0)
    n_chunks = len(FFN_CHUNKS)
    blocks = 2 * 2 * _nbytes((tm, d), F32) + 2 * _nbytes((1, d), F32)
    scratch = _nbytes((tm, d), BF16) + _nbytes((tm, D_FF), BF16) + 3 * _nbytes((d, D_FF), wg.dtype)
    temps = 4 * _nbytes((tm, d), F32)
    hbm = pl.BlockSpec(memory_space=pl.ANY)
    return pl.pallas_call(
        _ffn_kernel,
        grid=(n // tm,),
        in_specs=[pl.BlockSpec((tm, d), row), _resident((1, d)), hbm, hbm, hbm, _resident((1, d))],
        out_specs=pl.BlockSpec((tm, d), row),
        out_shape=jax.ShapeDtypeStruct((n, d), F32),
        scratch_shapes=[
            pltpu.VMEM((tm, d), BF16), pltpu.VMEM((tm, D_FF), BF16),
            pltpu.VMEM((d, D_FF), wg.dtype), pltpu.VMEM((d, D_FF), wu.dtype), pltpu.VMEM((D_FF, d), wd.dtype),
            pltpu.SemaphoreType.DMA((3, n_chunks)),
        ],
        compiler_params=pltpu.CompilerParams(
            dimension_semantics=("arbitrary",),
            vmem_limit_bytes=_vmem_limit(blocks, scratch, temps),
        ),
        name="ffn",
    )(x, pre_g, wg, wu, wd, post_g)


QKV_TM = 1024
QKV_TOTAL = 3 * NA_WIDTH + 3 * NB_WIDTH
N_DIL_OUTS = 3 * NB_GROUPS
STAGED_OUTS = [i for i in range(N_DIL_OUTS) if DILATIONS[i % NB_GROUPS] > 1]
N_STAGED_OUTS = len(STAGED_OUTS)
SPLIT_STRIDE = 4
SPLIT_OUTS = [i for i in STAGED_OUTS if DILATIONS[i % NB_GROUPS] == SPLIT_STRIDE ** 2]
N_SPLIT_OUTS = len(SPLIT_OUTS)


def _qkv_kernel(x_ref, g_ref, w32_ref, *refs):
    na_refs, dil_refs = refs[:3], refs[3:3 + N_DIL_OUTS]
    w_ref = refs[3 + N_DIL_OUTS]
    n_stage = N_STAGED_OUTS * (x_ref.shape[0] // SUB_TM)
    y_refs, t_refs = refs[4 + N_DIL_OUTS:4 + N_DIL_OUTS + n_stage], refs[4 + N_DIL_OUTS + n_stage:]

    @pl.when((pl.program_id(0) == 0) & (pl.program_id(1) == 0))
    def _():
        for c in range(QKV_TOTAL // GROUP_WIDTH):
            cols = slice(c * GROUP_WIDTH, (c + 1) * GROUP_WIDTH)
            w = w32_ref[:, cols]
            if cols.stop <= NA_WIDTH or 3 * NA_WIDTH <= cols.start < 3 * NA_WIDTH + NB_WIDTH:
                w = w * Q_SCALE
            w_ref[:, cols] = w.astype(BF16)

    for k, rows in enumerate(_sub_tiles(x_ref.shape[0])):
        h = _rmsnorm_f32(x_ref[rows], g_ref[...]).astype(BF16)
        for idx, o_ref in enumerate(dil_refs):
            which, g = divmod(idx, NB_GROUPS)
            dil = DILATIONS[g]
            off = 3 * NA_WIDTH + which * NB_WIDTH + g * GROUP_WIDTH
            y = _dot(h, w_ref[:, off:off + GROUP_WIDTH])
            if dil == 1:
                o_ref[0, rows] = y.astype(BF16)
                continue
            y_ref = y_refs[k * N_STAGED_OUTS + STAGED_OUTS.index(idx)]
            per_res = SUB_TM // dil
            for c in range(GROUP_LANE_BLOCKS):
                y_ref[c] = y[:, c * V7X_LANES:(c + 1) * V7X_LANES]
            if dil == SPLIT_STRIDE ** 2:
                t_ref = t_refs[k * N_SPLIT_OUTS + SPLIT_OUTS.index(idx)]
                quarter = SUB_TM // SPLIT_STRIDE
                for c in range(GROUP_LANE_BLOCKS):
                    for r4 in range(SPLIT_STRIDE):
                        t_ref[c, r4 * quarter:(r4 + 1) * quarter] = y_ref[c, pl.ds(r4, quarter, stride=SPLIT_STRIDE), :]
                for r in range(dil):
                    start = (r % SPLIT_STRIDE) * quarter + r // SPLIT_STRIDE
                    for c in range(GROUP_LANE_BLOCKS):
                        o_ref[r, k * per_res:(k + 1) * per_res, c * V7X_LANES:(c + 1) * V7X_LANES] = (
                            t_ref[c, pl.ds(start, per_res, stride=SPLIT_STRIDE), :].astype(BF16))
                continue
            for r in range(dil):
                for c in range(GROUP_LANE_BLOCKS):
                    o_ref[r, k * per_res:(k + 1) * per_res, c * V7X_LANES:(c + 1) * V7X_LANES] = (
                        y_ref[c, pl.ds(r, per_res, stride=dil), :].astype(BF16))
        for idx, o_ref in enumerate(na_refs):
            y = _dot(h, w_ref[:, idx * NA_WIDTH:(idx + 1) * NA_WIDTH])
            o_ref[rows] = y.astype(BF16)


def _qkv(x, g, w_in):
    batch, seq, d = x.shape
    tm = QKV_TM
    tok = lambda b, i: (b, i, 0)
    res = lambda b, i: (b, 0, i, 0)
    out_specs = [pl.BlockSpec((None, tm, NA_WIDTH), tok)] * 3
    out_shape = [jax.ShapeDtypeStruct((batch, seq, NA_WIDTH), BF16)] * 3
    for _ in range(3):
        for dil in DILATIONS:
            out_specs.append(pl.BlockSpec((None, dil, tm // dil, GROUP_WIDTH), res))
            out_shape.append(jax.ShapeDtypeStruct((batch, dil, seq // dil, GROUP_WIDTH), BF16))
    blocks = 2 * _nbytes((tm, d), F32) + _nbytes((d, QKV_TOTAL), w_in.dtype) + 2 * _nbytes((tm, QKV_TOTAL), BF16)
    stage = (GROUP_LANE_BLOCKS, SUB_TM, V7X_LANES)
    n_stage = N_STAGED_OUTS * (tm // SUB_TM)
    n_split = N_SPLIT_OUTS * (tm // SUB_TM)
    scratch = (n_stage + n_split) * _nbytes(stage, F32) + _nbytes((d, QKV_TOTAL), BF16)
    temps = _nbytes((SUB_TM, d), F32)
    outs = pl.pallas_call(
        _qkv_kernel,
        grid=(batch, seq // tm),
        in_specs=[pl.BlockSpec((None, tm, d), tok), _resident((1, d)), _resident((d, QKV_TOTAL))],
        out_specs=out_specs,
        out_shape=out_shape,
        scratch_shapes=[pltpu.VMEM((d, QKV_TOTAL), BF16)] + [pltpu.VMEM(stage, F32)] * (n_stage + n_split),
        compiler_params=pltpu.CompilerParams(
            dimension_semantics=("arbitrary", "arbitrary"),
            vmem_limit_bytes=_vmem_limit(blocks, scratch, temps),
        ),
        name="qkv",
    )(x, g, w_in)
    qa, ka, va = outs[:3]
    qb, kb, vb = (outs[3 + w * NB_GROUPS:3 + (w + 1) * NB_GROUPS] for w in range(3))
    return qa, ka, va, qb, kb, vb


NA_HEADS_PER_STEP = V7X_LANES // HEAD_DIM
NA_SPAN = NA_ROWS * GRID_W
NA_VARIANTS = NA_ROWS
NA_RPB_ROWS = 2 * NA_ROWS - 1
NA_RPB_COLS = 2 * NA_COLS - 1
NA_UNROLL = 32


NA_RPB_PAIRS = (NA_RPB_ROWS + 1) // 2


def _nbr_build_bias(rep_ref, tbl_ref):
    width = NA_RPB_PAIRS * V7X_LANES
    qc = lax.broadcasted_iota(jnp.int32, (GRID_W, width), 0)
    kc = lax.broadcasted_iota(jnp.int32, (GRID_W, width), 1) % GRID_W
    col0 = jnp.clip(qc - NA_COLS // 2, 0, GRID_W - NA_COLS)
    col_ok = (kc >= col0) & (kc < col0 + NA_COLS)
    for h in range(NA_HEADS_PER_STEP):
        blocks = []
        for p in range(NA_RPB_PAIRS):
            row = jnp.broadcast_to(rep_ref[h, p:p + 1, :], (GRID_W, V7X_LANES))
            blocks.append(pltpu.roll(row, V7X_LANES - (NA_COLS - 1), 1, stride=1, stride_axis=0))
        w = jnp.where(col_ok, jnp.concatenate(blocks, axis=1) * LOG2E, -jnp.inf)
        for variant in range(NA_VARIANTS):
            tbl_ref[h, variant] = w[:, variant * GRID_W:variant * GRID_W + NA_SPAN]


def _nbr_kernel(q_ref, k_ref, v_ref, rep_ref, o_ref, tbl_ref, s_scr, p_scr, *, rows):
    hp = NA_HEADS_PER_STEP
    ones = jnp.ones((NA_SPAN, V7X_LANES), BF16)

    @pl.when(pl.program_id(1) == 0)
    def _():
        _nbr_build_bias(rep_ref, tbl_ref)

    lane_head = lax.broadcasted_iota(jnp.int32, (GRID_W, V7X_LANES), 1) // HEAD_DIM

    def scores(r):
        row0 = jnp.clip(r - NA_ROWS // 2, 0, rows - NA_ROWS)
        variant = row0 - r + (NA_ROWS - 1)
        q = q_ref[pl.ds(pl.multiple_of(r * GRID_W, GRID_W), GRID_W), :]
        kw = k_ref[pl.ds(pl.multiple_of(row0 * GRID_W, GRID_W), NA_SPAN), :]
        qs = jnp.concatenate([jnp.where(lane_head == h, q, jnp.zeros_like(q)) for h in range(hp)], axis=0)
        return _dot_nt(qs, kw) + tbl_ref[:, variant].reshape(hp * GRID_W, NA_SPAN)

    def probs(s):
        return (jnp.exp2(s - jnp.max(s, axis=-1, keepdims=True)).astype(BF16),)

    def finish(r, p):
        row0 = jnp.clip(r - NA_ROWS // 2, 0, rows - NA_ROWS)
        vw = v_ref[pl.ds(pl.multiple_of(row0 * GRID_W, GRID_W), NA_SPAN), :]
        o = _dot(p, jnp.concatenate([vw, ones], axis=1))
        out, den = o[:GRID_W, :V7X_LANES], o[:GRID_W, V7X_LANES:]
        for h in range(1, hp):
            rows_h = slice(h * GRID_W, (h + 1) * GRID_W)
            out = jnp.where(lane_head == h, o[rows_h, :V7X_LANES], out)
            den = jnp.where(lane_head == h, o[rows_h, V7X_LANES:], den)
        o_ref[pl.ds(pl.multiple_of(r * GRID_W, GRID_W), GRID_W), :] = (out * (1.0 / den)).astype(BF16)

    _pipelined_attention(rows // NA_UNROLL, NA_UNROLL, scores, probs, finish, s_scr, p_scr, ())


def _nbr_attention(q, k, v, rpb):
    batch, seq, width = q.shape
    rows = seq // GRID_W
    n_pairs = width // V7X_LANES
    rep = jnp.pad(rpb.astype(F32), ((0, 0), (0, 2 * NA_RPB_PAIRS - NA_RPB_ROWS), (0, GRID_W - NA_RPB_COLS)))
    rep = rep.reshape(rpb.shape[0], NA_RPB_PAIRS, V7X_LANES)
    tok = lambda p, b: (b, 0, p)
    tbl_shape = (NA_HEADS_PER_STEP, NA_VARIANTS, GRID_W, NA_SPAN)
    rep_block = (NA_HEADS_PER_STEP, NA_RPB_PAIRS, V7X_LANES)
    blocks = 2 * 4 * _nbytes((seq, V7X_LANES), BF16) + 2 * _nbytes(rep_block, F32)
    tile = (2, NA_UNROLL, NA_HEADS_PER_STEP * GRID_W, NA_SPAN)
    scratch = _nbytes(tbl_shape, F32) + _nbytes(tile, F32) + _nbytes(tile, BF16)
    temps = NA_UNROLL * _nbytes(tile[2:], F32)
    return pl.pallas_call(
        functools.partial(_nbr_kernel, rows=rows),
        grid=(n_pairs, batch),
        in_specs=[
            pl.BlockSpec((None, seq, V7X_LANES), tok),
            pl.BlockSpec((None, seq, V7X_LANES), tok),
            pl.BlockSpec((None, seq, V7X_LANES), tok),
            pl.BlockSpec(rep_block, lambda p, b: (p, 0, 0)),
        ],
        out_specs=pl.BlockSpec((None, seq, V7X_LANES), tok),
        out_shape=jax.ShapeDtypeStruct((batch, seq, width), BF16),
        scratch_shapes=[pltpu.VMEM(tbl_shape, F32), pltpu.VMEM(tile, F32), pltpu.VMEM(tile, BF16)],
        compiler_params=pltpu.CompilerParams(
            dimension_semantics=("arbitrary", "arbitrary"),
            vmem_limit_bytes=_vmem_limit(blocks, scratch, temps),
        ),
        name="nbr_attn",
    )(q, k, v, rep)


DIL_HALF = 64
DIL_TQ = 128
DIL_TK = DIL_TQ + 2 * DIL_HALF
DIL_VARIANTS = 3
DIL_STEP_ROWS = 4096
DIL_UNROLL = 8


def _dil_slopes(group):
    heads = jnp.arange(NB_HEADS_PER_GROUP, dtype=F32) + group * NB_HEADS_PER_GROUP
    return jnp.exp2(-ALIBI_MAX_EXP * (heads + 1.0) / NB_HEADS)


def _dil_build_bias(slope_ref, bias_ref, dilation):
    qq = lax.broadcasted_iota(jnp.int32, (DIL_TQ, DIL_TK), 0)
    kk = lax.broadcasted_iota(jnp.int32, (DIL_TQ, DIL_TK), 1)
    for variant in range(DIL_VARIANTS):
        rel = kk - qq - variant * DIL_HALF
        dist = (dilation * jnp.abs(rel)).astype(F32)
        in_band = jnp.abs(rel) <= DIL_HALF
        for h in range(NB_HEADS_PER_GROUP):
            bias = -(slope_ref[h] * dist) * LOG2E
            bias_ref[variant, h * DIL_TQ:(h + 1) * DIL_TQ, :] = jnp.where(in_band, bias, -jnp.inf)


def _dil_kernel(q_ref, k_ref, v_ref, slope_ref, o_ref, lse_ref, bias_ref, s_scr, p_scr, m_scr,
                *, seq_len, stretch, dilation):
    nh = NB_HEADS_PER_GROUP
    hpb = V7X_LANES // HEAD_DIM
    ones = jnp.ones((DIL_TK, V7X_LANES), BF16)

    @pl.when((pl.program_id(0) == 0) & (pl.program_id(1) == 0) & (pl.program_id(2) == 0))
    def _():
        _dil_build_bias(slope_ref, bias_ref, dilation)

    group_lane_head = lax.broadcasted_iota(jnp.int32, (DIL_TQ, GROUP_WIDTH), 1) // HEAD_DIM
    lane_head = lax.broadcasted_iota(jnp.int32, (DIL_TQ, V7X_LANES), 1) // HEAD_DIM
    base = pl.program_id(2) * stretch
    tiles_per_residue = stretch // DIL_TQ

    def window(t):
        r, j = t // tiles_per_residue, t % tiles_per_residue
        qloc = pl.multiple_of(j * DIL_TQ, DIL_TQ)
        qs = base + qloc
        ws = jnp.clip(qs - DIL_HALF, 0, seq_len - DIL_TK)
        variant = (qs - ws) // DIL_HALF
        return r, qloc, pl.multiple_of(ws, DIL_HALF), variant

    def scores(t):
        r, qloc, ws, variant = window(t)
        q = q_ref[r, pl.ds(qloc, DIL_TQ), :]
        kw = k_ref[r, pl.ds(ws, DIL_TK), :]
        qst = jnp.concatenate([jnp.where(group_lane_head == h, q, jnp.zeros_like(q)) for h in range(nh)],
                              axis=0)
        return _dot_nt(qst, kw) + bias_ref[variant]

    def probs(s):
        m = jnp.max(s, axis=-1, keepdims=True)
        return jnp.exp2(s - m).astype(BF16), jnp.broadcast_to(m, (s.shape[0], V7X_LANES))

    def finish(t, p, m):
        r, qloc, ws, _ = window(t)
        vw = v_ref[r, pl.ds(ws, DIL_TK), :]
        outs, lses = [], []
        for c in range(GROUP_LANE_BLOCKS):
            rows_c = slice(c * hpb * DIL_TQ, (c + 1) * hpb * DIL_TQ)
            v_aug = jnp.concatenate([vw[:, c * V7X_LANES:(c + 1) * V7X_LANES], ones], axis=1)
            o = _dot(p[rows_c], v_aug)
            m_c = m[rows_c]
            out_c, den_c, max_c = o[:DIL_TQ, :V7X_LANES], o[:DIL_TQ, V7X_LANES:], m_c[:DIL_TQ]
            for h in range(1, hpb):
                rows_h = slice(h * DIL_TQ, (h + 1) * DIL_TQ)
                out_c = jnp.where(lane_head == h, o[rows_h, :V7X_LANES], out_c)
                den_c = jnp.where(lane_head == h, o[rows_h, V7X_LANES:], den_c)
                max_c = jnp.where(lane_head == h, m_c[rows_h], max_c)
            outs.append(out_c * (1.0 / den_c))
            lses.append(max_c * LN2 + jnp.log(den_c))
        o_ref[r, pl.ds(qloc, DIL_TQ), :] = jnp.concatenate(outs, axis=1).astype(BF16)
        lse_ref[r, pl.ds(qloc, DIL_TQ), :] = jnp.concatenate(lses, axis=1)

    n_tiles = q_ref.shape[0] * tiles_per_residue
    _pipelined_attention(n_tiles // DIL_UNROLL, DIL_UNROLL, scores, probs, finish, s_scr, p_scr, (m_scr,))


def _dil_attention(q, k, v, group):
    batch, dilation, seq_len, width = q.shape
    stretch = min(seq_len, DIL_STEP_ROWS)
    res_blk = DIL_STEP_ROWS // stretch
    assert dilation % res_blk == 0 and seq_len % stretch == 0
    qmap = lambda b, r, s: (b, r, s, 0)
    kvmap = lambda b, r, s: (b, r, 0, 0)
    bias_shape = (DIL_VARIANTS, NB_HEADS_PER_GROUP * DIL_TQ, DIL_TK)
    tile = (2, DIL_UNROLL, NB_HEADS_PER_GROUP * DIL_TQ, DIL_TK)
    stat = tile[:3] + (V7X_LANES,)
    kv_buffers = 2
    kv_spec = pl.BlockSpec((None, res_blk, seq_len, width), kvmap)
    blocks = (2 * (2 * _nbytes((DIL_STEP_ROWS, width), BF16) + _nbytes((DIL_STEP_ROWS, width), F32))
              + kv_buffers * 2 * _nbytes((res_blk * seq_len, width), BF16))
    scratch = _nbytes(bias_shape, F32) + _nbytes(tile, F32) + _nbytes(tile, BF16) + _nbytes(stat, F32)
    temps = DIL_UNROLL * _nbytes(tile[2:], F32)
    return pl.pallas_call(
        functools.partial(_dil_kernel, seq_len=seq_len, stretch=stretch, dilation=dilation),
        grid=(batch, dilation // res_blk, seq_len // stretch),
        in_specs=[
            pl.BlockSpec((None, res_blk, stretch, width), qmap),
            kv_spec,
            kv_spec,
            pl.BlockSpec(memory_space=pltpu.SMEM),
        ],
        out_specs=[pl.BlockSpec((None, res_blk, stretch, width), qmap)] * 2,
        out_shape=[jax.ShapeDtypeStruct(q.shape, BF16), jax.ShapeDtypeStruct(q.shape, F32)],
        scratch_shapes=[pltpu.VMEM(bias_shape, F32), pltpu.VMEM(tile, F32), pltpu.VMEM(tile, BF16),
                        pltpu.VMEM(stat, F32)],
        compiler_params=pltpu.CompilerParams(
            dimension_semantics=("arbitrary", "arbitrary", "arbitrary"),
            vmem_limit_bytes=_vmem_limit(blocks, scratch, temps),
        ),
        name=f"dil_attn_g{group}",
    )(q, k, v, _dil_slopes(group))


MERGE_TM = 1024
MERGE_CN = 256


def _to_token_order(src_ref, scr_ref, tmp_ref, dil, k):
    per_res = SUB_TM // dil
    res_rows = slice(k * per_res, (k + 1) * per_res)
    if dil == 1:
        return src_ref[0, res_rows].astype(F32)
    rows = slice(k * SUB_TM, (k + 1) * SUB_TM)
    if dil == SPLIT_STRIDE ** 2:
        quarter = SUB_TM // SPLIT_STRIDE
        for r in range(dil):
            start = k * SUB_TM + (r % SPLIT_STRIDE) * quarter + r // SPLIT_STRIDE
            for c in range(GROUP_LANE_BLOCKS):
                tmp_ref[c, pl.ds(start, per_res, stride=SPLIT_STRIDE), :] = (
                    src_ref[r, res_rows, c * V7X_LANES:(c + 1) * V7X_LANES].astype(F32))
        for c in range(GROUP_LANE_BLOCKS):
            for r4 in range(SPLIT_STRIDE):
                lo = k * SUB_TM + r4 * quarter
                scr_ref[c, pl.ds(k * SUB_TM + r4, quarter, stride=SPLIT_STRIDE), :] = tmp_ref[c, lo:lo + quarter]
        return jnp.concatenate([scr_ref[c, rows] for c in range(GROUP_LANE_BLOCKS)], axis=-1)
    for r in range(dil):
        for c in range(GROUP_LANE_BLOCKS):
            scr_ref[c, pl.ds(k * SUB_TM + r, per_res, stride=dil), :] = (
                src_ref[r, res_rows, c * V7X_LANES:(c + 1) * V7X_LANES].astype(F32))
    return jnp.concatenate([scr_ref[c, rows] for c in range(GROUP_LANE_BLOCKS)], axis=-1)


def _merge_kernel(x_ref, pre_g_ref, wgate_ref, oa_ref, wa_ref, *refs):
    ng = NB_GROUPS
    o_refs, lse_refs = refs[:ng], refs[ng:2 * ng]
    wb_ref, wout_ref, post_g_ref, out_ref, h_ref, ob_ref, mg_ref = refs[2 * ng:2 * ng + 7]
    scr_refs = iter(refs[2 * ng + 7:])
    o_scr = [next(scr_refs) if dil > 1 else None for dil in DILATIONS]
    lse_scr = [next(scr_refs) if dil > 1 else None for dil in DILATIONS]
    o_tmp = [next(scr_refs) if dil == SPLIT_STRIDE ** 2 else None for dil in DILATIONS]
    lse_tmp = [next(scr_refs) if dil == SPLIT_STRIDE ** 2 else None for dil in DILATIONS]
    subs = _sub_tiles(x_ref.shape[0])
    for k, rows in enumerate(subs):
        h_ref[rows] = _rmsnorm_f32(x_ref[rows], pre_g_ref[...]).astype(BF16)
        os_ = [_to_token_order(r, s, t, d, k) for r, s, t, d in zip(o_refs, o_scr, o_tmp, DILATIONS)]
        lses = [_to_token_order(r, s, t, d, k) for r, s, t, d in zip(lse_refs, lse_scr, lse_tmp, DILATIONS)]
        mx = jnp.maximum(jnp.maximum(lses[0], lses[1]), lses[2])
        es = [jnp.exp(l - mx) for l in lses]
        inv = 1.0 / (es[0] + es[1] + es[2])
        for g in range(ng):
            ob_ref[rows, g * GROUP_WIDTH:(g + 1) * GROUP_WIDTH] = (os_[g] * (es[g] * inv)).astype(BF16)
    chunks = [slice(n * MERGE_CN, (n + 1) * MERGE_CN) for n in range(D_MODEL // MERGE_CN)]
    for rows in subs:
        for cols in chunks:
            ga = _dot(h_ref[rows], wgate_ref[:, cols].astype(BF16))
            ya = _dot(oa_ref[rows], wa_ref[:, cols])
            out_ref[rows, cols] = _sigmoid(ga) * ya
    for rows in subs:
        for cols in chunks:
            cols_b = slice(D_MODEL + cols.start, D_MODEL + cols.stop)
            gb = _dot(h_ref[rows], wgate_ref[:, cols_b].astype(BF16))
            yb = _dot(ob_ref[rows], wb_ref[:, cols])
            mg_ref[rows, cols] = (out_ref[rows, cols] + _sigmoid(gb) * yb).astype(BF16)
    for rows in subs:
        out_ref[rows] = _dot(mg_ref[rows], wout_ref[...])
    for rows in subs:
        out_ref[rows] = x_ref[rows] + _rmsnorm_f32(out_ref[rows], post_g_ref[...])


def _merge(x, pre_g, wgate, oa, wa, obs, lses, wb, wout, post_g):
    batch, seq, d = x.shape
    tm = MERGE_TM
    tok = lambda b, i: (b, i, 0)
    res = lambda b, i: (b, 0, i, 0)
    tokspec = lambda w: pl.BlockSpec((None, tm, w), tok)
    resspecs = [pl.BlockSpec((None, dil, tm // dil, GROUP_WIDTH), res) for dil in DILATIONS]
    blocks = (2 * 2 * _nbytes((tm, d), F32) + 2 * _nbytes((tm, NA_WIDTH), BF16)
              + 2 * 3 * (_nbytes((tm, GROUP_WIDTH), BF16) + _nbytes((tm, GROUP_WIDTH), F32))
              + _nbytes((NA_WIDTH, d), wa.dtype) + _nbytes((d + NB_WIDTH, d), wb.dtype) + _nbytes((d, 2 * d), F32))
    work = [pltpu.VMEM((tm, d), BF16), pltpu.VMEM((tm, NB_WIDTH), BF16), pltpu.VMEM((tm, d), BF16)]
    n_interleave = 2 * sum(dil > 1 for dil in DILATIONS) + 2 * sum(dil == SPLIT_STRIDE ** 2 for dil in DILATIONS)
    scratch = (n_interleave * _nbytes((tm, GROUP_WIDTH), F32) + 2 * _nbytes((tm, d), BF16)
               + _nbytes((tm, NB_WIDTH), BF16))
    temps = 2 * _nbytes((SUB_TM, d), F32)
    return pl.pallas_call(
        _merge_kernel,
        grid=(batch, seq // tm),
        in_specs=[
            tokspec(d), _resident((1, d)),
            pl.BlockSpec((pl.Element(d), pl.Element(2 * d)), lambda *_: (0, QKV_TOTAL),
                         pipeline_mode=pl.Buffered(1)),
            tokspec(NA_WIDTH), _resident((NA_WIDTH, d)),
            *resspecs, *resspecs,
            _resident((NB_WIDTH, d)), _resident((d, d)), _resident((1, d)),
        ],
        out_specs=tokspec(d),
        out_shape=jax.ShapeDtypeStruct((batch, seq, d), F32),
        scratch_shapes=work + [pltpu.VMEM((GROUP_LANE_BLOCKS, tm, V7X_LANES), F32)] * n_interleave,
        compiler_params=pltpu.CompilerParams(
            dimension_semantics=("arbitrary", "arbitrary"),
            vmem_limit_bytes=_vmem_limit(blocks, scratch, temps),
        ),
        name="merge",
    )(x, pre_g, wgate, oa, wa, *obs, *lses, wb, wout, post_g)


def kernel(x, ffn1_pre_g, ffn1_w_gate, ffn1_w_up, ffn1_w_down, ffn1_post_g, mix_pre_g, w_in, na_rpb, w_branch_a, w_branch_b, w_out, mix_post_g, ffn2_pre_g, ffn2_w_gate, ffn2_w_up, ffn2_w_down, ffn2_post_g):
    batch, seq, d = x.shape
    depth = ffn1_pre_g.shape[0]
    for window, dilation in DIL_PAIRS:
        assert window // (2 * dilation) == DIL_HALF
    for l in range(depth):
        x = _ffn(x.reshape(batch * seq, d), ffn1_pre_g[l][None], ffn1_w_gate[l], ffn1_w_up[l],
                 ffn1_w_down[l], ffn1_post_g[l][None])
        x = x.reshape(batch, seq, d)
        qa, ka, va, qb, kb, vb = _qkv(x, mix_pre_g[l][None], w_in[l])
        oa = _nbr_attention(qa, ka, va, na_rpb[l])
        obs, lses = [], []
        for g in range(NB_GROUPS):
            o_g, lse_g = _dil_attention(qb[g], kb[g], vb[g], g)
            obs.append(o_g)
            lses.append(lse_g)
        x = _merge(x, mix_pre_g[l][None], w_in[l],
                   oa, w_branch_a[l].astype(BF16), obs, lses, w_branch_b[l].astype(BF16),
                   w_out[l].astype(BF16), mix_post_g[l][None])
        x = _ffn(x.reshape(batch * seq, d), ffn2_pre_g[l][None], ffn2_w_gate[l], ffn2_w_up[l],
                 ffn2_w_down[l], ffn2_post_g[l][None])
        x = x.reshape(batch, seq, d)
    return x
```
